```python
import math
import jax, jax.numpy as jnp
from jax import lax
import numpy as np

D_MODEL = 1024
BATCH = 8
SEQ = 2048
DEPTH = 1
DEC_BATCH = 128
DEC_SEQ = 1
PAST_LEN = 16384
PAGE_SIZE = 128

S5_WIDTH = D_MODEL // 2
S5_GROUP = 16
S5_GROUPS = S5_WIDTH // S5_GROUP
S5_STATE = 64
MLSTM_WIDTH = D_MODEL // 2
MLSTM_HEADS = 4
MLSTM_DH = MLSTM_WIDTH // MLSTM_HEADS
MLSTM_CHUNK = 64
N_EXPERTS = 32
TOP_K = 4
D_FF = D_MODEL
SWIGLU_LIMIT = 7.0
SWIGLU_ALPHA = 1.702
EPS = 1e-6
IN_COLS = S5_WIDTH + 4 * MLSTM_WIDTH + 2 * MLSTM_HEADS + 2 * D_MODEL

kernel_name = "hybrid_s5_mlstm_moe_adaln_step"

F32 = jnp.float32


def rmsnorm(x, g):
    xf = x.astype(F32)
    return xf * lax.rsqrt(jnp.mean(xf * xf, axis=-1, keepdims=True) + EPS) * g.astype(F32)


def _complex_affine_combine(e1, e2):
    ar1, ai1, br1, bi1 = e1
    ar2, ai2, br2, bi2 = e2
    return (ar2 * ar1 - ai2 * ai1,
            ar2 * ai1 + ai2 * ar1,
            ar2 * br1 - ai2 * bi1 + br2,
            ar2 * bi1 + ai2 * br1 + bi2)


def s5_mixer(u, x0_re, x0_im, lam_re, lam_im, log_dt, B_re, B_im, C_re, C_im, D_s5):
    bsz, L, _ = u.shape
    u = u.astype(F32).reshape(bsz, L, S5_GROUPS, S5_GROUP)
    dt = jnp.exp(log_dt.astype(F32))[:, None]
    lr, li = lam_re.astype(F32), lam_im.astype(F32)
    mag = jnp.exp(lr * dt)
    ab_re, ab_im = mag * jnp.cos(li * dt), mag * jnp.sin(li * dt)
    den = lr * lr + li * li
    nr, ni = ab_re - 1.0, ab_im
    coef_re = (nr * lr + ni * li) / den
    coef_im = (ni * lr - nr * li) / den
    Br, Bi = B_re.astype(F32), B_im.astype(F32)
    Bb_re = coef_re[..., None] * Br - coef_im[..., None] * Bi
    Bb_im = coef_re[..., None] * Bi + coef_im[..., None] * Br
    bu_re = jnp.einsum('blgh,gph->blgp', u, Bb_re)
    bu_im = jnp.einsum('blgh,gph->blgp', u, Bb_im)
    a_re = jnp.broadcast_to(ab_re, (1, L, S5_GROUPS, S5_STATE))
    a_im = jnp.broadcast_to(ab_im, (1, L, S5_GROUPS, S5_STATE))
    apow_re, apow_im, xs_re, xs_im = lax.associative_scan(
        _complex_affine_combine, (a_re, a_im, bu_re, bu_im), axis=1)
    x0r = x0_re.astype(F32)[:, None]
    x0i = x0_im.astype(F32)[:, None]
    x_re = xs_re + apow_re * x0r - apow_im * x0i
    x_im = xs_im + apow_re * x0i + apow_im * x0r
    y = (jnp.einsum('blgp,ghp->blgh', x_re, C_re.astype(F32))
         - jnp.einsum('blgp,ghp->blgh', x_im, C_im.astype(F32))
         + D_s5.astype(F32) * u)
    return y.reshape(bsz, L, S5_WIDTH), x_re[:, -1], x_im[:, -1]


def mlstm_mixer(q, k, v, i_pre, f_pre, C0, n0, m0):
    bsz, L = q.shape[:2]
    Lc = math.gcd(L, MLSTM_CHUNK)
    nc = L // Lc
    k = k * (MLSTM_DH ** -0.5)
    log_f = jax.nn.log_sigmoid(f_pre)

    def vec_chunks(a):
        return a.reshape(bsz, nc, Lc, MLSTM_HEADS, MLSTM_DH).transpose(1, 0, 3, 2, 4)

    def gate_chunks(a):
        return a.reshape(bsz, nc, Lc, MLSTM_HEADS).transpose(1, 0, 3, 2)

    causal = jnp.tril(jnp.ones((Lc, Lc), dtype=bool))

    def step(carry, inp):
        C, n, m = carry
        qc, kc, vc, ic, lfc = inp
        b = jnp.cumsum(lfc, axis=-1)
        Dm = b[..., :, None] - b[..., None, :] + ic[..., None, :]
        Dm = jnp.where(causal, Dm, -jnp.inf)
        m_inter = b + m[..., None]
        m_t = jnp.maximum(m_inter, jnp.max(Dm, axis=-1))
        w_inter = jnp.exp(m_inter - m_t)
        W = jnp.exp(Dm - m_t[..., None])
        S = jnp.einsum('bhtd,bhsd->bhts', qc, kc) * W
        num = (w_inter[..., None] * jnp.einsum('bhtd,bhde->bhte', qc, C)
               + jnp.einsum('bhts,bhse->bhte', S, vc))
        den = w_inter * jnp.einsum('bhtd,bhd->bht', qc, n) + jnp.sum(S, axis=-1)
        h = num / jnp.maximum(jnp.abs(den), jnp.exp(-m_t))[..., None]
        m_new = m_t[..., -1]
        w_last = W[..., -1, :]
        decay = w_inter[..., -1]
        C_new = decay[..., None, None] * C + jnp.einsum('bhs,bhsd,bhse->bhde', w_last, kc, vc)
        n_new = decay[..., None] * n + jnp.einsum('bhs,bhsd->bhd', w_last, kc)
        return (C_new, n_new, m_new), h

    carry0 = (C0.astype(F32), n0.astype(F32), m0.astype(F32))
    (C, n, m), h = lax.scan(step, carry0,
                            (vec_chunks(q), vec_chunks(k), vec_chunks(v), gate_chunks(i_pre), gate_chunks(log_f)))
    h = h.transpose(1, 0, 3, 2, 4).reshape(bsz, L, MLSTM_HEADS, MLSTM_DH)
    return h, C, n, m


def moe(h, router_w, router_b, w_up, b_up, w_down, b_down):
    bsz, L, _ = h.shape
    t = h.reshape(-1, D_MODEL)
    logits = (t @ router_w.astype(F32) + router_b.astype(F32)).astype(F32)
    top_vals, top_idx = lax.top_k(logits, TOP_K)
    top_w = jax.nn.softmax(top_vals, axis=-1)
    gates = jnp.sum(jax.nn.one_hot(top_idx, N_EXPERTS, dtype=F32) * top_w[..., None], axis=1)
    out = jnp.zeros((t.shape[0], D_MODEL), F32)
    for e in range(N_EXPERTS):
        gu = t @ w_up[e].astype(F32) + b_up[e].astype(F32)
        g = jnp.minimum(gu[:, :D_FF], SWIGLU_LIMIT)
        up = jnp.clip(gu[:, D_FF:], -SWIGLU_LIMIT, SWIGLU_LIMIT)
        act = (up + 1.0) * g * jax.nn.sigmoid(SWIGLU_ALPHA * g)
        out = out + gates[:, e:e + 1] * (act @ w_down[e].astype(F32) + b_down[e].astype(F32))
    return out.reshape(bsz, L, D_MODEL)


def block(x, c, s5_re0, s5_im0, C0, n0, m0, norm1_g, norm2_g, w_ada, b_ada, w_in,
          lam_re, lam_im, log_dt, B_re, B_im, C_re, C_im, D_s5, w_glu,
          b_i, b_f, mnorm_g, w_mout, w_out, router_w, router_b, w_up, b_up, w_down, b_down):
    x = x.astype(F32)
    bsz, L, _ = x.shape
    mod = jax.nn.silu(c.astype(F32)) @ w_ada.astype(F32) + b_ada.astype(F32)
    shift1, scale1, gate1, shift2, scale2, gate2 = [m[:, None, :] for m in jnp.split(mod, 6, axis=-1)]

    h = rmsnorm(x, norm1_g) * (1.0 + scale1) + shift1
    z = h @ w_in.astype(F32)
    sizes = (S5_WIDTH, MLSTM_WIDTH, MLSTM_WIDTH, MLSTM_WIDTH, MLSTM_WIDTH,
             MLSTM_HEADS, MLSTM_HEADS, D_MODEL, D_MODEL)
    u, q, k, v, o_pre, i_pre, f_pre, g_s5, g_m = jnp.split(z, np.cumsum(sizes)[:-1].tolist(), axis=-1)

    y5, s5_re, s5_im = s5_mixer(u, s5_re0, s5_im0, lam_re, lam_im, log_dt, B_re, B_im, C_re, C_im, D_s5)
    g5 = jax.nn.gelu(y5)
    glu = g5 @ w_glu.astype(F32)
    br_s5 = glu[..., :D_MODEL] * jax.nn.sigmoid(glu[..., D_MODEL:])

    hm, C_new, n_new, m_new = mlstm_mixer(
        q.reshape(bsz, L, MLSTM_HEADS, MLSTM_DH), k.reshape(bsz, L, MLSTM_HEADS, MLSTM_DH),
        v.reshape(bsz, L, MLSTM_HEADS, MLSTM_DH), i_pre + b_i.astype(F32), f_pre + b_f.astype(F32),
        C0, n0, m0)
    hm = hm * lax.rsqrt(jnp.mean(hm * hm, axis=-1, keepdims=True) + EPS)
    hm = hm.reshape(bsz, L, MLSTM_WIDTH) * mnorm_g.astype(F32) * jax.nn.sigmoid(o_pre)
    br_m = hm @ w_mout.astype(F32)

    merged = jax.nn.sigmoid(g_s5) * br_s5 + jax.nn.sigmoid(g_m) * br_m
    x = x + gate1 * (merged @ w_out.astype(F32))

    h2 = rmsnorm(x, norm2_g) * (1.0 + scale2) + shift2
    x = x + gate2 * moe(h2, router_w, router_b, w_up, b_up, w_down, b_down)
    return x, s5_re, s5_im, C_new, n_new, m_new


def setup_inputs(seed: int = 0) -> dict:
    key = jax.random.key(seed)
    ks = iter(jax.random.split(key, 48))

    def nrm(shape, scale):
        return jax.random.normal(next(ks), shape, F32) * scale

    Dd = DEPTH
    inp = {}
    inp['x_prompt'] = nrm((BATCH, SEQ, D_MODEL), 1.0)
    inp['x_sample'] = nrm((DEC_BATCH, DEC_SEQ, D_MODEL), 1.0)
    inp['c_prompt'] = nrm((BATCH, D_MODEL), 1.0)
    inp['c_sample'] = nrm((DEC_BATCH, D_MODEL), 1.0)
    inp['state_s5_re'] = nrm((Dd, DEC_BATCH, S5_GROUPS, S5_STATE), 0.3)
    inp['state_s5_im'] = nrm((Dd, DEC_BATCH, S5_GROUPS, S5_STATE), 0.3)
    inp['state_mlstm_C'] = nrm((Dd, DEC_BATCH, MLSTM_HEADS, MLSTM_DH, MLSTM_DH), 0.05)
    inp['state_mlstm_n'] = nrm((Dd, DEC_BATCH, MLSTM_HEADS, MLSTM_DH), 0.3)
    inp['state_mlstm_m'] = jax.random.uniform(next(ks), (Dd, DEC_BATCH, MLSTM_HEADS), F32, 0.0, 4.0)
    inp['norm1_g'] = 1.0 + nrm((Dd, D_MODEL), 0.05)
    inp['norm2_g'] = 1.0 + nrm((Dd, D_MODEL), 0.05)
    inp['final_norm_g'] = 1.0 + nrm((D_MODEL,), 0.05)
    inp['w_ada'] = nrm((Dd, D_MODEL, 6 * D_MODEL), 0.5 * D_MODEL ** -0.5)
    inp['b_ada'] = nrm((Dd, 6 * D_MODEL), 0.02)
    inp['w_in'] = nrm((Dd, D_MODEL, IN_COLS), D_MODEL ** -0.5)
    inp['s5_lambda_re'] = -0.5 + nrm((Dd, S5_GROUPS, S5_STATE), 0.01)
    inp['s5_lambda_im'] = (jnp.pi * jnp.arange(S5_STATE, dtype=F32))[None, None, :] + nrm((Dd, S5_GROUPS, S5_STATE), 0.01)
    inp['s5_log_dt'] = jax.random.uniform(next(ks), (Dd, S5_GROUPS), F32, math.log(1e-3), math.log(1e-1))
    inp['s5_B_re'] = nrm((Dd, S5_GROUPS, S5_STATE, S5_GROUP), (2 * S5_GROUP) ** -0.5)
    inp['s5_B_im'] = nrm((Dd, S5_GROUPS, S5_STATE, S5_GROUP), (2 * S5_GROUP) ** -0.5)
    inp['s5_C_re'] = nrm((Dd, S5_GROUPS, S5_GROUP, S5_STATE), (2 * S5_STATE) ** -0.5)
    inp['s5_C_im'] = nrm((Dd, S5_GROUPS, S5_GROUP, S5_STATE), (2 * S5_STATE) ** -0.5)
    inp['s5_D'] = nrm((Dd, S5_GROUPS, S5_GROUP), 1.0)
    inp['s5_w_glu'] = nrm((Dd, S5_WIDTH, 2 * D_MODEL), S5_WIDTH ** -0.5)
    inp['mlstm_b_i'] = nrm((Dd, MLSTM_HEADS), 0.1)
    inp['mlstm_b_f'] = jnp.linspace(3.0, 6.0, MLSTM_HEADS, dtype=F32)[None, :] + nrm((Dd, MLSTM_HEADS), 0.1)
    inp['mlstm_norm_g'] = 1.0 + nrm((Dd, MLSTM_WIDTH), 0.05)
    inp['mlstm_w_out'] = nrm((Dd, MLSTM_WIDTH, D_MODEL), MLSTM_WIDTH ** -0.5)
    inp['w_out'] = nrm((Dd, D_MODEL, D_MODEL), D_MODEL ** -0.5)
    inp['router_w'] = nrm((Dd, D_MODEL, N_EXPERTS), D_MODEL ** -0.5)
    inp['router_b'] = nrm((Dd, N_EXPERTS), 0.01)
    inp['expert_w_up'] = nrm((Dd, N_EXPERTS, D_MODEL, 2 * D_FF), D_MODEL ** -0.5)
    inp['expert_b_up'] = nrm((Dd, N_EXPERTS, 2 * D_FF), 0.02)
    inp['expert_w_down'] = nrm((Dd, N_EXPERTS, D_FF, D_MODEL), D_FF ** -0.5)
    inp['expert_b_down'] = nrm((Dd, N_EXPERTS, D_MODEL), 0.02)
    return inp


def reference(x_prompt, x_sample, c_prompt, c_sample, state_s5_re, state_s5_im, state_mlstm_C,
              state_mlstm_n, state_mlstm_m, norm1_g, norm2_g, final_norm_g, w_ada, b_ada, w_in,
              s5_lambda_re, s5_lambda_im, s5_log_dt, s5_B_re, s5_B_im, s5_C_re, s5_C_im, s5_D, s5_w_glu,
              mlstm_b_i, mlstm_b_f, mlstm_norm_g, mlstm_w_out, w_out, router_w, router_b,
              expert_w_up, expert_b_up, expert_w_down, expert_b_down):
    bp = x_prompt.shape[0]
    xp, xs = x_prompt, x_sample
    p_re, p_im, p_C, p_n, p_m = [], [], [], [], []
    s_re, s_im, s_C, s_n, s_m = [], [], [], [], []
    for l in range(DEPTH):
        w = (norm1_g[l], norm2_g[l], w_ada[l], b_ada[l], w_in[l],
             s5_lambda_re[l], s5_lambda_im[l], s5_log_dt[l], s5_B_re[l], s5_B_im[l], s5_C_re[l], s5_C_im[l],
             s5_D[l], s5_w_glu[l], mlstm_b_i[l], mlstm_b_f[l], mlstm_norm_g[l], mlstm_w_out[l], w_out[l],
             router_w[l], router_b[l], expert_w_up[l], expert_b_up[l], expert_w_down[l], expert_b_down[l])
        z_s5 = jnp.zeros((bp, S5_GROUPS, S5_STATE), F32)
        xp, a, b, cc, d, e = block(xp, c_prompt, z_s5, z_s5,
                                   jnp.zeros((bp, MLSTM_HEADS, MLSTM_DH, MLSTM_DH), F32),
                                   jnp.zeros((bp, MLSTM_HEADS, MLSTM_DH), F32),
                                   jnp.zeros((bp, MLSTM_HEADS), F32), *w)
        p_re.append(a); p_im.append(b); p_C.append(cc); p_n.append(d); p_m.append(e)
        xs, a, b, cc, d, e = block(xs, c_sample, state_s5_re[l], state_s5_im[l], state_mlstm_C[l],
                                   state_mlstm_n[l], state_mlstm_m[l], *w)
        s_re.append(a); s_im.append(b); s_C.append(cc); s_n.append(d); s_m.append(e)
    y_prompt = rmsnorm(xp, final_norm_g).astype(x_prompt.dtype)
    y_sample = rmsnorm(xs, final_norm_g).astype(x_sample.dtype)
    return (y_prompt, y_sample,
            jnp.stack(p_re), jnp.stack(p_im), jnp.stack(p_C), jnp.stack(p_n), jnp.stack(p_m),
            jnp.stack(s_re), jnp.stack(s_im), jnp.stack(s_C), jnp.stack(s_n), jnp.stack(s_m))
```

```python
import functools
import math

import jax
import jax.numpy as jnp
from jax import lax
from jax.experimental import pallas as pl
from jax.experimental.pallas import tpu as pltpu

F32 = jnp.float32
BF16 = jnp.bfloat16

D_MODEL = 1024
SEQ = 2048
N_PROMPT = 8
N_SAMPLE = 128
S5_WIDTH = 512
S5_GROUP = 16
S5_GROUPS = 32
S5_STATE = 64
HEADS = 4
DH = 128
MWIDTH = HEADS * DH
N_EXPERTS = 32
TOP_K = 4
D_FF = 1024
SWIGLU_LIMIT = 7.0
SWIGLU_ALPHA = 1.702
EPS = 1e-6

LANES = 128
S5_CHUNK = 16
S5_SUPER = LANES // S5_GROUP
N_SUPER = S5_GROUPS // S5_SUPER
SUPER_STATE = 2 * S5_SUPER * S5_STATE
MCHUNK = 128
VMEM_LIMIT = 56 * 1024 * 1024


def _cparams(*sem):
    return pltpu.CompilerParams(dimension_semantics=sem, vmem_limit_bytes=VMEM_LIMIT)


def _dot(a, b):
    return jnp.dot(a, b, preferred_element_type=F32)


def _dot_dims(a, b, dims):
    return lax.dot_general(a, b, (dims, ((), ())), preferred_element_type=F32)


def _split_bf16(a):
    hi = a.astype(BF16)
    lo = (a - hi.astype(F32)).astype(BF16)
    return hi, lo


def _dot_hp(a, b, dims=((1,), (0,))):
    ah, al = _split_bf16(a)
    bh, bl = _split_bf16(b)
    return _dot_dims(ah, bh, dims) + (_dot_dims(al, bh, dims) + _dot_dims(ah, bl, dims))


def _dot_exact_rhs(a, b_exact, dims=((1,), (0,))):
    ah, al = _split_bf16(a)
    return _dot_dims(ah, b_exact, dims) + _dot_dims(al, b_exact, dims)


def _log_sigmoid(x):
    return -(jnp.maximum(-x, 0.0) + jnp.log1p(jnp.exp(-jnp.abs(x))))


def _sigmoid(x):
    return 1.0 / (1.0 + jnp.exp(-x))


def _gelu_tanh(x):
    c = math.sqrt(2.0 / math.pi)
    return 0.5 * x * (1.0 + jnp.tanh(c * (x + 0.044715 * (x * x * x))))


def _adaln_body(c_ref, w_ref, b_ref, o_ref):
    c = c_ref[...]
    s = c * _sigmoid(c)
    o_ref[...] = _dot_hp(s, w_ref[...]) + b_ref[...]


def _adaln(c_all, w_ada, b_ada):
    n = c_all.shape[0]
    tn = 1024
    return pl.pallas_call(
        _adaln_body,
        grid=(6 * D_MODEL // tn,),
        in_specs=[pl.BlockSpec((n, D_MODEL), lambda j: (0, 0)),
                  pl.BlockSpec((D_MODEL, tn), lambda j: (0, j)),
                  pl.BlockSpec((1, tn), lambda j: (0, j))],
        out_specs=pl.BlockSpec((n, tn), lambda j: (0, j)),
        out_shape=jax.ShapeDtypeStruct((n, 6 * D_MODEL), F32),
        compiler_params=_cparams("parallel"),
        name="adaln",
    )(c_all, w_ada, b_ada.reshape(1, -1))


def _inproj_body(x_ref, g1_ref, sc_ref, sh_ref, wu_ref, wq_ref, wg_ref, wif_ref, wift_ref, bif_ref, bift_ref,
                 u_ref, q_ref, gg_ref, if_ref, ift_ref):
    x = x_ref[...]
    ms = jnp.mean(x * x, axis=-1, keepdims=True)
    h = x * lax.rsqrt(ms + EPS) * g1_ref[...] * (1.0 + sc_ref[...]) + sh_ref[...]
    hb = h.astype(BF16)
    u_ref[...] = _dot(hb, wu_ref[...])
    q_ref[...] = _dot(hb, wq_ref[...]).astype(BF16)
    gg_ref[...] = _dot(hb, wg_ref[...]).astype(BF16)
    gc = _dot_hp(h, wif_ref[...]) + bif_ref[...]
    lane = lax.broadcasted_iota(jnp.int32, gc.shape, 1)
    if_ref[...] = jnp.where(lane < HEADS, gc, _log_sigmoid(gc))
    gr = _dot_hp(wift_ref[...], h, ((1,), (1,))) + bift_ref[...]
    sub = lax.broadcasted_iota(jnp.int32, gr.shape, 0)
    ift_ref[...] = jnp.where(sub < HEADS, gr, _log_sigmoid(gr))


def _inproj(x2d, mod3, rows_per_mod, tm, g1, wparts):
    t = x2d.shape[0]
    wu, wq, wg, wif, wift, bif, bift = wparts
    mrows = mod3.shape[1]

    def mod_spec(j):
        return pl.BlockSpec((None, mrows, D_MODEL), lambda i: ((i * tm) // rows_per_mod, 0, j))

    const = lambda shape: pl.BlockSpec(shape, lambda i: (0,) * len(shape))
    return pl.pallas_call(
        _inproj_body,
        grid=(t // tm,),
        in_specs=[pl.BlockSpec((tm, D_MODEL), lambda i: (i, 0)),
                  const((1, D_MODEL)), mod_spec(1), mod_spec(0),
                  const(wu.shape), const(wq.shape), const(wg.shape), const(wif.shape), const(wift.shape),
                  const(bif.shape), const(bift.shape)],
        out_specs=[pl.BlockSpec((tm, S5_WIDTH), lambda i: (i, 0)),
                   pl.BlockSpec((tm, 4 * MWIDTH), lambda i: (i, 0)),
                   pl.BlockSpec((tm, 2 * D_MODEL), lambda i: (i, 0)),
                   pl.BlockSpec((tm, LANES), lambda i: (i, 0)),
                   pl.BlockSpec((8, tm), lambda i: (0, i))],
        out_shape=[jax.ShapeDtypeStruct((t, S5_WIDTH), F32),
                   jax.ShapeDtypeStruct((t, 4 * MWIDTH), BF16),
                   jax.ShapeDtypeStruct((t, 2 * D_MODEL), BF16),
                   jax.ShapeDtypeStruct((t, LANES), F32),
                   jax.ShapeDtypeStruct((8, t), F32)],
        compiler_params=_cparams("parallel"),
        name="inproj",
    )(x2d, g1, mod3, mod3, wu, wq, wg, wif, wift, bif, bift)


def _s5_tables(lam_re, lam_im, log_dt, b_re, b_im, c_re, c_im, d_s5):
    hi = lax.Precision.HIGHEST
    dt = jnp.exp(log_dt.astype(F32))[:, None]
    lr, li = lam_re.astype(F32), lam_im.astype(F32)
    dpow = jnp.arange(S5_CHUNK + 1, dtype=F32)[:, None, None]
    mag = jnp.exp(dpow * (lr * dt))
    pw_re, pw_im = mag * jnp.cos(dpow * (li * dt)), mag * jnp.sin(dpow * (li * dt))
    ab_re, ab_im = pw_re[1], pw_im[1]
    den = lr * lr + li * li
    nr, ni = ab_re - 1.0, ab_im
    coef_re = (nr * lr + ni * li) / den
    coef_im = (ni * lr - nr * li) / den
    br, bi = b_re.astype(F32), b_im.astype(F32)
    bb_re = coef_re[..., None] * br - coef_im[..., None] * bi
    bb_im = coef_re[..., None] * bi + coef_im[..., None] * br
    cr, ci = c_re.astype(F32), c_im.astype(F32)
    cl_re = cr[None] * pw_re[:, :, None, :] - ci[None] * pw_im[:, :, None, :]
    cl_im = cr[None] * pw_im[:, :, None, :] + ci[None] * pw_re[:, :, None, :]
    kd = (jnp.einsum('dgop,gpi->dgio', cl_re[:S5_CHUNK], bb_re, precision=hi)
          - jnp.einsum('dgop,gpi->dgio', cl_im[:S5_CHUNK], bb_im, precision=hi))
    kd = kd.at[0].add(d_s5.astype(F32)[:, :, None] * jnp.eye(S5_GROUP, dtype=F32)[None])
    eye = jnp.eye(S5_SUPER, dtype=F32)
    kblk = jnp.einsum('dsaio,ab->dsaibo', kd.reshape(S5_CHUNK, N_SUPER, S5_SUPER, S5_GROUP, S5_GROUP), eye)
    kblk = kblk.reshape(S5_CHUNK, N_SUPER, LANES, LANES)
    kext = jnp.concatenate([jnp.zeros_like(kblk[:1]), kblk], axis=0)
    kpair = jnp.concatenate([kext[S5_CHUNK - 1::-1], kext[S5_CHUNK:0:-1]], axis=-1)
    kpair = kpair.transpose(1, 0, 2, 3).reshape(N_SUPER, S5_CHUNK * LANES, 2 * LANES)
    rp_re, rp_im = pw_re[S5_CHUNK - 1::-1], pw_im[S5_CHUNK - 1::-1]
    f_re = rp_re[..., None] * bb_re[None] - rp_im[..., None] * bb_im[None]
    f_im = rp_re[..., None] * bb_im[None] + rp_im[..., None] * bb_re[None]
    fcat = jnp.stack([f_re, f_im], axis=2).transpose(0, 1, 4, 2, 3)
    fr = fcat.reshape(S5_CHUNK, N_SUPER, S5_SUPER, S5_GROUP, 2, S5_STATE)
    fmat = jnp.einsum('isahcp,ab->siahcbp', fr, eye).reshape(N_SUPER, S5_CHUNK * LANES, SUPER_STATE)
    ecat = jnp.stack([cl_re[1:], -cl_im[1:]], axis=0)
    er = ecat.transpose(2, 0, 4, 1, 3).reshape(N_SUPER, S5_SUPER, 2, S5_STATE, S5_CHUNK, S5_GROUP)
    emat = jnp.einsum('sacpjo,ab->scapjbo', er, eye).reshape(N_SUPER, SUPER_STATE, S5_CHUNK * LANES)

    def lay(re, im):
        z = jnp.stack([re, im], axis=0).reshape(2, N_SUPER, S5_SUPER * S5_STATE)
        return z.transpose(1, 0, 2).reshape(N_SUPER, 1, SUPER_STATE)

    a16 = lay(pw_re[S5_CHUNK], pw_im[S5_CHUNK])
    a1 = lay(ab_re, ab_im)
    k0 = kblk[0].astype(BF16)
    e0 = emat[:, :, :LANES].astype(BF16)
    f1 = fmat[:, (S5_CHUNK - 1) * LANES:, :].astype(BF16)
    return kpair.astype(BF16), fmat.astype(BF16), emat.astype(BF16), a16, a1, k0, e0, f1


def _s5_prompt_body(u_ref, kp_ref, f_ref, e_ref, a_ref, y_ref, xe_ref, ucat, v_sc, xp_sc):
    n = SEQ // S5_CHUNK
    half = SUPER_STATE // 2
    for j in range(S5_CHUNK):
        ucat[:, j * LANES:(j + 1) * LANES] = u_ref[pl.ds(j, n, stride=S5_CHUNK), :].astype(BF16)
    v_sc[...] = _dot(ucat[...], f_ref[...])
    a_r, a_i = a_ref[:, :half], a_ref[:, half:]
    xr = jnp.zeros((1, half), F32)
    xi = jnp.zeros((1, half), F32)
    for c in range(n):
        xp_sc[c:c + 1, :half] = xr
        xp_sc[c:c + 1, half:] = xi
        vr, vi = v_sc[c:c + 1, :half], v_sc[c:c + 1, half:]
        xr, xi = a_r * xr - a_i * xi + vr, a_r * xi + a_i * xr + vi
    xe_ref[:, :half] = xr
    xe_ref[:, half:] = xi
    ye = _dot(xp_sc[...].astype(BF16), e_ref[...])
    for m in range(S5_CHUNK // 2):
        rows = (2 * m + 2) * LANES
        yp = _dot(ucat[:, :rows], kp_ref[(S5_CHUNK - 2 - 2 * m) * LANES:, :]) + ye[:, 2 * m * LANES:(2 * m + 2) * LANES]
        y_ref[pl.ds(2 * m, n, stride=S5_CHUNK), :] = yp[:, :LANES]
        y_ref[pl.ds(2 * m + 1, n, stride=S5_CHUNK), :] = yp[:, LANES:]


def _s5_prompt(u, kpair, fmat, emat, a16):
    n = SEQ // S5_CHUNK
    return pl.pallas_call(
        _s5_prompt_body,
        grid=(N_SUPER, N_PROMPT),
        in_specs=[pl.BlockSpec((SEQ, LANES), lambda s, b: (b, s)),
                  pl.BlockSpec((None,) + kpair.shape[1:], lambda s, b: (s, 0, 0)),
                  pl.BlockSpec((None,) + fmat.shape[1:], lambda s, b: (s, 0, 0)),
                  pl.BlockSpec((None,) + emat.shape[1:], lambda s, b: (s, 0, 0)),
                  pl.BlockSpec((None, 1, SUPER_STATE), lambda s, b: (s, 0, 0))],
        out_specs=[pl.BlockSpec((SEQ, LANES), lambda s, b: (b, s)),
                   pl.BlockSpec((None, None, 1, SUPER_STATE), lambda s, b: (s, b, 0, 0))],
        out_shape=[jax.ShapeDtypeStruct(u.shape, F32),
                   jax.ShapeDtypeStruct((N_SUPER, N_PROMPT, 1, SUPER_STATE), F32)],
        scratch_shapes=[pltpu.VMEM((n, S5_CHUNK * LANES), BF16),
                        pltpu.VMEM((n, SUPER_STATE), F32),
                        pltpu.VMEM((n, SUPER_STATE), F32)],
        compiler_params=_cparams("parallel", "parallel"),
        name="s5_prompt",
    )(u, kpair, fmat, emat, a16)


def _s5_step_body(u_ref, x0_ref, k0_ref, e0_ref, f1_ref, a_ref, y_ref, xn_ref):
    half = SUPER_STATE // 2
    ub = u_ref[...].astype(BF16)
    x0 = x0_ref[...]
    bu = _dot(ub, f1_ref[...])
    a_r, a_i = a_ref[:, :half], a_ref[:, half:]
    x0r, x0i = x0[:, :half], x0[:, half:]
    xn_ref[:, :half] = a_r * x0r - a_i * x0i + bu[:, :half]
    xn_ref[:, half:] = a_r * x0i + a_i * x0r + bu[:, half:]
    y_ref[...] = _dot(x0.astype(BF16), e0_ref[...]) + _dot(ub, k0_ref[...])


def _s5_step(u, x0, k0, e0, f1, a1):
    t = u.shape[0]
    per = lambda a: pl.BlockSpec((None,) + a.shape[1:], lambda s: (s,) + (0,) * (a.ndim - 1))
    return pl.pallas_call(
        _s5_step_body,
        grid=(N_SUPER,),
        in_specs=[pl.BlockSpec((t, LANES), lambda s: (0, s)), per(x0), per(k0), per(e0), per(f1), per(a1)],
        out_specs=[pl.BlockSpec((t, LANES), lambda s: (0, s)), per(x0)],
        out_shape=[jax.ShapeDtypeStruct(u.shape, F32), jax.ShapeDtypeStruct(x0.shape, F32)],
        compiler_params=_cparams("parallel"),
        name="s5_step",
    )(u, x0, k0, e0, f1, a1)


def _mlstm_prompt_body(q_ref, gc_ref, gr_ref, mg_ref, hm_ref, c_out, n_out, m_out, c_sc, n_sc, m_sc):
    lc = MCHUNK
    ci = pl.program_id(1)

    @pl.when(ci == 0)
    def _():
        c_sc[...] = jnp.zeros_like(c_sc)
        n_sc[...] = jnp.zeros_like(n_sc)
        m_sc[...] = jnp.zeros_like(m_sc)

    scale = DH ** -0.5
    row = lax.broadcasted_iota(jnp.int32, (lc, lc), 0)
    col = lax.broadcasted_iota(jnp.int32, (lc, lc), 1)
    causal = col <= row
    tri = causal.astype(BF16)
    ones = jnp.ones((lc, DH), BF16)
    gc = gc_ref[...]
    gr = gr_ref[...]
    gc_hi, gc_lo = _split_bf16(gc)
    bcol_all = _dot(tri, gc_hi) + _dot(tri, gc_lo)
    brow_all = _dot_exact_rhs(gr, tri, ((1,), (1,)))
    for hd in range(HEADS):
        q = q_ref[:, hd * DH:(hd + 1) * DH]
        k = q_ref[:, MWIDTH + hd * DH:MWIDTH + (hd + 1) * DH]
        v = q_ref[:, 2 * MWIDTH + hd * DH:2 * MWIDTH + (hd + 1) * DH]
        o = q_ref[:, 3 * MWIDTH + hd * DH:3 * MWIDTH + (hd + 1) * DH]
        i_col = gc[:, hd:hd + 1]
        b_col = bcol_all[:, HEADS + hd:HEADS + hd + 1]
        i_row = gr[hd:hd + 1, :]
        b_row = brow_all[HEADS + hd:HEADS + hd + 1, :]
        m_prev = m_sc[hd:hd + 1, :][:, :1]
        dm = jnp.where(causal, b_col - b_row + i_row, -jnp.inf)
        m_inter = b_col + m_prev
        m_t = jnp.maximum(m_inter, jnp.max(dm, axis=-1, keepdims=True))
        w_inter = jnp.exp(m_inter - m_t)
        w = jnp.exp(dm - m_t)
        s = _dot_dims(q, k, ((1,), (1,))) * scale * w
        sb = s.astype(BF16)
        c_prev = c_sc[hd]
        n_prev = n_sc[hd]
        num = w_inter * _dot(q, c_prev.astype(BF16)) + _dot(sb, v)
        den = w_inter * _dot(q, n_prev.astype(BF16)) + _dot(sb, ones)
        h = num / jnp.maximum(jnp.abs(den), jnp.exp(-m_t))
        hn = h * lax.rsqrt(jnp.mean(h * h, axis=-1, keepdims=True) + EPS)
        hm_ref[:, hd * DH:(hd + 1) * DH] = (hn * mg_ref[:, hd * DH:(hd + 1) * DH] * _sigmoid(o.astype(F32))).astype(BF16)
        m_last = m_t[lc - 1:lc, :]
        b_last = b_col[lc - 1:lc, :]
        decay = w_inter[lc - 1:lc, :]
        w_last = jnp.exp(b_last - b_col + i_col - m_last) * scale
        kw = (k.astype(F32) * w_last).astype(BF16)
        c_sc[hd] = decay * c_prev + _dot_dims(kw, v, ((0,), (0,)))
        n_sc[hd] = decay * n_prev + _dot_dims(kw, ones, ((0,), (0,)))
        m_sc[hd:hd + 1, :] = jnp.broadcast_to(m_last, (1, LANES))

    @pl.when(ci == pl.num_programs(1) - 1)
    def _():
        c_out[...] = c_sc[...]
        n_out[...] = n_sc[...]
        m_out[...] = m_sc[...]


def _mlstm_prompt(qkvo, gcol, grow, mnorm_g):
    nc = SEQ // MCHUNK
    return pl.pallas_call(
        _mlstm_prompt_body,
        grid=(N_PROMPT, nc),
        in_specs=[pl.BlockSpec((MCHUNK, 4 * MWIDTH), lambda b, c: (b * nc + c, 0)),
                  pl.BlockSpec((MCHUNK, LANES), lambda b, c: (b * nc + c, 0)),
                  pl.BlockSpec((8, MCHUNK), lambda b, c: (0, b * nc + c)),
                  pl.BlockSpec((1, MWIDTH), lambda b, c: (0, 0))],
        out_specs=[pl.BlockSpec((MCHUNK, MWIDTH), lambda b, c: (b * nc + c, 0)),
                   pl.BlockSpec((None, HEADS, DH, DH), lambda b, c: (b, 0, 0, 0)),
                   pl.BlockSpec((None, HEADS, DH, DH), lambda b, c: (b, 0, 0, 0)),
                   pl.BlockSpec((None, 8, LANES), lambda b, c: (b, 0, 0))],
        out_shape=[jax.ShapeDtypeStruct((N_PROMPT * SEQ, MWIDTH), BF16),
                   jax.ShapeDtypeStruct((N_PROMPT, HEADS, DH, DH), F32),
                   jax.ShapeDtypeStruct((N_PROMPT, HEADS, DH, DH), F32),
                   jax.ShapeDtypeStruct((N_PROMPT, 8, LANES), F32)],
        scratch_shapes=[pltpu.VMEM((HEADS, DH, DH), F32),
                        pltpu.VMEM((HEADS, DH, DH), F32),
                        pltpu.VMEM((8, LANES), F32)],
        compiler_params=_cparams("parallel", "arbitrary"),
        name="mlstm_prompt",
    )(qkvo, gcol, grow, mnorm_g)


MSTEP_TOK = 8


def _mlstm_step_body(q_ref, qkt_ref, gc_ref, m0_ref, c0_ref, n0_ref, mg_ref, hm_ref, c_out, n_out, m_out):
    scale = DH ** -0.5
    for j in range(MSTEP_TOK):
        for hd in range(HEADS):
            sl = slice(hd * DH, (hd + 1) * DH)
            q_row = q_ref[j:j + 1, sl].astype(F32)
            k_row = q_ref[j:j + 1, MWIDTH + hd * DH:MWIDTH + (hd + 1) * DH].astype(F32)
            v_row = q_ref[j:j + 1, 2 * MWIDTH + hd * DH:2 * MWIDTH + (hd + 1) * DH].astype(F32)
            o_row = q_ref[j:j + 1, 3 * MWIDTH + hd * DH:3 * MWIDTH + (hd + 1) * DH].astype(F32)
            q_col = qkt_ref[hd, :, j:j + 1]
            k_col = qkt_ref[HEADS + hd, :, j:j + 1]
            i_g = gc_ref[j:j + 1, hd:hd + 1]
            lf = gc_ref[j:j + 1, HEADS + hd:HEADS + hd + 1]
            m0 = m0_ref[j:j + 1, hd:hd + 1]
            c0 = c0_ref[j, hd]
            n0 = n0_ref[j:j + 1, sl]
            m_inter = lf + m0
            m_t = jnp.maximum(m_inter, i_g)
            w_inter = jnp.exp(m_inter - m_t)
            w = jnp.exp(i_g - m_t)
            s = jnp.sum(q_row * k_row, axis=-1, keepdims=True) * scale * w
            num = w_inter * jnp.sum(q_col * c0, axis=0, keepdims=True) + s * v_row
            den = w_inter * jnp.sum(q_row * n0, axis=-1, keepdims=True) + s
            h = num / jnp.maximum(jnp.abs(den), jnp.exp(-m_t))
            hn = h * lax.rsqrt(jnp.mean(h * h, axis=-1, keepdims=True) + EPS)
            hm_ref[j:j + 1, sl] = hn * mg_ref[:, sl] * _sigmoid(o_row)
            c_out[j, hd] = w_inter * c0 + (w * scale) * (k_col * v_row)
            n_out[j:j + 1, sl] = w_inter * n0 + (w * scale) * k_row
            m_out[j:j + 1, hd:hd + 1] = m_t


def _mlstm_step(qkvo, qkt, gcol, m0, c0, n0, mnorm_g):
    t = qkvo.shape[0]
    tk = MSTEP_TOK
    return pl.pallas_call(
        _mlstm_step_body,
        grid=(t // tk,),
        in_specs=[pl.BlockSpec((tk, 4 * MWIDTH), lambda i: (i, 0)),
                  pl.BlockSpec((None, 2 * HEADS, DH, tk), lambda i: (i, 0, 0, 0)),
                  pl.BlockSpec((tk, LANES), lambda i: (i, 0)),
                  pl.BlockSpec((tk, LANES), lambda i: (i, 0)),
                  pl.BlockSpec((tk, HEADS, DH, DH), lambda i: (i, 0, 0, 0)),
                  pl.BlockSpec((tk, MWIDTH), lambda i: (i, 0)),
                  pl.BlockSpec((1, MWIDTH), lambda i: (0, 0))],
        out_specs=[pl.BlockSpec((tk, MWIDTH), lambda i: (i, 0)),
                   pl.BlockSpec((tk, HEADS, DH, DH), lambda i: (i, 0, 0, 0)),
                   pl.BlockSpec((tk, MWIDTH), lambda i: (i, 0)),
                   pl.BlockSpec((tk, LANES), lambda i: (i, 0))],
        out_shape=[jax.ShapeDtypeStruct((t, MWIDTH), F32),
                   jax.ShapeDtypeStruct((t, HEADS, DH, DH), F32),
                   jax.ShapeDtypeStruct((t, MWIDTH), F32),
                   jax.ShapeDtypeStruct((t, LANES), F32)],
        compiler_params=_cparams("parallel"),
        name="mlstm_step",
    )(qkvo, qkt, gcol, m0, c0, n0, mnorm_g)


def _post_body(x_ref, y5_ref, hm_ref, gg_ref, g1_ref, sc_ref, sh_ref, n2_ref, wglu_ref, wmo_ref, wo_ref, rw_ref, rb_ref,
               x1_ref, h2_ref, gates_ref):
    g5 = _gelu_tanh(y5_ref[...]).astype(BF16)
    glu = _dot(g5, wglu_ref[...])
    br_s5 = glu[:, :D_MODEL] * _sigmoid(glu[:, D_MODEL:])
    br_m = _dot(hm_ref[...].astype(BF16), wmo_ref[...])
    gg = gg_ref[...].astype(F32)
    merged = _sigmoid(gg[:, :D_MODEL]) * br_s5 + _sigmoid(gg[:, D_MODEL:]) * br_m
    x1 = x_ref[...] + g1_ref[...] * _dot(merged.astype(BF16), wo_ref[...])
    x1_ref[...] = x1
    ms = jnp.mean(x1 * x1, axis=-1, keepdims=True)
    h2 = x1 * lax.rsqrt(ms + EPS) * n2_ref[...] * (1.0 + sc_ref[...]) + sh_ref[...]
    h2_ref[...] = h2.astype(BF16)
    logits = _dot_hp(h2, rw_ref[...]) + rb_ref[...]
    lane = lax.broadcasted_iota(jnp.int32, logits.shape, 1)
    l = jnp.where(lane < N_EXPERTS, logits, -jnp.inf)
    vals, hots = [], []
    for _ in range(TOP_K):
        mx = jnp.max(l, axis=-1, keepdims=True)
        idx = jnp.min(jnp.where(l == mx, lane, LANES), axis=-1, keepdims=True)
        hot = lane == idx
        vals.append(mx)
        hots.append(hot)
        l = jnp.where(hot, -jnp.inf, l)
    ex = [jnp.exp(vk - vals[0]) for vk in vals]
    tot = ex[0] + ex[1] + ex[2] + ex[3]
    gates = jnp.zeros(logits.shape, F32)
    for k in range(TOP_K):
        gates = gates + jnp.where(hots[k], ex[k] / tot, 0.0)
    gates_ref[...] = gates


def _post(x2d, y5, hm, gg, mod3, rows_per_mod, tm, n2, wglu, wmo, wo, rw, rb):
    t = x2d.shape[0]
    mrows = mod3.shape[1]

    def mod_spec(j):
        return pl.BlockSpec((None, mrows, D_MODEL), lambda i: ((i * tm) // rows_per_mod, 0, j))

    const = lambda shape: pl.BlockSpec(shape, lambda i: (0,) * len(shape))
    tile = lambda w: pl.BlockSpec((tm, w), lambda i: (i, 0))
    return pl.pallas_call(
        _post_body,
        grid=(t // tm,),
        in_specs=[tile(D_MODEL), tile(S5_WIDTH), tile(MWIDTH), tile(2 * D_MODEL),
                  mod_spec(2), mod_spec(4), mod_spec(3), const((1, D_MODEL)),
                  const(wglu.shape), const(wmo.shape), const(wo.shape), const(rw.shape), const(rb.shape)],
        out_specs=[tile(D_MODEL), tile(D_MODEL), tile(LANES)],
        out_shape=[jax.ShapeDtypeStruct((t, D_MODEL), F32),
                   jax.ShapeDtypeStruct((t, D_MODEL), BF16),
                   jax.ShapeDtypeStruct((t, LANES), F32)],
        compiler_params=_cparams("parallel"),
        name="post",
    )(x2d, y5, hm, gg, mod3, mod3, mod3, n2, wglu, wmo, wo, rw, rb)


def _moe_body(h2_ref, gates_ref, x1_ref, g2_ref, fg_ref, wup_ref, bup_ref, wdn_ref, bdn_ref, y_ref, acc):
    e = pl.program_id(1)

    @pl.when(e == 0)
    def _():
        acc[...] = jnp.zeros_like(acc)

    gu = _dot(h2_ref[...], wup_ref[...]) + bup_ref[...]
    g = jnp.minimum(gu[:, :D_FF], SWIGLU_LIMIT)
    up = jnp.clip(gu[:, D_FF:], -SWIGLU_LIMIT, SWIGLU_LIMIT)
    act = (up + 1.0) * g * _sigmoid(SWIGLU_ALPHA * g)
    o = _dot(act.astype(BF16), wdn_ref[...]) + bdn_ref[...]
    gates = gates_ref[...]
    lane = lax.broadcasted_iota(jnp.int32, gates.shape, 1)
    ge = jnp.sum(jnp.where(lane == e, gates, 0.0), axis=-1, keepdims=True)
    acc[...] += ge * o

    @pl.when(e == N_EXPERTS - 1)
    def _():
        xo = x1_ref[...] + g2_ref[...] * acc[...]
        ms = jnp.mean(xo * xo, axis=-1, keepdims=True)
        y_ref[...] = xo * lax.rsqrt(ms + EPS) * fg_ref[...]


def _moe(h2, gates, x1, mod3, rows_per_mod, tm, fg, wup, bup, wdn, bdn):
    t = h2.shape[0]
    mrows = mod3.shape[1]
    tile = lambda w: pl.BlockSpec((tm, w), lambda i, e: (i, 0))
    return pl.pallas_call(
        _moe_body,
        grid=(t // tm, N_EXPERTS),
        in_specs=[tile(D_MODEL), tile(LANES), tile(D_MODEL),
                  pl.BlockSpec((None, mrows, D_MODEL), lambda i, e: ((i * tm) // rows_per_mod, 0, 5)),
                  pl.BlockSpec((1, D_MODEL), lambda i, e: (0, 0)),
                  pl.BlockSpec((None, D_MODEL, 2 * D_FF), lambda i, e: (e, 0, 0)),
                  pl.BlockSpec((None, 1, 2 * D_FF), lambda i, e: (e, 0, 0)),
                  pl.BlockSpec((None, D_FF, D_MODEL), lambda i, e: (e, 0, 0)),
                  pl.BlockSpec((None, 1, D_MODEL), lambda i, e: (e, 0, 0))],
        out_specs=tile(D_MODEL),
        out_shape=jax.ShapeDtypeStruct((t, D_MODEL), F32),
        scratch_shapes=[pltpu.VMEM((tm, D_MODEL), F32)],
        compiler_params=_cparams("parallel", "arbitrary"),
        name="moe",
    )(h2, gates, x1, mod3, fg, wup, bup, wdn, bdn)


def _unpack_s5_state(x):
    n = x.shape[1]
    z = x.reshape(N_SUPER, n, 2, S5_SUPER, S5_STATE).transpose(2, 1, 0, 3, 4).reshape(2, n, S5_GROUPS, S5_STATE)
    return z[0], z[1]


def _pack_s5_state(re, im):
    n = re.shape[0]
    z = jnp.stack([re, im], axis=0).reshape(2, n, N_SUPER, S5_SUPER, S5_STATE)
    return z.transpose(2, 1, 0, 3, 4).reshape(N_SUPER, n, SUPER_STATE)


def kernel(x_prompt, x_sample, c_prompt, c_sample, state_s5_re, state_s5_im, state_mlstm_C, state_mlstm_n, state_mlstm_m, norm1_g, norm2_g, final_norm_g, w_ada, b_ada, w_in, s5_lambda_re, s5_lambda_im, s5_log_dt, s5_B_re, s5_B_im, s5_C_re, s5_C_im, s5_D, s5_w_glu, mlstm_b_i, mlstm_b_f, mlstm_norm_g, mlstm_w_out, w_out, router_w, router_b, expert_w_up, expert_b_up, expert_w_down, expert_b_down):
    assert w_in.shape[0] == 1, "single layer"
    tp = N_PROMPT * SEQ
    xp = x_prompt.reshape(tp, D_MODEL).astype(F32)
    xs = x_sample.reshape(N_SAMPLE, D_MODEL).astype(F32)

    w = w_in[0]
    c0 = S5_WIDTH
    c1 = c0 + 4 * MWIDTH
    c2 = c1 + 2 * HEADS
    wu = w[:, :c0].astype(BF16)
    wq = w[:, c0:c1].astype(BF16)
    wif = jnp.pad(w[:, c1:c2].astype(F32), ((0, 0), (0, LANES - 2 * HEADS)))
    wift = w[:, c1:c2].astype(F32).T
    wg = w[:, c2:].astype(BF16)
    bvec = jnp.concatenate([mlstm_b_i[0], mlstm_b_f[0]]).astype(F32)
    bif = jnp.pad(bvec, (0, LANES - 2 * HEADS)).reshape(1, LANES)
    bift = bvec.reshape(2 * HEADS, 1)
    wparts = (wu, wq, wg, wif, wift, bif, bift)
    g1 = norm1_g[0].reshape(1, D_MODEL).astype(F32)
    n2 = norm2_g[0].reshape(1, D_MODEL).astype(F32)
    fg = final_norm_g.reshape(1, D_MODEL).astype(F32)
    mg = mlstm_norm_g[0].reshape(1, MWIDTH).astype(F32)
    wglu = s5_w_glu[0].astype(BF16)
    wmo = mlstm_w_out[0].astype(BF16)
    wo = w_out[0].astype(BF16)
    rw = jnp.pad(router_w[0].astype(F32), ((0, 0), (0, LANES - N_EXPERTS)))
    rb = jnp.pad(router_b[0].astype(F32), (0, LANES - N_EXPERTS)).reshape(1, LANES)
    wup = expert_w_up[0].astype(BF16)
    wdn = expert_w_down[0].astype(BF16)
    bup = expert_b_up[0].astype(F32).reshape(N_EXPERTS, 1, 2 * D_FF)
    bdn = expert_b_down[0].astype(F32).reshape(N_EXPERTS, 1, D_MODEL)
    kpair, fmat, emat, a16, a1, k0, e0, f1 = _s5_tables(
        s5_lambda_re[0], s5_lambda_im[0], s5_log_dt[0], s5_B_re[0], s5_B_im[0], s5_C_re[0], s5_C_im[0], s5_D[0])

    c_all = jnp.concatenate([c_prompt, c_sample], axis=0).astype(F32)
    mod = _adaln(c_all, w_ada[0].astype(F32), b_ada[0].astype(F32))
    mod_p = mod[:N_PROMPT].reshape(N_PROMPT, 1, 6 * D_MODEL)
    mod_s = mod[N_PROMPT:].reshape(1, N_SAMPLE, 6 * D_MODEL)

    u, qkvo, gg, gcol, grow = _inproj(xp, mod_p, SEQ, 512, g1, wparts)
    y5, xend = _s5_prompt(u, kpair, fmat, emat, a16)
    hm, c_p, n_p, m_p = _mlstm_prompt(qkvo, gcol, grow, mg)
    x1, h2, gates = _post(xp, y5, hm, gg, mod_p, SEQ, 512, n2, wglu, wmo, wo, rw, rb)
    y_p = _moe(h2, gates, x1, mod_p, SEQ, 1024, fg, wup, bup, wdn, bdn)
    p_re, p_im = _unpack_s5_state(xend.reshape(N_SUPER, N_PROMPT, SUPER_STATE))
    n_p = n_p[..., 0]
    m_p = m_p[:, :HEADS, 0]

    us, qs, ggs, gcs, _ = _inproj(xs, mod_s, N_SAMPLE, N_SAMPLE, g1, wparts)
    x0 = _pack_s5_state(state_s5_re[0].astype(F32), state_s5_im[0].astype(F32))
    y5s, xns = _s5_step(us, x0, k0, e0, f1, a1)
    s_re, s_im = _unpack_s5_state(xns)
    qkt = qs[:, :2 * MWIDTH].astype(F32).reshape(N_SAMPLE // MSTEP_TOK, MSTEP_TOK, 2 * HEADS, DH).transpose(0, 2, 3, 1)
    m0 = jnp.pad(state_mlstm_m[0].astype(F32), ((0, 0), (0, LANES - HEADS)))
    hms, c_s, n_s, m_s = _mlstm_step(qs, qkt, gcs, m0, state_mlstm_C[0].astype(F32),
                                     state_mlstm_n[0].astype(F32).reshape(N_SAMPLE, MWIDTH), mg)
    x1s, h2s, gates_s = _post(xs, y5s, hms, ggs, mod_s, N_SAMPLE, N_SAMPLE, n2, wglu, wmo, wo, rw, rb)
    y_s = _moe(h2s, gates_s, x1s, mod_s, N_SAMPLE, N_SAMPLE, fg, wup, bup, wdn, bdn)

    return (y_p.reshape(N_PROMPT, SEQ, D_MODEL).astype(x_prompt.dtype),
            y_s.reshape(N_SAMPLE, 1, D_MODEL).astype(x_sample.dtype),
            p_re[None], p_im[None], c_p[None], n_p[None], m_p[None],
            s_re[None], s_im[None], c_s[None],
            n_s.reshape(1, N_SAMPLE, HEADS, DH), m_s[:, :HEADS][None])
```

```python
import functools
import math

import jax
import jax.numpy as jnp
from jax import lax
from jax.experimental import pallas as pl
from jax.experimental.pallas import tpu as pltpu

F32 = jnp.float32
BF16 = jnp.bfloat16

D_MODEL = 1024
SEQ = 2048
N_PROMPT = 8
N_SAMPLE = 128
S5_WIDTH = 512
S5_GROUP = 16
S5_GROUPS = 32
S5_STATE = 64
HEADS = 4
DH = 128
MWIDTH = HEADS * DH
N_EXPERTS = 32
TOP_K = 4
D_FF = 1024
SWIGLU_LIMIT = 7.0
SWIGLU_ALPHA = 1.702
EPS = 1e-6

LANES = 128
S5_CHUNK = 16
S5_SUPER = LANES // S5_GROUP
N_SUPER = S5_GROUPS // S5_SUPER
SUPER_STATE = 2 * S5_SUPER * S5_STATE
MCHUNK = 128
VMEM_LIMIT = 56 * 1024 * 1024


def _cparams(*sem):
    return pltpu.CompilerParams(dimension_semantics=sem, vmem_limit_bytes=VMEM_LIMIT)


def _dot(a, b):
    return jnp.dot(a, b, preferred_element_type=F32)


def _dot_dims(a, b, dims):
    return lax.dot_general(a, b, (dims, ((), ())), preferred_element_type=F32)


def _split_bf16(a):
    hi = a.astype(BF16)
    lo = (a - hi.astype(F32)).astype(BF16)
    return hi, lo


def _dot_hp(a, b, dims=((1,), (0,))):
    ah, al = _split_bf16(a)
    bh, bl = _split_bf16(b)
    return _dot_dims(ah, bh, dims) + (_dot_dims(al, bh, dims) + _dot_dims(ah, bl, dims))


def _dot_exact_rhs(a, b_exact, dims=((1,), (0,))):
    ah, al = _split_bf16(a)
    return _dot_dims(ah, b_exact, dims) + _dot_dims(al, b_exact, dims)


def _log_sigmoid(x):
    return -(jnp.maximum(-x, 0.0) + jnp.log1p(jnp.exp(-jnp.abs(x))))


def _sigmoid(x):
    return 1.0 / (1.0 + jnp.exp(-x))


def _gelu_tanh(x):
    c = math.sqrt(2.0 / math.pi)
    return 0.5 * x * (1.0 + jnp.tanh(c * (x + 0.044715 * (x * x * x))))


def _adaln_body(c_ref, w_ref, b_ref, o_ref):
    c = c_ref[...]
    s = c * _sigmoid(c)
    o_ref[...] = _dot_hp(s, w_ref[...]) + b_ref[...]


def _adaln(c_all, w_ada, b_ada):
    n = c_all.shape[0]
    tn = 1024
    return pl.pallas_call(
        _adaln_body,
        grid=(6 * D_MODEL // tn,),
        in_specs=[pl.BlockSpec((n, D_MODEL), lambda j: (0, 0)),
                  pl.BlockSpec((D_MODEL, tn), lambda j: (0, j)),
                  pl.BlockSpec((1, tn), lambda j: (0, j))],
        out_specs=pl.BlockSpec((n, tn), lambda j: (0, j)),
        out_shape=jax.ShapeDtypeStruct((n, 6 * D_MODEL), F32),
        compiler_params=_cparams("parallel"),
        name="adaln",
    )(c_all, w_ada, b_ada.reshape(1, -1))


def _inproj_body(x_ref, g1_ref, sc_ref, sh_ref, wu_ref, wq_ref, wg_ref, wif_ref, wift_ref, bif_ref, bift_ref,
                 u_ref, q_ref, gg_ref, if_ref, ift_ref):
    x = x_ref[...]
    ms = jnp.mean(x * x, axis=-1, keepdims=True)
    h = x * lax.rsqrt(ms + EPS) * g1_ref[...] * (1.0 + sc_ref[...]) + sh_ref[...]
    hb = h.astype(BF16)
    u_ref[...] = _dot(hb, wu_ref[...])
    q_ref[...] = _dot(hb, wq_ref[...]).astype(BF16)
    gg_ref[...] = _dot(hb, wg_ref[...]).astype(BF16)
    gc = _dot_hp(h, wif_ref[...]) + bif_ref[...]
    lane = lax.broadcasted_iota(jnp.int32, gc.shape, 1)
    if_ref[...] = jnp.where(lane < HEADS, gc, _log_sigmoid(gc))
    gr = _dot_hp(wift_ref[...], h, ((1,), (1,))) + bift_ref[...]
    sub = lax.broadcasted_iota(jnp.int32, gr.shape, 0)
    ift_ref[...] = jnp.where(sub < HEADS, gr, _log_sigmoid(gr))


def _inproj(x2d, mod3, rows_per_mod, tm, g1, wparts):
    t = x2d.shape[0]
    wu, wq, wg, wif, wift, bif, bift = wparts
    mrows = mod3.shape[1]

    def mod_spec(j):
        return pl.BlockSpec((None, mrows, D_MODEL), lambda i: ((i * tm) // rows_per_mod, 0, j))

    const = lambda shape: pl.BlockSpec(shape, lambda i: (0,) * len(shape))
    return pl.pallas_call(
        _inproj_body,
        grid=(t // tm,),
        in_specs=[pl.BlockSpec((tm, D_MODEL), lambda i: (i, 0)),
                  const((1, D_MODEL)), mod_spec(1), mod_spec(0),
                  const(wu.shape), const(wq.shape), const(wg.shape), const(wif.shape), const(wift.shape),
                  const(bif.shape), const(bift.shape)],
        out_specs=[pl.BlockSpec((tm, S5_WIDTH), lambda i: (i, 0)),
                   pl.BlockSpec((tm, 4 * MWIDTH), lambda i: (i, 0)),
                   pl.BlockSpec((tm, 2 * D_MODEL), lambda i: (i, 0)),
                   pl.BlockSpec((tm, LANES), lambda i: (i, 0)),
                   pl.BlockSpec((8, tm), lambda i: (0, i))],
        out_shape=[jax.ShapeDtypeStruct((t, S5_WIDTH), F32),
                   jax.ShapeDtypeStruct((t, 4 * MWIDTH), BF16),
                   jax.ShapeDtypeStruct((t, 2 * D_MODEL), BF16),
                   jax.ShapeDtypeStruct((t, LANES), F32),
                   jax.ShapeDtypeStruct((8, t), F32)],
        compiler_params=_cparams("parallel"),
        name="inproj",
    )(x2d, g1, mod3, mod3, wu, wq, wg, wif, wift, bif, bift)


def _s5_tables(lam_re, lam_im, log_dt, b_re, b_im, c_re, c_im, d_s5):
    hi = lax.Precision.HIGHEST
    dt = jnp.exp(log_dt.astype(F32))[:, None]
    lr, li = lam_re.astype(F32), lam_im.astype(F32)
    dpow = jnp.arange(S5_CHUNK + 1, dtype=F32)[:, None, None]
    mag = jnp.exp(dpow * (lr * dt))
    pw_re, pw_im = mag * jnp.cos(dpow * (li * dt)), mag * jnp.sin(dpow * (li * dt))
    ab_re, ab_im = pw_re[1], pw_im[1]
    den = lr * lr + li * li
    nr, ni = ab_re - 1.0, ab_im
    coef_re = (nr * lr + ni * li) / den
    coef_im = (ni * lr - nr * li) / den
    br, bi = b_re.astype(F32), b_im.astype(F32)
    bb_re = coef_re[..., None] * br - coef_im[..., None] * bi
    bb_im = coef_re[..., None] * bi + coef_im[..., None] * br
    cr, ci = c_re.astype(F32), c_im.astype(F32)
    cl_re = cr[None] * pw_re[:, :, None, :] - ci[None] * pw_im[:, :, None, :]
    cl_im = cr[None] * pw_im[:, :, None, :] + ci[None] * pw_re[:, :, None, :]
    kd = (jnp.einsum('dgop,gpi->dgio', cl_re[:S5_CHUNK], bb_re, precision=hi)
          - jnp.einsum('dgop,gpi->dgio', cl_im[:S5_CHUNK], bb_im, precision=hi))
    kd = kd.at[0].add(d_s5.astype(F32)[:, :, None] * jnp.eye(S5_GROUP, dtype=F32)[None])
    eye = jnp.eye(S5_SUPER, dtype=F32)
    kblk = jnp.einsum('dsaio,ab->dsaibo', kd.reshape(S5_CHUNK, N_SUPER, S5_SUPER, S5_GROUP, S5_GROUP), eye)
    kblk = kblk.reshape(S5_CHUNK, N_SUPER, LANES, LANES)
    kext = jnp.concatenate([jnp.zeros_like(kblk[:1]), kblk], axis=0)
    kpair = jnp.concatenate([kext[S5_CHUNK - 1::-1], kext[S5_CHUNK:0:-1]], axis=-1)
    kpair = kpair.transpose(1, 0, 2, 3).reshape(N_SUPER, S5_CHUNK * LANES, 2 * LANES)
    rp_re, rp_im = pw_re[S5_CHUNK - 1::-1], pw_im[S5_CHUNK - 1::-1]
    f_re = rp_re[..., None] * bb_re[None] - rp_im[..., None] * bb_im[None]
    f_im = rp_re[..., None] * bb_im[None] + rp_im[..., None] * bb_re[None]
    fcat = jnp.stack([f_re, f_im], axis=2).transpose(0, 1, 4, 2, 3)
    fr = fcat.reshape(S5_CHUNK, N_SUPER, S5_SUPER, S5_GROUP, 2, S5_STATE)
    fmat = jnp.einsum('isahcp,ab->siahcbp', fr, eye).reshape(N_SUPER, S5_CHUNK * LANES, SUPER_STATE)
    ecat = jnp.stack([cl_re[1:], -cl_im[1:]], axis=0)
    er = ecat.transpose(2, 0, 4, 1, 3).reshape(N_SUPER, S5_SUPER, 2, S5_STATE, S5_CHUNK, S5_GROUP)
    emat = jnp.einsum('sacpjo,ab->scapjbo', er, eye).reshape(N_SUPER, SUPER_STATE, S5_CHUNK * LANES)

    def lay(re, im):
        z = jnp.stack([re, im], axis=0).reshape(2, N_SUPER, S5_SUPER * S5_STATE)
        return z.transpose(1, 0, 2).reshape(N_SUPER, 1, SUPER_STATE)

    a16 = lay(pw_re[S5_CHUNK], pw_im[S5_CHUNK])
    a1 = lay(ab_re, ab_im)
    k0 = kblk[0].astype(BF16)
    e0 = emat[:, :, :LANES].astype(BF16)
    f1 = fmat[:, (S5_CHUNK - 1) * LANES:, :].astype(BF16)
    return kpair.astype(BF16), fmat.astype(BF16), emat.astype(BF16), a16, a1, k0, e0, f1


def _s5_prompt_body(u_ref, kp_ref, f_ref, e_ref, a_ref, y_ref, xe_ref, ucat, v_sc, xp_sc):
    n = SEQ // S5_CHUNK
    half = SUPER_STATE // 2
    for j in range(S5_CHUNK):
        ucat[:, j * LANES:(j + 1) * LANES] = u_ref[pl.ds(j, n, stride=S5_CHUNK), :].astype(BF16)
    v_sc[...] = _dot(ucat[...], f_ref[...])
    a_r, a_i = a_ref[:, :half], a_ref[:, half:]
    xr = jnp.zeros((1, half), F32)
    xi = jnp.zeros((1, half), F32)
    for c in range(n):
        xp_sc[c:c + 1, :half] = xr
        xp_sc[c:c + 1, half:] = xi
        vr, vi = v_sc[c:c + 1, :half], v_sc[c:c + 1, half:]
        xr, xi = a_r * xr - a_i * xi + vr, a_r * xi + a_i * xr + vi
    xe_ref[:, :half] = xr
    xe_ref[:, half:] = xi
    ye = _dot(xp_sc[...].astype(BF16), e_ref[...])
    for m in range(S5_CHUNK // 2):
        rows = (2 * m + 2) * LANES
        yp = _dot(ucat[:, :rows], kp_ref[(S5_CHUNK - 2 - 2 * m) * LANES:, :]) + ye[:, 2 * m * LANES:(2 * m + 2) * LANES]
        y_ref[pl.ds(2 * m, n, stride=S5_CHUNK), :] = yp[:, :LANES]
        y_ref[pl.ds(2 * m + 1, n, stride=S5_CHUNK), :] = yp[:, LANES:]


def _s5_prompt(u, kpair, fmat, emat, a16):
    n = SEQ // S5_CHUNK
    return pl.pallas_call(
        _s5_prompt_body,
        grid=(N_SUPER, N_PROMPT),
        in_specs=[pl.BlockSpec((SEQ, LANES), lambda s, b: (b, s)),
                  pl.BlockSpec((None,) + kpair.shape[1:], lambda s, b: (s, 0, 0)),
                  pl.BlockSpec((None,) + fmat.shape[1:], lambda s, b: (s, 0, 0)),
                  pl.BlockSpec((None,) + emat.shape[1:], lambda s, b: (s, 0, 0)),
                  pl.BlockSpec((None, 1, SUPER_STATE), lambda s, b: (s, 0, 0))],
        out_specs=[pl.BlockSpec((SEQ, LANES), lambda s, b: (b, s)),
                   pl.BlockSpec((None, None, 1, SUPER_STATE), lambda s, b: (s, b, 0, 0))],
        out_shape=[jax.ShapeDtypeStruct(u.shape, F32),
                   jax.ShapeDtypeStruct((N_SUPER, N_PROMPT, 1, SUPER_STATE), F32)],
        scratch_shapes=[pltpu.VMEM((n, S5_CHUNK * LANES), BF16),
                        pltpu.VMEM((n, SUPER_STATE), F32),
                        pltpu.VMEM((n, SUPER_STATE), F32)],
        compiler_params=_cparams("parallel", "parallel"),
        name="s5_prompt",
    )(u, kpair, fmat, emat, a16)


def _s5_step_body(u_ref, x0_ref, k0_ref, e0_ref, f1_ref, a_ref, y_ref, xn_ref):
    half = SUPER_STATE // 2
    ub = u_ref[...].astype(BF16)
    x0 = x0_ref[...]
    bu = _dot(ub, f1_ref[...])
    a_r, a_i = a_ref[:, :half], a_ref[:, half:]
    x0r, x0i = x0[:, :half], x0[:, half:]
    xn_ref[:, :half] = a_r * x0r - a_i * x0i + bu[:, :half]
    xn_ref[:, half:] = a_r * x0i + a_i * x0r + bu[:, half:]
    y_ref[...] = _dot(x0.astype(BF16), e0_ref[...]) + _dot(ub, k0_ref[...])


def _s5_step(u, x0, k0, e0, f1, a1):
    t = u.shape[0]
    per = lambda a: pl.BlockSpec((None,) + a.shape[1:], lambda s: (s,) + (0,) * (a.ndim - 1))
    return pl.pallas_call(
        _s5_step_body,
        grid=(N_SUPER,),
        in_specs=[pl.BlockSpec((t, LANES), lambda s: (0, s)), per(x0), per(k0), per(e0), per(f1), per(a1)],
        out_specs=[pl.BlockSpec((t, LANES), lambda s: (0, s)), per(x0)],
        out_shape=[jax.ShapeDtypeStruct(u.shape, F32), jax.ShapeDtypeStruct(x0.shape, F32)],
        compiler_params=_cparams("parallel"),
        name="s5_step",
    )(u, x0, k0, e0, f1, a1)


def _mlstm_prompt_body(q_ref, gc_ref, gr_ref, mg_ref, hm_ref, c_out, n_out, m_out, c_sc, n_sc, m_sc):
    lc = MCHUNK
    ci = pl.program_id(1)

    @pl.when(ci == 0)
    def _():
        c_sc[...] = jnp.zeros_like(c_sc)
        n_sc[...] = jnp.zeros_like(n_sc)
        m_sc[...] = jnp.zeros_like(m_sc)

    scale = DH ** -0.5
    row = lax.broadcasted_iota(jnp.int32, (lc, lc), 0)
    col = lax.broadcasted_iota(jnp.int32, (lc, lc), 1)
    causal = col <= row
    tri = causal.astype(BF16)
    ones = jnp.ones((lc, DH), BF16)
    gc = gc_ref[...]
    gr = gr_ref[...]
    gc_hi, gc_lo = _split_bf16(gc)
    bcol_all = _dot(tri, gc_hi) + _dot(tri, gc_lo)
    brow_all = _dot_exact_rhs(gr, tri, ((1,), (1,)))
    for hd in range(HEADS):
        q = q_ref[:, hd * DH:(hd + 1) * DH]
        k = q_ref[:, MWIDTH + hd * DH:MWIDTH + (hd + 1) * DH]
        v = q_ref[:, 2 * MWIDTH + hd * DH:2 * MWIDTH + (hd + 1) * DH]
        o = q_ref[:, 3 * MWIDTH + hd * DH:3 * MWIDTH + (hd + 1) * DH]
        i_col = gc[:, hd:hd + 1]
        b_col = bcol_all[:, HEADS + hd:HEADS + hd + 1]
        i_row = gr[hd:hd + 1, :]
        b_row = brow_all[HEADS + hd:HEADS + hd + 1, :]
        m_prev = m_sc[hd:hd + 1, :][:, :1]
        dm = jnp.where(causal, b_col - b_row + i_row, -jnp.inf)
        m_inter = b_col + m_prev
        m_t = jnp.maximum(m_inter, jnp.max(dm, axis=-1, keepdims=True))
        w_inter = jnp.exp(m_inter - m_t)
        w = jnp.exp(dm - m_t)
        s = _dot_dims(q, k, ((1,), (1,))) * scale * w
        sb = s.astype(BF16)
        c_prev = c_sc[hd]
        n_prev = n_sc[hd]
        num = w_inter * _dot(q, c_prev.astype(BF16)) + _dot(sb, v)
        den = w_inter * _dot(q, n_prev.astype(BF16)) + _dot(sb, ones)
        h = num / jnp.maximum(jnp.abs(den), jnp.exp(-m_t))
        hn = h * lax.rsqrt(jnp.mean(h * h, axis=-1, keepdims=True) + EPS)
        hm_ref[:, hd * DH:(hd + 1) * DH] = (hn * mg_ref[:, hd * DH:(hd + 1) * DH] * _sigmoid(o.astype(F32))).astype(BF16)
        m_last = m_t[lc - 1:lc, :]
        b_last = b_col[lc - 1:lc, :]
        decay = w_inter[lc - 1:lc, :]
        w_last = jnp.exp(b_last - b_col + i_col - m_last) * scale
        kw = (k.astype(F32) * w_last).astype(BF16)
        c_sc[hd] = decay * c_prev + _dot_dims(kw, v, ((0,), (0,)))
        n_sc[hd] = decay * n_prev + _dot_dims(kw, ones, ((0,), (0,)))
        m_sc[hd:hd + 1, :] = jnp.broadcast_to(m_last, (1, LANES))

    @pl.when(ci == pl.num_programs(1) - 1)
    def _():
        c_out[...] = c_sc[...]
        n_out[...] = n_sc[...]
        m_out[...] = m_sc[...]


def _mlstm_prompt(qkvo, gcol, grow, mnorm_g):
    nc = SEQ // MCHUNK
    return pl.pallas_call(
        _mlstm_prompt_body,
        grid=(N_PROMPT, nc),
        in_specs=[pl.BlockSpec((MCHUNK, 4 * MWIDTH), lambda b, c: (b * nc + c, 0)),
                  pl.BlockSpec((MCHUNK, LANES), lambda b, c: (b * nc + c, 0)),
                  pl.BlockSpec((8, MCHUNK), lambda b, c: (0, b * nc + c)),
                  pl.BlockSpec((1, MWIDTH), lambda b, c: (0, 0))],
        out_specs=[pl.BlockSpec((MCHUNK, MWIDTH), lambda b, c: (b * nc + c, 0)),
                   pl.BlockSpec((None, HEADS, DH, DH), lambda b, c: (b, 0, 0, 0)),
                   pl.BlockSpec((None, HEADS, DH, DH), lambda b, c: (b, 0, 0, 0)),
                   pl.BlockSpec((None, 8, LANES), lambda b, c: (b, 0, 0))],
        out_shape=[jax.ShapeDtypeStruct((N_PROMPT * SEQ, MWIDTH), BF16),
                   jax.ShapeDtypeStruct((N_PROMPT, HEADS, DH, DH), F32),
                   jax.ShapeDtypeStruct((N_PROMPT, HEADS, DH, DH), F32),
                   jax.ShapeDtypeStruct((N_PROMPT, 8, LANES), F32)],
        scratch_shapes=[pltpu.VMEM((HEADS, DH, DH), F32),
                        pltpu.VMEM((HEADS, DH, DH), F32),
                        pltpu.VMEM((8, LANES), F32)],
        compiler_params=_cparams("parallel", "arbitrary"),
        name="mlstm_prompt",
    )(qkvo, gcol, grow, mnorm_g)


MSTEP_TOK = 8


def _mlstm_step_body(q_ref, qkt_ref, gc_ref, m0_ref, c0_ref, n0_ref, mg_ref, hm_ref, c_out, n_out, m_out):
    scale = DH ** -0.5
    m_out[...] = jnp.zeros_like(m_out)
    for j in range(MSTEP_TOK):
        for hd in range(HEADS):
            sl = slice(hd * DH, (hd + 1) * DH)
            q_row = q_ref[j:j + 1, sl].astype(F32)
            k_row = q_ref[j:j + 1, MWIDTH + hd * DH:MWIDTH + (hd + 1) * DH].astype(F32)
            v_row = q_ref[j:j + 1, 2 * MWIDTH + hd * DH:2 * MWIDTH + (hd + 1) * DH].astype(F32)
            o_row = q_ref[j:j + 1, 3 * MWIDTH + hd * DH:3 * MWIDTH + (hd + 1) * DH].astype(F32)
            q_col = qkt_ref[hd, :, j:j + 1]
            k_col = qkt_ref[HEADS + hd, :, j:j + 1]
            i_g = gc_ref[j:j + 1, hd:hd + 1]
            lf = gc_ref[j:j + 1, HEADS + hd:HEADS + hd + 1]
            m0 = m0_ref[j:j + 1, hd:hd + 1]
            c0 = c0_ref[j, hd]
            n0 = n0_ref[j:j + 1, sl]
            m_inter = lf + m0
            m_t = jnp.maximum(m_inter, i_g)
            w_inter = jnp.exp(m_inter - m_t)
            w = jnp.exp(i_g - m_t)
            s = jnp.sum(q_row * k_row, axis=-1, keepdims=True) * scale * w
            num = w_inter * jnp.sum(q_col * c0, axis=0, keepdims=True) + s * v_row
            den = w_inter * jnp.sum(q_row * n0, axis=-1, keepdims=True) + s
            h = num / jnp.maximum(jnp.abs(den), jnp.exp(-m_t))
            hn = h * lax.rsqrt(jnp.mean(h * h, axis=-1, keepdims=True) + EPS)
            hm_ref[j:j + 1, sl] = hn * mg_ref[:, sl] * _sigmoid(o_row)
            c_out[j, hd] = w_inter * c0 + (w * scale) * (k_col * v_row)
            n_out[j:j + 1, sl] = w_inter * n0 + (w * scale) * k_row
            m_out[j:j + 1, hd:hd + 1] = m_t


def _mlstm_step(qkvo, qkt, gcol, m0, c0, n0, mnorm_g):
    t = qkvo.shape[0]
    tk = MSTEP_TOK
    return pl.pallas_call(
        _mlstm_step_body,
        grid=(t // tk,),
        in_specs=[pl.BlockSpec((tk, 4 * MWIDTH), lambda i: (i, 0)),
                  pl.BlockSpec((None, 2 * HEADS, DH, tk), lambda i: (i, 0, 0, 0)),
                  pl.BlockSpec((tk, LANES), lambda i: (i, 0)),
                  pl.BlockSpec((tk, LANES), lambda i: (i, 0)),
                  pl.BlockSpec((tk, HEADS, DH, DH), lambda i: (i, 0, 0, 0)),
                  pl.BlockSpec((tk, MWIDTH), lambda i: (i, 0)),
                  pl.BlockSpec((1, MWIDTH), lambda i: (0, 0))],
        out_specs=[pl.BlockSpec((tk, MWIDTH), lambda i: (i, 0)),
                   pl.BlockSpec((tk, HEADS, DH, DH), lambda i: (i, 0, 0, 0)),
                   pl.BlockSpec((tk, MWIDTH), lambda i: (i, 0)),
                   pl.BlockSpec((tk, LANES), lambda i: (i, 0))],
        out_shape=[jax.ShapeDtypeStruct((t, MWIDTH), F32),
                   jax.ShapeDtypeStruct((t, HEADS, DH, DH), F32),
                   jax.ShapeDtypeStruct((t, MWIDTH), F32),
                   jax.ShapeDtypeStruct((t, LANES), F32)],
        compiler_params=_cparams("parallel"),
        name="mlstm_step",
    )(qkvo, qkt, gcol, m0, c0, n0, mnorm_g)


def _post_body(x_ref, y5_ref, hm_ref, gg_ref, g1_ref, sc_ref, sh_ref, n2_ref, wglu_ref, wmo_ref, wo_ref, rw_ref, rb_ref,
               cnt0_ref, x1_ref, h2_ref, eid_ref, wgt_ref, rank_ref, cnt_ref, run_sc, *, n_real, n_steps):
    i = pl.program_id(0)

    @pl.when(i == 0)
    def _():
        run_sc[...] = cnt0_ref[...]

    if n_steps > n_real:
        @pl.when(i >= n_real)
        def _():
            h2_ref[...] = jnp.zeros_like(h2_ref)

        pl.when(i < n_real)(functools.partial(
            _post_tile, x_ref, y5_ref, hm_ref, gg_ref, g1_ref, sc_ref, sh_ref, n2_ref, wglu_ref, wmo_ref, wo_ref,
            rw_ref, rb_ref, x1_ref, h2_ref, eid_ref, wgt_ref, rank_ref, run_sc))
    else:
        _post_tile(x_ref, y5_ref, hm_ref, gg_ref, g1_ref, sc_ref, sh_ref, n2_ref, wglu_ref, wmo_ref, wo_ref,
                   rw_ref, rb_ref, x1_ref, h2_ref, eid_ref, wgt_ref, rank_ref, run_sc)
    cnt_ref[...] = run_sc[...]


def _post_tile(x_ref, y5_ref, hm_ref, gg_ref, g1_ref, sc_ref, sh_ref, n2_ref, wglu_ref, wmo_ref, wo_ref, rw_ref, rb_ref,
               x1_ref, h2_ref, eid_ref, wgt_ref, rank_ref, run_sc):
    g5 = _gelu_tanh(y5_ref[...]).astype(BF16)
    glu = _dot(g5, wglu_ref[...])
    br_s5 = glu[:, :D_MODEL] * _sigmoid(glu[:, D_MODEL:])
    br_m = _dot(hm_ref[...].astype(BF16), wmo_ref[...])
    gg = gg_ref[...].astype(F32)
    merged = _sigmoid(gg[:, :D_MODEL]) * br_s5 + _sigmoid(gg[:, D_MODEL:]) * br_m
    x1 = x_ref[...] + g1_ref[...] * _dot(merged.astype(BF16), wo_ref[...])
    x1_ref[...] = x1
    ms = jnp.mean(x1 * x1, axis=-1, keepdims=True)
    h2 = x1 * lax.rsqrt(ms + EPS) * n2_ref[...] * (1.0 + sc_ref[...]) + sh_ref[...]
    h2_ref[...] = h2
    logits = _dot_hp(h2, rw_ref[...]) + rb_ref[...]
    tm = logits.shape[0]
    lane = lax.broadcasted_iota(jnp.int32, logits.shape, 1)
    l = jnp.where(lane < N_EXPERTS, logits, -jnp.inf)
    vals, hots, idxs = [], [], []
    for _ in range(TOP_K):
        mx = jnp.max(l, axis=-1, keepdims=True)
        idx = jnp.min(jnp.where(l == mx, lane, LANES), axis=-1, keepdims=True)
        hot = lane == idx
        vals.append(mx)
        hots.append(hot)
        idxs.append(idx)
        l = jnp.where(hot, -jnp.inf, l)
    ex = [jnp.exp(vk - vals[0]) for vk in vals]
    tot = ex[0] + ex[1] + ex[2] + ex[3]
    chosen = jnp.where(hots[0] | hots[1] | hots[2] | hots[3], 1.0, 0.0)
    r_io = lax.broadcasted_iota(jnp.int32, (tm, tm), 0)
    c_io = lax.broadcasted_iota(jnp.int32, (tm, tm), 1)
    earlier = (c_io < r_io).astype(BF16)
    before = run_sc[...] + _dot(earlier, chosen.astype(BF16))
    eid = jnp.zeros(logits.shape, jnp.int32)
    wgt = jnp.zeros(logits.shape, F32)
    rank = jnp.zeros(logits.shape, F32)
    for k in range(TOP_K):
        eid = jnp.where(lane == k, idxs[k], eid)
        wgt = jnp.where(lane == k, ex[k] / tot, wgt)
        rank = jnp.where(lane == k, jnp.sum(jnp.where(hots[k], before, 0.0), axis=-1, keepdims=True), rank)
    eid_ref[...] = eid
    wgt_ref[...] = wgt
    rank_ref[...] = rank.astype(jnp.int32)
    run_sc[...] += jnp.sum(chosen, axis=0, keepdims=True)


def _post(x2d, y5, hm, gg, mod3, rows_per_mod, tm, n2, wglu, wmo, wo, rw, rb, cnt0, h2_rows):
    t = x2d.shape[0]
    mrows = mod3.shape[1]
    n_real = t // tm
    n_steps = -(-h2_rows // tm)
    last = n_real - 1

    def mod_spec(j):
        return pl.BlockSpec((None, mrows, D_MODEL), lambda i: ((jnp.minimum(i, last) * tm) // rows_per_mod, 0, j))

    const = lambda shape: pl.BlockSpec(shape, lambda i: (0,) * len(shape))
    tile = lambda w: pl.BlockSpec((tm, w), lambda i: (jnp.minimum(i, last), 0))
    return pl.pallas_call(
        functools.partial(_post_body, n_real=n_real, n_steps=n_steps),
        grid=(n_steps,),
        in_specs=[tile(D_MODEL), tile(S5_WIDTH), tile(MWIDTH), tile(2 * D_MODEL),
                  mod_spec(2), mod_spec(4), mod_spec(3), const((1, D_MODEL)),
                  const(wglu.shape), const(wmo.shape), const(wo.shape), const(rw.shape), const(rb.shape),
                  const((1, LANES))],
        out_specs=[tile(D_MODEL), pl.BlockSpec((tm, D_MODEL), lambda i: (i, 0)), tile(LANES), tile(LANES), tile(LANES),
                   const((1, LANES))],
        out_shape=[jax.ShapeDtypeStruct((t, D_MODEL), F32),
                   jax.ShapeDtypeStruct((h2_rows, D_MODEL), F32),
                   jax.ShapeDtypeStruct((t, LANES), jnp.int32),
                   jax.ShapeDtypeStruct((t, LANES), F32),
                   jax.ShapeDtypeStruct((t, LANES), jnp.int32),
                   jax.ShapeDtypeStruct((1, LANES), F32)],
        scratch_shapes=[pltpu.VMEM((1, LANES), F32)],
        compiler_params=_cparams("arbitrary"),
        name="post",
    )(x2d, y5, hm, gg, mod3, mod3, mod3, n2, wglu, wmo, wo, rw, rb, cnt0)


N_TOKENS = N_PROMPT * SEQ + N_SAMPLE
N_ASSIGN = N_TOKENS * TOP_K
MOE_TILE = 256
MOE_TILES = N_ASSIGN // MOE_TILE + N_EXPERTS
N_SLOT = MOE_TILES * MOE_TILE
CMB_TILE = 128


def _moe_body(te_ref, nt_ref, cur_ref, nxt_ref, h2_hbm, wup_ref, bup_ref, wdn_ref, bdn_ref, y_ref,
              xb0, xb1, sem, wup_bf, wdn_bf):
    i = pl.program_id(0)
    nt = nt_ref[0]

    def gather(idx_ref, buf, s):
        for r in range(MOE_TILE):
            pltpu.make_async_copy(h2_hbm.at[pl.ds(idx_ref[0, r], 1)], buf.at[pl.ds(r, 1)], sem.at[s]).start()

    def wait(buf, s):
        pltpu.make_async_copy(h2_hbm.at[pl.ds(0, MOE_TILE)], buf, sem.at[s]).wait()

    @pl.when(i == 0)
    def _():
        gather(cur_ref, xb0, 0)

    new_expert = jnp.logical_or(i == 0, te_ref[i] != te_ref[jnp.maximum(i - 1, 0)])

    @pl.when(jnp.logical_and(new_expert, i < nt))
    def _():
        wup_bf[...] = wup_ref[...].astype(BF16)
        wdn_bf[...] = wdn_ref[...].astype(BF16)

    def step(cur, nxt, s_cur, s_nxt):
        @pl.when(i + 1 < nt)
        def _():
            gather(nxt_ref, nxt, s_nxt)

        wait(cur, s_cur)
        gu = _dot(cur[...].astype(BF16), wup_bf[...]) + bup_ref[...]
        g = jnp.minimum(gu[:, :D_FF], SWIGLU_LIMIT)
        up = jnp.clip(gu[:, D_FF:], -SWIGLU_LIMIT, SWIGLU_LIMIT)
        act = (up + 1.0) * g * _sigmoid(SWIGLU_ALPHA * g)
        y_ref[...] = _dot(act.astype(BF16), wdn_bf[...]) + bdn_ref[...]

    live = i < nt
    pl.when(jnp.logical_and(live, i % 2 == 0))(functools.partial(step, xb0, xb1, 0, 1))
    pl.when(jnp.logical_and(live, i % 2 == 1))(functools.partial(step, xb1, xb0, 1, 0))

    @pl.when(jnp.logical_not(live))
    def _():
        y_ref[...] = jnp.zeros_like(y_ref)


def _moe(tile_expert, n_tiles, slot_token, h2_all, wup, bup, wdn, bdn):
    idx3 = slot_token.reshape(MOE_TILES, 1, MOE_TILE)
    grid_spec = pltpu.PrefetchScalarGridSpec(
        num_scalar_prefetch=2,
        grid=(MOE_TILES,),
        in_specs=[pl.BlockSpec((None, 1, MOE_TILE), lambda i, te, nt: (i, 0, 0), memory_space=pltpu.SMEM),
                  pl.BlockSpec((None, 1, MOE_TILE), lambda i, te, nt: (jnp.minimum(i + 1, MOE_TILES - 1), 0, 0),
                               memory_space=pltpu.SMEM),
                  pl.BlockSpec(memory_space=pl.ANY),
                  pl.BlockSpec((None, D_MODEL, 2 * D_FF), lambda i, te, nt: (te[i], 0, 0)),
                  pl.BlockSpec((None, 1, 2 * D_FF), lambda i, te, nt: (te[i], 0, 0)),
                  pl.BlockSpec((None, D_FF, D_MODEL), lambda i, te, nt: (te[i], 0, 0)),
                  pl.BlockSpec((None, 1, D_MODEL), lambda i, te, nt: (te[i], 0, 0))],
        out_specs=pl.BlockSpec((MOE_TILE, D_MODEL), lambda i, te, nt: (i, 0)),
        scratch_shapes=[pltpu.VMEM((MOE_TILE, D_MODEL), F32), pltpu.VMEM((MOE_TILE, D_MODEL), F32),
                        pltpu.SemaphoreType.DMA((2,)),
                        pltpu.VMEM((D_MODEL, 2 * D_FF), BF16), pltpu.VMEM((D_FF, D_MODEL), BF16)])
    return pl.pallas_call(
        _moe_body,
        grid_spec=grid_spec,
        out_shape=jax.ShapeDtypeStruct((N_SLOT, D_MODEL), F32),
        compiler_params=_cparams("arbitrary"),
        name="moe",
    )(tile_expert, n_tiles, idx3, idx3, h2_all, wup, bup, wdn, bdn)


def _combine_body(cur_ref, nxt_ref, wgt_ref, x1_ref, g2_ref, fg_ref, y_hbm, o_ref, yb0, yb1, sem):
    i = pl.program_id(0)
    n = pl.num_programs(0)
    tm = CMB_TILE

    def gather(idx_ref, buf, s):
        def body(r, carry):
            for k in range(TOP_K):
                pltpu.make_async_copy(y_hbm.at[pl.ds(idx_ref[0, r * TOP_K + k], 1)],
                                      buf.at[pl.ds(k * tm + r, 1)], sem.at[s]).start()
            return carry
        lax.fori_loop(0, tm, body, 0, unroll=8)

    def wait(buf, s):
        pltpu.make_async_copy(y_hbm.at[pl.ds(0, TOP_K * tm)], buf, sem.at[s]).wait()

    @pl.when(i == 0)
    def _():
        gather(cur_ref, yb0, 0)

    def step(cur, nxt, s_cur, s_nxt):
        @pl.when(i + 1 < n)
        def _():
            gather(nxt_ref, nxt, s_nxt)

        wait(cur, s_cur)
        wgt = wgt_ref[...]
        acc = wgt[:, 0:1] * cur[0:tm, :]
        for k in range(1, TOP_K):
            acc = acc + wgt[:, k:k + 1] * cur[k * tm:(k + 1) * tm, :]
        xo = x1_ref[...] + g2_ref[...] * acc
        ms = jnp.mean(xo * xo, axis=-1, keepdims=True)
        o_ref[...] = xo * lax.rsqrt(ms + EPS) * fg_ref[...]

    pl.when(i % 2 == 0)(functools.partial(step, yb0, yb1, 0, 1))
    pl.when(i % 2 == 1)(functools.partial(step, yb1, yb0, 1, 0))


def _combine(pos, wgt, x1, mod3, rows_per_mod, fg, y_slots):
    t = x1.shape[0]
    tm = CMB_TILE
    n = t // tm
    mrows = mod3.shape[1]
    assert mrows in (1, tm)
    pos3 = pos.reshape(n, 1, tm * TOP_K)
    return pl.pallas_call(
        _combine_body,
        grid=(n,),
        in_specs=[pl.BlockSpec((None, 1, tm * TOP_K), lambda i: (i, 0, 0), memory_space=pltpu.SMEM),
                  pl.BlockSpec((None, 1, tm * TOP_K), lambda i: (jnp.minimum(i + 1, n - 1), 0, 0),
                               memory_space=pltpu.SMEM),
                  pl.BlockSpec((tm, LANES), lambda i: (i, 0)),
                  pl.BlockSpec((tm, D_MODEL), lambda i: (i, 0)),
                  pl.BlockSpec((None, mrows, D_MODEL), lambda i: ((i * tm) // rows_per_mod, 0, 5)),
                  pl.BlockSpec((1, D_MODEL), lambda i: (0, 0)),
                  pl.BlockSpec(memory_space=pl.ANY)],
        out_specs=pl.BlockSpec((tm, D_MODEL), lambda i: (i, 0)),
        out_shape=jax.ShapeDtypeStruct((t, D_MODEL), F32),
        scratch_shapes=[pltpu.VMEM((TOP_K * tm, D_MODEL), F32), pltpu.VMEM((TOP_K * tm, D_MODEL), F32),
                        pltpu.SemaphoreType.DMA((2,))],
        compiler_params=_cparams("arbitrary"),
        name="combine",
    )(pos3, pos3, wgt, x1, mod3, fg, y_slots)


def _routing(eid_p, rank_p, eid_s, rank_s, counts):
    cnt = counts[0, :N_EXPERTS].astype(jnp.int32)
    ntile = (cnt + MOE_TILE - 1) // MOE_TILE
    tile_end = jnp.cumsum(ntile)
    poff = (tile_end - ntile) * MOE_TILE
    total = tile_end[-1]
    j = jnp.arange(MOE_TILES, dtype=jnp.int32)
    te = jnp.sum((j[:, None] >= tile_end[None, :]).astype(jnp.int32), axis=1)
    te_last = jnp.sum(((total - 1) >= tile_end).astype(jnp.int32))
    tile_expert = jnp.where(j < total, te, te_last).astype(jnp.int32)
    ex = jnp.arange(N_EXPERTS, dtype=jnp.int32)

    def pos_of(eid, rank):
        e = eid[:, :TOP_K]
        off = jnp.sum(jnp.where(e[:, :, None] == ex[None, None, :], poff[None, None, :], 0), axis=-1)
        return off + rank[:, :TOP_K]

    pos_p = pos_of(eid_p, rank_p)
    pos_s = pos_of(eid_s, rank_s)
    tok = jnp.concatenate([jnp.repeat(jnp.arange(N_PROMPT * SEQ, dtype=jnp.int32), TOP_K),
                           jnp.repeat(N_PROMPT * SEQ + jnp.arange(N_SAMPLE, dtype=jnp.int32), TOP_K)])
    pos_all = jnp.concatenate([pos_p.reshape(-1), pos_s.reshape(-1)])
    slot_token = jnp.zeros((N_SLOT,), jnp.int32).at[pos_all].set(tok, unique_indices=True)
    return tile_expert, total.reshape(1).astype(jnp.int32), slot_token, pos_p, pos_s


def _unpack_s5_state(x):
    n = x.shape[1]
    z = x.reshape(N_SUPER, n, 2, S5_SUPER, S5_STATE).transpose(2, 1, 0, 3, 4).reshape(2, n, S5_GROUPS, S5_STATE)
    return z[0], z[1]


def _pack_s5_state(re, im):
    n = re.shape[0]
    z = jnp.stack([re, im], axis=0).reshape(2, n, N_SUPER, S5_SUPER, S5_STATE)
    return z.transpose(2, 1, 0, 3, 4).reshape(N_SUPER, n, SUPER_STATE)


def kernel(x_prompt, x_sample, c_prompt, c_sample, state_s5_re, state_s5_im, state_mlstm_C, state_mlstm_n, state_mlstm_m, norm1_g, norm2_g, final_norm_g, w_ada, b_ada, w_in, s5_lambda_re, s5_lambda_im, s5_log_dt, s5_B_re, s5_B_im, s5_C_re, s5_C_im, s5_D, s5_w_glu, mlstm_b_i, mlstm_b_f, mlstm_norm_g, mlstm_w_out, w_out, router_w, router_b, expert_w_up, expert_b_up, expert_w_down, expert_b_down):
    assert w_in.shape[0] == 1, "single layer"
    tp = N_PROMPT * SEQ
    xp = x_prompt.reshape(tp, D_MODEL).astype(F32)
    xs = x_sample.reshape(N_SAMPLE, D_MODEL).astype(F32)

    w = w_in[0]
    c0 = S5_WIDTH
    c1 = c0 + 4 * MWIDTH
    c2 = c1 + 2 * HEADS
    wu = w[:, :c0].astype(BF16)
    wq = w[:, c0:c1].astype(BF16)
    wif = jnp.pad(w[:, c1:c2].astype(F32), ((0, 0), (0, LANES - 2 * HEADS)))
    wift = w[:, c1:c2].astype(F32).T
    wg = w[:, c2:].astype(BF16)
    bvec = jnp.concatenate([mlstm_b_i[0], mlstm_b_f[0]]).astype(F32)
    bif = jnp.pad(bvec, (0, LANES - 2 * HEADS)).reshape(1, LANES)
    bift = bvec.reshape(2 * HEADS, 1)
    wparts = (wu, wq, wg, wif, wift, bif, bift)
    g1 = norm1_g[0].reshape(1, D_MODEL).astype(F32)
    n2 = norm2_g[0].reshape(1, D_MODEL).astype(F32)
    fg = final_norm_g.reshape(1, D_MODEL).astype(F32)
    mg = mlstm_norm_g[0].reshape(1, MWIDTH).astype(F32)
    wglu = s5_w_glu[0].astype(BF16)
    wmo = mlstm_w_out[0].astype(BF16)
    wo = w_out[0].astype(BF16)
    rw = jnp.pad(router_w[0].astype(F32), ((0, 0), (0, LANES - N_EXPERTS)))
    rb = jnp.pad(router_b[0].astype(F32), (0, LANES - N_EXPERTS)).reshape(1, LANES)
    wup = expert_w_up[0].astype(F32)
    wdn = expert_w_down[0].astype(F32)
    bup = expert_b_up[0].astype(F32).reshape(N_EXPERTS, 1, 2 * D_FF)
    bdn = expert_b_down[0].astype(F32).reshape(N_EXPERTS, 1, D_MODEL)
    kpair, fmat, emat, a16, a1, k0, e0, f1 = _s5_tables(
        s5_lambda_re[0], s5_lambda_im[0], s5_log_dt[0], s5_B_re[0], s5_B_im[0], s5_C_re[0], s5_C_im[0], s5_D[0])

    c_all = jnp.concatenate([c_prompt, c_sample], axis=0).astype(F32)
    mod = _adaln(c_all, w_ada[0].astype(F32), b_ada[0].astype(F32))
    mod_p = mod[:N_PROMPT].reshape(N_PROMPT, 1, 6 * D_MODEL)
    mod_s = mod[N_PROMPT:].reshape(1, N_SAMPLE, 6 * D_MODEL)

    u, qkvo, gg, gcol, grow = _inproj(xp, mod_p, SEQ, 512, g1, wparts)
    y5, xend = _s5_prompt(u, kpair, fmat, emat, a16)
    hm, c_p, n_p, m_p = _mlstm_prompt(qkvo, gcol, grow, mg)
    cnt0 = jnp.zeros((1, LANES), F32)
    x1, h2_all, eid_p, wgt_p, rank_p, cnt_p = _post(xp, y5, hm, gg, mod_p, SEQ, 512, n2, wglu, wmo, wo, rw, rb,
                                                    cnt0, N_TOKENS)
    p_re, p_im = _unpack_s5_state(xend.reshape(N_SUPER, N_PROMPT, SUPER_STATE))
    n_p = n_p[..., 0]
    m_p = m_p[:, :HEADS, 0]

    us, qs, ggs, gcs, _ = _inproj(xs, mod_s, N_SAMPLE, N_SAMPLE, g1, wparts)
    x0 = _pack_s5_state(state_s5_re[0].astype(F32), state_s5_im[0].astype(F32))
    y5s, xns = _s5_step(us, x0, k0, e0, f1, a1)
    s_re, s_im = _unpack_s5_state(xns)
    qkt = qs[:, :2 * MWIDTH].astype(F32).reshape(N_SAMPLE // MSTEP_TOK, MSTEP_TOK, 2 * HEADS, DH).transpose(0, 2, 3, 1)
    m0 = jnp.pad(state_mlstm_m[0].astype(F32), ((0, 0), (0, LANES - HEADS)))
    hms, c_s, n_s, m_s = _mlstm_step(qs, qkt, gcs, m0, state_mlstm_C[0].astype(F32),
                                     state_mlstm_n[0].astype(F32).reshape(N_SAMPLE, MWIDTH), mg)
    x1s, h2s, eid_s, wgt_s, rank_s, cnt_all = _post(xs, y5s, hms, ggs, mod_s, N_SAMPLE, N_SAMPLE, n2, wglu, wmo, wo, rw, rb,
                                                    cnt_p, N_SAMPLE)

    h2_all = lax.dynamic_update_slice(h2_all, h2s, (tp, 0))
    tile_expert, n_tiles, slot_token, pos_p, pos_s = _routing(eid_p, rank_p, eid_s, rank_s, cnt_all)
    y_slots = _moe(tile_expert, n_tiles, slot_token, h2_all, wup, bup, wdn, bdn)
    y_p = _combine(pos_p, wgt_p, x1, mod_p, SEQ, fg, y_slots)
    y_s = _combine(pos_s, wgt_s, x1s, mod_s, N_SAMPLE, fg, y_slots)

    return (y_p.reshape(N_PROMPT, SEQ, D_MODEL).astype(x_prompt.dtype),
            y_s.reshape(N_SAMPLE, 1, D_MODEL).astype(x_sample.dtype),
            p_re[None], p_im[None], c_p[None], n_p[None], m_p[None],
            s_re[None], s_im[None], c_s[None],
            n_s.reshape(1, N_SAMPLE, HEADS, DH), m_s[:, :HEADS][None])
```

```python
import functools
import math

import jax
import jax.numpy as jnp
from jax import lax
from jax.experimental import pallas as pl
from jax.experimental.pallas import tpu as pltpu

F32 = jnp.float32
BF16 = jnp.bfloat16

D_MODEL = 1024
SEQ = 2048
N_PROMPT = 8
N_SAMPLE = 128
S5_WIDTH = 512
S5_GROUP = 16
S5_GROUPS = 32
S5_STATE = 64
HEADS = 4
DH = 128
MWIDTH = HEADS * DH
N_EXPERTS = 32
TOP_K = 4
D_FF = 1024
SWIGLU_LIMIT = 7.0
SWIGLU_ALPHA = 1.702
EPS = 1e-6

LANES = 128
S5_CHUNK = 16
S5_SUPER = LANES // S5_GROUP
N_SUPER = S5_GROUPS // S5_SUPER
SUPER_STATE = 2 * S5_SUPER * S5_STATE
MCHUNK = 128
VMEM_LIMIT = 56 * 1024 * 1024


def _cparams(*sem):
    return pltpu.CompilerParams(dimension_semantics=sem, vmem_limit_bytes=VMEM_LIMIT)


def _dot(a, b):
    return jnp.dot(a, b, preferred_element_type=F32)


def _dot_dims(a, b, dims):
    return lax.dot_general(a, b, (dims, ((), ())), preferred_element_type=F32)


def _split_bf16(a):
    hi = a.astype(BF16)
    lo = (a - hi.astype(F32)).astype(BF16)
    return hi, lo


def _dot_hp(a, b, dims=((1,), (0,))):
    ah, al = _split_bf16(a)
    bh, bl = _split_bf16(b)
    return _dot_dims(ah, bh, dims) + (_dot_dims(al, bh, dims) + _dot_dims(ah, bl, dims))


def _dot_exact_rhs(a, b_exact, dims=((1,), (0,))):
    ah, al = _split_bf16(a)
    return _dot_dims(ah, b_exact, dims) + _dot_dims(al, b_exact, dims)


def _log_sigmoid(x):
    return -(jnp.maximum(-x, 0.0) + jnp.log1p(jnp.exp(-jnp.abs(x))))


def _sigmoid(x):
    return 1.0 / (1.0 + jnp.exp(-x))


def _gelu_tanh(x):
    c = math.sqrt(2.0 / math.pi)
    return 0.5 * x * (1.0 + jnp.tanh(c * (x + 0.044715 * (x * x * x))))


def _adaln_body(c_ref, w_ref, b_ref, o_ref):
    c = c_ref[...]
    s = c * _sigmoid(c)
    o_ref[...] = _dot_hp(s, w_ref[...]) + b_ref[...]


def _adaln(c_all, w_ada, b_ada):
    n = c_all.shape[0]
    tn = 1024
    return pl.pallas_call(
        _adaln_body,
        grid=(6 * D_MODEL // tn,),
        in_specs=[pl.BlockSpec((n, D_MODEL), lambda j: (0, 0)),
                  pl.BlockSpec((D_MODEL, tn), lambda j: (0, j)),
                  pl.BlockSpec((1, tn), lambda j: (0, j))],
        out_specs=pl.BlockSpec((n, tn), lambda j: (0, j)),
        out_shape=jax.ShapeDtypeStruct((n, 6 * D_MODEL), F32),
        compiler_params=_cparams("parallel"),
        name="adaln",
    )(c_all, w_ada, b_ada.reshape(1, -1))


def _inproj_body(x_ref, g1_ref, sc_ref, sh_ref, wu_ref, wq_ref, wg_ref, wif_ref, wift_ref, bif_ref, bift_ref,
                 u_ref, q_ref, gg_ref, if_ref, ift_ref):
    x = x_ref[...]
    ms = jnp.mean(x * x, axis=-1, keepdims=True)
    h = x * lax.rsqrt(ms + EPS) * g1_ref[...] * (1.0 + sc_ref[...]) + sh_ref[...]
    hb = h.astype(BF16)
    u_ref[...] = _dot(hb, wu_ref[...])
    q_ref[...] = _dot(hb, wq_ref[...]).astype(BF16)
    gg_ref[...] = _dot(hb, wg_ref[...]).astype(BF16)
    gc = _dot_hp(h, wif_ref[...]) + bif_ref[...]
    lane = lax.broadcasted_iota(jnp.int32, gc.shape, 1)
    if_ref[...] = jnp.where(lane < HEADS, gc, _log_sigmoid(gc))
    gr = _dot_hp(wift_ref[...], h, ((1,), (1,))) + bift_ref[...]
    sub = lax.broadcasted_iota(jnp.int32, gr.shape, 0)
    ift_ref[...] = jnp.where(sub < HEADS, gr, _log_sigmoid(gr))


def _inproj(x2d, mod3, rows_per_mod, tm, g1, wparts):
    t = x2d.shape[0]
    wu, wq, wg, wif, wift, bif, bift = wparts
    mrows = mod3.shape[1]

    def mod_spec(j):
        return pl.BlockSpec((None, mrows, D_MODEL), lambda i: ((i * tm) // rows_per_mod, 0, j))

    const = lambda shape: pl.BlockSpec(shape, lambda i: (0,) * len(shape))
    return pl.pallas_call(
        _inproj_body,
        grid=(t // tm,),
        in_specs=[pl.BlockSpec((tm, D_MODEL), lambda i: (i, 0)),
                  const((1, D_MODEL)), mod_spec(1), mod_spec(0),
                  const(wu.shape), const(wq.shape), const(wg.shape), const(wif.shape), const(wift.shape),
                  const(bif.shape), const(bift.shape)],
        out_specs=[pl.BlockSpec((tm, S5_WIDTH), lambda i: (i, 0)),
                   pl.BlockSpec((tm, 4 * MWIDTH), lambda i: (i, 0)),
                   pl.BlockSpec((tm, 2 * D_MODEL), lambda i: (i, 0)),
                   pl.BlockSpec((tm, LANES), lambda i: (i, 0)),
                   pl.BlockSpec((8, tm), lambda i: (0, i))],
        out_shape=[jax.ShapeDtypeStruct((t, S5_WIDTH), F32),
                   jax.ShapeDtypeStruct((t, 4 * MWIDTH), BF16),
                   jax.ShapeDtypeStruct((t, 2 * D_MODEL), BF16),
                   jax.ShapeDtypeStruct((t, LANES), F32),
                   jax.ShapeDtypeStruct((8, t), F32)],
        compiler_params=_cparams("parallel"),
        name="inproj",
    )(x2d, g1, mod3, mod3, wu, wq, wg, wif, wift, bif, bift)


def _s5_tables(lam_re, lam_im, log_dt, b_re, b_im, c_re, c_im, d_s5):
    hi = lax.Precision.HIGHEST
    dt = jnp.exp(log_dt.astype(F32))[:, None]
    lr, li = lam_re.astype(F32), lam_im.astype(F32)
    dpow = jnp.arange(S5_CHUNK + 1, dtype=F32)[:, None, None]
    mag = jnp.exp(dpow * (lr * dt))
    pw_re, pw_im = mag * jnp.cos(dpow * (li * dt)), mag * jnp.sin(dpow * (li * dt))
    ab_re, ab_im = pw_re[1], pw_im[1]
    den = lr * lr + li * li
    nr, ni = ab_re - 1.0, ab_im
    coef_re = (nr * lr + ni * li) / den
    coef_im = (ni * lr - nr * li) / den
    br, bi = b_re.astype(F32), b_im.astype(F32)
    bb_re = coef_re[..., None] * br - coef_im[..., None] * bi
    bb_im = coef_re[..., None] * bi + coef_im[..., None] * br
    cr, ci = c_re.astype(F32), c_im.astype(F32)
    cl_re = cr[None] * pw_re[:, :, None, :] - ci[None] * pw_im[:, :, None, :]
    cl_im = cr[None] * pw_im[:, :, None, :] + ci[None] * pw_re[:, :, None, :]
    kd = (jnp.einsum('dgop,gpi->dgio', cl_re[:S5_CHUNK], bb_re, precision=hi)
          - jnp.einsum('dgop,gpi->dgio', cl_im[:S5_CHUNK], bb_im, precision=hi))
    kd = kd.at[0].add(d_s5.astype(F32)[:, :, None] * jnp.eye(S5_GROUP, dtype=F32)[None])
    eye = jnp.eye(S5_SUPER, dtype=F32)
    kblk = jnp.einsum('dsaio,ab->dsaibo', kd.reshape(S5_CHUNK, N_SUPER, S5_SUPER, S5_GROUP, S5_GROUP), eye)
    kblk = kblk.reshape(S5_CHUNK, N_SUPER, LANES, LANES)
    kext = jnp.concatenate([jnp.zeros_like(kblk[:1]), kblk], axis=0)
    kpair = jnp.concatenate([kext[S5_CHUNK - 1::-1], kext[S5_CHUNK:0:-1]], axis=-1)
    kpair = kpair.transpose(1, 0, 2, 3).reshape(N_SUPER, S5_CHUNK * LANES, 2 * LANES)
    rp_re, rp_im = pw_re[S5_CHUNK - 1::-1], pw_im[S5_CHUNK - 1::-1]
    f_re = rp_re[..., None] * bb_re[None] - rp_im[..., None] * bb_im[None]
    f_im = rp_re[..., None] * bb_im[None] + rp_im[..., None] * bb_re[None]
    fcat = jnp.stack([f_re, f_im], axis=2).transpose(0, 1, 4, 2, 3)
    fr = fcat.reshape(S5_CHUNK, N_SUPER, S5_SUPER, S5_GROUP, 2, S5_STATE)
    fmat = jnp.einsum('isahcp,ab->siahcbp', fr, eye).reshape(N_SUPER, S5_CHUNK * LANES, SUPER_STATE)
    ecat = jnp.stack([cl_re[1:], -cl_im[1:]], axis=0)
    er = ecat.transpose(2, 0, 4, 1, 3).reshape(N_SUPER, S5_SUPER, 2, S5_STATE, S5_CHUNK, S5_GROUP)
    emat = jnp.einsum('sacpjo,ab->scapjbo', er, eye).reshape(N_SUPER, SUPER_STATE, S5_CHUNK * LANES)

    def lay(re, im):
        z = jnp.stack([re, im], axis=0).reshape(2, N_SUPER, S5_SUPER * S5_STATE)
        return z.transpose(1, 0, 2).reshape(N_SUPER, 1, SUPER_STATE)

    a16 = lay(pw_re[S5_CHUNK], pw_im[S5_CHUNK])
    a1 = lay(ab_re, ab_im)
    k0 = kblk[0].astype(BF16)
    e0 = emat[:, :, :LANES].astype(BF16)
    f1 = fmat[:, (S5_CHUNK - 1) * LANES:, :].astype(BF16)
    return kpair.astype(BF16), fmat.astype(BF16), emat.astype(BF16), a16, a1, k0, e0, f1


def _s5_prompt_body(u_ref, kp_ref, f_ref, e_ref, a_ref, y_ref, xe_ref, ucat, v_sc, xp_sc):
    n = SEQ // S5_CHUNK
    half = SUPER_STATE // 2
    for j in range(S5_CHUNK):
        ucat[:, j * LANES:(j + 1) * LANES] = u_ref[pl.ds(j, n, stride=S5_CHUNK), :].astype(BF16)
    v_sc[...] = _dot(ucat[...], f_ref[...])
    a_r, a_i = a_ref[:, :half], a_ref[:, half:]
    xr = jnp.zeros((1, half), F32)
    xi = jnp.zeros((1, half), F32)
    for c in range(n):
        xp_sc[c:c + 1, :half] = xr
        xp_sc[c:c + 1, half:] = xi
        vr, vi = v_sc[c:c + 1, :half], v_sc[c:c + 1, half:]
        xr, xi = a_r * xr - a_i * xi + vr, a_r * xi + a_i * xr + vi
    xe_ref[:, :half] = xr
    xe_ref[:, half:] = xi
    ye = _dot(xp_sc[...].astype(BF16), e_ref[...])
    for m in range(S5_CHUNK // 2):
        rows = (2 * m + 2) * LANES
        yp = _dot(ucat[:, :rows], kp_ref[(S5_CHUNK - 2 - 2 * m) * LANES:, :]) + ye[:, 2 * m * LANES:(2 * m + 2) * LANES]
        y_ref[pl.ds(2 * m, n, stride=S5_CHUNK), :] = yp[:, :LANES]
        y_ref[pl.ds(2 * m + 1, n, stride=S5_CHUNK), :] = yp[:, LANES:]


def _s5_prompt(u, kpair, fmat, emat, a16):
    n = SEQ // S5_CHUNK
    return pl.pallas_call(
        _s5_prompt_body,
        grid=(N_SUPER, N_PROMPT),
        in_specs=[pl.BlockSpec((SEQ, LANES), lambda s, b: (b, s)),
                  pl.BlockSpec((None,) + kpair.shape[1:], lambda s, b: (s, 0, 0)),
                  pl.BlockSpec((None,) + fmat.shape[1:], lambda s, b: (s, 0, 0)),
                  pl.BlockSpec((None,) + emat.shape[1:], lambda s, b: (s, 0, 0)),
                  pl.BlockSpec((None, 1, SUPER_STATE), lambda s, b: (s, 0, 0))],
        out_specs=[pl.BlockSpec((SEQ, LANES), lambda s, b: (b, s)),
                   pl.BlockSpec((None, None, 1, SUPER_STATE), lambda s, b: (s, b, 0, 0))],
        out_shape=[jax.ShapeDtypeStruct(u.shape, F32),
                   jax.ShapeDtypeStruct((N_SUPER, N_PROMPT, 1, SUPER_STATE), F32)],
        scratch_shapes=[pltpu.VMEM((n, S5_CHUNK * LANES), BF16),
                        pltpu.VMEM((n, SUPER_STATE), F32),
                        pltpu.VMEM((n, SUPER_STATE), F32)],
        compiler_params=_cparams("parallel", "parallel"),
        name="s5_prompt",
    )(u, kpair, fmat, emat, a16)


def _s5_step_body(u_ref, x0_ref, k0_ref, e0_ref, f1_ref, a_ref, y_ref, xn_ref):
    half = SUPER_STATE // 2
    ub = u_ref[...].astype(BF16)
    x0 = x0_ref[...]
    bu = _dot(ub, f1_ref[...])
    a_r, a_i = a_ref[:, :half], a_ref[:, half:]
    x0r, x0i = x0[:, :half], x0[:, half:]
    xn_ref[:, :half] = a_r * x0r - a_i * x0i + bu[:, :half]
    xn_ref[:, half:] = a_r * x0i + a_i * x0r + bu[:, half:]
    y_ref[...] = _dot(x0.astype(BF16), e0_ref[...]) + _dot(ub, k0_ref[...])


def _s5_step(u, x0, k0, e0, f1, a1):
    t = u.shape[0]
    per = lambda a: pl.BlockSpec((None,) + a.shape[1:], lambda s: (s,) + (0,) * (a.ndim - 1))
    return pl.pallas_call(
        _s5_step_body,
        grid=(N_SUPER,),
        in_specs=[pl.BlockSpec((t, LANES), lambda s: (0, s)), per(x0), per(k0), per(e0), per(f1), per(a1)],
        out_specs=[pl.BlockSpec((t, LANES), lambda s: (0, s)), per(x0)],
        out_shape=[jax.ShapeDtypeStruct(u.shape, F32), jax.ShapeDtypeStruct(x0.shape, F32)],
        compiler_params=_cparams("parallel"),
        name="s5_step",
    )(u, x0, k0, e0, f1, a1)


def _mlstm_prompt_body(q_ref, gc_ref, gr_ref, mg_ref, hm_ref, c_out, n_out, m_out, c_sc, n_sc, m_sc):
    lc = MCHUNK
    ci = pl.program_id(1)

    @pl.when(ci == 0)
    def _():
        c_sc[...] = jnp.zeros_like(c_sc)
        n_sc[...] = jnp.zeros_like(n_sc)
        m_sc[...] = jnp.zeros_like(m_sc)

    scale = DH ** -0.5
    row = lax.broadcasted_iota(jnp.int32, (lc, lc), 0)
    col = lax.broadcasted_iota(jnp.int32, (lc, lc), 1)
    causal = col <= row
    tri = causal.astype(BF16)
    ones = jnp.ones((lc, DH), BF16)
    gc = gc_ref[...]
    gr = gr_ref[...]
    gc_hi, gc_lo = _split_bf16(gc)
    bcol_all = _dot(tri, gc_hi) + _dot(tri, gc_lo)
    brow_all = _dot_exact_rhs(gr, tri, ((1,), (1,)))
    for hd in range(HEADS):
        q = q_ref[:, hd * DH:(hd + 1) * DH]
        k = q_ref[:, MWIDTH + hd * DH:MWIDTH + (hd + 1) * DH]
        v = q_ref[:, 2 * MWIDTH + hd * DH:2 * MWIDTH + (hd + 1) * DH]
        o = q_ref[:, 3 * MWIDTH + hd * DH:3 * MWIDTH + (hd + 1) * DH]
        i_col = gc[:, hd:hd + 1]
        b_col = bcol_all[:, HEADS + hd:HEADS + hd + 1]
        i_row = gr[hd:hd + 1, :]
        b_row = brow_all[HEADS + hd:HEADS + hd + 1, :]
        m_prev = m_sc[hd:hd + 1, :][:, :1]
        dm = jnp.where(causal, b_col - b_row + i_row, -jnp.inf)
        m_inter = b_col + m_prev
        m_t = jnp.maximum(m_inter, jnp.max(dm, axis=-1, keepdims=True))
        w_inter = jnp.exp(m_inter - m_t)
        w = jnp.exp(dm - m_t)
        s = _dot_dims(q, k, ((1,), (1,))) * scale * w
        sb = s.astype(BF16)
        c_prev = c_sc[hd]
        n_prev = n_sc[hd]
        num = w_inter * _dot(q, c_prev.astype(BF16)) + _dot(sb, v)
        den = w_inter * _dot(q, n_prev.astype(BF16)) + _dot(sb, ones)
        h = num / jnp.maximum(jnp.abs(den), jnp.exp(-m_t))
        hn = h * lax.rsqrt(jnp.mean(h * h, axis=-1, keepdims=True) + EPS)
        hm_ref[:, hd * DH:(hd + 1) * DH] = (hn * mg_ref[:, hd * DH:(hd + 1) * DH] * _sigmoid(o.astype(F32))).astype(BF16)
        m_last = m_t[lc - 1:lc, :]
        b_last = b_col[lc - 1:lc, :]
        decay = w_inter[lc - 1:lc, :]
        w_last = jnp.exp(b_last - b_col + i_col - m_last) * scale
        kw = (k.astype(F32) * w_last).astype(BF16)
        c_sc[hd] = decay * c_prev + _dot_dims(kw, v, ((0,), (0,)))
        n_sc[hd] = decay * n_prev + _dot_dims(kw, ones, ((0,), (0,)))
        m_sc[hd:hd + 1, :] = jnp.broadcast_to(m_last, (1, LANES))

    @pl.when(ci == pl.num_programs(1) - 1)
    def _():
        c_out[...] = c_sc[...]
        n_out[...] = n_sc[...]
        m_out[...] = m_sc[...]


def _mlstm_prompt(qkvo, gcol, grow, mnorm_g):
    nc = SEQ // MCHUNK
    return pl.pallas_call(
        _mlstm_prompt_body,
        grid=(N_PROMPT, nc),
        in_specs=[pl.BlockSpec((MCHUNK, 4 * MWIDTH), lambda b, c: (b * nc + c, 0)),
                  pl.BlockSpec((MCHUNK, LANES), lambda b, c: (b * nc + c, 0)),
                  pl.BlockSpec((8, MCHUNK), lambda b, c: (0, b * nc + c)),
                  pl.BlockSpec((1, MWIDTH), lambda b, c: (0, 0))],
        out_specs=[pl.BlockSpec((MCHUNK, MWIDTH), lambda b, c: (b * nc + c, 0)),
                   pl.BlockSpec((None, HEADS, DH, DH), lambda b, c: (b, 0, 0, 0)),
                   pl.BlockSpec((None, HEADS, DH, DH), lambda b, c: (b, 0, 0, 0)),
                   pl.BlockSpec((None, 8, LANES), lambda b, c: (b, 0, 0))],
        out_shape=[jax.ShapeDtypeStruct((N_PROMPT * SEQ, MWIDTH), BF16),
                   jax.ShapeDtypeStruct((N_PROMPT, HEADS, DH, DH), F32),
                   jax.ShapeDtypeStruct((N_PROMPT, HEADS, DH, DH), F32),
                   jax.ShapeDtypeStruct((N_PROMPT, 8, LANES), F32)],
        scratch_shapes=[pltpu.VMEM((HEADS, DH, DH), F32),
                        pltpu.VMEM((HEADS, DH, DH), F32),
                        pltpu.VMEM((8, LANES), F32)],
        compiler_params=_cparams("parallel", "arbitrary"),
        name="mlstm_prompt",
    )(qkvo, gcol, grow, mnorm_g)


MSTEP_TOK = 8


def _mlstm_step_body(q_ref, qkt_ref, ig_ref, lf_ref, m0_ref, c0_ref, n0_ref, mg_ref, hm_ref, c_out, n_out, m_out, qc_sc):
    scale = DH ** -0.5
    ig = ig_ref[...]
    m_inter = lf_ref[...] + m0_ref[...]
    m_t = jnp.maximum(m_inter, ig)
    w_inter_all = jnp.exp(m_inter - m_t)
    w_all = jnp.exp(ig - m_t) * scale
    floor_all = jnp.exp(-m_t)
    m_out[...] = m_t
    for hd in range(HEADS):
        sl = slice(hd * DH, (hd + 1) * DH)
        q = q_ref[:, sl].astype(F32)
        k = q_ref[:, MWIDTH + hd * DH:MWIDTH + (hd + 1) * DH].astype(F32)
        v = q_ref[:, 2 * MWIDTH + hd * DH:2 * MWIDTH + (hd + 1) * DH].astype(F32)
        o = q_ref[:, 3 * MWIDTH + hd * DH:3 * MWIDTH + (hd + 1) * DH].astype(F32)
        n0 = n0_ref[:, sl]
        wi = w_inter_all[:, hd:hd + 1]
        wk = w_all[:, hd:hd + 1]
        s = jnp.sum(q * k, axis=-1, keepdims=True) * wk
        qn = jnp.sum(q * n0, axis=-1, keepdims=True)
        for j in range(MSTEP_TOK):
            c0 = c0_ref[j, hd]
            q_col = qkt_ref[hd, :, j:j + 1]
            k_col = qkt_ref[HEADS + hd, :, j:j + 1]
            qc_sc[j:j + 1, sl] = jnp.sum(q_col * c0, axis=0, keepdims=True)
            c_out[j, hd] = wi[j:j + 1, :] * c0 + (wk[j:j + 1, :] * k_col) * v[j:j + 1, :]
        num = wi * qc_sc[:, sl] + s * v
        den = wi * qn + s
        h = num / jnp.maximum(jnp.abs(den), floor_all[:, hd:hd + 1])
        hn = h * lax.rsqrt(jnp.mean(h * h, axis=-1, keepdims=True) + EPS)
        hm_ref[:, sl] = hn * mg_ref[:, sl] * _sigmoid(o)
        n_out[:, sl] = wi * n0 + wk * k


def _mlstm_step(qkvo, qkt, ig, lf, m0, c0, n0, mnorm_g):
    t = qkvo.shape[0]
    tk = MSTEP_TOK
    return pl.pallas_call(
        _mlstm_step_body,
        grid=(t // tk,),
        in_specs=[pl.BlockSpec((tk, 4 * MWIDTH), lambda i: (i, 0)),
                  pl.BlockSpec((None, 2 * HEADS, DH, tk), lambda i: (i, 0, 0, 0)),
                  pl.BlockSpec((tk, LANES), lambda i: (i, 0)),
                  pl.BlockSpec((tk, LANES), lambda i: (i, 0)),
                  pl.BlockSpec((tk, LANES), lambda i: (i, 0)),
                  pl.BlockSpec((tk, HEADS, DH, DH), lambda i: (i, 0, 0, 0)),
                  pl.BlockSpec((tk, MWIDTH), lambda i: (i, 0)),
                  pl.BlockSpec((1, MWIDTH), lambda i: (0, 0))],
        out_specs=[pl.BlockSpec((tk, MWIDTH), lambda i: (i, 0)),
                   pl.BlockSpec((tk, HEADS, DH, DH), lambda i: (i, 0, 0, 0)),
                   pl.BlockSpec((tk, MWIDTH), lambda i: (i, 0)),
                   pl.BlockSpec((tk, LANES), lambda i: (i, 0))],
        out_shape=[jax.ShapeDtypeStruct((t, MWIDTH), F32),
                   jax.ShapeDtypeStruct((t, HEADS, DH, DH), F32),
                   jax.ShapeDtypeStruct((t, MWIDTH), F32),
                   jax.ShapeDtypeStruct((t, LANES), F32)],
        scratch_shapes=[pltpu.VMEM((tk, MWIDTH), F32)],
        compiler_params=_cparams("parallel"),
        name="mlstm_step",
    )(qkvo, qkt, ig, lf, m0, c0, n0, mnorm_g)


def _post_body(x_ref, y5_ref, hm_ref, gg_ref, g1_ref, sc_ref, sh_ref, n2_ref, wglu_ref, wmo_ref, wo_ref, rw_ref, rb_ref,
               cnt0_ref, x1_ref, h2_ref, eid_ref, wgt_ref, rank_ref, cnt_ref, run_sc, *, n_real, n_steps):
    i = pl.program_id(0)

    @pl.when(i == 0)
    def _():
        run_sc[...] = cnt0_ref[...]

    if n_steps > n_real:
        @pl.when(i >= n_real)
        def _():
            h2_ref[...] = jnp.zeros_like(h2_ref)

        pl.when(i < n_real)(functools.partial(
            _post_tile, x_ref, y5_ref, hm_ref, gg_ref, g1_ref, sc_ref, sh_ref, n2_ref, wglu_ref, wmo_ref, wo_ref,
            rw_ref, rb_ref, x1_ref, h2_ref, eid_ref, wgt_ref, rank_ref, run_sc))
    else:
        _post_tile(x_ref, y5_ref, hm_ref, gg_ref, g1_ref, sc_ref, sh_ref, n2_ref, wglu_ref, wmo_ref, wo_ref,
                   rw_ref, rb_ref, x1_ref, h2_ref, eid_ref, wgt_ref, rank_ref, run_sc)
    cnt_ref[...] = run_sc[...]


def _post_tile(x_ref, y5_ref, hm_ref, gg_ref, g1_ref, sc_ref, sh_ref, n2_ref, wglu_ref, wmo_ref, wo_ref, rw_ref, rb_ref,
               x1_ref, h2_ref, eid_ref, wgt_ref, rank_ref, run_sc):
    g5 = _gelu_tanh(y5_ref[...]).astype(BF16)
    glu = _dot(g5, wglu_ref[...])
    br_s5 = glu[:, :D_MODEL] * _sigmoid(glu[:, D_MODEL:])
    br_m = _dot(hm_ref[...].astype(BF16), wmo_ref[...])
    gg = gg_ref[...].astype(F32)
    merged = _sigmoid(gg[:, :D_MODEL]) * br_s5 + _sigmoid(gg[:, D_MODEL:]) * br_m
    x1 = x_ref[...] + g1_ref[...] * _dot(merged.astype(BF16), wo_ref[...])
    x1_ref[...] = x1
    ms = jnp.mean(x1 * x1, axis=-1, keepdims=True)
    h2 = x1 * lax.rsqrt(ms + EPS) * n2_ref[...] * (1.0 + sc_ref[...]) + sh_ref[...]
    h2_ref[...] = h2
    logits = _dot_hp(h2, rw_ref[...]) + rb_ref[...]
    tm = logits.shape[0]
    lane = lax.broadcasted_iota(jnp.int32, logits.shape, 1)
    l = jnp.where(lane < N_EXPERTS, logits, -jnp.inf)
    vals, hots, idxs = [], [], []
    for _ in range(TOP_K):
        mx = jnp.max(l, axis=-1, keepdims=True)
        idx = jnp.min(jnp.where(l == mx, lane, LANES), axis=-1, keepdims=True)
        hot = lane == idx
        vals.append(mx)
        hots.append(hot)
        idxs.append(idx)
        l = jnp.where(hot, -jnp.inf, l)
    ex = [jnp.exp(vk - vals[0]) for vk in vals]
    tot = ex[0] + ex[1] + ex[2] + ex[3]
    chosen = jnp.where(hots[0] | hots[1] | hots[2] | hots[3], 1.0, 0.0)
    r_io = lax.broadcasted_iota(jnp.int32, (tm, tm), 0)
    c_io = lax.broadcasted_iota(jnp.int32, (tm, tm), 1)
    earlier = (c_io < r_io).astype(BF16)
    before = run_sc[...] + _dot(earlier, chosen.astype(BF16))
    eid = jnp.zeros(logits.shape, jnp.int32)
    wgt = jnp.zeros(logits.shape, F32)
    rank = jnp.zeros(logits.shape, F32)
    for k in range(TOP_K):
        eid = jnp.where(lane == k, idxs[k], eid)
        wgt = jnp.where(lane == k, ex[k] / tot, wgt)
        rank = jnp.where(lane == k, jnp.sum(jnp.where(hots[k], before, 0.0), axis=-1, keepdims=True), rank)
    eid_ref[...] = eid
    wgt_ref[...] = wgt
    rank_ref[...] = rank.astype(jnp.int32)
    run_sc[...] += jnp.sum(chosen, axis=0, keepdims=True)


def _post(x2d, y5, hm, gg, mod3, rows_per_mod, tm, n2, wglu, wmo, wo, rw, rb, cnt0, h2_rows):
    t = x2d.shape[0]
    mrows = mod3.shape[1]
    n_real = t // tm
    n_steps = -(-h2_rows // tm)
    last = n_real - 1

    def mod_spec(j):
        return pl.BlockSpec((None, mrows, D_MODEL), lambda i: ((jnp.minimum(i, last) * tm) // rows_per_mod, 0, j))

    const = lambda shape: pl.BlockSpec(shape, lambda i: (0,) * len(shape))
    tile = lambda w: pl.BlockSpec((tm, w), lambda i: (jnp.minimum(i, last), 0))
    return pl.pallas_call(
        functools.partial(_post_body, n_real=n_real, n_steps=n_steps),
        grid=(n_steps,),
        in_specs=[tile(D_MODEL), tile(S5_WIDTH), tile(MWIDTH), tile(2 * D_MODEL),
                  mod_spec(2), mod_spec(4), mod_spec(3), const((1, D_MODEL)),
                  const(wglu.shape), const(wmo.shape), const(wo.shape), const(rw.shape), const(rb.shape),
                  const((1, LANES))],
        out_specs=[tile(D_MODEL), pl.BlockSpec((tm, D_MODEL), lambda i: (i, 0)), tile(LANES), tile(LANES), tile(LANES),
                   const((1, LANES))],
        out_shape=[jax.ShapeDtypeStruct((t, D_MODEL), F32),
                   jax.ShapeDtypeStruct((h2_rows, D_MODEL), F32),
                   jax.ShapeDtypeStruct((t, LANES), jnp.int32),
                   jax.ShapeDtypeStruct((t, LANES), F32),
                   jax.ShapeDtypeStruct((t, LANES), jnp.int32),
                   jax.ShapeDtypeStruct((1, LANES), F32)],
        scratch_shapes=[pltpu.VMEM((1, LANES), F32)],
        compiler_params=_cparams("arbitrary"),
        name="post",
    )(x2d, y5, hm, gg, mod3, mod3, mod3, n2, wglu, wmo, wo, rw, rb, cnt0)


N_TOKENS = N_PROMPT * SEQ + N_SAMPLE
N_ASSIGN = N_TOKENS * TOP_K
MOE_TILE = 256
MOE_TILES = N_ASSIGN // MOE_TILE + N_EXPERTS
N_SLOT = MOE_TILES * MOE_TILE
CMB_TILE = 128


def _moe_body(te_ref, nt_ref, cur_ref, nxt_ref, h2_hbm, wup_ref, bup_ref, wdn_ref, bdn_ref, y_ref,
              xb0, xb1, sem, wup_bf, wdn_bf):
    i = pl.program_id(0)
    nt = nt_ref[0]

    def gather(idx_ref, buf, s):
        for r in range(MOE_TILE):
            pltpu.make_async_copy(h2_hbm.at[pl.ds(idx_ref[0, r], 1)], buf.at[pl.ds(r, 1)], sem.at[s]).start()

    def wait(buf, s):
        pltpu.make_async_copy(h2_hbm.at[pl.ds(0, MOE_TILE)], buf, sem.at[s]).wait()

    @pl.when(i == 0)
    def _():
        gather(cur_ref, xb0, 0)

    new_expert = jnp.logical_or(i == 0, te_ref[i] != te_ref[jnp.maximum(i - 1, 0)])

    @pl.when(jnp.logical_and(new_expert, i < nt))
    def _():
        wup_bf[...] = wup_ref[...].astype(BF16)
        wdn_bf[...] = wdn_ref[...].astype(BF16)

    def step(cur, nxt, s_cur, s_nxt):
        wait(cur, s_cur)
        gather(nxt_ref, nxt, s_nxt)
        gu = _dot(cur[...].astype(BF16), wup_bf[...]) + bup_ref[...]
        g = jnp.minimum(gu[:, :D_FF], SWIGLU_LIMIT)
        up = jnp.clip(gu[:, D_FF:], -SWIGLU_LIMIT, SWIGLU_LIMIT)
        act = (up + 1.0) * g * _sigmoid(SWIGLU_ALPHA * g)
        y_ref[...] = _dot(act.astype(BF16), wdn_bf[...]) + bdn_ref[...]

    live = i < nt
    even = i % 2 == 0
    pl.when(jnp.logical_and(live, even))(functools.partial(step, xb0, xb1, 0, 1))
    pl.when(jnp.logical_and(live, jnp.logical_not(even)))(functools.partial(step, xb1, xb0, 1, 0))

    @pl.when(jnp.logical_not(live))
    def _():
        y_ref[...] = jnp.zeros_like(y_ref)

    first_dead = i == nt
    pl.when(jnp.logical_and(first_dead, even))(functools.partial(wait, xb0, 0))
    pl.when(jnp.logical_and(first_dead, jnp.logical_not(even)))(functools.partial(wait, xb1, 1))
    last_slot = MOE_TILES % 2
    pl.when(jnp.logical_and(live, i == MOE_TILES - 1))(
        functools.partial(wait, xb1 if last_slot else xb0, last_slot))


def _moe(tile_expert, n_tiles, slot_token, h2_all, wup, bup, wdn, bdn):
    idx3 = slot_token.reshape(MOE_TILES, 1, MOE_TILE)
    grid_spec = pltpu.PrefetchScalarGridSpec(
        num_scalar_prefetch=2,
        grid=(MOE_TILES,),
        in_specs=[pl.BlockSpec((None, 1, MOE_TILE), lambda i, te, nt: (i, 0, 0), memory_space=pltpu.SMEM),
                  pl.BlockSpec((None, 1, MOE_TILE), lambda i, te, nt: (jnp.minimum(i + 1, MOE_TILES - 1), 0, 0),
                               memory_space=pltpu.SMEM),
                  pl.BlockSpec(memory_space=pl.ANY),
                  pl.BlockSpec((None, D_MODEL, 2 * D_FF), lambda i, te, nt: (te[i], 0, 0)),
                  pl.BlockSpec((None, 1, 2 * D_FF), lambda i, te, nt: (te[i], 0, 0)),
                  pl.BlockSpec((None, D_FF, D_MODEL), lambda i, te, nt: (te[i], 0, 0)),
                  pl.BlockSpec((None, 1, D_MODEL), lambda i, te, nt: (te[i], 0, 0))],
        out_specs=pl.BlockSpec((MOE_TILE, D_MODEL), lambda i, te, nt: (i, 0)),
        scratch_shapes=[pltpu.VMEM((MOE_TILE, D_MODEL), F32), pltpu.VMEM((MOE_TILE, D_MODEL), F32),
                        pltpu.SemaphoreType.DMA((2,)),
                        pltpu.VMEM((D_MODEL, 2 * D_FF), BF16), pltpu.VMEM((D_FF, D_MODEL), BF16)])
    return pl.pallas_call(
        _moe_body,
        grid_spec=grid_spec,
        out_shape=jax.ShapeDtypeStruct((N_SLOT, D_MODEL), F32),
        compiler_params=_cparams("arbitrary"),
        name="moe",
    )(tile_expert, n_tiles, idx3, idx3, h2_all, wup, bup, wdn, bdn)


def _combine_body(cur_ref, nxt_ref, wgt_ref, x1_ref, g2_ref, fg_ref, y_hbm, o_ref, yb0, yb1, sem, *, n_steps):
    i = pl.program_id(0)
    tm = CMB_TILE

    def gather(idx_ref, buf, s):
        for r in range(tm):
            for k in range(TOP_K):
                pltpu.make_async_copy(y_hbm.at[pl.ds(idx_ref[0, r * TOP_K + k], 1)],
                                      buf.at[pl.ds(k * tm + r, 1)], sem.at[s]).start()

    def wait(buf, s):
        pltpu.make_async_copy(y_hbm.at[pl.ds(0, TOP_K * tm)], buf, sem.at[s]).wait()

    @pl.when(i == 0)
    def _():
        gather(cur_ref, yb0, 0)

    def step(cur, nxt, s_cur, s_nxt):
        wait(cur, s_cur)
        gather(nxt_ref, nxt, s_nxt)
        wgt = wgt_ref[...]
        acc = wgt[:, 0:1] * cur[0:tm, :]
        for k in range(1, TOP_K):
            acc = acc + wgt[:, k:k + 1] * cur[k * tm:(k + 1) * tm, :]
        xo = x1_ref[...] + g2_ref[...] * acc
        ms = jnp.mean(xo * xo, axis=-1, keepdims=True)
        o_ref[...] = xo * lax.rsqrt(ms + EPS) * fg_ref[...]

    pl.when(i % 2 == 0)(functools.partial(step, yb0, yb1, 0, 1))
    pl.when(i % 2 == 1)(functools.partial(step, yb1, yb0, 1, 0))
    last_slot = n_steps % 2
    pl.when(i == n_steps - 1)(functools.partial(wait, yb1 if last_slot else yb0, last_slot))


def _combine(pos, wgt, x1, mod3, rows_per_mod, fg, y_slots):
    t = x1.shape[0]
    tm = CMB_TILE
    n = t // tm
    mrows = mod3.shape[1]
    assert mrows in (1, tm)
    pos3 = pos.reshape(n, 1, tm * TOP_K)
    return pl.pallas_call(
        functools.partial(_combine_body, n_steps=n),
        grid=(n,),
        in_specs=[pl.BlockSpec((None, 1, tm * TOP_K), lambda i: (i, 0, 0), memory_space=pltpu.SMEM),
                  pl.BlockSpec((None, 1, tm * TOP_K), lambda i: (jnp.minimum(i + 1, n - 1), 0, 0),
                               memory_space=pltpu.SMEM),
                  pl.BlockSpec((tm, LANES), lambda i: (i, 0)),
                  pl.BlockSpec((tm, D_MODEL), lambda i: (i, 0)),
                  pl.BlockSpec((None, mrows, D_MODEL), lambda i: ((i * tm) // rows_per_mod, 0, 5)),
                  pl.BlockSpec((1, D_MODEL), lambda i: (0, 0)),
                  pl.BlockSpec(memory_space=pl.ANY)],
        out_specs=pl.BlockSpec((tm, D_MODEL), lambda i: (i, 0)),
        out_shape=jax.ShapeDtypeStruct((t, D_MODEL), F32),
        scratch_shapes=[pltpu.VMEM((TOP_K * tm, D_MODEL), F32), pltpu.VMEM((TOP_K * tm, D_MODEL), F32),
                        pltpu.SemaphoreType.DMA((2,))],
        compiler_params=_cparams("arbitrary"),
        name="combine",
    )(pos3, pos3, wgt, x1, mod3, fg, y_slots)


def _routing(eid_p, rank_p, eid_s, rank_s, counts):
    cnt = counts[0, :N_EXPERTS].astype(jnp.int32)
    ntile = (cnt + MOE_TILE - 1) // MOE_TILE
    tile_end = jnp.cumsum(ntile)
    poff = (tile_end - ntile) * MOE_TILE
    total = tile_end[-1]
    j = jnp.arange(MOE_TILES, dtype=jnp.int32)
    te = jnp.sum((j[:, None] >= tile_end[None, :]).astype(jnp.int32), axis=1)
    te_last = jnp.sum(((total - 1) >= tile_end).astype(jnp.int32))
    tile_expert = jnp.where(j < total, te, te_last).astype(jnp.int32)
    ex = jnp.arange(N_EXPERTS, dtype=jnp.int32)

    def pos_of(eid, rank):
        e = eid[:, :TOP_K]
        off = jnp.sum(jnp.where(e[:, :, None] == ex[None, None, :], poff[None, None, :], 0), axis=-1)
        return off + rank[:, :TOP_K]

    pos_p = pos_of(eid_p, rank_p)
    pos_s = pos_of(eid_s, rank_s)
    tok = jnp.concatenate([jnp.repeat(jnp.arange(N_PROMPT * SEQ, dtype=jnp.int32), TOP_K),
                           jnp.repeat(N_PROMPT * SEQ + jnp.arange(N_SAMPLE, dtype=jnp.int32), TOP_K)])
    pos_all = jnp.concatenate([pos_p.reshape(-1), pos_s.reshape(-1)])
    slot_token = jnp.zeros((N_SLOT,), jnp.int32).at[pos_all].set(tok, unique_indices=True)
    return tile_expert, total.reshape(1).astype(jnp.int32), slot_token, pos_p, pos_s


def _unpack_s5_state(x):
    n = x.shape[1]
    z = x.reshape(N_SUPER, n, 2, S5_SUPER, S5_STATE).transpose(2, 1, 0, 3, 4).reshape(2, n, S5_GROUPS, S5_STATE)
    return z[0], z[1]


def _pack_s5_state(re, im):
    n = re.shape[0]
    z = jnp.stack([re, im], axis=0).reshape(2, n, N_SUPER, S5_SUPER, S5_STATE)
    return z.transpose(2, 1, 0, 3, 4).reshape(N_SUPER, n, SUPER_STATE)


def kernel(x_prompt, x_sample, c_prompt, c_sample, state_s5_re, state_s5_im, state_mlstm_C, state_mlstm_n, state_mlstm_m, norm1_g, norm2_g, final_norm_g, w_ada, b_ada, w_in, s5_lambda_re, s5_lambda_im, s5_log_dt, s5_B_re, s5_B_im, s5_C_re, s5_C_im, s5_D, s5_w_glu, mlstm_b_i, mlstm_b_f, mlstm_norm_g, mlstm_w_out, w_out, router_w, router_b, expert_w_up, expert_b_up, expert_w_down, expert_b_down):
    assert w_in.shape[0] == 1, "single layer"
    tp = N_PROMPT * SEQ
    xp = x_prompt.reshape(tp, D_MODEL).astype(F32)
    xs = x_sample.reshape(N_SAMPLE, D_MODEL).astype(F32)

    w = w_in[0]
    c0 = S5_WIDTH
    c1 = c0 + 4 * MWIDTH
    c2 = c1 + 2 * HEADS
    wu = w[:, :c0].astype(BF16)
    wq = w[:, c0:c1].astype(BF16)
    wif = jnp.pad(w[:, c1:c2].astype(F32), ((0, 0), (0, LANES - 2 * HEADS)))
    wift = w[:, c1:c2].astype(F32).T
    wg = w[:, c2:].astype(BF16)
    bvec = jnp.concatenate([mlstm_b_i[0], mlstm_b_f[0]]).astype(F32)
    bif = jnp.pad(bvec, (0, LANES - 2 * HEADS)).reshape(1, LANES)
    bift = bvec.reshape(2 * HEADS, 1)
    wparts = (wu, wq, wg, wif, wift, bif, bift)
    g1 = norm1_g[0].reshape(1, D_MODEL).astype(F32)
    n2 = norm2_g[0].reshape(1, D_MODEL).astype(F32)
    fg = final_norm_g.reshape(1, D_MODEL).astype(F32)
    mg = mlstm_norm_g[0].reshape(1, MWIDTH).astype(F32)
    wglu = s5_w_glu[0].astype(BF16)
    wmo = mlstm_w_out[0].astype(BF16)
    wo = w_out[0].astype(BF16)
    rw = jnp.pad(router_w[0].astype(F32), ((0, 0), (0, LANES - N_EXPERTS)))
    rb = jnp.pad(router_b[0].astype(F32), (0, LANES - N_EXPERTS)).reshape(1, LANES)
    wup = expert_w_up[0].astype(F32)
    wdn = expert_w_down[0].astype(F32)
    bup = expert_b_up[0].astype(F32).reshape(N_EXPERTS, 1, 2 * D_FF)
    bdn = expert_b_down[0].astype(F32).reshape(N_EXPERTS, 1, D_MODEL)
    kpair, fmat, emat, a16, a1, k0, e0, f1 = _s5_tables(
        s5_lambda_re[0], s5_lambda_im[0], s5_log_dt[0], s5_B_re[0], s5_B_im[0], s5_C_re[0], s5_C_im[0], s5_D[0])

    c_all = jnp.concatenate([c_prompt, c_sample], axis=0).astype(F32)
    mod = _adaln(c_all, w_ada[0].astype(F32), b_ada[0].astype(F32))
    mod_p = mod[:N_PROMPT].reshape(N_PROMPT, 1, 6 * D_MODEL)
    mod_s = mod[N_PROMPT:].reshape(1, N_SAMPLE, 6 * D_MODEL)

    u, qkvo, gg, gcol, grow = _inproj(xp, mod_p, SEQ, 512, g1, wparts)
    y5, xend = _s5_prompt(u, kpair, fmat, emat, a16)
    hm, c_p, n_p, m_p = _mlstm_prompt(qkvo, gcol, grow, mg)
    cnt0 = jnp.zeros((1, LANES), F32)
    x1, h2_all, eid_p, wgt_p, rank_p, cnt_p = _post(xp, y5, hm, gg, mod_p, SEQ, 512, n2, wglu, wmo, wo, rw, rb,
                                                    cnt0, N_TOKENS)
    p_re, p_im = _unpack_s5_state(xend.reshape(N_SUPER, N_PROMPT, SUPER_STATE))
    n_p = n_p[..., 0]
    m_p = m_p[:, :HEADS, 0]

    us, qs, ggs, gcs, _ = _inproj(xs, mod_s, N_SAMPLE, N_SAMPLE, g1, wparts)
    x0 = _pack_s5_state(state_s5_re[0].astype(F32), state_s5_im[0].astype(F32))
    y5s, xns = _s5_step(us, x0, k0, e0, f1, a1)
    s_re, s_im = _unpack_s5_state(xns)
    qkt = qs[:, :2 * MWIDTH].astype(F32).reshape(N_SAMPLE // MSTEP_TOK, MSTEP_TOK, 2 * HEADS, DH).transpose(0, 2, 3, 1)
    pad_heads = lambda a: jnp.pad(a, ((0, 0), (0, LANES - HEADS)))
    m0 = pad_heads(state_mlstm_m[0].astype(F32))
    hms, c_s, n_s, m_s = _mlstm_step(qs, qkt, pad_heads(gcs[:, :HEADS]), pad_heads(gcs[:, HEADS:2 * HEADS]), m0,
                                     state_mlstm_C[0].astype(F32),
                                     state_mlstm_n[0].astype(F32).reshape(N_SAMPLE, MWIDTH), mg)
    x1s, h2s, eid_s, wgt_s, rank_s, cnt_all = _post(xs, y5s, hms, ggs, mod_s, N_SAMPLE, N_SAMPLE, n2, wglu, wmo, wo, rw, rb,
                                                    cnt_p, N_SAMPLE)

    h2_all = lax.dynamic_update_slice(h2_all, h2s, (tp, 0))
    tile_expert, n_tiles, slot_token, pos_p, pos_s = _routing(eid_p, rank_p, eid_s, rank_s, cnt_all)
    y_slots = _moe(tile_expert, n_tiles, slot_token, h2_all, wup, bup, wdn, bdn)
    y_p = _combine(pos_p, wgt_p, x1, mod_p, SEQ, fg, y_slots)
    y_s = _combine(pos_s, wgt_s, x1s, mod_s, N_SAMPLE, fg, y_slots)

    return (y_p.reshape(N_PROMPT, SEQ, D_MODEL).astype(x_prompt.dtype),
            y_s.reshape(N_SAMPLE, 1, D_MODEL).astype(x_sample.dtype),
            p_re[None], p_im[None], c_p[None], n_p[None], m_p[None],
            s_re[None], s_im[None], c_s[None],
            n_s.reshape(1, N_SAMPLE, HEADS, DH), m_s[:, :HEADS][None])
```

```python
import functools
import math

import jax
import jax.numpy as jnp
from jax import lax
from jax.experimental import pallas as pl
from jax.experimental.pallas import tpu as pltpu

F32 = jnp.float32
BF16 = jnp.bfloat16

D_MODEL = 1024
SEQ = 2048
N_PROMPT = 8
N_SAMPLE = 128
S5_WIDTH = 512
S5_GROUP = 16
S5_GROUPS = 32
S5_STATE = 64
HEADS = 4
DH = 128
MWIDTH = HEADS * DH
N_EXPERTS = 32
TOP_K = 4
D_FF = 1024
SWIGLU_LIMIT = 7.0
SWIGLU_ALPHA = 1.702
EPS = 1e-6

LANES = 128
S5_CHUNK = 16
S5_SUPER = LANES // S5_GROUP
N_SUPER = S5_GROUPS // S5_SUPER
SUPER_STATE = 2 * S5_SUPER * S5_STATE
MCHUNK = 128
VMEM_LIMIT = 56 * 1024 * 1024


def _cparams(*sem):
    return pltpu.CompilerParams(dimension_semantics=sem, vmem_limit_bytes=VMEM_LIMIT)


def _dot(a, b):
    return jnp.dot(a, b, preferred_element_type=F32)


def _dot_dims(a, b, dims):
    return lax.dot_general(a, b, (dims, ((), ())), preferred_element_type=F32)


def _split_bf16(a):
    hi = a.astype(BF16)
    lo = (a - hi.astype(F32)).astype(BF16)
    return hi, lo


def _dot_hp(a, b, dims=((1,), (0,))):
    ah, al = _split_bf16(a)
    bh, bl = _split_bf16(b)
    return _dot_dims(ah, bh, dims) + (_dot_dims(al, bh, dims) + _dot_dims(ah, bl, dims))


def _dot_exact_rhs(a, b_exact, dims=((1,), (0,))):
    ah, al = _split_bf16(a)
    return _dot_dims(ah, b_exact, dims) + _dot_dims(al, b_exact, dims)


def _log_sigmoid(x):
    return -(jnp.maximum(-x, 0.0) + jnp.log1p(jnp.exp(-jnp.abs(x))))


def _sigmoid(x):
    return 1.0 / (1.0 + jnp.exp(-x))


def _gelu_tanh(x):
    c = math.sqrt(2.0 / math.pi)
    return 0.5 * x * (1.0 + jnp.tanh(c * (x + 0.044715 * (x * x * x))))


def _adaln_body(c_ref, w_ref, b_ref, o_ref):
    c = c_ref[...]
    s = c * _sigmoid(c)
    o_ref[...] = _dot_hp(s, w_ref[...]) + b_ref[...]


def _adaln(c_all, w_ada, b_ada):
    n = c_all.shape[0]
    tn = 1024
    return pl.pallas_call(
        _adaln_body,
        grid=(6 * D_MODEL // tn,),
        in_specs=[pl.BlockSpec((n, D_MODEL), lambda j: (0, 0)),
                  pl.BlockSpec((D_MODEL, tn), lambda j: (0, j)),
                  pl.BlockSpec((1, tn), lambda j: (0, j))],
        out_specs=pl.BlockSpec((n, tn), lambda j: (0, j)),
        out_shape=jax.ShapeDtypeStruct((n, 6 * D_MODEL), F32),
        compiler_params=_cparams("parallel"),
        name="adaln",
    )(c_all, w_ada, b_ada.reshape(1, -1))


def _inproj_body(x_ref, g1_ref, sc_ref, sh_ref, wu_ref, wq_ref, wg_ref, wif_ref, wift_ref, bif_ref, bift_ref,
                 u_ref, q_ref, gg_ref, if_ref, ift_ref):
    x = x_ref[...]
    ms = jnp.mean(x * x, axis=-1, keepdims=True)
    h = x * lax.rsqrt(ms + EPS) * g1_ref[...] * (1.0 + sc_ref[...]) + sh_ref[...]
    hb = h.astype(BF16)
    u_ref[...] = _dot(hb, wu_ref[...])
    q_ref[...] = _dot(hb, wq_ref[...]).astype(BF16)
    gg_ref[...] = _dot(hb, wg_ref[...]).astype(BF16)
    gc = _dot_hp(h, wif_ref[...]) + bif_ref[...]
    lane = lax.broadcasted_iota(jnp.int32, gc.shape, 1)
    if_ref[...] = jnp.where(lane < HEADS, gc, _log_sigmoid(gc))
    gr = _dot_hp(wift_ref[...], h, ((1,), (1,))) + bift_ref[...]
    sub = lax.broadcasted_iota(jnp.int32, gr.shape, 0)
    ift_ref[...] = jnp.where(sub < HEADS, gr, _log_sigmoid(gr))


def _inproj(x2d, mod3, rows_per_mod, tm, g1, wparts):
    t = x2d.shape[0]
    wu, wq, wg, wif, wift, bif, bift = wparts
    mrows = mod3.shape[1]

    def mod_spec(j):
        return pl.BlockSpec((None, mrows, D_MODEL), lambda i: ((i * tm) // rows_per_mod, 0, j))

    const = lambda shape: pl.BlockSpec(shape, lambda i: (0,) * len(shape), pipeline_mode=pl.Buffered(1))
    return pl.pallas_call(
        _inproj_body,
        grid=(t // tm,),
        in_specs=[pl.BlockSpec((tm, D_MODEL), lambda i: (i, 0)),
                  const((1, D_MODEL)), mod_spec(1), mod_spec(0),
                  const(wu.shape), const(wq.shape), const(wg.shape), const(wif.shape), const(wift.shape),
                  const(bif.shape), const(bift.shape)],
        out_specs=[pl.BlockSpec((tm, S5_WIDTH), lambda i: (i, 0)),
                   pl.BlockSpec((tm, 4 * MWIDTH), lambda i: (i, 0)),
                   pl.BlockSpec((tm, 2 * D_MODEL), lambda i: (i, 0)),
                   pl.BlockSpec((tm, LANES), lambda i: (i, 0)),
                   pl.BlockSpec((8, tm), lambda i: (0, i))],
        out_shape=[jax.ShapeDtypeStruct((t, S5_WIDTH), F32),
                   jax.ShapeDtypeStruct((t, 4 * MWIDTH), BF16),
                   jax.ShapeDtypeStruct((t, 2 * D_MODEL), BF16),
                   jax.ShapeDtypeStruct((t, LANES), F32),
                   jax.ShapeDtypeStruct((8, t), F32)],
        compiler_params=_cparams("parallel"),
        name="inproj",
    )(x2d, g1, mod3, mod3, wu, wq, wg, wif, wift, bif, bift)


def _s5_tables(lam_re, lam_im, log_dt, b_re, b_im, c_re, c_im, d_s5):
    hi = lax.Precision.HIGHEST
    dt = jnp.exp(log_dt.astype(F32))[:, None]
    lr, li = lam_re.astype(F32), lam_im.astype(F32)
    dpow = jnp.arange(S5_CHUNK + 1, dtype=F32)[:, None, None]
    mag = jnp.exp(dpow * (lr * dt))
    pw_re, pw_im = mag * jnp.cos(dpow * (li * dt)), mag * jnp.sin(dpow * (li * dt))
    ab_re, ab_im = pw_re[1], pw_im[1]
    den = lr * lr + li * li
    nr, ni = ab_re - 1.0, ab_im
    coef_re = (nr * lr + ni * li) / den
    coef_im = (ni * lr - nr * li) / den
    br, bi = b_re.astype(F32), b_im.astype(F32)
    bb_re = coef_re[..., None] * br - coef_im[..., None] * bi
    bb_im = coef_re[..., None] * bi + coef_im[..., None] * br
    cr, ci = c_re.astype(F32), c_im.astype(F32)
    cl_re = cr[None] * pw_re[:, :, None, :] - ci[None] * pw_im[:, :, None, :]
    cl_im = cr[None] * pw_im[:, :, None, :] + ci[None] * pw_re[:, :, None, :]
    kd = (jnp.einsum('dgop,gpi->dgio', cl_re[:S5_CHUNK], bb_re, precision=hi)
          - jnp.einsum('dgop,gpi->dgio', cl_im[:S5_CHUNK], bb_im, precision=hi))
    kd = kd.at[0].add(d_s5.astype(F32)[:, :, None] * jnp.eye(S5_GROUP, dtype=F32)[None])
    eye = jnp.eye(S5_SUPER, dtype=F32)
    kblk = jnp.einsum('dsaio,ab->dsaibo', kd.reshape(S5_CHUNK, N_SUPER, S5_SUPER, S5_GROUP, S5_GROUP), eye)
    kblk = kblk.reshape(S5_CHUNK, N_SUPER, LANES, LANES)
    kext = jnp.concatenate([jnp.zeros_like(kblk[:1]), kblk], axis=0)
    kpair = jnp.concatenate([kext[S5_CHUNK - 1::-1], kext[S5_CHUNK:0:-1]], axis=-1)
    kpair = kpair.transpose(1, 0, 2, 3).reshape(N_SUPER, S5_CHUNK * LANES, 2 * LANES)
    rp_re, rp_im = pw_re[S5_CHUNK - 1::-1], pw_im[S5_CHUNK - 1::-1]
    f_re = rp_re[..., None] * bb_re[None] - rp_im[..., None] * bb_im[None]
    f_im = rp_re[..., None] * bb_im[None] + rp_im[..., None] * bb_re[None]
    fcat = jnp.stack([f_re, f_im], axis=2).transpose(0, 1, 4, 2, 3)
    fr = fcat.reshape(S5_CHUNK, N_SUPER, S5_SUPER, S5_GROUP, 2, S5_STATE)
    fmat = jnp.einsum('isahcp,ab->siahcbp', fr, eye).reshape(N_SUPER, S5_CHUNK * LANES, SUPER_STATE)
    ecat = jnp.stack([cl_re[1:], -cl_im[1:]], axis=0)
    er = ecat.transpose(2, 0, 4, 1, 3).reshape(N_SUPER, S5_SUPER, 2, S5_STATE, S5_CHUNK, S5_GROUP)
    emat = jnp.einsum('sacpjo,ab->scapjbo', er, eye).reshape(N_SUPER, SUPER_STATE, S5_CHUNK * LANES)

    def lay(re, im):
        z = jnp.stack([re, im], axis=0).reshape(2, N_SUPER, S5_SUPER * S5_STATE)
        return z.transpose(1, 0, 2).reshape(N_SUPER, 1, SUPER_STATE)

    a16 = lay(pw_re[S5_CHUNK], pw_im[S5_CHUNK])
    a1 = lay(ab_re, ab_im)
    k0 = kblk[0].astype(BF16)
    e0 = emat[:, :, :LANES].astype(BF16)
    f1 = fmat[:, (S5_CHUNK - 1) * LANES:, :].astype(BF16)
    return kpair.astype(BF16), fmat.astype(BF16), emat.astype(BF16), a16, a1, k0, e0, f1


S5_NB = 4


def _s5_prompt_body(u_ref, kp_ref, f_ref, e_ref, a_ref, y_ref, xe_ref, ucat, v_sc, xp_sc):
    n = SEQ // S5_CHUNK
    half = SUPER_STATE // 2
    for bl in range(S5_NB):
        for j in range(S5_CHUNK):
            ucat[bl * n:(bl + 1) * n, j * LANES:(j + 1) * LANES] = (
                u_ref[pl.ds(bl * SEQ + j, n, stride=S5_CHUNK), :].astype(BF16))
    v_sc[...] = _dot(ucat[...], f_ref[...])
    a_r, a_i = a_ref[:, :half], a_ref[:, half:]
    xr = [jnp.zeros((1, half), F32) for _ in range(S5_NB)]
    xi = [jnp.zeros((1, half), F32) for _ in range(S5_NB)]
    for c in range(n):
        for bl in range(S5_NB):
            row = bl * n + c
            xp_sc[row:row + 1, :half] = xr[bl]
            xp_sc[row:row + 1, half:] = xi[bl]
            vr, vi = v_sc[row:row + 1, :half], v_sc[row:row + 1, half:]
            xr[bl], xi[bl] = a_r * xr[bl] - a_i * xi[bl] + vr, a_r * xi[bl] + a_i * xr[bl] + vi
    for bl in range(S5_NB):
        xe_ref[bl:bl + 1, :half] = xr[bl]
        xe_ref[bl:bl + 1, half:] = xi[bl]
    ye = _dot(xp_sc[...].astype(BF16), e_ref[...])
    for m in range(S5_CHUNK // 2):
        rows = (2 * m + 2) * LANES
        yp = _dot(ucat[:, :rows], kp_ref[(S5_CHUNK - 2 - 2 * m) * LANES:, :]) + ye[:, 2 * m * LANES:(2 * m + 2) * LANES]
        for bl in range(S5_NB):
            y_ref[pl.ds(bl * SEQ + 2 * m, n, stride=S5_CHUNK), :] = yp[bl * n:(bl + 1) * n, :LANES]
            y_ref[pl.ds(bl * SEQ + 2 * m + 1, n, stride=S5_CHUNK), :] = yp[bl * n:(bl + 1) * n, LANES:]


def _s5_prompt(u, kpair, fmat, emat, a16):
    n = S5_NB * SEQ // S5_CHUNK
    return pl.pallas_call(
        _s5_prompt_body,
        grid=(N_SUPER, N_PROMPT // S5_NB),
        in_specs=[pl.BlockSpec((S5_NB * SEQ, LANES), lambda s, b: (b, s)),
                  pl.BlockSpec((None,) + kpair.shape[1:], lambda s, b: (s, 0, 0)),
                  pl.BlockSpec((None,) + fmat.shape[1:], lambda s, b: (s, 0, 0)),
                  pl.BlockSpec((None,) + emat.shape[1:], lambda s, b: (s, 0, 0)),
                  pl.BlockSpec((None, 1, SUPER_STATE), lambda s, b: (s, 0, 0))],
        out_specs=[pl.BlockSpec((S5_NB * SEQ, LANES), lambda s, b: (b, s)),
                   pl.BlockSpec((None, None, S5_NB, SUPER_STATE), lambda s, b: (s, b, 0, 0))],
        out_shape=[jax.ShapeDtypeStruct(u.shape, F32),
                   jax.ShapeDtypeStruct((N_SUPER, N_PROMPT // S5_NB, S5_NB, SUPER_STATE), F32)],
        scratch_shapes=[pltpu.VMEM((n, S5_CHUNK * LANES), BF16),
                        pltpu.VMEM((n, SUPER_STATE), F32),
                        pltpu.VMEM((n, SUPER_STATE), F32)],
        compiler_params=_cparams("parallel", "parallel"),
        name="s5_prompt",
    )(u, kpair, fmat, emat, a16)


def _s5_step_body(u_ref, x0_ref, k0_ref, e0_ref, f1_ref, a_ref, y_ref, xn_ref):
    half = SUPER_STATE // 2
    ub = u_ref[...].astype(BF16)
    x0 = x0_ref[...]
    bu = _dot(ub, f1_ref[...])
    a_r, a_i = a_ref[:, :half], a_ref[:, half:]
    x0r, x0i = x0[:, :half], x0[:, half:]
    xn_ref[:, :half] = a_r * x0r - a_i * x0i + bu[:, :half]
    xn_ref[:, half:] = a_r * x0i + a_i * x0r + bu[:, half:]
    y_ref[...] = _dot(x0.astype(BF16), e0_ref[...]) + _dot(ub, k0_ref[...])


def _s5_step(u, x0, k0, e0, f1, a1):
    t = u.shape[0]
    per = lambda a: pl.BlockSpec((None,) + a.shape[1:], lambda s: (s,) + (0,) * (a.ndim - 1))
    return pl.pallas_call(
        _s5_step_body,
        grid=(N_SUPER,),
        in_specs=[pl.BlockSpec((t, LANES), lambda s: (0, s)), per(x0), per(k0), per(e0), per(f1), per(a1)],
        out_specs=[pl.BlockSpec((t, LANES), lambda s: (0, s)), per(x0)],
        out_shape=[jax.ShapeDtypeStruct(u.shape, F32), jax.ShapeDtypeStruct(x0.shape, F32)],
        compiler_params=_cparams("parallel"),
        name="s5_step",
    )(u, x0, k0, e0, f1, a1)


def _mlstm_prompt_body(q_ref, gc_ref, gr_ref, mg_ref, hm_ref, c_out, n_out, m_out, c_sc, n_sc, m_sc):
    lc = MCHUNK
    ci = pl.program_id(1)

    @pl.when(ci == 0)
    def _():
        c_sc[...] = jnp.zeros_like(c_sc)
        n_sc[...] = jnp.zeros_like(n_sc)
        m_sc[...] = jnp.zeros_like(m_sc)

    scale = DH ** -0.5
    row = lax.broadcasted_iota(jnp.int32, (lc, lc), 0)
    col = lax.broadcasted_iota(jnp.int32, (lc, lc), 1)
    causal = col <= row
    tri = causal.astype(BF16)
    ones = jnp.ones((lc, DH), BF16)
    gc = gc_ref[...]
    gr = gr_ref[...]
    gc_hi, gc_lo = _split_bf16(gc)
    bcol_all = _dot(tri, gc_hi) + _dot(tri, gc_lo)
    brow_all = _dot_exact_rhs(gr, tri, ((1,), (1,)))
    for hd in range(HEADS):
        q = q_ref[:, hd * DH:(hd + 1) * DH]
        k = q_ref[:, MWIDTH + hd * DH:MWIDTH + (hd + 1) * DH]
        v = q_ref[:, 2 * MWIDTH + hd * DH:2 * MWIDTH + (hd + 1) * DH]
        o = q_ref[:, 3 * MWIDTH + hd * DH:3 * MWIDTH + (hd + 1) * DH]
        i_col = gc[:, hd:hd + 1]
        b_col = bcol_all[:, HEADS + hd:HEADS + hd + 1]
        i_row = gr[hd:hd + 1, :]
        b_row = brow_all[HEADS + hd:HEADS + hd + 1, :]
        m_prev = m_sc[hd:hd + 1, :][:, :1]
        dm = jnp.where(causal, b_col - b_row + i_row, -jnp.inf)
        m_inter = b_col + m_prev
        m_t = jnp.maximum(m_inter, jnp.max(dm, axis=-1, keepdims=True))
        w_inter = jnp.exp(m_inter - m_t)
        w = jnp.exp(dm - m_t)
        s = _dot_dims(q, k, ((1,), (1,))) * scale * w
        sb = s.astype(BF16)
        c_prev = c_sc[hd]
        n_prev = n_sc[hd]
        num = w_inter * _dot(q, c_prev.astype(BF16)) + _dot(sb, v)
        den = w_inter * _dot(q, n_prev.astype(BF16)) + _dot(sb, ones)
        h = num / jnp.maximum(jnp.abs(den), jnp.exp(-m_t))
        hn = h * lax.rsqrt(jnp.mean(h * h, axis=-1, keepdims=True) + EPS)
        hm_ref[:, hd * DH:(hd + 1) * DH] = (hn * mg_ref[:, hd * DH:(hd + 1) * DH] * _sigmoid(o.astype(F32))).astype(BF16)
        m_last = m_t[lc - 1:lc, :]
        b_last = b_col[lc - 1:lc, :]
        decay = w_inter[lc - 1:lc, :]
        w_last = jnp.exp(b_last - b_col + i_col - m_last) * scale
        kw = (k.astype(F32) * w_last).astype(BF16)
        c_sc[hd] = decay * c_prev + _dot_dims(kw, v, ((0,), (0,)))
        n_sc[hd] = decay * n_prev + _dot_dims(kw, ones, ((0,), (0,)))
        m_sc[hd:hd + 1, :] = jnp.broadcast_to(m_last, (1, LANES))

    @pl.when(ci == pl.num_programs(1) - 1)
    def _():
        c_out[...] = c_sc[...]
        n_out[...] = n_sc[...]
        m_out[...] = m_sc[...]


def _mlstm_prompt(qkvo, gcol, grow, mnorm_g):
    nc = SEQ // MCHUNK
    return pl.pallas_call(
        _mlstm_prompt_body,
        grid=(N_PROMPT, nc),
        in_specs=[pl.BlockSpec((MCHUNK, 4 * MWIDTH), lambda b, c: (b * nc + c, 0)),
                  pl.BlockSpec((MCHUNK, LANES), lambda b, c: (b * nc + c, 0)),
                  pl.BlockSpec((8, MCHUNK), lambda b, c: (0, b * nc + c)),
                  pl.BlockSpec((1, MWIDTH), lambda b, c: (0, 0))],
        out_specs=[pl.BlockSpec((MCHUNK, MWIDTH), lambda b, c: (b * nc + c, 0)),
                   pl.BlockSpec((None, HEADS, DH, DH), lambda b, c: (b, 0, 0, 0)),
                   pl.BlockSpec((None, HEADS, DH, DH), lambda b, c: (b, 0, 0, 0)),
                   pl.BlockSpec((None, 8, LANES), lambda b, c: (b, 0, 0))],
        out_shape=[jax.ShapeDtypeStruct((N_PROMPT * SEQ, MWIDTH), BF16),
                   jax.ShapeDtypeStruct((N_PROMPT, HEADS, DH, DH), F32),
                   jax.ShapeDtypeStruct((N_PROMPT, HEADS, DH, DH), F32),
                   jax.ShapeDtypeStruct((N_PROMPT, 8, LANES), F32)],
        scratch_shapes=[pltpu.VMEM((HEADS, DH, DH), F32),
                        pltpu.VMEM((HEADS, DH, DH), F32),
                        pltpu.VMEM((8, LANES), F32)],
        compiler_params=_cparams("parallel", "arbitrary"),
        name="mlstm_prompt",
    )(qkvo, gcol, grow, mnorm_g)


MSTEP_TOK = 8


def _mlstm_step_body(q_ref, qkt_ref, ig_ref, lf_ref, m0_ref, c0_ref, n0_ref, mg_ref, hm_ref, c_out, n_out, m_out, qc_sc):
    scale = DH ** -0.5
    ig = ig_ref[...]
    m_inter = lf_ref[...] + m0_ref[...]
    m_t = jnp.maximum(m_inter, ig)
    w_inter_all = jnp.exp(m_inter - m_t)
    w_all = jnp.exp(ig - m_t) * scale
    floor_all = jnp.exp(-m_t)
    m_out[...] = m_t
    for hd in range(HEADS):
        sl = slice(hd * DH, (hd + 1) * DH)
        q = q_ref[:, sl].astype(F32)
        k = q_ref[:, MWIDTH + hd * DH:MWIDTH + (hd + 1) * DH].astype(F32)
        v = q_ref[:, 2 * MWIDTH + hd * DH:2 * MWIDTH + (hd + 1) * DH].astype(F32)
        o = q_ref[:, 3 * MWIDTH + hd * DH:3 * MWIDTH + (hd + 1) * DH].astype(F32)
        n0 = n0_ref[:, sl]
        wi = w_inter_all[:, hd:hd + 1]
        wk = w_all[:, hd:hd + 1]
        s = jnp.sum(q * k, axis=-1, keepdims=True) * wk
        qn = jnp.sum(q * n0, axis=-1, keepdims=True)
        for j in range(MSTEP_TOK):
            c0 = c0_ref[j, hd]
            q_col = qkt_ref[hd, :, j:j + 1]
            k_col = qkt_ref[HEADS + hd, :, j:j + 1]
            qc_sc[j:j + 1, sl] = jnp.sum(q_col * c0, axis=0, keepdims=True)
            c_out[j, hd] = wi[j:j + 1, :] * c0 + (wk[j:j + 1, :] * k_col) * v[j:j + 1, :]
        num = wi * qc_sc[:, sl] + s * v
        den = wi * qn + s
        h = num / jnp.maximum(jnp.abs(den), floor_all[:, hd:hd + 1])
        hn = h * lax.rsqrt(jnp.mean(h * h, axis=-1, keepdims=True) + EPS)
        hm_ref[:, sl] = hn * mg_ref[:, sl] * _sigmoid(o)
        n_out[:, sl] = wi * n0 + wk * k


def _mlstm_step(qkvo, qkt, ig, lf, m0, c0, n0, mnorm_g):
    t = qkvo.shape[0]
    tk = MSTEP_TOK
    return pl.pallas_call(
        _mlstm_step_body,
        grid=(t // tk,),
        in_specs=[pl.BlockSpec((tk, 4 * MWIDTH), lambda i: (i, 0)),
                  pl.BlockSpec((None, 2 * HEADS, DH, tk), lambda i: (i, 0, 0, 0)),
                  pl.BlockSpec((tk, LANES), lambda i: (i, 0)),
                  pl.BlockSpec((tk, LANES), lambda i: (i, 0)),
                  pl.BlockSpec((tk, LANES), lambda i: (i, 0)),
                  pl.BlockSpec((tk, HEADS, DH, DH), lambda i: (i, 0, 0, 0)),
                  pl.BlockSpec((tk, MWIDTH), lambda i: (i, 0)),
                  pl.BlockSpec((1, MWIDTH), lambda i: (0, 0))],
        out_specs=[pl.BlockSpec((tk, MWIDTH), lambda i: (i, 0)),
                   pl.BlockSpec((tk, HEADS, DH, DH), lambda i: (i, 0, 0, 0)),
                   pl.BlockSpec((tk, MWIDTH), lambda i: (i, 0)),
                   pl.BlockSpec((tk, LANES), lambda i: (i, 0))],
        out_shape=[jax.ShapeDtypeStruct((t, MWIDTH), F32),
                   jax.ShapeDtypeStruct((t, HEADS, DH, DH), F32),
                   jax.ShapeDtypeStruct((t, MWIDTH), F32),
                   jax.ShapeDtypeStruct((t, LANES), F32)],
        scratch_shapes=[pltpu.VMEM((tk, MWIDTH), F32)],
        compiler_params=_cparams("parallel"),
        name="mlstm_step",
    )(qkvo, qkt, ig, lf, m0, c0, n0, mnorm_g)


def _post_body(x_ref, y5_ref, hm_ref, gg_ref, g1_ref, sc_ref, sh_ref, n2_ref, wglu_ref, wmo_ref, wo_ref, rw_ref, rb_ref,
               cnt0_ref, x1_ref, h2_ref, eid_ref, wgt_ref, rank_ref, cnt_ref, run_sc, *, n_real, n_steps):
    i = pl.program_id(0)

    @pl.when(i == 0)
    def _():
        run_sc[...] = cnt0_ref[...]

    if n_steps > n_real:
        @pl.when(i >= n_real)
        def _():
            h2_ref[...] = jnp.zeros_like(h2_ref)

        pl.when(i < n_real)(functools.partial(
            _post_tile, x_ref, y5_ref, hm_ref, gg_ref, g1_ref, sc_ref, sh_ref, n2_ref, wglu_ref, wmo_ref, wo_ref,
            rw_ref, rb_ref, x1_ref, h2_ref, eid_ref, wgt_ref, rank_ref, run_sc))
    else:
        _post_tile(x_ref, y5_ref, hm_ref, gg_ref, g1_ref, sc_ref, sh_ref, n2_ref, wglu_ref, wmo_ref, wo_ref,
                   rw_ref, rb_ref, x1_ref, h2_ref, eid_ref, wgt_ref, rank_ref, run_sc)
    cnt_ref[...] = run_sc[...]


def _post_tile(x_ref, y5_ref, hm_ref, gg_ref, g1_ref, sc_ref, sh_ref, n2_ref, wglu_ref, wmo_ref, wo_ref, rw_ref, rb_ref,
               x1_ref, h2_ref, eid_ref, wgt_ref, rank_ref, run_sc):
    g5 = _gelu_tanh(y5_ref[...]).astype(BF16)
    glu = _dot(g5, wglu_ref[...])
    br_s5 = glu[:, :D_MODEL] * _sigmoid(glu[:, D_MODEL:])
    br_m = _dot(hm_ref[...].astype(BF16), wmo_ref[...])
    gg = gg_ref[...].astype(F32)
    merged = _sigmoid(gg[:, :D_MODEL]) * br_s5 + _sigmoid(gg[:, D_MODEL:]) * br_m
    x1 = x_ref[...] + g1_ref[...] * _dot(merged.astype(BF16), wo_ref[...])
    x1_ref[...] = x1
    ms = jnp.mean(x1 * x1, axis=-1, keepdims=True)
    h2 = x1 * lax.rsqrt(ms + EPS) * n2_ref[...] * (1.0 + sc_ref[...]) + sh_ref[...]
    h2_ref[...] = h2
    logits = _dot_hp(h2, rw_ref[...]) + rb_ref[...]
    tm = logits.shape[0]
    lane = lax.broadcasted_iota(jnp.int32, logits.shape, 1)
    l = jnp.where(lane < N_EXPERTS, logits, -jnp.inf)
    vals, hots, idxs = [], [], []
    for _ in range(TOP_K):
        mx = jnp.max(l, axis=-1, keepdims=True)
        idx = jnp.min(jnp.where(l == mx, lane, LANES), axis=-1, keepdims=True)
        hot = lane == idx
        vals.append(mx)
        hots.append(hot)
        idxs.append(idx)
        l = jnp.where(hot, -jnp.inf, l)
    ex = [jnp.exp(vk - vals[0]) for vk in vals]
    tot = ex[0] + ex[1] + ex[2] + ex[3]
    chosen = jnp.where(hots[0] | hots[1] | hots[2] | hots[3], 1.0, 0.0)
    r_io = lax.broadcasted_iota(jnp.int32, (tm, tm), 0)
    c_io = lax.broadcasted_iota(jnp.int32, (tm, tm), 1)
    earlier = (c_io < r_io).astype(BF16)
    before = run_sc[...] + _dot(earlier, chosen.astype(BF16))
    eid = jnp.zeros(logits.shape, jnp.int32)
    wgt = jnp.zeros(logits.shape, F32)
    rank = jnp.zeros(logits.shape, F32)
    for k in range(TOP_K):
        eid = jnp.where(lane == k, idxs[k], eid)
        wgt = jnp.where(lane == k, ex[k] / tot, wgt)
        rank = jnp.where(lane == k, jnp.sum(jnp.where(hots[k], before, 0.0), axis=-1, keepdims=True), rank)
    eid_ref[...] = eid
    wgt_ref[...] = wgt
    rank_ref[...] = rank.astype(jnp.int32)
    run_sc[...] += jnp.sum(chosen, axis=0, keepdims=True)


def _post(x2d, y5, hm, gg, mod3, rows_per_mod, tm, n2, wglu, wmo, wo, rw, rb, cnt0, h2_rows):
    t = x2d.shape[0]
    mrows = mod3.shape[1]
    n_real = t // tm
    n_steps = -(-h2_rows // tm)
    last = n_real - 1

    def mod_spec(j):
        return pl.BlockSpec((None, mrows, D_MODEL), lambda i: ((jnp.minimum(i, last) * tm) // rows_per_mod, 0, j))

    const = lambda shape: pl.BlockSpec(shape, lambda i: (0,) * len(shape))
    tile = lambda w: pl.BlockSpec((tm, w), lambda i: (jnp.minimum(i, last), 0))
    return pl.pallas_call(
        functools.partial(_post_body, n_real=n_real, n_steps=n_steps),
        grid=(n_steps,),
        in_specs=[tile(D_MODEL), tile(S5_WIDTH), tile(MWIDTH), tile(2 * D_MODEL),
                  mod_spec(2), mod_spec(4), mod_spec(3), const((1, D_MODEL)),
                  const(wglu.shape), const(wmo.shape), const(wo.shape), const(rw.shape), const(rb.shape),
                  const((1, LANES))],
        out_specs=[tile(D_MODEL), pl.BlockSpec((tm, D_MODEL), lambda i: (i, 0)), tile(LANES), tile(LANES), tile(LANES),
                   const((1, LANES))],
        out_shape=[jax.ShapeDtypeStruct((t, D_MODEL), F32),
                   jax.ShapeDtypeStruct((h2_rows, D_MODEL), F32),
                   jax.ShapeDtypeStruct((t, LANES), jnp.int32),
                   jax.ShapeDtypeStruct((t, LANES), F32),
                   jax.ShapeDtypeStruct((t, LANES), jnp.int32),
                   jax.ShapeDtypeStruct((1, LANES), F32)],
        scratch_shapes=[pltpu.VMEM((1, LANES), F32)],
        compiler_params=_cparams("arbitrary"),
        name="post",
    )(x2d, y5, hm, gg, mod3, mod3, mod3, n2, wglu, wmo, wo, rw, rb, cnt0)


N_TOKENS = N_PROMPT * SEQ + N_SAMPLE
N_ASSIGN = N_TOKENS * TOP_K
MOE_TILE = 256
MOE_TILES = N_ASSIGN // MOE_TILE + N_EXPERTS
N_SLOT = MOE_TILES * MOE_TILE
CMB_TILE = 128


def _moe_body(te_ref, nt_ref, cur_ref, nxt_ref, h2_hbm, wup_ref, bup_ref, wdn_ref, bdn_ref, y_ref,
              xb0, xb1, sem, wup_bf, wdn_bf):
    i = pl.program_id(0)
    nt = nt_ref[0]

    def gather(idx_ref, buf, s):
        for r in range(MOE_TILE):
            pltpu.make_async_copy(h2_hbm.at[pl.ds(idx_ref[0, r], 1)], buf.at[pl.ds(r, 1)], sem.at[s]).start()

    def wait(buf, s):
        pltpu.make_async_copy(h2_hbm.at[pl.ds(0, MOE_TILE)], buf, sem.at[s]).wait()

    @pl.when(i == 0)
    def _():
        gather(cur_ref, xb0, 0)

    new_expert = jnp.logical_or(i == 0, te_ref[i] != te_ref[jnp.maximum(i - 1, 0)])

    @pl.when(jnp.logical_and(new_expert, i < nt))
    def _():
        wup_bf[...] = wup_ref[...].T.astype(BF16)
        wdn_bf[...] = wdn_ref[...].T.astype(BF16)

    def step(cur, nxt, s_cur, s_nxt):
        wait(cur, s_cur)
        gather(nxt_ref, nxt, s_nxt)
        xt = cur[...].T.astype(BF16)
        gu = _dot(wup_bf[...], xt) + bup_ref[...]
        g = jnp.minimum(gu[:D_FF, :], SWIGLU_LIMIT)
        up = jnp.clip(gu[D_FF:, :], -SWIGLU_LIMIT, SWIGLU_LIMIT)
        act = (up + 1.0) * g * _sigmoid(SWIGLU_ALPHA * g)
        yt = _dot(wdn_bf[...], act.astype(BF16)) + bdn_ref[...]
        y_ref[...] = yt.T

    live = i < nt
    even = i % 2 == 0
    pl.when(jnp.logical_and(live, even))(functools.partial(step, xb0, xb1, 0, 1))
    pl.when(jnp.logical_and(live, jnp.logical_not(even)))(functools.partial(step, xb1, xb0, 1, 0))

    @pl.when(jnp.logical_not(live))
    def _():
        y_ref[...] = jnp.zeros_like(y_ref)

    first_dead = i == nt
    pl.when(jnp.logical_and(first_dead, even))(functools.partial(wait, xb0, 0))
    pl.when(jnp.logical_and(first_dead, jnp.logical_not(even)))(functools.partial(wait, xb1, 1))
    last_slot = MOE_TILES % 2
    pl.when(jnp.logical_and(live, i == MOE_TILES - 1))(
        functools.partial(wait, xb1 if last_slot else xb0, last_slot))


def _moe(tile_expert, n_tiles, slot_token, h2_all, wup, bup, wdn, bdn):
    idx3 = slot_token.reshape(MOE_TILES, 1, MOE_TILE)
    grid_spec = pltpu.PrefetchScalarGridSpec(
        num_scalar_prefetch=2,
        grid=(MOE_TILES,),
        in_specs=[pl.BlockSpec((None, 1, MOE_TILE), lambda i, te, nt: (i, 0, 0), memory_space=pltpu.SMEM),
                  pl.BlockSpec((None, 1, MOE_TILE), lambda i, te, nt: (jnp.minimum(i + 1, MOE_TILES - 1), 0, 0),
                               memory_space=pltpu.SMEM),
                  pl.BlockSpec(memory_space=pl.ANY),
                  pl.BlockSpec((None, D_MODEL, 2 * D_FF), lambda i, te, nt: (te[i], 0, 0)),
                  pl.BlockSpec((None, 2 * D_FF, 1), lambda i, te, nt: (te[i], 0, 0)),
                  pl.BlockSpec((None, D_FF, D_MODEL), lambda i, te, nt: (te[i], 0, 0)),
                  pl.BlockSpec((None, D_MODEL, 1), lambda i, te, nt: (te[i], 0, 0))],
        out_specs=pl.BlockSpec((MOE_TILE, D_MODEL), lambda i, te, nt: (i, 0)),
        scratch_shapes=[pltpu.VMEM((MOE_TILE, D_MODEL), F32), pltpu.VMEM((MOE_TILE, D_MODEL), F32),
                        pltpu.SemaphoreType.DMA((2,)),
                        pltpu.VMEM((2 * D_FF, D_MODEL), BF16), pltpu.VMEM((D_MODEL, D_FF), BF16)])
    return pl.pallas_call(
        _moe_body,
        grid_spec=grid_spec,
        out_shape=jax.ShapeDtypeStruct((N_SLOT, D_MODEL), F32),
        compiler_params=_cparams("arbitrary"),
        name="moe",
    )(tile_expert, n_tiles, idx3, idx3, h2_all, wup, bup, wdn, bdn)


def _combine_body(cur_ref, nxt_ref, wgt_ref, x1_ref, g2_ref, fg_ref, y_hbm, o_ref, yb0, yb1, sem, *, n_steps):
    i = pl.program_id(0)
    tm = CMB_TILE

    def gather(idx_ref, buf, s):
        for r in range(tm):
            for k in range(TOP_K):
                pltpu.make_async_copy(y_hbm.at[pl.ds(idx_ref[0, r * TOP_K + k], 1)],
                                      buf.at[pl.ds(k * tm + r, 1)], sem.at[s]).start(priority=k % 2)

    def wait(buf, s):
        pltpu.make_async_copy(y_hbm.at[pl.ds(0, TOP_K * tm)], buf, sem.at[s]).wait()

    @pl.when(i == 0)
    def _():
        gather(cur_ref, yb0, 0)

    def step(cur, nxt, s_cur, s_nxt):
        wait(cur, s_cur)
        gather(nxt_ref, nxt, s_nxt)
        wgt = wgt_ref[...]
        acc = wgt[:, 0:1] * cur[0:tm, :]
        for k in range(1, TOP_K):
            acc = acc + wgt[:, k:k + 1] * cur[k * tm:(k + 1) * tm, :]
        xo = x1_ref[...] + g2_ref[...] * acc
        ms = jnp.mean(xo * xo, axis=-1, keepdims=True)
        o_ref[...] = xo * lax.rsqrt(ms + EPS) * fg_ref[...]

    pl.when(i % 2 == 0)(functools.partial(step, yb0, yb1, 0, 1))
    pl.when(i % 2 == 1)(functools.partial(step, yb1, yb0, 1, 0))
    last_slot = n_steps % 2
    pl.when(i == n_steps - 1)(functools.partial(wait, yb1 if last_slot else yb0, last_slot))


def _combine(pos, wgt, x1, mod3, rows_per_mod, fg, y_slots):
    t = x1.shape[0]
    tm = CMB_TILE
    n = t // tm
    mrows = mod3.shape[1]
    assert mrows in (1, tm)
    pos3 = pos.reshape(n, 1, tm * TOP_K)
    return pl.pallas_call(
        functools.partial(_combine_body, n_steps=n),
        grid=(n,),
        in_specs=[pl.BlockSpec((None, 1, tm * TOP_K), lambda i: (i, 0, 0), memory_space=pltpu.SMEM),
                  pl.BlockSpec((None, 1, tm * TOP_K), lambda i: (jnp.minimum(i + 1, n - 1), 0, 0),
                               memory_space=pltpu.SMEM),
                  pl.BlockSpec((tm, LANES), lambda i: (i, 0)),
                  pl.BlockSpec((tm, D_MODEL), lambda i: (i, 0)),
                  pl.BlockSpec((None, mrows, D_MODEL), lambda i: ((i * tm) // rows_per_mod, 0, 5)),
                  pl.BlockSpec((1, D_MODEL), lambda i: (0, 0)),
                  pl.BlockSpec(memory_space=pl.ANY)],
        out_specs=pl.BlockSpec((tm, D_MODEL), lambda i: (i, 0)),
        out_shape=jax.ShapeDtypeStruct((t, D_MODEL), F32),
        scratch_shapes=[pltpu.VMEM((TOP_K * tm, D_MODEL), F32), pltpu.VMEM((TOP_K * tm, D_MODEL), F32),
                        pltpu.SemaphoreType.DMA((2,))],
        compiler_params=_cparams("arbitrary"),
        name="combine",
    )(pos3, pos3, wgt, x1, mod3, fg, y_slots)


def _routing(eid_p, rank_p, eid_s, rank_s, counts):
    cnt = counts[0, :N_EXPERTS].astype(jnp.int32)
    ntile = (cnt + MOE_TILE - 1) // MOE_TILE
    tile_end = jnp.cumsum(ntile)
    poff = (tile_end - ntile) * MOE_TILE
    total = tile_end[-1]
    j = jnp.arange(MOE_TILES, dtype=jnp.int32)
    te = jnp.sum((j[:, None] >= tile_end[None, :]).astype(jnp.int32), axis=1)
    te_last = jnp.sum(((total - 1) >= tile_end).astype(jnp.int32))
    tile_expert = jnp.where(j < total, te, te_last).astype(jnp.int32)
    ex = jnp.arange(N_EXPERTS, dtype=jnp.int32)

    def pos_of(eid, rank):
        e = eid[:, :TOP_K]
        off = jnp.sum(jnp.where(e[:, :, None] == ex[None, None, :], poff[None, None, :], 0), axis=-1)
        return off + rank[:, :TOP_K]

    pos_p = pos_of(eid_p, rank_p)
    pos_s = pos_of(eid_s, rank_s)
    tok = jnp.concatenate([jnp.repeat(jnp.arange(N_PROMPT * SEQ, dtype=jnp.int32), TOP_K),
                           jnp.repeat(N_PROMPT * SEQ + jnp.arange(N_SAMPLE, dtype=jnp.int32), TOP_K)])
    pos_all = jnp.concatenate([pos_p.reshape(-1), pos_s.reshape(-1)])
    slot_token = jnp.zeros((N_SLOT,), jnp.int32).at[pos_all].set(tok, unique_indices=True)
    return tile_expert, total.reshape(1).astype(jnp.int32), slot_token, pos_p, pos_s


def _unpack_s5_state(x):
    n = x.shape[1]
    z = x.reshape(N_SUPER, n, 2, S5_SUPER, S5_STATE).transpose(2, 1, 0, 3, 4).reshape(2, n, S5_GROUPS, S5_STATE)
    return z[0], z[1]


def _pack_s5_state(re, im):
    n = re.shape[0]
    z = jnp.stack([re, im], axis=0).reshape(2, n, N_SUPER, S5_SUPER, S5_STATE)
    return z.transpose(2, 1, 0, 3, 4).reshape(N_SUPER, n, SUPER_STATE)


def kernel(x_prompt, x_sample, c_prompt, c_sample, state_s5_re, state_s5_im, state_mlstm_C, state_mlstm_n, state_mlstm_m, norm1_g, norm2_g, final_norm_g, w_ada, b_ada, w_in, s5_lambda_re, s5_lambda_im, s5_log_dt, s5_B_re, s5_B_im, s5_C_re, s5_C_im, s5_D, s5_w_glu, mlstm_b_i, mlstm_b_f, mlstm_norm_g, mlstm_w_out, w_out, router_w, router_b, expert_w_up, expert_b_up, expert_w_down, expert_b_down):
    assert w_in.shape[0] == 1, "single layer"
    tp = N_PROMPT * SEQ
    xp = x_prompt.reshape(tp, D_MODEL).astype(F32)
    xs = x_sample.reshape(N_SAMPLE, D_MODEL).astype(F32)

    w = w_in[0]
    c0 = S5_WIDTH
    c1 = c0 + 4 * MWIDTH
    c2 = c1 + 2 * HEADS
    wu = w[:, :c0].astype(BF16)
    wq = w[:, c0:c1].astype(BF16)
    wif = jnp.pad(w[:, c1:c2].astype(F32), ((0, 0), (0, LANES - 2 * HEADS)))
    wift = w[:, c1:c2].astype(F32).T
    wg = w[:, c2:].astype(BF16)
    bvec = jnp.concatenate([mlstm_b_i[0], mlstm_b_f[0]]).astype(F32)
    bif = jnp.pad(bvec, (0, LANES - 2 * HEADS)).reshape(1, LANES)
    bift = bvec.reshape(2 * HEADS, 1)
    wparts = (wu, wq, wg, wif, wift, bif, bift)
    g1 = norm1_g[0].reshape(1, D_MODEL).astype(F32)
    n2 = norm2_g[0].reshape(1, D_MODEL).astype(F32)
    fg = final_norm_g.reshape(1, D_MODEL).astype(F32)
    mg = mlstm_norm_g[0].reshape(1, MWIDTH).astype(F32)
    wglu = s5_w_glu[0].astype(BF16)
    wmo = mlstm_w_out[0].astype(BF16)
    wo = w_out[0].astype(BF16)
    rw = jnp.pad(router_w[0].astype(F32), ((0, 0), (0, LANES - N_EXPERTS)))
    rb = jnp.pad(router_b[0].astype(F32), (0, LANES - N_EXPERTS)).reshape(1, LANES)
    wup = expert_w_up[0].astype(F32)
    wdn = expert_w_down[0].astype(F32)
    bup = expert_b_up[0].astype(F32).reshape(N_EXPERTS, 2 * D_FF, 1)
    bdn = expert_b_down[0].astype(F32).reshape(N_EXPERTS, D_MODEL, 1)
    kpair, fmat, emat, a16, a1, k0, e0, f1 = _s5_tables(
        s5_lambda_re[0], s5_lambda_im[0], s5_log_dt[0], s5_B_re[0], s5_B_im[0], s5_C_re[0], s5_C_im[0], s5_D[0])

    c_all = jnp.concatenate([c_prompt, c_sample], axis=0).astype(F32)
    mod = _adaln(c_all, w_ada[0].astype(F32), b_ada[0].astype(F32))
    mod_p = mod[:N_PROMPT].reshape(N_PROMPT, 1, 6 * D_MODEL)
    mod_s = mod[N_PROMPT:].reshape(1, N_SAMPLE, 6 * D_MODEL)

    u, qkvo, gg, gcol, grow = _inproj(xp, mod_p, SEQ, 1024, g1, wparts)
    y5, xend = _s5_prompt(u, kpair, fmat, emat, a16)
    hm, c_p, n_p, m_p = _mlstm_prompt(qkvo, gcol, grow, mg)
    cnt0 = jnp.zeros((1, LANES), F32)
    x1, h2_all, eid_p, wgt_p, rank_p, cnt_p = _post(xp, y5, hm, gg, mod_p, SEQ, 512, n2, wglu, wmo, wo, rw, rb,
                                                    cnt0, N_TOKENS)
    p_re, p_im = _unpack_s5_state(xend.reshape(N_SUPER, N_PROMPT, SUPER_STATE))
    n_p = n_p[..., 0]
    m_p = m_p[:, :HEADS, 0]

    us, qs, ggs, gcs, _ = _inproj(xs, mod_s, N_SAMPLE, N_SAMPLE, g1, wparts)
    x0 = _pack_s5_state(state_s5_re[0].astype(F32), state_s5_im[0].astype(F32))
    y5s, xns = _s5_step(us, x0, k0, e0, f1, a1)
    s_re, s_im = _unpack_s5_state(xns)
    qkt = qs[:, :2 * MWIDTH].astype(F32).reshape(N_SAMPLE // MSTEP_TOK, MSTEP_TOK, 2 * HEADS, DH).transpose(0, 2, 3, 1)
    pad_heads = lambda a: jnp.pad(a, ((0, 0), (0, LANES - HEADS)))
    m0 = pad_heads(state_mlstm_m[0].astype(F32))
    hms, c_s, n_s, m_s = _mlstm_step(qs, qkt, pad_heads(gcs[:, :HEADS]), pad_heads(gcs[:, HEADS:2 * HEADS]), m0,
                                     state_mlstm_C[0].astype(F32),
                                     state_mlstm_n[0].astype(F32).reshape(N_SAMPLE, MWIDTH), mg)
    x1s, h2s, eid_s, wgt_s, rank_s, cnt_all = _post(xs, y5s, hms, ggs, mod_s, N_SAMPLE, N_SAMPLE, n2, wglu, wmo, wo, rw, rb,
                                                    cnt_p, N_SAMPLE)

    h2_all = lax.dynamic_update_slice(h2_all, h2s, (tp, 0))
    tile_expert, n_tiles, slot_token, pos_p, pos_s = _routing(eid_p, rank_p, eid_s, rank_s, cnt_all)
    y_slots = _moe(tile_expert, n_tiles, slot_token, h2_all, wup, bup, wdn, bdn)
    y_p = _combine(pos_p, wgt_p, x1, mod_p, SEQ, fg, y_slots)
    y_s = _combine(pos_s, wgt_s, x1s, mod_s, N_SAMPLE, fg, y_slots)

    return (y_p.reshape(N_PROMPT, SEQ, D_MODEL).astype(x_prompt.dtype),
            y_s.reshape(N_SAMPLE, 1, D_MODEL).astype(x_sample.dtype),
            p_re[None], p_im[None], c_p[None], n_p[None], m_p[None],
            s_re[None], s_im[None], c_s[None],
            n_s.reshape(1, N_SAMPLE, HEADS, DH), m_s[:, :HEADS][None])
```

```python
import functools
import math

import jax
import jax.numpy as jnp
from jax import lax
from jax.experimental import pallas as pl
from jax.experimental.pallas import tpu as pltpu

F32 = jnp.float32
BF16 = jnp.bfloat16

D_MODEL = 1024
SEQ = 2048
N_PROMPT = 8
N_SAMPLE = 128
S5_WIDTH = 512
S5_GROUP = 16
S5_GROUPS = 32
S5_STATE = 64
HEADS = 4
DH = 128
MWIDTH = HEADS * DH
N_EXPERTS = 32
TOP_K = 4
D_FF = 1024
SWIGLU_LIMIT = 7.0
SWIGLU_ALPHA = 1.702
EPS = 1e-6

LANES = 128
S5_CHUNK = 16
S5_SUPER = LANES // S5_GROUP
N_SUPER = S5_GROUPS // S5_SUPER
SUPER_STATE = 2 * S5_SUPER * S5_STATE
MCHUNK = 128
VMEM_LIMIT = 56 * 1024 * 1024


def _cparams(*sem):
    return pltpu.CompilerParams(dimension_semantics=sem, vmem_limit_bytes=VMEM_LIMIT)


def _dot(a, b):
    return jnp.dot(a, b, preferred_element_type=F32)


def _dot_dims(a, b, dims):
    return lax.dot_general(a, b, (dims, ((), ())), preferred_element_type=F32)


def _split_bf16(a):
    hi = a.astype(BF16)
    lo = (a - hi.astype(F32)).astype(BF16)
    return hi, lo


def _dot_hp(a, b, dims=((1,), (0,))):
    ah, al = _split_bf16(a)
    bh, bl = _split_bf16(b)
    return _dot_dims(ah, bh, dims) + (_dot_dims(al, bh, dims) + _dot_dims(ah, bl, dims))


def _dot_exact_rhs(a, b_exact, dims=((1,), (0,))):
    ah, al = _split_bf16(a)
    return _dot_dims(ah, b_exact, dims) + _dot_dims(al, b_exact, dims)


def _log_sigmoid(x):
    return -(jnp.maximum(-x, 0.0) + jnp.log1p(jnp.exp(-jnp.abs(x))))


def _sigmoid(x):
    return 1.0 / (1.0 + jnp.exp(-x))


def _gelu_tanh(x):
    c = math.sqrt(2.0 / math.pi)
    return 0.5 * x * (1.0 + jnp.tanh(c * (x + 0.044715 * (x * x * x))))


def _adaln_body(c_ref, w_ref, b_ref, o_ref):
    c = c_ref[...]
    s = c * _sigmoid(c)
    o_ref[...] = _dot_hp(s, w_ref[...]) + b_ref[...]


def _adaln(c_all, w_ada, b_ada):
    n = c_all.shape[0]
    tn = 1024
    return pl.pallas_call(
        _adaln_body,
        grid=(6 * D_MODEL // tn,),
        in_specs=[pl.BlockSpec((n, D_MODEL), lambda j: (0, 0)),
                  pl.BlockSpec((D_MODEL, tn), lambda j: (0, j)),
                  pl.BlockSpec((1, tn), lambda j: (0, j))],
        out_specs=pl.BlockSpec((n, tn), lambda j: (0, j)),
        out_shape=jax.ShapeDtypeStruct((n, 6 * D_MODEL), F32),
        compiler_params=_cparams("parallel"),
        name="adaln",
    )(c_all, w_ada, b_ada.reshape(1, -1))


def _inproj_body(x_ref, g1_ref, sc_ref, sh_ref, wu_ref, wq_ref, wg_ref, wif_ref, wift_ref, bif_ref, bift_ref,
                 u_ref, q_ref, gg_ref, if_ref, ift_ref):
    x = x_ref[...]
    ms = jnp.mean(x * x, axis=-1, keepdims=True)
    h = x * lax.rsqrt(ms + EPS) * g1_ref[...] * (1.0 + sc_ref[...]) + sh_ref[...]
    hb = h.astype(BF16)
    u_ref[...] = _dot(hb, wu_ref[...])
    q_ref[...] = _dot(hb, wq_ref[...]).astype(BF16)
    gg_ref[...] = _dot(hb, wg_ref[...]).astype(BF16)
    gc = _dot_hp(h, wif_ref[...]) + bif_ref[...]
    lane = lax.broadcasted_iota(jnp.int32, gc.shape, 1)
    if_ref[...] = jnp.where(lane < HEADS, gc, _log_sigmoid(gc))
    gr = _dot_hp(wift_ref[...], h, ((1,), (1,))) + bift_ref[...]
    sub = lax.broadcasted_iota(jnp.int32, gr.shape, 0)
    ift_ref[...] = jnp.where(sub < HEADS, gr, _log_sigmoid(gr))


def _inproj(x2d, mod3, rows_per_mod, tm, g1, wparts):
    t = x2d.shape[0]
    wu, wq, wg, wif, wift, bif, bift = wparts
    mrows = mod3.shape[1]

    def mod_spec(j):
        return pl.BlockSpec((None, mrows, D_MODEL), lambda i: ((i * tm) // rows_per_mod, 0, j))

    const = lambda shape: pl.BlockSpec(shape, lambda i: (0,) * len(shape), pipeline_mode=pl.Buffered(1))
    return pl.pallas_call(
        _inproj_body,
        grid=(t // tm,),
        in_specs=[pl.BlockSpec((tm, D_MODEL), lambda i: (i, 0)),
                  const((1, D_MODEL)), mod_spec(1), mod_spec(0),
                  const(wu.shape), const(wq.shape), const(wg.shape), const(wif.shape), const(wift.shape),
                  const(bif.shape), const(bift.shape)],
        out_specs=[pl.BlockSpec((tm, S5_WIDTH), lambda i: (i, 0)),
                   pl.BlockSpec((tm, 4 * MWIDTH), lambda i: (i, 0)),
                   pl.BlockSpec((tm, 2 * D_MODEL), lambda i: (i, 0)),
                   pl.BlockSpec((tm, LANES), lambda i: (i, 0)),
                   pl.BlockSpec((8, tm), lambda i: (0, i))],
        out_shape=[jax.ShapeDtypeStruct((t, S5_WIDTH), F32),
                   jax.ShapeDtypeStruct((t, 4 * MWIDTH), BF16),
                   jax.ShapeDtypeStruct((t, 2 * D_MODEL), BF16),
                   jax.ShapeDtypeStruct((t, LANES), F32),
                   jax.ShapeDtypeStruct((8, t), F32)],
        compiler_params=_cparams("parallel"),
        name="inproj",
    )(x2d, g1, mod3, mod3, wu, wq, wg, wif, wift, bif, bift)


def _s5_tables(lam_re, lam_im, log_dt, b_re, b_im, c_re, c_im, d_s5):
    hi = lax.Precision.HIGHEST
    dt = jnp.exp(log_dt.astype(F32))[:, None]
    lr, li = lam_re.astype(F32), lam_im.astype(F32)
    dpow = jnp.arange(S5_CHUNK + 1, dtype=F32)[:, None, None]
    mag = jnp.exp(dpow * (lr * dt))
    pw_re, pw_im = mag * jnp.cos(dpow * (li * dt)), mag * jnp.sin(dpow * (li * dt))
    ab_re, ab_im = pw_re[1], pw_im[1]
    den = lr * lr + li * li
    nr, ni = ab_re - 1.0, ab_im
    coef_re = (nr * lr + ni * li) / den
    coef_im = (ni * lr - nr * li) / den
    br, bi = b_re.astype(F32), b_im.astype(F32)
    bb_re = coef_re[..., None] * br - coef_im[..., None] * bi
    bb_im = coef_re[..., None] * bi + coef_im[..., None] * br
    cr, ci = c_re.astype(F32), c_im.astype(F32)
    cl_re = cr[None] * pw_re[:, :, None, :] - ci[None] * pw_im[:, :, None, :]
    cl_im = cr[None] * pw_im[:, :, None, :] + ci[None] * pw_re[:, :, None, :]
    kd = (jnp.einsum('dgop,gpi->dgio', cl_re[:S5_CHUNK], bb_re, precision=hi)
          - jnp.einsum('dgop,gpi->dgio', cl_im[:S5_CHUNK], bb_im, precision=hi))
    kd = kd.at[0].add(d_s5.astype(F32)[:, :, None] * jnp.eye(S5_GROUP, dtype=F32)[None])
    eye = jnp.eye(S5_SUPER, dtype=F32)
    kd6 = kd.reshape(S5_CHUNK, N_SUPER, S5_SUPER, S5_GROUP, 1, S5_GROUP)
    kblk = (kd6 * eye[None, None, :, None, :, None]).reshape(S5_CHUNK, N_SUPER, LANES, LANES)
    kext = jnp.concatenate([jnp.zeros_like(kblk[:1]), kblk], axis=0)
    kpair = jnp.concatenate([kext[S5_CHUNK - 1::-1], kext[S5_CHUNK:0:-1]], axis=-1)
    kpair = kpair.transpose(1, 0, 2, 3).reshape(N_SUPER, S5_CHUNK * LANES, 2 * LANES)
    rp_re, rp_im = pw_re[S5_CHUNK - 1::-1], pw_im[S5_CHUNK - 1::-1]
    f_re = rp_re[..., None] * bb_re[None] - rp_im[..., None] * bb_im[None]
    f_im = rp_re[..., None] * bb_im[None] + rp_im[..., None] * bb_re[None]
    fcat = jnp.stack([f_re, f_im], axis=2).transpose(0, 1, 4, 2, 3)
    fr = fcat.reshape(S5_CHUNK, N_SUPER, S5_SUPER, S5_GROUP, 2, S5_STATE)
    fr7 = fr.transpose(1, 0, 2, 3, 4, 5)[:, :, :, :, :, None, :]
    fmat = (fr7 * eye[None, None, :, None, None, :, None]).astype(BF16).reshape(N_SUPER, S5_CHUNK * LANES, SUPER_STATE)
    ecat = jnp.stack([cl_re[1:], -cl_im[1:]], axis=0)
    er = ecat.transpose(2, 0, 4, 1, 3).reshape(N_SUPER, S5_SUPER, 2, S5_STATE, S5_CHUNK, S5_GROUP)
    er7 = er.transpose(0, 2, 1, 3, 4, 5)[:, :, :, :, :, None, :]
    emat = (er7 * eye[None, None, :, None, None, :, None]).astype(BF16).reshape(N_SUPER, SUPER_STATE, S5_CHUNK * LANES)

    def lay(re, im):
        z = jnp.stack([re, im], axis=0).reshape(2, N_SUPER, S5_SUPER * S5_STATE)
        return z.transpose(1, 0, 2).reshape(N_SUPER, 1, SUPER_STATE)

    a16 = lay(pw_re[S5_CHUNK], pw_im[S5_CHUNK])
    a1 = lay(ab_re, ab_im)
    k0 = kblk[0].astype(BF16)
    e0 = emat[:, :, :LANES].astype(BF16)
    f1 = fmat[:, (S5_CHUNK - 1) * LANES:, :].astype(BF16)
    return kpair.astype(BF16), fmat.astype(BF16), emat.astype(BF16), a16, a1, k0, e0, f1


S5_NB = 4


def _s5_prompt_body(u_ref, kp_ref, f_ref, e_ref, a_ref, y_ref, xe_ref, ucat, v_sc, xp_sc):
    n = SEQ // S5_CHUNK
    half = SUPER_STATE // 2
    for bl in range(S5_NB):
        for j in range(S5_CHUNK):
            ucat[bl * n:(bl + 1) * n, j * LANES:(j + 1) * LANES] = (
                u_ref[pl.ds(bl * SEQ + j, n, stride=S5_CHUNK), :].astype(BF16))
    v_sc[...] = _dot(ucat[...], f_ref[...])
    a_r, a_i = a_ref[:, :half], a_ref[:, half:]
    xr = [jnp.zeros((1, half), F32) for _ in range(S5_NB)]
    xi = [jnp.zeros((1, half), F32) for _ in range(S5_NB)]
    for c in range(n):
        for bl in range(S5_NB):
            row = bl * n + c
            xp_sc[row:row + 1, :half] = xr[bl]
            xp_sc[row:row + 1, half:] = xi[bl]
            vr, vi = v_sc[row:row + 1, :half], v_sc[row:row + 1, half:]
            xr[bl], xi[bl] = a_r * xr[bl] - a_i * xi[bl] + vr, a_r * xi[bl] + a_i * xr[bl] + vi
    for bl in range(S5_NB):
        xe_ref[bl:bl + 1, :half] = xr[bl]
        xe_ref[bl:bl + 1, half:] = xi[bl]
    ye = _dot(xp_sc[...].astype(BF16), e_ref[...])
    for m in range(S5_CHUNK // 2):
        rows = (2 * m + 2) * LANES
        yp = _dot(ucat[:, :rows], kp_ref[(S5_CHUNK - 2 - 2 * m) * LANES:, :]) + ye[:, 2 * m * LANES:(2 * m + 2) * LANES]
        for bl in range(S5_NB):
            y_ref[pl.ds(bl * SEQ + 2 * m, n, stride=S5_CHUNK), :] = yp[bl * n:(bl + 1) * n, :LANES]
            y_ref[pl.ds(bl * SEQ + 2 * m + 1, n, stride=S5_CHUNK), :] = yp[bl * n:(bl + 1) * n, LANES:]


def _s5_prompt(u, kpair, fmat, emat, a16):
    n = S5_NB * SEQ // S5_CHUNK
    return pl.pallas_call(
        _s5_prompt_body,
        grid=(N_SUPER, N_PROMPT // S5_NB),
        in_specs=[pl.BlockSpec((S5_NB * SEQ, LANES), lambda s, b: (b, s)),
                  pl.BlockSpec((None,) + kpair.shape[1:], lambda s, b: (s, 0, 0)),
                  pl.BlockSpec((None,) + fmat.shape[1:], lambda s, b: (s, 0, 0)),
                  pl.BlockSpec((None,) + emat.shape[1:], lambda s, b: (s, 0, 0)),
                  pl.BlockSpec((None, 1, SUPER_STATE), lambda s, b: (s, 0, 0))],
        out_specs=[pl.BlockSpec((S5_NB * SEQ, LANES), lambda s, b: (b, s)),
                   pl.BlockSpec((None, None, S5_NB, SUPER_STATE), lambda s, b: (s, b, 0, 0))],
        out_shape=[jax.ShapeDtypeStruct(u.shape, F32),
                   jax.ShapeDtypeStruct((N_SUPER, N_PROMPT // S5_NB, S5_NB, SUPER_STATE), F32)],
        scratch_shapes=[pltpu.VMEM((n, S5_CHUNK * LANES), BF16),
                        pltpu.VMEM((n, SUPER_STATE), F32),
                        pltpu.VMEM((n, SUPER_STATE), F32)],
        compiler_params=_cparams("parallel", "parallel"),
        name="s5_prompt",
    )(u, kpair, fmat, emat, a16)


def _s5_step_body(u_ref, x0_ref, k0_ref, e0_ref, f1_ref, a_ref, y_ref, xn_ref):
    half = SUPER_STATE // 2
    ub = u_ref[...].astype(BF16)
    x0 = x0_ref[...]
    bu = _dot(ub, f1_ref[...])
    a_r, a_i = a_ref[:, :half], a_ref[:, half:]
    x0r, x0i = x0[:, :half], x0[:, half:]
    xn_ref[:, :half] = a_r * x0r - a_i * x0i + bu[:, :half]
    xn_ref[:, half:] = a_r * x0i + a_i * x0r + bu[:, half:]
    y_ref[...] = _dot(x0.astype(BF16), e0_ref[...]) + _dot(ub, k0_ref[...])


def _s5_step(u, x0, k0, e0, f1, a1):
    t = u.shape[0]
    per = lambda a: pl.BlockSpec((None,) + a.shape[1:], lambda s: (s,) + (0,) * (a.ndim - 1))
    return pl.pallas_call(
        _s5_step_body,
        grid=(N_SUPER,),
        in_specs=[pl.BlockSpec((t, LANES), lambda s: (0, s)), per(x0), per(k0), per(e0), per(f1), per(a1)],
        out_specs=[pl.BlockSpec((t, LANES), lambda s: (0, s)), per(x0)],
        out_shape=[jax.ShapeDtypeStruct(u.shape, F32), jax.ShapeDtypeStruct(x0.shape, F32)],
        compiler_params=_cparams("parallel"),
        name="s5_step",
    )(u, x0, k0, e0, f1, a1)


def _mlstm_prompt_body(q_ref, gc_ref, gr_ref, mg_ref, hm_ref, c_out, n_out, m_out, c_sc, n_sc, m_sc):
    lc = MCHUNK
    ci = pl.program_id(1)

    @pl.when(ci == 0)
    def _():
        c_sc[...] = jnp.zeros_like(c_sc)
        n_sc[...] = jnp.zeros_like(n_sc)
        m_sc[...] = jnp.zeros_like(m_sc)

    scale = DH ** -0.5
    row = lax.broadcasted_iota(jnp.int32, (lc, lc), 0)
    col = lax.broadcasted_iota(jnp.int32, (lc, lc), 1)
    causal = col <= row
    tri = causal.astype(BF16)
    ones = jnp.ones((lc, DH), BF16)
    gc = gc_ref[...]
    gr = gr_ref[...]
    gc_hi, gc_lo = _split_bf16(gc)
    bcol_all = _dot(tri, gc_hi) + _dot(tri, gc_lo)
    brow_all = _dot_exact_rhs(gr, tri, ((1,), (1,)))
    for hd in range(HEADS):
        q = q_ref[:, hd * DH:(hd + 1) * DH]
        k = q_ref[:, MWIDTH + hd * DH:MWIDTH + (hd + 1) * DH]
        v = q_ref[:, 2 * MWIDTH + hd * DH:2 * MWIDTH + (hd + 1) * DH]
        o = q_ref[:, 3 * MWIDTH + hd * DH:3 * MWIDTH + (hd + 1) * DH]
        i_col = gc[:, hd:hd + 1]
        b_col = bcol_all[:, HEADS + hd:HEADS + hd + 1]
        i_row = gr[hd:hd + 1, :]
        b_row = brow_all[HEADS + hd:HEADS + hd + 1, :]
        m_prev = m_sc[hd:hd + 1, :][:, :1]
        dm = jnp.where(causal, b_col - b_row + i_row, -jnp.inf)
        m_inter = b_col + m_prev
        m_t = jnp.maximum(m_inter, jnp.max(dm, axis=-1, keepdims=True))
        w_inter = jnp.exp(m_inter - m_t)
        w = jnp.exp(dm - m_t)
        s = _dot_dims(q, k, ((1,), (1,))) * scale * w
        sb = s.astype(BF16)
        c_prev = c_sc[hd]
        n_prev = n_sc[hd]
        num = w_inter * _dot(q, c_prev.astype(BF16)) + _dot(sb, v)
        den = w_inter * _dot(q, n_prev.astype(BF16)) + _dot(sb, ones)
        h = num / jnp.maximum(jnp.abs(den), jnp.exp(-m_t))
        hn = h * lax.rsqrt(jnp.mean(h * h, axis=-1, keepdims=True) + EPS)
        hm_ref[:, hd * DH:(hd + 1) * DH] = (hn * mg_ref[:, hd * DH:(hd + 1) * DH] * _sigmoid(o.astype(F32))).astype(BF16)
        m_last = m_t[lc - 1:lc, :]
        b_last = b_col[lc - 1:lc, :]
        decay = w_inter[lc - 1:lc, :]
        w_last = jnp.exp(b_last - b_col + i_col - m_last) * scale
        kw = (k.astype(F32) * w_last).astype(BF16)
        c_sc[hd] = decay * c_prev + _dot_dims(kw, v, ((0,), (0,)))
        n_sc[hd] = decay * n_prev + _dot_dims(kw, ones, ((0,), (0,)))
        m_sc[hd:hd + 1, :] = jnp.broadcast_to(m_last, (1, LANES))

    @pl.when(ci == pl.num_programs(1) - 1)
    def _():
        c_out[...] = c_sc[...]
        n_out[...] = n_sc[...]
        m_out[...] = m_sc[...]


def _mlstm_prompt(qkvo, gcol, grow, mnorm_g):
    nc = SEQ // MCHUNK
    return pl.pallas_call(
        _mlstm_prompt_body,
        grid=(N_PROMPT, nc),
        in_specs=[pl.BlockSpec((MCHUNK, 4 * MWIDTH), lambda b, c: (b * nc + c, 0)),
                  pl.BlockSpec((MCHUNK, LANES), lambda b, c: (b * nc + c, 0)),
                  pl.BlockSpec((8, MCHUNK), lambda b, c: (0, b * nc + c)),
                  pl.BlockSpec((1, MWIDTH), lambda b, c: (0, 0))],
        out_specs=[pl.BlockSpec((MCHUNK, MWIDTH), lambda b, c: (b * nc + c, 0)),
                   pl.BlockSpec((None, HEADS, DH, DH), lambda b, c: (b, 0, 0, 0)),
                   pl.BlockSpec((None, HEADS, DH, DH), lambda b, c: (b, 0, 0, 0)),
                   pl.BlockSpec((None, 8, LANES), lambda b, c: (b, 0, 0))],
        out_shape=[jax.ShapeDtypeStruct((N_PROMPT * SEQ, MWIDTH), BF16),
                   jax.ShapeDtypeStruct((N_PROMPT, HEADS, DH, DH), F32),
                   jax.ShapeDtypeStruct((N_PROMPT, HEADS, DH, DH), F32),
                   jax.ShapeDtypeStruct((N_PROMPT, 8, LANES), F32)],
        scratch_shapes=[pltpu.VMEM((HEADS, DH, DH), F32),
                        pltpu.VMEM((HEADS, DH, DH), F32),
                        pltpu.VMEM((8, LANES), F32)],
        compiler_params=_cparams("parallel", "arbitrary"),
        name="mlstm_prompt",
    )(qkvo, gcol, grow, mnorm_g)


MSTEP_TOK = 8


def _mlstm_step_body(q_ref, qkt_ref, ig_ref, lf_ref, m0_ref, c0_ref, n0_ref, mg_ref, hm_ref, c_out, n_out, m_out, qc_sc):
    scale = DH ** -0.5
    ig = ig_ref[...]
    m_inter = lf_ref[...] + m0_ref[...]
    m_t = jnp.maximum(m_inter, ig)
    w_inter_all = jnp.exp(m_inter - m_t)
    w_all = jnp.exp(ig - m_t) * scale
    floor_all = jnp.exp(-m_t)
    m_out[...] = m_t
    for hd in range(HEADS):
        sl = slice(hd * DH, (hd + 1) * DH)
        q = q_ref[:, sl].astype(F32)
        k = q_ref[:, MWIDTH + hd * DH:MWIDTH + (hd + 1) * DH].astype(F32)
        v = q_ref[:, 2 * MWIDTH + hd * DH:2 * MWIDTH + (hd + 1) * DH].astype(F32)
        o = q_ref[:, 3 * MWIDTH + hd * DH:3 * MWIDTH + (hd + 1) * DH].astype(F32)
        n0 = n0_ref[:, sl]
        wi = w_inter_all[:, hd:hd + 1]
        wk = w_all[:, hd:hd + 1]
        s = jnp.sum(q * k, axis=-1, keepdims=True) * wk
        qn = jnp.sum(q * n0, axis=-1, keepdims=True)
        for j in range(MSTEP_TOK):
            c0 = c0_ref[j, hd]
            q_col = qkt_ref[hd, :, j:j + 1]
            k_col = qkt_ref[HEADS + hd, :, j:j + 1]
            qc_sc[j:j + 1, sl] = jnp.sum(q_col * c0, axis=0, keepdims=True)
            c_out[j, hd] = wi[j:j + 1, :] * c0 + (wk[j:j + 1, :] * k_col) * v[j:j + 1, :]
        num = wi * qc_sc[:, sl] + s * v
        den = wi * qn + s
        h = num / jnp.maximum(jnp.abs(den), floor_all[:, hd:hd + 1])
        hn = h * lax.rsqrt(jnp.mean(h * h, axis=-1, keepdims=True) + EPS)
        hm_ref[:, sl] = hn * mg_ref[:, sl] * _sigmoid(o)
        n_out[:, sl] = wi * n0 + wk * k


def _mlstm_step(qkvo, qkt, ig, lf, m0, c0, n0, mnorm_g):
    t = qkvo.shape[0]
    tk = MSTEP_TOK
    return pl.pallas_call(
        _mlstm_step_body,
        grid=(t // tk,),
        in_specs=[pl.BlockSpec((tk, 4 * MWIDTH), lambda i: (i, 0)),
                  pl.BlockSpec((None, 2 * HEADS, DH, tk), lambda i: (i, 0, 0, 0)),
                  pl.BlockSpec((tk, LANES), lambda i: (i, 0)),
                  pl.BlockSpec((tk, LANES), lambda i: (i, 0)),
                  pl.BlockSpec((tk, LANES), lambda i: (i, 0)),
                  pl.BlockSpec((tk, HEADS, DH, DH), lambda i: (i, 0, 0, 0)),
                  pl.BlockSpec((tk, MWIDTH), lambda i: (i, 0)),
                  pl.BlockSpec((1, MWIDTH), lambda i: (0, 0))],
        out_specs=[pl.BlockSpec((tk, MWIDTH), lambda i: (i, 0)),
                   pl.BlockSpec((tk, HEADS, DH, DH), lambda i: (i, 0, 0, 0)),
                   pl.BlockSpec((tk, MWIDTH), lambda i: (i, 0)),
                   pl.BlockSpec((tk, LANES), lambda i: (i, 0))],
        out_shape=[jax.ShapeDtypeStruct((t, MWIDTH), F32),
                   jax.ShapeDtypeStruct((t, HEADS, DH, DH), F32),
                   jax.ShapeDtypeStruct((t, MWIDTH), F32),
                   jax.ShapeDtypeStruct((t, LANES), F32)],
        scratch_shapes=[pltpu.VMEM((tk, MWIDTH), F32)],
        compiler_params=_cparams("parallel"),
        name="mlstm_step",
    )(qkvo, qkt, ig, lf, m0, c0, n0, mnorm_g)


def _post_body(x_ref, y5_ref, hm_ref, gg_ref, g1_ref, sc_ref, sh_ref, n2_ref, wglu_ref, wmo_ref, wo_ref, rw_ref, rb_ref,
               cnt0_ref, x1_ref, h2_ref, eid_ref, wgt_ref, rank_ref, cnt_ref, run_sc, *, n_real, n_steps):
    i = pl.program_id(0)

    @pl.when(i == 0)
    def _():
        run_sc[...] = cnt0_ref[...]

    if n_steps > n_real:
        @pl.when(i >= n_real)
        def _():
            h2_ref[...] = jnp.zeros_like(h2_ref)

        pl.when(i < n_real)(functools.partial(
            _post_tile, x_ref, y5_ref, hm_ref, gg_ref, g1_ref, sc_ref, sh_ref, n2_ref, wglu_ref, wmo_ref, wo_ref,
            rw_ref, rb_ref, x1_ref, h2_ref, eid_ref, wgt_ref, rank_ref, run_sc))
    else:
        _post_tile(x_ref, y5_ref, hm_ref, gg_ref, g1_ref, sc_ref, sh_ref, n2_ref, wglu_ref, wmo_ref, wo_ref,
                   rw_ref, rb_ref, x1_ref, h2_ref, eid_ref, wgt_ref, rank_ref, run_sc)
    cnt_ref[...] = run_sc[...]


def _post_tile(x_ref, y5_ref, hm_ref, gg_ref, g1_ref, sc_ref, sh_ref, n2_ref, wglu_ref, wmo_ref, wo_ref, rw_ref, rb_ref,
               x1_ref, h2_ref, eid_ref, wgt_ref, rank_ref, run_sc):
    g5 = _gelu_tanh(y5_ref[...]).astype(BF16)
    glu = _dot(g5, wglu_ref[...])
    br_s5 = glu[:, :D_MODEL] * _sigmoid(glu[:, D_MODEL:])
    br_m = _dot(hm_ref[...].astype(BF16), wmo_ref[...])
    gg = gg_ref[...].astype(F32)
    merged = _sigmoid(gg[:, :D_MODEL]) * br_s5 + _sigmoid(gg[:, D_MODEL:]) * br_m
    x1 = x_ref[...] + g1_ref[...] * _dot(merged.astype(BF16), wo_ref[...])
    x1_ref[...] = x1
    ms = jnp.mean(x1 * x1, axis=-1, keepdims=True)
    h2 = x1 * lax.rsqrt(ms + EPS) * n2_ref[...] * (1.0 + sc_ref[...]) + sh_ref[...]
    h2_ref[...] = h2
    logits = _dot_hp(h2, rw_ref[...]) + rb_ref[...]
    tm = logits.shape[0]
    lane = lax.broadcasted_iota(jnp.int32, logits.shape, 1)
    l = jnp.where(lane < N_EXPERTS, logits, -jnp.inf)
    vals, hots, idxs = [], [], []
    for _ in range(TOP_K):
        mx = jnp.max(l, axis=-1, keepdims=True)
        idx = jnp.min(jnp.where(l == mx, lane, LANES), axis=-1, keepdims=True)
        hot = lane == idx
        vals.append(mx)
        hots.append(hot)
        idxs.append(idx)
        l = jnp.where(hot, -jnp.inf, l)
    ex = [jnp.exp(vk - vals[0]) for vk in vals]
    tot = ex[0] + ex[1] + ex[2] + ex[3]
    chosen = jnp.where(hots[0] | hots[1] | hots[2] | hots[3], 1.0, 0.0)
    r_io = lax.broadcasted_iota(jnp.int32, (tm, tm), 0)
    c_io = lax.broadcasted_iota(jnp.int32, (tm, tm), 1)
    earlier = (c_io < r_io).astype(BF16)
    before = run_sc[...] + _dot(earlier, chosen.astype(BF16))
    eid = jnp.zeros(logits.shape, jnp.int32)
    wgt = jnp.zeros(logits.shape, F32)
    rank = jnp.zeros(logits.shape, F32)
    for k in range(TOP_K):
        eid = jnp.where(lane == k, idxs[k], eid)
        wgt = jnp.where(lane == k, ex[k] / tot, wgt)
        rank = jnp.where(lane == k, jnp.sum(jnp.where(hots[k], before, 0.0), axis=-1, keepdims=True), rank)
    eid_ref[...] = eid
    wgt_ref[...] = wgt
    rank_ref[...] = rank.astype(jnp.int32)
    run_sc[...] += jnp.sum(chosen, axis=0, keepdims=True)


def _post(x2d, y5, hm, gg, mod3, rows_per_mod, tm, n2, wglu, wmo, wo, rw, rb, cnt0, h2_rows):
    t = x2d.shape[0]
    mrows = mod3.shape[1]
    n_real = t // tm
    n_steps = -(-h2_rows // tm)
    last = n_real - 1

    def mod_spec(j):
        return pl.BlockSpec((None, mrows, D_MODEL), lambda i: ((jnp.minimum(i, last) * tm) // rows_per_mod, 0, j))

    const = lambda shape: pl.BlockSpec(shape, lambda i: (0,) * len(shape))
    tile = lambda w: pl.BlockSpec((tm, w), lambda i: (jnp.minimum(i, last), 0))
    return pl.pallas_call(
        functools.partial(_post_body, n_real=n_real, n_steps=n_steps),
        grid=(n_steps,),
        in_specs=[tile(D_MODEL), tile(S5_WIDTH), tile(MWIDTH), tile(2 * D_MODEL),
                  mod_spec(2), mod_spec(4), mod_spec(3), const((1, D_MODEL)),
                  const(wglu.shape), const(wmo.shape), const(wo.shape), const(rw.shape), const(rb.shape),
                  const((1, LANES))],
        out_specs=[tile(D_MODEL), pl.BlockSpec((tm, D_MODEL), lambda i: (i, 0)), tile(LANES), tile(LANES), tile(LANES),
                   const((1, LANES))],
        out_shape=[jax.ShapeDtypeStruct((t, D_MODEL), F32),
                   jax.ShapeDtypeStruct((h2_rows, D_MODEL), F32),
                   jax.ShapeDtypeStruct((t, LANES), jnp.int32),
                   jax.ShapeDtypeStruct((t, LANES), F32),
                   jax.ShapeDtypeStruct((t, LANES), jnp.int32),
                   jax.ShapeDtypeStruct((1, LANES), F32)],
        scratch_shapes=[pltpu.VMEM((1, LANES), F32)],
        compiler_params=_cparams("arbitrary"),
        name="post",
    )(x2d, y5, hm, gg, mod3, mod3, mod3, n2, wglu, wmo, wo, rw, rb, cnt0)


N_TOKENS = N_PROMPT * SEQ + N_SAMPLE
N_ASSIGN = N_TOKENS * TOP_K
MOE_TILE = 256
MOE_TILES = N_ASSIGN // MOE_TILE + N_EXPERTS
N_SLOT = MOE_TILES * MOE_TILE
CMB_TILE = 128


MOE_RING = 3


def _moe_body(te_ref, nt_ref, nxe_ref, gpar_ref, idx0_ref, idx1_ref, idx2_ref, h2_hbm, wup_hbm, bup_ref, wdn_hbm, bdn_ref,
              y_ref, xb0, xb1, xb2, sem, wbuf_up, wbuf_dn, wsem, wup_bf, wdn_bf):
    i = pl.program_id(0)
    nt = nt_ref[0]
    bufs = (xb0, xb1, xb2)
    last = MOE_TILES - 1

    def gather(idx_ref, slot):
        for r in range(MOE_TILE):
            pltpu.make_async_copy(h2_hbm.at[pl.ds(idx_ref[0, r], 1)], bufs[slot].at[pl.ds(r, 1)],
                                  sem.at[slot]).start()

    def wait(slot):
        pltpu.make_async_copy(h2_hbm.at[pl.ds(0, MOE_TILE)], bufs[slot], sem.at[slot]).wait()

    def weights(e, slot):
        return (pltpu.make_async_copy(wup_hbm.at[e], wbuf_up.at[slot], wsem.at[0, slot]),
                pltpu.make_async_copy(wdn_hbm.at[e], wbuf_dn.at[slot], wsem.at[1, slot]))

    wslot = gpar_ref[i]

    @pl.when(i == 0)
    def _():
        for c in weights(te_ref[0], wslot):
            c.start(priority=1)
        gather(idx0_ref, 0)
        gather(idx1_ref, 1)

    new_expert = jnp.logical_or(i == 0, te_ref[i] != te_ref[jnp.maximum(i - 1, 0)])
    live = i < nt

    @pl.when(jnp.logical_and(new_expert, live))
    def _():
        for c in weights(0, wslot):
            c.wait()
        wup_bf[...] = wbuf_up[wslot].astype(BF16)
        wdn_bf[...] = wbuf_dn[wslot].astype(BF16)
        nxe = nxe_ref[i]

        @pl.when(nxe >= 0)
        def _():
            for c in weights(nxe, 1 - wslot):
                c.start(priority=1)

    def step(slot):
        wait(slot)
        gather(idx2_ref, (slot + 2) % MOE_RING)
        gu = _dot(bufs[slot][...].astype(BF16), wup_bf[...]) + bup_ref[...]
        g = jnp.minimum(gu[:, :D_FF], SWIGLU_LIMIT)
        up = jnp.clip(gu[:, D_FF:], -SWIGLU_LIMIT, SWIGLU_LIMIT)
        act = (up + 1.0) * g * _sigmoid(SWIGLU_ALPHA * g)
        y_ref[...] = _dot(act.astype(BF16), wdn_bf[...]) + bdn_ref[...]

    for slot in range(MOE_RING):
        pl.when(jnp.logical_and(live, i % MOE_RING == slot))(functools.partial(step, slot))

    @pl.when(jnp.logical_not(live))
    def _():
        y_ref[...] = jnp.zeros_like(y_ref)

    drain = jnp.logical_and(jnp.logical_not(live), i < nt + 2)
    for slot in range(MOE_RING):
        pl.when(jnp.logical_and(drain, i % MOE_RING == slot))(functools.partial(wait, slot))
    pl.when(jnp.logical_and(i == last, last - 1 < nt))(functools.partial(wait, (last + 1) % MOE_RING))
    pl.when(jnp.logical_and(i == last, last < nt))(functools.partial(wait, (last + 2) % MOE_RING))


def _moe(tile_expert, n_tiles, next_expert, group_parity, slot_token, h2_all, wup, bup, wdn, bdn):
    idx3 = slot_token.reshape(MOE_TILES, 1, MOE_TILE)
    ahead = lambda k: pl.BlockSpec((None, 1, MOE_TILE),
                                   lambda i, *_: (jnp.minimum(i + k, MOE_TILES - 1), 0, 0), memory_space=pltpu.SMEM)
    grid_spec = pltpu.PrefetchScalarGridSpec(
        num_scalar_prefetch=4,
        grid=(MOE_TILES,),
        in_specs=[ahead(0), ahead(1), ahead(2),
                  pl.BlockSpec(memory_space=pl.ANY),
                  pl.BlockSpec(memory_space=pl.ANY),
                  pl.BlockSpec((None, 1, 2 * D_FF), lambda i, te, *_: (te[i], 0, 0)),
                  pl.BlockSpec(memory_space=pl.ANY),
                  pl.BlockSpec((None, 1, D_MODEL), lambda i, te, *_: (te[i], 0, 0))],
        out_specs=pl.BlockSpec((MOE_TILE, D_MODEL), lambda i, *_: (i, 0)),
        scratch_shapes=[pltpu.VMEM((MOE_TILE, D_MODEL), F32), pltpu.VMEM((MOE_TILE, D_MODEL), F32),
                        pltpu.VMEM((MOE_TILE, D_MODEL), F32),
                        pltpu.SemaphoreType.DMA((MOE_RING,)),
                        pltpu.VMEM((2, D_MODEL, 2 * D_FF), F32), pltpu.VMEM((2, D_FF, D_MODEL), F32),
                        pltpu.SemaphoreType.DMA((2, 2)),
                        pltpu.VMEM((D_MODEL, 2 * D_FF), BF16), pltpu.VMEM((D_FF, D_MODEL), BF16)])
    return pl.pallas_call(
        _moe_body,
        grid_spec=grid_spec,
        out_shape=jax.ShapeDtypeStruct((N_SLOT, D_MODEL), F32),
        compiler_params=_cparams("arbitrary"),
        name="moe",
    )(tile_expert, n_tiles, next_expert, group_parity, idx3, idx3, idx3, h2_all, wup, bup, wdn, bdn)


def _combine_body(cur_ref, nxt_ref, wgt_ref, x1_ref, g2_ref, fg_ref, y_hbm, o_ref, yb0, yb1, sem, *, n_steps):
    i = pl.program_id(0)
    tm = CMB_TILE

    def gather(idx_ref, buf, s):
        for r in range(tm):
            for k in range(TOP_K):
                pltpu.make_async_copy(y_hbm.at[pl.ds(idx_ref[0, r * TOP_K + k], 1)],
                                      buf.at[pl.ds(k * tm + r, 1)], sem.at[s]).start(priority=k % 2)

    def wait(buf, s):
        pltpu.make_async_copy(y_hbm.at[pl.ds(0, TOP_K * tm)], buf, sem.at[s]).wait()

    @pl.when(i == 0)
    def _():
        gather(cur_ref, yb0, 0)

    def step(cur, nxt, s_cur, s_nxt):
        wait(cur, s_cur)
        gather(nxt_ref, nxt, s_nxt)
        wgt = wgt_ref[...]
        acc = wgt[:, 0:1] * cur[0:tm, :]
        for k in range(1, TOP_K):
            acc = acc + wgt[:, k:k + 1] * cur[k * tm:(k + 1) * tm, :]
        xo = x1_ref[...] + g2_ref[...] * acc
        ms = jnp.mean(xo * xo, axis=-1, keepdims=True)
        o_ref[...] = xo * lax.rsqrt(ms + EPS) * fg_ref[...]

    pl.when(i % 2 == 0)(functools.partial(step, yb0, yb1, 0, 1))
    pl.when(i % 2 == 1)(functools.partial(step, yb1, yb0, 1, 0))
    last_slot = n_steps % 2
    pl.when(i == n_steps - 1)(functools.partial(wait, yb1 if last_slot else yb0, last_slot))


def _combine(pos, wgt, x1, mod3, rows_per_mod, fg, y_slots):
    t = x1.shape[0]
    tm = CMB_TILE
    n = t // tm
    mrows = mod3.shape[1]
    assert mrows in (1, tm)
    pos3 = pos.reshape(n, 1, tm * TOP_K)
    return pl.pallas_call(
        functools.partial(_combine_body, n_steps=n),
        grid=(n,),
        in_specs=[pl.BlockSpec((None, 1, tm * TOP_K), lambda i: (i, 0, 0), memory_space=pltpu.SMEM),
                  pl.BlockSpec((None, 1, tm * TOP_K), lambda i: (jnp.minimum(i + 1, n - 1), 0, 0),
                               memory_space=pltpu.SMEM),
                  pl.BlockSpec((tm, LANES), lambda i: (i, 0)),
                  pl.BlockSpec((tm, D_MODEL), lambda i: (i, 0)),
                  pl.BlockSpec((None, mrows, D_MODEL), lambda i: ((i * tm) // rows_per_mod, 0, 5)),
                  pl.BlockSpec((1, D_MODEL), lambda i: (0, 0)),
                  pl.BlockSpec(memory_space=pl.ANY)],
        out_specs=pl.BlockSpec((tm, D_MODEL), lambda i: (i, 0)),
        out_shape=jax.ShapeDtypeStruct((t, D_MODEL), F32),
        scratch_shapes=[pltpu.VMEM((TOP_K * tm, D_MODEL), F32), pltpu.VMEM((TOP_K * tm, D_MODEL), F32),
                        pltpu.SemaphoreType.DMA((2,))],
        compiler_params=_cparams("arbitrary"),
        name="combine",
    )(pos3, pos3, wgt, x1, mod3, fg, y_slots)


def _routing(eid_p, rank_p, eid_s, rank_s, counts):
    cnt = counts[0, :N_EXPERTS].astype(jnp.int32)
    ntile = (cnt + MOE_TILE - 1) // MOE_TILE
    tile_end = jnp.cumsum(ntile)
    poff = (tile_end - ntile) * MOE_TILE
    total = tile_end[-1]
    j = jnp.arange(MOE_TILES, dtype=jnp.int32)
    te = jnp.sum((j[:, None] >= tile_end[None, :]).astype(jnp.int32), axis=1)
    te_last = jnp.sum(((total - 1) >= tile_end).astype(jnp.int32))
    tile_expert = jnp.where(j < total, te, te_last).astype(jnp.int32)
    ex = jnp.arange(N_EXPERTS, dtype=jnp.int32)
    used = ntile > 0
    later = jnp.where(used[None, :] & (ex[None, :] > ex[:, None]), ex[None, :], N_EXPERTS)
    nxt_e = jnp.min(later, axis=1)
    nxt_e = jnp.where(nxt_e < N_EXPERTS, nxt_e, -1).astype(jnp.int32)
    par_e = ((jnp.cumsum(used.astype(jnp.int32)) - 1) % 2).astype(jnp.int32)
    pick = tile_expert[:, None] == ex[None, :]
    next_expert = jnp.sum(jnp.where(pick, nxt_e[None, :], 0), axis=1).astype(jnp.int32)
    group_parity = jnp.sum(jnp.where(pick, par_e[None, :], 0), axis=1).astype(jnp.int32)

    def pos_of(eid, rank):
        e = eid[:, :TOP_K]
        off = jnp.sum(jnp.where(e[:, :, None] == ex[None, None, :], poff[None, None, :], 0), axis=-1)
        return off + rank[:, :TOP_K]

    pos_p = pos_of(eid_p, rank_p)
    pos_s = pos_of(eid_s, rank_s)
    tok = jnp.concatenate([jnp.repeat(jnp.arange(N_PROMPT * SEQ, dtype=jnp.int32), TOP_K),
                           jnp.repeat(N_PROMPT * SEQ + jnp.arange(N_SAMPLE, dtype=jnp.int32), TOP_K)])
    pos_all = jnp.concatenate([pos_p.reshape(-1), pos_s.reshape(-1)])
    slot_token = jnp.zeros((N_SLOT,), jnp.int32).at[pos_all].set(tok, unique_indices=True)
    return tile_expert, total.reshape(1).astype(jnp.int32), next_expert, group_parity, slot_token, pos_p, pos_s


def _unpack_s5_state(x):
    n = x.shape[1]
    z = x.reshape(N_SUPER, n, 2, S5_SUPER, S5_STATE).transpose(2, 1, 0, 3, 4).reshape(2, n, S5_GROUPS, S5_STATE)
    return z[0], z[1]


def _pack_s5_state(re, im):
    n = re.shape[0]
    z = jnp.stack([re, im], axis=0).reshape(2, n, N_SUPER, S5_SUPER, S5_STATE)
    return z.transpose(2, 1, 0, 3, 4).reshape(N_SUPER, n, SUPER_STATE)


def kernel(x_prompt, x_sample, c_prompt, c_sample, state_s5_re, state_s5_im, state_mlstm_C, state_mlstm_n, state_mlstm_m, norm1_g, norm2_g, final_norm_g, w_ada, b_ada, w_in, s5_lambda_re, s5_lambda_im, s5_log_dt, s5_B_re, s5_B_im, s5_C_re, s5_C_im, s5_D, s5_w_glu, mlstm_b_i, mlstm_b_f, mlstm_norm_g, mlstm_w_out, w_out, router_w, router_b, expert_w_up, expert_b_up, expert_w_down, expert_b_down):
    assert w_in.shape[0] == 1, "single layer"
    tp = N_PROMPT * SEQ
    xp = x_prompt.reshape(tp, D_MODEL).astype(F32)
    xs = x_sample.reshape(N_SAMPLE, D_MODEL).astype(F32)

    w = w_in[0]
    c0 = S5_WIDTH
    c1 = c0 + 4 * MWIDTH
    c2 = c1 + 2 * HEADS
    wu = w[:, :c0].astype(BF16)
    wq = w[:, c0:c1].astype(BF16)
    wif = jnp.pad(w[:, c1:c2].astype(F32), ((0, 0), (0, LANES - 2 * HEADS)))
    wift = w[:, c1:c2].astype(F32).T
    wg = w[:, c2:].astype(BF16)
    bvec = jnp.concatenate([mlstm_b_i[0], mlstm_b_f[0]]).astype(F32)
    bif = jnp.pad(bvec, (0, LANES - 2 * HEADS)).reshape(1, LANES)
    bift = bvec.reshape(2 * HEADS, 1)
    wparts = (wu, wq, wg, wif, wift, bif, bift)
    g1 = norm1_g[0].reshape(1, D_MODEL).astype(F32)
    n2 = norm2_g[0].reshape(1, D_MODEL).astype(F32)
    fg = final_norm_g.reshape(1, D_MODEL).astype(F32)
    mg = mlstm_norm_g[0].reshape(1, MWIDTH).astype(F32)
    wglu = s5_w_glu[0].astype(BF16)
    wmo = mlstm_w_out[0].astype(BF16)
    wo = w_out[0].astype(BF16)
    rw = jnp.pad(router_w[0].astype(F32), ((0, 0), (0, LANES - N_EXPERTS)))
    rb = jnp.pad(router_b[0].astype(F32), (0, LANES - N_EXPERTS)).reshape(1, LANES)
    wup = expert_w_up[0].astype(F32)
    wdn = expert_w_down[0].astype(F32)
    bup = expert_b_up[0].astype(F32).reshape(N_EXPERTS, 1, 2 * D_FF)
    bdn = expert_b_down[0].astype(F32).reshape(N_EXPERTS, 1, D_MODEL)
    kpair, fmat, emat, a16, a1, k0, e0, f1 = _s5_tables(
        s5_lambda_re[0], s5_lambda_im[0], s5_log_dt[0], s5_B_re[0], s5_B_im[0], s5_C_re[0], s5_C_im[0], s5_D[0])

    c_all = jnp.concatenate([c_prompt, c_sample], axis=0).astype(F32)
    mod = _adaln(c_all, w_ada[0].astype(F32), b_ada[0].astype(F32))
    mod_p = mod[:N_PROMPT].reshape(N_PROMPT, 1, 6 * D_MODEL)
    mod_s = mod[N_PROMPT:].reshape(1, N_SAMPLE, 6 * D_MODEL)

    u, qkvo, gg, gcol, grow = _inproj(xp, mod_p, SEQ, 1024, g1, wparts)
    y5, xend = _s5_prompt(u, kpair, fmat, emat, a16)
    hm, c_p, n_p, m_p = _mlstm_prompt(qkvo, gcol, grow, mg)
    cnt0 = jnp.zeros((1, LANES), F32)
    x1, h2_all, eid_p, wgt_p, rank_p, cnt_p = _post(xp, y5, hm, gg, mod_p, SEQ, 512, n2, wglu, wmo, wo, rw, rb,
                                                    cnt0, N_TOKENS)
    p_re, p_im = _unpack_s5_state(xend.reshape(N_SUPER, N_PROMPT, SUPER_STATE))
    n_p = n_p[..., 0]
    m_p = m_p[:, :HEADS, 0]

    us, qs, ggs, gcs, _ = _inproj(xs, mod_s, N_SAMPLE, N_SAMPLE, g1, wparts)
    x0 = _pack_s5_state(state_s5_re[0].astype(F32), state_s5_im[0].astype(F32))
    y5s, xns = _s5_step(us, x0, k0, e0, f1, a1)
    s_re, s_im = _unpack_s5_state(xns)
    qkt = qs[:, :2 * MWIDTH].astype(F32).reshape(N_SAMPLE // MSTEP_TOK, MSTEP_TOK, 2 * HEADS, DH).transpose(0, 2, 3, 1)
    pad_heads = lambda a: jnp.pad(a, ((0, 0), (0, LANES - HEADS)))
    m0 = pad_heads(state_mlstm_m[0].astype(F32))
    hms, c_s, n_s, m_s = _mlstm_step(qs, qkt, pad_heads(gcs[:, :HEADS]), pad_heads(gcs[:, HEADS:2 * HEADS]), m0,
                                     state_mlstm_C[0].astype(F32),
                                     state_mlstm_n[0].astype(F32).reshape(N_SAMPLE, MWIDTH), mg)
    x1s, h2s, eid_s, wgt_s, rank_s, cnt_all = _post(xs, y5s, hms, ggs, mod_s, N_SAMPLE, N_SAMPLE, n2, wglu, wmo, wo, rw, rb,
                                                    cnt_p, N_SAMPLE)

    h2_all = lax.dynamic_update_slice(h2_all, h2s, (tp, 0))
    tile_expert, n_tiles, next_expert, group_parity, slot_token, pos_p, pos_s = _routing(
        eid_p, rank_p, eid_s, rank_s, cnt_all)
    y_slots = _moe(tile_expert, n_tiles, next_expert, group_parity, slot_token, h2_all, wup, bup, wdn, bdn)
    y_p = _combine(pos_p, wgt_p, x1, mod_p, SEQ, fg, y_slots)
    y_s = _combine(pos_s, wgt_s, x1s, mod_s, N_SAMPLE, fg, y_slots)

    return (y_p.reshape(N_PROMPT, SEQ, D_MODEL).astype(x_prompt.dtype),
            y_s.reshape(N_SAMPLE, 1, D_MODEL).astype(x_sample.dtype),
            p_re[None], p_im[None], c_p[None], n_p[None], m_p[None],
            s_re[None], s_im[None], c_s[None],
            n_s.reshape(1, N_SAMPLE, HEADS, DH), m_s[:, :HEADS][None])
```

```python
import functools
import math

import jax
import jax.numpy as jnp
from jax import lax
from jax.experimental import pallas as pl
from jax.experimental.pallas import tpu as pltpu

F32 = jnp.float32
BF16 = jnp.bfloat16

D_MODEL = 1024
SEQ = 2048
N_PROMPT = 8
N_SAMPLE = 128
S5_WIDTH = 512
S5_GROUP = 16
S5_GROUPS = 32
S5_STATE = 64
HEADS = 4
DH = 128
MWIDTH = HEADS * DH
N_EXPERTS = 32
TOP_K = 4
D_FF = 1024
SWIGLU_LIMIT = 7.0
SWIGLU_ALPHA = 1.702
EPS = 1e-6

LANES = 128
S5_CHUNK = 16
S5_SUPER = LANES // S5_GROUP
N_SUPER = S5_GROUPS // S5_SUPER
SUPER_STATE = 2 * S5_SUPER * S5_STATE
MCHUNK = 128
VMEM_LIMIT = 56 * 1024 * 1024


def _cparams(*sem):
    return pltpu.CompilerParams(dimension_semantics=sem, vmem_limit_bytes=VMEM_LIMIT)


def _dot(a, b):
    return jnp.dot(a, b, preferred_element_type=F32)


def _dot_dims(a, b, dims):
    return lax.dot_general(a, b, (dims, ((), ())), preferred_element_type=F32)


def _split_bf16(a):
    hi = a.astype(BF16)
    lo = (a - hi.astype(F32)).astype(BF16)
    return hi, lo


def _dot_hp(a, b, dims=((1,), (0,))):
    ah, al = _split_bf16(a)
    bh, bl = _split_bf16(b)
    return _dot_dims(ah, bh, dims) + (_dot_dims(al, bh, dims) + _dot_dims(ah, bl, dims))


def _dot_exact_rhs(a, b_exact, dims=((1,), (0,))):
    ah, al = _split_bf16(a)
    return _dot_dims(ah, b_exact, dims) + _dot_dims(al, b_exact, dims)


def _log_sigmoid(x):
    return -(jnp.maximum(-x, 0.0) + jnp.log1p(jnp.exp(-jnp.abs(x))))


def _sigmoid(x):
    return 0.5 * (1.0 + jnp.tanh(0.5 * x))


def _gelu_tanh(x):
    c = math.sqrt(2.0 / math.pi)
    return 0.5 * x * (1.0 + jnp.tanh(c * (x + 0.044715 * (x * x * x))))


def _adaln_body(c_ref, w_ref, b_ref, o_ref):
    c = c_ref[...]
    s = c * _sigmoid(c)
    o_ref[...] = _dot_hp(s, w_ref[...]) + b_ref[...]


def _adaln(c_all, w_ada, b_ada):
    n = c_all.shape[0]
    tn = 1024
    return pl.pallas_call(
        _adaln_body,
        grid=(6 * D_MODEL // tn,),
        in_specs=[pl.BlockSpec((n, D_MODEL), lambda j: (0, 0)),
                  pl.BlockSpec((D_MODEL, tn), lambda j: (0, j)),
                  pl.BlockSpec((1, tn), lambda j: (0, j))],
        out_specs=pl.BlockSpec((n, tn), lambda j: (0, j)),
        out_shape=jax.ShapeDtypeStruct((n, 6 * D_MODEL), F32),
        compiler_params=_cparams("parallel"),
        name="adaln",
    )(c_all, w_ada, b_ada.reshape(1, -1))


def _inproj_body(x_ref, g1_ref, sc_ref, sh_ref, wu_ref, wq_ref, wg_ref, wif_ref, wift_ref, bif_ref, bift_ref,
                 u_ref, q_ref, gg_ref, if_ref, ift_ref):
    x = x_ref[...]
    ms = jnp.mean(x * x, axis=-1, keepdims=True)
    h = x * lax.rsqrt(ms + EPS) * g1_ref[...] * (1.0 + sc_ref[...]) + sh_ref[...]
    hb = h.astype(BF16)
    u_ref[...] = _dot(hb, wu_ref[...])
    q_ref[...] = _dot(hb, wq_ref[...]).astype(BF16)
    gg_ref[...] = _dot(hb, wg_ref[...]).astype(BF16)
    gc = _dot_hp(h, wif_ref[...]) + bif_ref[...]
    lane = lax.broadcasted_iota(jnp.int32, gc.shape, 1)
    if_ref[...] = jnp.where(lane < HEADS, gc, _log_sigmoid(gc))
    gr = _dot_hp(wift_ref[...], h, ((1,), (1,))) + bift_ref[...]
    sub = lax.broadcasted_iota(jnp.int32, gr.shape, 0)
    ift_ref[...] = jnp.where(sub < HEADS, gr, _log_sigmoid(gr))


def _inproj(x2d, mod3, rows_per_mod, tm, g1, wparts):
    t = x2d.shape[0]
    wu, wq, wg, wif, wift, bif, bift = wparts
    mrows = mod3.shape[1]

    def mod_spec(j):
        return pl.BlockSpec((None, mrows, D_MODEL), lambda i: ((i * tm) // rows_per_mod, 0, j))

    const = lambda shape: pl.BlockSpec(shape, lambda i: (0,) * len(shape), pipeline_mode=pl.Buffered(1))
    return pl.pallas_call(
        _inproj_body,
        grid=(t // tm,),
        in_specs=[pl.BlockSpec((tm, D_MODEL), lambda i: (i, 0)),
                  const((1, D_MODEL)), mod_spec(1), mod_spec(0),
                  const(wu.shape), const(wq.shape), const(wg.shape), const(wif.shape), const(wift.shape),
                  const(bif.shape), const(bift.shape)],
        out_specs=[pl.BlockSpec((tm, S5_WIDTH), lambda i: (i, 0)),
                   pl.BlockSpec((tm, 4 * MWIDTH), lambda i: (i, 0)),
                   pl.BlockSpec((tm, 2 * D_MODEL), lambda i: (i, 0)),
                   pl.BlockSpec((tm, LANES), lambda i: (i, 0)),
                   pl.BlockSpec((8, tm), lambda i: (0, i))],
        out_shape=[jax.ShapeDtypeStruct((t, S5_WIDTH), F32),
                   jax.ShapeDtypeStruct((t, 4 * MWIDTH), BF16),
                   jax.ShapeDtypeStruct((t, 2 * D_MODEL), BF16),
                   jax.ShapeDtypeStruct((t, LANES), F32),
                   jax.ShapeDtypeStruct((8, t), F32)],
        compiler_params=_cparams("parallel"),
        name="inproj",
    )(x2d, g1, mod3, mod3, wu, wq, wg, wif, wift, bif, bift)


def _s5_tables(lam_re, lam_im, log_dt, b_re, b_im, c_re, c_im, d_s5):
    hi = lax.Precision.HIGHEST
    dt = jnp.exp(log_dt.astype(F32))[:, None]
    lr, li = lam_re.astype(F32), lam_im.astype(F32)
    dpow = jnp.arange(S5_CHUNK + 1, dtype=F32)[:, None, None]
    mag = jnp.exp(dpow * (lr * dt))
    pw_re, pw_im = mag * jnp.cos(dpow * (li * dt)), mag * jnp.sin(dpow * (li * dt))
    ab_re, ab_im = pw_re[1], pw_im[1]
    den = lr * lr + li * li
    nr, ni = ab_re - 1.0, ab_im
    coef_re = (nr * lr + ni * li) / den
    coef_im = (ni * lr - nr * li) / den
    br, bi = b_re.astype(F32), b_im.astype(F32)
    bb_re = coef_re[..., None] * br - coef_im[..., None] * bi
    bb_im = coef_re[..., None] * bi + coef_im[..., None] * br
    cr, ci = c_re.astype(F32), c_im.astype(F32)
    cl_re = cr[None] * pw_re[:, :, None, :] - ci[None] * pw_im[:, :, None, :]
    cl_im = cr[None] * pw_im[:, :, None, :] + ci[None] * pw_re[:, :, None, :]
    kd = (jnp.einsum('dgop,gpi->dgio', cl_re[:S5_CHUNK], bb_re, precision=hi)
          - jnp.einsum('dgop,gpi->dgio', cl_im[:S5_CHUNK], bb_im, precision=hi))
    kd = kd.at[0].add(d_s5.astype(F32)[:, :, None] * jnp.eye(S5_GROUP, dtype=F32)[None])
    kext = jnp.concatenate([jnp.zeros_like(kd[:1]), kd], axis=0)
    kc = jnp.stack([kext[S5_CHUNK - 1::-1], kext[S5_CHUNK:0:-1]], axis=3)
    kc = kc.reshape(S5_CHUNK, N_SUPER, S5_SUPER * S5_GROUP, 2 * S5_GROUP).transpose(1, 0, 2, 3)
    kcomp = jnp.pad(kc.reshape(N_SUPER, S5_CHUNK * LANES, 2 * S5_GROUP), ((0, 0), (0, 0), (0, LANES - 2 * S5_GROUP)))
    rp_re, rp_im = pw_re[S5_CHUNK - 1::-1], pw_im[S5_CHUNK - 1::-1]
    f_re = rp_re[..., None] * bb_re[None] - rp_im[..., None] * bb_im[None]
    f_im = rp_re[..., None] * bb_im[None] + rp_im[..., None] * bb_re[None]
    fcat = jnp.stack([f_re, f_im], axis=2).transpose(0, 1, 4, 2, 3)
    fcomp = fcat.reshape(S5_CHUNK, N_SUPER, LANES, 2 * S5_STATE).transpose(1, 0, 2, 3)
    fcomp = fcomp.reshape(N_SUPER, S5_CHUNK * LANES, 2 * S5_STATE)
    ecat = jnp.stack([cl_re[1:], -cl_im[1:]], axis=0)
    er = ecat.transpose(2, 0, 4, 1, 3).reshape(N_SUPER, S5_SUPER, 2, S5_STATE, S5_CHUNK * S5_GROUP)
    ecomp = er.transpose(0, 2, 1, 3, 4).reshape(N_SUPER, SUPER_STATE, S5_CHUNK * S5_GROUP)

    def lay(re, im):
        z = jnp.stack([re, im], axis=0).reshape(2, N_SUPER, S5_SUPER * S5_STATE)
        return z.transpose(1, 0, 2).reshape(N_SUPER, 1, SUPER_STATE)

    a16 = lay(pw_re[S5_CHUNK], pw_im[S5_CHUNK])
    a1 = lay(ab_re, ab_im)
    return kcomp, fcomp, ecomp, a16, a1


def _expand_groups(comp, row_div, col_blk, n_cols, col_div):
    r, c = comp.shape
    ci = lax.broadcasted_iota(jnp.int32, (c, n_cols), 0)
    co = lax.broadcasted_iota(jnp.int32, (c, n_cols), 1)
    sel = jnp.logical_and(ci // col_blk == co // (S5_SUPER * col_blk), ci % col_blk == co % col_blk)
    rep = _dot(comp.astype(BF16), sel.astype(BF16))
    ro = lax.broadcasted_iota(jnp.int32, (r, n_cols), 0)
    cc = lax.broadcasted_iota(jnp.int32, (r, n_cols), 1)
    same = (ro // row_div) % S5_SUPER == (cc // col_div) % S5_SUPER
    return jnp.where(same, rep, 0.0).astype(BF16)


S5_NB = 4


def _s5_prompt_body(u_ref, kc_ref, fc_ref, ec_ref, a_ref, y_ref, xe_ref, k0_ref, e0_ref, f1_ref,
                    ucat, v_sc, xp_sc, kp_ref, f_ref, e_ref):
    n = SEQ // S5_CHUNK
    half = SUPER_STATE // 2

    @pl.when(pl.program_id(1) == 0)
    def _():
        kp_ref[...] = _expand_groups(kc_ref[...], S5_GROUP, S5_GROUP, 2 * LANES, S5_GROUP)
        f_ref[...] = _expand_groups(fc_ref[...], S5_GROUP, S5_STATE, SUPER_STATE, S5_STATE)
        e_ref[...] = _expand_groups(ec_ref[...], S5_STATE, S5_GROUP, S5_CHUNK * LANES, S5_GROUP)
        k0_ref[...] = kp_ref[(S5_CHUNK - 2) * LANES:(S5_CHUNK - 1) * LANES, :LANES]
        e0_ref[...] = e_ref[:, :LANES]
        f1_ref[...] = f_ref[(S5_CHUNK - 1) * LANES:, :]

    for bl in range(S5_NB):
        for j in range(S5_CHUNK):
            ucat[bl * n:(bl + 1) * n, j * LANES:(j + 1) * LANES] = (
                u_ref[pl.ds(bl * SEQ + j, n, stride=S5_CHUNK), :].astype(BF16))
    v_sc[...] = _dot(ucat[...], f_ref[...])
    a_r, a_i = a_ref[:, :half], a_ref[:, half:]
    xr = [jnp.zeros((1, half), F32) for _ in range(S5_NB)]
    xi = [jnp.zeros((1, half), F32) for _ in range(S5_NB)]
    for c in range(n):
        for bl in range(S5_NB):
            row = bl * n + c
            xp_sc[row:row + 1, :half] = xr[bl]
            xp_sc[row:row + 1, half:] = xi[bl]
            vr, vi = v_sc[row:row + 1, :half], v_sc[row:row + 1, half:]
            xr[bl], xi[bl] = a_r * xr[bl] - a_i * xi[bl] + vr, a_r * xi[bl] + a_i * xr[bl] + vi
    for bl in range(S5_NB):
        xe_ref[bl:bl + 1, :half] = xr[bl]
        xe_ref[bl:bl + 1, half:] = xi[bl]
    ye = _dot(xp_sc[...].astype(BF16), e_ref[...])
    for m in range(S5_CHUNK // 2):
        rows = (2 * m + 2) * LANES
        yp = _dot(ucat[:, :rows], kp_ref[(S5_CHUNK - 2 - 2 * m) * LANES:, :]) + ye[:, 2 * m * LANES:(2 * m + 2) * LANES]
        for bl in range(S5_NB):
            y_ref[pl.ds(bl * SEQ + 2 * m, n, stride=S5_CHUNK), :] = yp[bl * n:(bl + 1) * n, :LANES]
            y_ref[pl.ds(bl * SEQ + 2 * m + 1, n, stride=S5_CHUNK), :] = yp[bl * n:(bl + 1) * n, LANES:]


def _s5_prompt(u, kcomp, fcomp, ecomp, a16):
    n = S5_NB * SEQ // S5_CHUNK
    per_super = lambda shape: pl.BlockSpec((None,) + shape, lambda s, b: (s, 0, 0))
    return pl.pallas_call(
        _s5_prompt_body,
        grid=(N_SUPER, N_PROMPT // S5_NB),
        in_specs=[pl.BlockSpec((S5_NB * SEQ, LANES), lambda s, b: (b, s)),
                  per_super(kcomp.shape[1:]), per_super(fcomp.shape[1:]), per_super(ecomp.shape[1:]),
                  per_super((1, SUPER_STATE))],
        out_specs=[pl.BlockSpec((S5_NB * SEQ, LANES), lambda s, b: (b, s)),
                   pl.BlockSpec((None, None, S5_NB, SUPER_STATE), lambda s, b: (s, b, 0, 0)),
                   per_super((LANES, LANES)), per_super((SUPER_STATE, LANES)), per_super((LANES, SUPER_STATE))],
        out_shape=[jax.ShapeDtypeStruct(u.shape, F32),
                   jax.ShapeDtypeStruct((N_SUPER, N_PROMPT // S5_NB, S5_NB, SUPER_STATE), F32),
                   jax.ShapeDtypeStruct((N_SUPER, LANES, LANES), BF16),
                   jax.ShapeDtypeStruct((N_SUPER, SUPER_STATE, LANES), BF16),
                   jax.ShapeDtypeStruct((N_SUPER, LANES, SUPER_STATE), BF16)],
        scratch_shapes=[pltpu.VMEM((n, S5_CHUNK * LANES), BF16),
                        pltpu.VMEM((n, SUPER_STATE), F32),
                        pltpu.VMEM((n, SUPER_STATE), F32),
                        pltpu.VMEM((S5_CHUNK * LANES, 2 * LANES), BF16),
                        pltpu.VMEM((S5_CHUNK * LANES, SUPER_STATE), BF16),
                        pltpu.VMEM((SUPER_STATE, S5_CHUNK * LANES), BF16)],
        compiler_params=_cparams("parallel", "arbitrary"),
        name="s5_prompt",
    )(u, kcomp, fcomp, ecomp, a16)


def _s5_step_body(u_ref, x0_ref, k0_ref, e0_ref, f1_ref, a_ref, y_ref, xn_ref):
    half = SUPER_STATE // 2
    ub = u_ref[...].astype(BF16)
    x0 = x0_ref[...]
    bu = _dot(ub, f1_ref[...])
    a_r, a_i = a_ref[:, :half], a_ref[:, half:]
    x0r, x0i = x0[:, :half], x0[:, half:]
    xn_ref[:, :half] = a_r * x0r - a_i * x0i + bu[:, :half]
    xn_ref[:, half:] = a_r * x0i + a_i * x0r + bu[:, half:]
    y_ref[...] = _dot(x0.astype(BF16), e0_ref[...]) + _dot(ub, k0_ref[...])


def _s5_step(u, x0, k0, e0, f1, a1):
    t = u.shape[0]
    per = lambda a: pl.BlockSpec((None,) + a.shape[1:], lambda s: (s,) + (0,) * (a.ndim - 1))
    return pl.pallas_call(
        _s5_step_body,
        grid=(N_SUPER,),
        in_specs=[pl.BlockSpec((t, LANES), lambda s: (0, s)), per(x0), per(k0), per(e0), per(f1), per(a1)],
        out_specs=[pl.BlockSpec((t, LANES), lambda s: (0, s)), per(x0)],
        out_shape=[jax.ShapeDtypeStruct(u.shape, F32), jax.ShapeDtypeStruct(x0.shape, F32)],
        compiler_params=_cparams("parallel"),
        name="s5_step",
    )(u, x0, k0, e0, f1, a1)


MSEQ = 2


def _mlstm_prompt_body(q_all, gc_all, *rest):
    gr_refs, (mg_ref, hm_all, c_out, n_out, m_out, c_all, n_all, m_all) = rest[:MSEQ], rest[MSEQ:]
    ci = pl.program_id(1)

    @pl.when(ci == 0)
    def _():
        c_all[...] = jnp.zeros_like(c_all)
        n_all[...] = jnp.zeros_like(n_all)
        m_all[...] = jnp.zeros_like(m_all)

    for sq in range(MSEQ):
        _mlstm_chunk(q_all.at[sq], gc_all.at[sq], gr_refs[sq], mg_ref, hm_all.at[sq],
                     c_all.at[sq], n_all.at[sq], m_all.at[sq])

    @pl.when(ci == pl.num_programs(1) - 1)
    def _():
        c_out[...] = c_all[...]
        n_out[...] = n_all[...]
        m_out[...] = m_all[...]


def _mlstm_chunk(q_ref, gc_ref, gr_ref, mg_ref, hm_ref, c_sc, n_sc, m_sc):
    lc = MCHUNK
    scale = DH ** -0.5
    row = lax.broadcasted_iota(jnp.int32, (lc, lc), 0)
    col = lax.broadcasted_iota(jnp.int32, (lc, lc), 1)
    causal = col <= row
    tri = causal.astype(BF16)
    ones = jnp.ones((lc, DH), BF16)
    gc = gc_ref[...]
    gr = gr_ref[...]
    gc_hi, gc_lo = _split_bf16(gc)
    bcol_all = _dot(tri, gc_hi) + _dot(tri, gc_lo)
    brow_all = _dot_exact_rhs(gr, tri, ((1,), (1,)))
    for hd in range(HEADS):
        q = q_ref[:, hd * DH:(hd + 1) * DH]
        k = q_ref[:, MWIDTH + hd * DH:MWIDTH + (hd + 1) * DH]
        v = q_ref[:, 2 * MWIDTH + hd * DH:2 * MWIDTH + (hd + 1) * DH]
        o = q_ref[:, 3 * MWIDTH + hd * DH:3 * MWIDTH + (hd + 1) * DH]
        i_col = gc[:, hd:hd + 1]
        b_col = bcol_all[:, HEADS + hd:HEADS + hd + 1]
        i_row = gr[hd:hd + 1, :]
        b_row = brow_all[HEADS + hd:HEADS + hd + 1, :]
        m_prev = m_sc[hd:hd + 1, :][:, :1]
        dm = jnp.where(causal, b_col - b_row + i_row, -jnp.inf)
        m_inter = b_col + m_prev
        m_t = jnp.maximum(m_inter, jnp.max(dm, axis=-1, keepdims=True))
        w_inter = jnp.exp(m_inter - m_t)
        w = jnp.exp(dm - m_t)
        s = _dot_dims(q, k, ((1,), (1,))) * scale * w
        sb = s.astype(BF16)
        c_prev = c_sc[hd]
        n_prev = n_sc[hd]
        num = w_inter * _dot(q, c_prev.astype(BF16)) + _dot(sb, v)
        den = w_inter * _dot(q, n_prev.astype(BF16)) + _dot(sb, ones)
        h = num / jnp.maximum(jnp.abs(den), jnp.exp(-m_t))
        hn = h * lax.rsqrt(jnp.mean(h * h, axis=-1, keepdims=True) + EPS)
        hm_ref[:, hd * DH:(hd + 1) * DH] = (hn * mg_ref[:, hd * DH:(hd + 1) * DH] * _sigmoid(o.astype(F32))).astype(BF16)
        m_last = m_t[lc - 1:lc, :]
        b_last = b_col[lc - 1:lc, :]
        decay = w_inter[lc - 1:lc, :]
        w_last = jnp.exp(b_last - b_col + i_col - m_last) * scale
        kw = (k.astype(F32) * w_last).astype(BF16)
        c_sc[hd] = decay * c_prev + _dot_dims(kw, v, ((0,), (0,)))
        n_sc[hd] = decay * n_prev + _dot_dims(kw, ones, ((0,), (0,)))
        m_sc[hd:hd + 1, :] = jnp.broadcast_to(m_last, (1, LANES))


def _mlstm_prompt(qkvo, gcol, grow, mnorm_g):
    nc = SEQ // MCHUNK
    q3 = qkvo.reshape(N_PROMPT, SEQ, 4 * MWIDTH)
    g3 = gcol.reshape(N_PROMPT, SEQ, LANES)
    seq_blk = lambda w: pl.BlockSpec((MSEQ, MCHUNK, w), lambda b, c: (b, c, 0))
    state = lambda shape: pl.BlockSpec((MSEQ,) + shape, lambda b, c: (b,) + (0,) * len(shape))
    gr_specs = [pl.BlockSpec((8, MCHUNK), functools.partial(lambda b, c, sq: (0, (b * MSEQ + sq) * nc + c), sq=sq))
                for sq in range(MSEQ)]
    hm, c_p, n_p, m_p = pl.pallas_call(
        _mlstm_prompt_body,
        grid=(N_PROMPT // MSEQ, nc),
        in_specs=[seq_blk(4 * MWIDTH), seq_blk(LANES)] + gr_specs + [pl.BlockSpec((1, MWIDTH), lambda b, c: (0, 0))],
        out_specs=[seq_blk(MWIDTH), state((HEADS, DH, DH)), state((HEADS, DH, DH)), state((8, LANES))],
        out_shape=[jax.ShapeDtypeStruct((N_PROMPT, SEQ, MWIDTH), BF16),
                   jax.ShapeDtypeStruct((N_PROMPT, HEADS, DH, DH), F32),
                   jax.ShapeDtypeStruct((N_PROMPT, HEADS, DH, DH), F32),
                   jax.ShapeDtypeStruct((N_PROMPT, 8, LANES), F32)],
        scratch_shapes=[pltpu.VMEM((MSEQ, HEADS, DH, DH), F32),
                        pltpu.VMEM((MSEQ, HEADS, DH, DH), F32),
                        pltpu.VMEM((MSEQ, 8, LANES), F32)],
        compiler_params=_cparams("parallel", "arbitrary"),
        name="mlstm_prompt",
    )(q3, g3, *([grow] * MSEQ), mnorm_g)
    return hm.reshape(N_PROMPT * SEQ, MWIDTH), c_p, n_p, m_p


MSTEP_TOK = 8


def _mlstm_step_body(q_ref, qkt_ref, ig_ref, lf_ref, m0_ref, c0_ref, n0_ref, mg_ref, hm_ref, c_out, n_out, m_out, qc_sc):
    scale = DH ** -0.5
    ig = ig_ref[...]
    m_inter = lf_ref[...] + m0_ref[...]
    m_t = jnp.maximum(m_inter, ig)
    w_inter_all = jnp.exp(m_inter - m_t)
    w_all = jnp.exp(ig - m_t) * scale
    floor_all = jnp.exp(-m_t)
    m_out[...] = m_t
    for hd in range(HEADS):
        sl = slice(hd * DH, (hd + 1) * DH)
        q = q_ref[:, sl].astype(F32)
        k = q_ref[:, MWIDTH + hd * DH:MWIDTH + (hd + 1) * DH].astype(F32)
        v = q_ref[:, 2 * MWIDTH + hd * DH:2 * MWIDTH + (hd + 1) * DH].astype(F32)
        o = q_ref[:, 3 * MWIDTH + hd * DH:3 * MWIDTH + (hd + 1) * DH].astype(F32)
        n0 = n0_ref[:, sl]
        wi = w_inter_all[:, hd:hd + 1]
        wk = w_all[:, hd:hd + 1]
        s = jnp.sum(q * k, axis=-1, keepdims=True) * wk
        qn = jnp.sum(q * n0, axis=-1, keepdims=True)
        for j in range(MSTEP_TOK):
            c0 = c0_ref[j, hd]
            q_col = qkt_ref[hd, :, j:j + 1]
            k_col = qkt_ref[HEADS + hd, :, j:j + 1]
            qc_sc[j:j + 1, sl] = jnp.sum(q_col * c0, axis=0, keepdims=True)
            c_out[j, hd] = wi[j:j + 1, :] * c0 + (wk[j:j + 1, :] * k_col) * v[j:j + 1, :]
        num = wi * qc_sc[:, sl] + s * v
        den = wi * qn + s
        h = num / jnp.maximum(jnp.abs(den), floor_all[:, hd:hd + 1])
        hn = h * lax.rsqrt(jnp.mean(h * h, axis=-1, keepdims=True) + EPS)
        hm_ref[:, sl] = hn * mg_ref[:, sl] * _sigmoid(o)
        n_out[:, sl] = wi * n0 + wk * k


def _mlstm_step(qkvo, qkt, ig, lf, m0, c0, n0, mnorm_g):
    t = qkvo.shape[0]
    tk = MSTEP_TOK
    return pl.pallas_call(
        _mlstm_step_body,
        grid=(t // tk,),
        in_specs=[pl.BlockSpec((tk, 4 * MWIDTH), lambda i: (i, 0)),
                  pl.BlockSpec((None, 2 * HEADS, DH, tk), lambda i: (i, 0, 0, 0)),
                  pl.BlockSpec((tk, LANES), lambda i: (i, 0)),
                  pl.BlockSpec((tk, LANES), lambda i: (i, 0)),
                  pl.BlockSpec((tk, LANES), lambda i: (i, 0)),
                  pl.BlockSpec((tk, HEADS, DH, DH), lambda i: (i, 0, 0, 0)),
                  pl.BlockSpec((tk, MWIDTH), lambda i: (i, 0)),
                  pl.BlockSpec((1, MWIDTH), lambda i: (0, 0))],
        out_specs=[pl.BlockSpec((tk, MWIDTH), lambda i: (i, 0)),
                   pl.BlockSpec((tk, HEADS, DH, DH), lambda i: (i, 0, 0, 0)),
                   pl.BlockSpec((tk, MWIDTH), lambda i: (i, 0)),
                   pl.BlockSpec((tk, LANES), lambda i: (i, 0))],
        out_shape=[jax.ShapeDtypeStruct((t, MWIDTH), F32),
                   jax.ShapeDtypeStruct((t, HEADS, DH, DH), F32),
                   jax.ShapeDtypeStruct((t, MWIDTH), F32),
                   jax.ShapeDtypeStruct((t, LANES), F32)],
        scratch_shapes=[pltpu.VMEM((tk, MWIDTH), F32)],
        compiler_params=_cparams("parallel"),
        name="mlstm_step",
    )(qkvo, qkt, ig, lf, m0, c0, n0, mnorm_g)


def _post_body(x_ref, y5_ref, hm_ref, gg_ref, g1_ref, sc_ref, sh_ref, n2_ref, wglu_ref, wmo_ref, wo_ref, rw_ref, rb_ref,
               cnt0_ref, x1_ref, h2_ref, eid_ref, wgt_ref, rank_ref, cnt_ref, run_sc, *, n_real, n_steps):
    i = pl.program_id(0)

    @pl.when(i == 0)
    def _():
        run_sc[...] = cnt0_ref[...]

    if n_steps > n_real:
        @pl.when(i >= n_real)
        def _():
            h2_ref[...] = jnp.zeros_like(h2_ref)

        pl.when(i < n_real)(functools.partial(
            _post_tile, x_ref, y5_ref, hm_ref, gg_ref, g1_ref, sc_ref, sh_ref, n2_ref, wglu_ref, wmo_ref, wo_ref,
            rw_ref, rb_ref, x1_ref, h2_ref, eid_ref, wgt_ref, rank_ref, run_sc))
    else:
        _post_tile(x_ref, y5_ref, hm_ref, gg_ref, g1_ref, sc_ref, sh_ref, n2_ref, wglu_ref, wmo_ref, wo_ref,
                   rw_ref, rb_ref, x1_ref, h2_ref, eid_ref, wgt_ref, rank_ref, run_sc)
    cnt_ref[...] = run_sc[...]


def _post_tile(x_ref, y5_ref, hm_ref, gg_ref, g1_ref, sc_ref, sh_ref, n2_ref, wglu_ref, wmo_ref, wo_ref, rw_ref, rb_ref,
               x1_ref, h2_ref, eid_ref, wgt_ref, rank_ref, run_sc):
    g5 = _gelu_tanh(y5_ref[...]).astype(BF16)
    glu = _dot(g5, wglu_ref[...])
    br_s5 = glu[:, :D_MODEL] * _sigmoid(glu[:, D_MODEL:])
    br_m = _dot(hm_ref[...].astype(BF16), wmo_ref[...])
    gg = gg_ref[...].astype(F32)
    merged = _sigmoid(gg[:, :D_MODEL]) * br_s5 + _sigmoid(gg[:, D_MODEL:]) * br_m
    x1 = x_ref[...] + g1_ref[...] * _dot(merged.astype(BF16), wo_ref[...])
    x1_ref[...] = x1
    ms = jnp.mean(x1 * x1, axis=-1, keepdims=True)
    h2 = x1 * lax.rsqrt(ms + EPS) * n2_ref[...] * (1.0 + sc_ref[...]) + sh_ref[...]
    h2_ref[...] = h2
    logits = _dot_hp(h2, rw_ref[...]) + rb_ref[...]
    tm = logits.shape[0]
    lane = lax.broadcasted_iota(jnp.int32, logits.shape, 1)
    l = jnp.where(lane < N_EXPERTS, logits, -jnp.inf)
    vals, hots, idxs = [], [], []
    for _ in range(TOP_K):
        mx = jnp.max(l, axis=-1, keepdims=True)
        idx = jnp.min(jnp.where(l == mx, lane, LANES), axis=-1, keepdims=True)
        hot = lane == idx
        vals.append(mx)
        hots.append(hot)
        idxs.append(idx)
        l = jnp.where(hot, -jnp.inf, l)
    ex = [jnp.exp(vk - vals[0]) for vk in vals]
    tot = ex[0] + ex[1] + ex[2] + ex[3]
    chosen = jnp.where(hots[0] | hots[1] | hots[2] | hots[3], 1.0, 0.0)
    r_io = lax.broadcasted_iota(jnp.int32, (tm, tm), 0)
    c_io = lax.broadcasted_iota(jnp.int32, (tm, tm), 1)
    earlier = (c_io < r_io).astype(BF16)
    before = run_sc[...] + _dot(earlier, chosen.astype(BF16))
    eid = jnp.zeros(logits.shape, jnp.int32)
    wgt = jnp.zeros(logits.shape, F32)
    rank = jnp.zeros(logits.shape, F32)
    for k in range(TOP_K):
        eid = jnp.where(lane == k, idxs[k], eid)
        wgt = jnp.where(lane == k, ex[k] / tot, wgt)
        rank = jnp.where(lane == k, jnp.sum(jnp.where(hots[k], before, 0.0), axis=-1, keepdims=True), rank)
    eid_ref[...] = eid
    wgt_ref[...] = wgt
    rank_ref[...] = rank.astype(jnp.int32)
    run_sc[...] += jnp.sum(chosen, axis=0, keepdims=True)


def _post(x2d, y5, hm, gg, mod3, rows_per_mod, tm, n2, wglu, wmo, wo, rw, rb, cnt0, h2_rows):
    t = x2d.shape[0]
    mrows = mod3.shape[1]
    n_real = t // tm
    n_steps = -(-h2_rows // tm)
    last = n_real - 1

    def mod_spec(j):
        return pl.BlockSpec((None, mrows, D_MODEL), lambda i: ((jnp.minimum(i, last) * tm) // rows_per_mod, 0, j))

    const = lambda shape: pl.BlockSpec(shape, lambda i: (0,) * len(shape))
    tile = lambda w: pl.BlockSpec((tm, w), lambda i: (jnp.minimum(i, last), 0))
    return pl.pallas_call(
        functools.partial(_post_body, n_real=n_real, n_steps=n_steps),
        grid=(n_steps,),
        in_specs=[tile(D_MODEL), tile(S5_WIDTH), tile(MWIDTH), tile(2 * D_MODEL),
                  mod_spec(2), mod_spec(4), mod_spec(3), const((1, D_MODEL)),
                  const(wglu.shape), const(wmo.shape), const(wo.shape), const(rw.shape), const(rb.shape),
                  const((1, LANES))],
        out_specs=[tile(D_MODEL), pl.BlockSpec((tm, D_MODEL), lambda i: (i, 0)), tile(LANES), tile(LANES), tile(LANES),
                   const((1, LANES))],
        out_shape=[jax.ShapeDtypeStruct((t, D_MODEL), F32),
                   jax.ShapeDtypeStruct((h2_rows, D_MODEL), F32),
                   jax.ShapeDtypeStruct((t, LANES), jnp.int32),
                   jax.ShapeDtypeStruct((t, LANES), F32),
                   jax.ShapeDtypeStruct((t, LANES), jnp.int32),
                   jax.ShapeDtypeStruct((1, LANES), F32)],
        scratch_shapes=[pltpu.VMEM((1, LANES), F32)],
        compiler_params=_cparams("arbitrary"),
        name="post",
    )(x2d, y5, hm, gg, mod3, mod3, mod3, n2, wglu, wmo, wo, rw, rb, cnt0)


N_TOKENS = N_PROMPT * SEQ + N_SAMPLE
N_ASSIGN = N_TOKENS * TOP_K
MOE_TILE = 256
MOE_TILES = N_ASSIGN // MOE_TILE + N_EXPERTS
N_SLOT = MOE_TILES * MOE_TILE
CMB_TILE = 128


MOE_RING = 3


def _moe_body(te_ref, nt_ref, nxe_ref, gpar_ref, idx0_ref, idx1_ref, idx2_ref, h2_hbm, wup_hbm, bup_ref, wdn_hbm, bdn_ref,
              y_ref, xb0, xb1, xb2, sem, wbuf_up, wbuf_dn, wsem, wup_bf, wdn_bf):
    i = pl.program_id(0)
    nt = nt_ref[0]
    bufs = (xb0, xb1, xb2)
    last = MOE_TILES - 1

    def gather(idx_ref, slot):
        for r in range(MOE_TILE):
            pltpu.make_async_copy(h2_hbm.at[pl.ds(idx_ref[0, r], 1)], bufs[slot].at[pl.ds(r, 1)],
                                  sem.at[slot]).start()

    def wait(slot):
        pltpu.make_async_copy(h2_hbm.at[pl.ds(0, MOE_TILE)], bufs[slot], sem.at[slot]).wait()

    def weights(e, slot):
        return (pltpu.make_async_copy(wup_hbm.at[e], wbuf_up.at[slot], wsem.at[0, slot]),
                pltpu.make_async_copy(wdn_hbm.at[e], wbuf_dn.at[slot], wsem.at[1, slot]))

    wslot = gpar_ref[i]

    @pl.when(i == 0)
    def _():
        for c in weights(te_ref[0], wslot):
            c.start(priority=1)
        gather(idx0_ref, 0)
        gather(idx1_ref, 1)

    new_expert = jnp.logical_or(i == 0, te_ref[i] != te_ref[jnp.maximum(i - 1, 0)])
    live = i < nt

    @pl.when(jnp.logical_and(new_expert, live))
    def _():
        for c in weights(0, wslot):
            c.wait()
        wup_bf[...] = wbuf_up[wslot].astype(BF16)
        wdn_bf[...] = wbuf_dn[wslot].astype(BF16)
        nxe = nxe_ref[i]

        @pl.when(nxe >= 0)
        def _():
            for c in weights(nxe, 1 - wslot):
                c.start(priority=1)

    def step(slot):
        wait(slot)
        gather(idx2_ref, (slot + 2) % MOE_RING)
        gu = _dot(bufs[slot][...].astype(BF16), wup_bf[...]) + bup_ref[...]
        g = jnp.minimum(gu[:, :D_FF], SWIGLU_LIMIT)
        up = jnp.clip(gu[:, D_FF:], -SWIGLU_LIMIT, SWIGLU_LIMIT)
        act = (up + 1.0) * g * _sigmoid(SWIGLU_ALPHA * g)
        y_ref[...] = _dot(act.astype(BF16), wdn_bf[...]) + bdn_ref[...]

    for slot in range(MOE_RING):
        pl.when(jnp.logical_and(live, i % MOE_RING == slot))(functools.partial(step, slot))

    @pl.when(jnp.logical_not(live))
    def _():
        y_ref[...] = jnp.zeros_like(y_ref)

    drain = jnp.logical_and(jnp.logical_not(live), i < nt + 2)
    for slot in range(MOE_RING):
        pl.when(jnp.logical_and(drain, i % MOE_RING == slot))(functools.partial(wait, slot))
    pl.when(jnp.logical_and(i == last, last - 1 < nt))(functools.partial(wait, (last + 1) % MOE_RING))
    pl.when(jnp.logical_and(i == last, last < nt))(functools.partial(wait, (last + 2) % MOE_RING))


def _moe(tile_expert, n_tiles, next_expert, group_parity, slot_token, h2_all, wup, bup, wdn, bdn):
    idx3 = slot_token.reshape(MOE_TILES, 1, MOE_TILE)
    ahead = lambda k: pl.BlockSpec((None, 1, MOE_TILE),
                                   lambda i, *_: (jnp.minimum(i + k, MOE_TILES - 1), 0, 0), memory_space=pltpu.SMEM)
    grid_spec = pltpu.PrefetchScalarGridSpec(
        num_scalar_prefetch=4,
        grid=(MOE_TILES,),
        in_specs=[ahead(0), ahead(1), ahead(2),
                  pl.BlockSpec(memory_space=pl.ANY),
                  pl.BlockSpec(memory_space=pl.ANY),
                  pl.BlockSpec((None, 1, 2 * D_FF), lambda i, te, *_: (te[i], 0, 0)),
                  pl.BlockSpec(memory_space=pl.ANY),
                  pl.BlockSpec((None, 1, D_MODEL), lambda i, te, *_: (te[i], 0, 0))],
        out_specs=pl.BlockSpec((MOE_TILE, D_MODEL), lambda i, *_: (i, 0)),
        scratch_shapes=[pltpu.VMEM((MOE_TILE, D_MODEL), F32), pltpu.VMEM((MOE_TILE, D_MODEL), F32),
                        pltpu.VMEM((MOE_TILE, D_MODEL), F32),
                        pltpu.SemaphoreType.DMA((MOE_RING,)),
                        pltpu.VMEM((2, D_MODEL, 2 * D_FF), F32), pltpu.VMEM((2, D_FF, D_MODEL), F32),
                        pltpu.SemaphoreType.DMA((2, 2)),
                        pltpu.VMEM((D_MODEL, 2 * D_FF), BF16), pltpu.VMEM((D_FF, D_MODEL), BF16)])
    return pl.pallas_call(
        _moe_body,
        grid_spec=grid_spec,
        out_shape=jax.ShapeDtypeStruct((N_SLOT, D_MODEL), F32),
        compiler_params=_cparams("arbitrary"),
        name="moe",
    )(tile_expert, n_tiles, next_expert, group_parity, idx3, idx3, idx3, h2_all, wup, bup, wdn, bdn)


def _combine_body(cur_ref, nxt_ref, wgt_ref, x1_ref, g2_ref, fg_ref, y_hbm, o_ref, yb0, yb1, sem, *, n_steps):
    i = pl.program_id(0)
    tm = CMB_TILE

    def gather(idx_ref, buf, s):
        for r in range(tm):
            for k in range(TOP_K):
                pltpu.make_async_copy(y_hbm.at[pl.ds(idx_ref[0, r * TOP_K + k], 1)],
                                      buf.at[pl.ds(k * tm + r, 1)], sem.at[s]).start(priority=k % 2)

    def wait(buf, s):
        pltpu.make_async_copy(y_hbm.at[pl.ds(0, TOP_K * tm)], buf, sem.at[s]).wait()

    @pl.when(i == 0)
    def _():
        gather(cur_ref, yb0, 0)

    def step(cur, nxt, s_cur, s_nxt):
        wait(cur, s_cur)
        gather(nxt_ref, nxt, s_nxt)
        wgt = wgt_ref[...]
        acc = wgt[:, 0:1] * cur[0:tm, :]
        for k in range(1, TOP_K):
            acc = acc + wgt[:, k:k + 1] * cur[k * tm:(k + 1) * tm, :]
        xo = x1_ref[...] + g2_ref[...] * acc
        ms = jnp.mean(xo * xo, axis=-1, keepdims=True)
        o_ref[...] = xo * lax.rsqrt(ms + EPS) * fg_ref[...]

    pl.when(i % 2 == 0)(functools.partial(step, yb0, yb1, 0, 1))
    pl.when(i % 2 == 1)(functools.partial(step, yb1, yb0, 1, 0))
    last_slot = n_steps % 2
    pl.when(i == n_steps - 1)(functools.partial(wait, yb1 if last_slot else yb0, last_slot))


def _combine(pos, wgt, x1, mod3, rows_per_mod, fg, y_slots):
    t = x1.shape[0]
    tm = CMB_TILE
    n = t // tm
    mrows = mod3.shape[1]
    assert mrows in (1, tm)
    pos3 = pos.reshape(n, 1, tm * TOP_K)
    return pl.pallas_call(
        functools.partial(_combine_body, n_steps=n),
        grid=(n,),
        in_specs=[pl.BlockSpec((None, 1, tm * TOP_K), lambda i: (i, 0, 0), memory_space=pltpu.SMEM),
                  pl.BlockSpec((None, 1, tm * TOP_K), lambda i: (jnp.minimum(i + 1, n - 1), 0, 0),
                               memory_space=pltpu.SMEM),
                  pl.BlockSpec((tm, LANES), lambda i: (i, 0)),
                  pl.BlockSpec((tm, D_MODEL), lambda i: (i, 0)),
                  pl.BlockSpec((None, mrows, D_MODEL), lambda i: ((i * tm) // rows_per_mod, 0, 5)),
                  pl.BlockSpec((1, D_MODEL), lambda i: (0, 0)),
                  pl.BlockSpec(memory_space=pl.ANY)],
        out_specs=pl.BlockSpec((tm, D_MODEL), lambda i: (i, 0)),
        out_shape=jax.ShapeDtypeStruct((t, D_MODEL), F32),
        scratch_shapes=[pltpu.VMEM((TOP_K * tm, D_MODEL), F32), pltpu.VMEM((TOP_K * tm, D_MODEL), F32),
                        pltpu.SemaphoreType.DMA((2,))],
        compiler_params=_cparams("arbitrary"),
        name="combine",
    )(pos3, pos3, wgt, x1, mod3, fg, y_slots)


def _routing(eid_p, rank_p, eid_s, rank_s, counts):
    cnt = counts[0, :N_EXPERTS].astype(jnp.int32)
    ntile = (cnt + MOE_TILE - 1) // MOE_TILE
    tile_end = jnp.cumsum(ntile)
    poff = (tile_end - ntile) * MOE_TILE
    total = tile_end[-1]
    j = jnp.arange(MOE_TILES, dtype=jnp.int32)
    te = jnp.sum((j[:, None] >= tile_end[None, :]).astype(jnp.int32), axis=1)
    te_last = jnp.sum(((total - 1) >= tile_end).astype(jnp.int32))
    tile_expert = jnp.where(j < total, te, te_last).astype(jnp.int32)
    ex = jnp.arange(N_EXPERTS, dtype=jnp.int32)
    used = ntile > 0
    later = jnp.where(used[None, :] & (ex[None, :] > ex[:, None]), ex[None, :], N_EXPERTS)
    nxt_e = jnp.min(later, axis=1)
    nxt_e = jnp.where(nxt_e < N_EXPERTS, nxt_e, -1).astype(jnp.int32)
    par_e = ((jnp.cumsum(used.astype(jnp.int32)) - 1) % 2).astype(jnp.int32)
    pick = tile_expert[:, None] == ex[None, :]
    next_expert = jnp.sum(jnp.where(pick, nxt_e[None, :], 0), axis=1).astype(jnp.int32)
    group_parity = jnp.sum(jnp.where(pick, par_e[None, :], 0), axis=1).astype(jnp.int32)

    def pos_of(eid, rank):
        e = eid[:, :TOP_K]
        off = jnp.sum(jnp.where(e[:, :, None] == ex[None, None, :], poff[None, None, :], 0), axis=-1)
        return off + rank[:, :TOP_K]

    pos_p = pos_of(eid_p, rank_p)
    pos_s = pos_of(eid_s, rank_s)
    tok = jnp.concatenate([jnp.repeat(jnp.arange(N_PROMPT * SEQ, dtype=jnp.int32), TOP_K),
                           jnp.repeat(N_PROMPT * SEQ + jnp.arange(N_SAMPLE, dtype=jnp.int32), TOP_K)])
    pos_all = jnp.concatenate([pos_p.reshape(-1), pos_s.reshape(-1)])
    slot_token = jnp.zeros((N_SLOT,), jnp.int32).at[pos_all].set(tok, unique_indices=True)
    return tile_expert, total.reshape(1).astype(jnp.int32), next_expert, group_parity, slot_token, pos_p, pos_s


def _unpack_s5_state(x):
    n = x.shape[1]
    z = x.reshape(N_SUPER, n, 2, S5_SUPER, S5_STATE).transpose(2, 1, 0, 3, 4).reshape(2, n, S5_GROUPS, S5_STATE)
    return z[0], z[1]


def _pack_s5_state(re, im):
    n = re.shape[0]
    z = jnp.stack([re, im], axis=0).reshape(2, n, N_SUPER, S5_SUPER, S5_STATE)
    return z.transpose(2, 1, 0, 3, 4).reshape(N_SUPER, n, SUPER_STATE)


def kernel(x_prompt, x_sample, c_prompt, c_sample, state_s5_re, state_s5_im, state_mlstm_C, state_mlstm_n, state_mlstm_m, norm1_g, norm2_g, final_norm_g, w_ada, b_ada, w_in, s5_lambda_re, s5_lambda_im, s5_log_dt, s5_B_re, s5_B_im, s5_C_re, s5_C_im, s5_D, s5_w_glu, mlstm_b_i, mlstm_b_f, mlstm_norm_g, mlstm_w_out, w_out, router_w, router_b, expert_w_up, expert_b_up, expert_w_down, expert_b_down):
    assert w_in.shape[0] == 1, "single layer"
    tp = N_PROMPT * SEQ
    xp = x_prompt.reshape(tp, D_MODEL).astype(F32)
    xs = x_sample.reshape(N_SAMPLE, D_MODEL).astype(F32)

    w = w_in[0]
    c0 = S5_WIDTH
    c1 = c0 + 4 * MWIDTH
    c2 = c1 + 2 * HEADS
    wu = w[:, :c0].astype(BF16)
    wq = w[:, c0:c1].astype(BF16)
    wif = jnp.pad(w[:, c1:c2].astype(F32), ((0, 0), (0, LANES - 2 * HEADS)))
    wift = w[:, c1:c2].astype(F32).T
    wg = w[:, c2:].astype(BF16)
    bvec = jnp.concatenate([mlstm_b_i[0], mlstm_b_f[0]]).astype(F32)
    bif = jnp.pad(bvec, (0, LANES - 2 * HEADS)).reshape(1, LANES)
    bift = bvec.reshape(2 * HEADS, 1)
    wparts = (wu, wq, wg, wif, wift, bif, bift)
    g1 = norm1_g[0].reshape(1, D_MODEL).astype(F32)
    n2 = norm2_g[0].reshape(1, D_MODEL).astype(F32)
    fg = final_norm_g.reshape(1, D_MODEL).astype(F32)
    mg = mlstm_norm_g[0].reshape(1, MWIDTH).astype(F32)
    wglu = s5_w_glu[0].astype(BF16)
    wmo = mlstm_w_out[0].astype(BF16)
    wo = w_out[0].astype(BF16)
    rw = jnp.pad(router_w[0].astype(F32), ((0, 0), (0, LANES - N_EXPERTS)))
    rb = jnp.pad(router_b[0].astype(F32), (0, LANES - N_EXPERTS)).reshape(1, LANES)
    wup = expert_w_up[0].astype(F32)
    wdn = expert_w_down[0].astype(F32)
    bup = expert_b_up[0].astype(F32).reshape(N_EXPERTS, 1, 2 * D_FF)
    bdn = expert_b_down[0].astype(F32).reshape(N_EXPERTS, 1, D_MODEL)
    kcomp, fcomp, ecomp, a16, a1 = _s5_tables(
        s5_lambda_re[0], s5_lambda_im[0], s5_log_dt[0], s5_B_re[0], s5_B_im[0], s5_C_re[0], s5_C_im[0], s5_D[0])

    c_all = jnp.concatenate([c_prompt, c_sample], axis=0).astype(F32)
    mod = _adaln(c_all, w_ada[0].astype(F32), b_ada[0].astype(F32))
    mod_p = mod[:N_PROMPT].reshape(N_PROMPT, 1, 6 * D_MODEL)
    mod_s = mod[N_PROMPT:].reshape(1, N_SAMPLE, 6 * D_MODEL)

    u, qkvo, gg, gcol, grow = _inproj(xp, mod_p, SEQ, 1024, g1, wparts)
    y5, xend, k0, e0, f1 = _s5_prompt(u, kcomp, fcomp, ecomp, a16)
    hm, c_p, n_p, m_p = _mlstm_prompt(qkvo, gcol, grow, mg)
    cnt0 = jnp.zeros((1, LANES), F32)
    x1, h2_all, eid_p, wgt_p, rank_p, cnt_p = _post(xp, y5, hm, gg, mod_p, SEQ, 512, n2, wglu, wmo, wo, rw, rb,
                                                    cnt0, N_TOKENS)
    p_re, p_im = _unpack_s5_state(xend.reshape(N_SUPER, N_PROMPT, SUPER_STATE))
    n_p = n_p[..., 0]
    m_p = m_p[:, :HEADS, 0]

    us, qs, ggs, gcs, _ = _inproj(xs, mod_s, N_SAMPLE, N_SAMPLE, g1, wparts)
    x0 = _pack_s5_state(state_s5_re[0].astype(F32), state_s5_im[0].astype(F32))
    y5s, xns = _s5_step(us, x0, k0, e0, f1, a1)
    s_re, s_im = _unpack_s5_state(xns)
    qkt = qs[:, :2 * MWIDTH].astype(F32).reshape(N_SAMPLE // MSTEP_TOK, MSTEP_TOK, 2 * HEADS, DH).transpose(0, 2, 3, 1)
    pad_heads = lambda a: jnp.pad(a, ((0, 0), (0, LANES - HEADS)))
    m0 = pad_heads(state_mlstm_m[0].astype(F32))
    hms, c_s, n_s, m_s = _mlstm_step(qs, qkt, pad_heads(gcs[:, :HEADS]), pad_heads(gcs[:, HEADS:2 * HEADS]), m0,
                                     state_mlstm_C[0].astype(F32),
                                     state_mlstm_n[0].astype(F32).reshape(N_SAMPLE, MWIDTH), mg)
    x1s, h2s, eid_s, wgt_s, rank_s, cnt_all = _post(xs, y5s, hms, ggs, mod_s, N_SAMPLE, N_SAMPLE, n2, wglu, wmo, wo, rw, rb,
                                                    cnt_p, N_SAMPLE)

    h2_all = lax.dynamic_update_slice(h2_all, h2s, (tp, 0))
    tile_expert, n_tiles, next_expert, group_parity, slot_token, pos_p, pos_s = _routing(
        eid_p, rank_p, eid_s, rank_s, cnt_all)
    y_slots = _moe(tile_expert, n_tiles, next_expert, group_parity, slot_token, h2_all, wup, bup, wdn, bdn)
    y_p = _combine(pos_p, wgt_p, x1, mod_p, SEQ, fg, y_slots)
    y_s = _combine(pos_s, wgt_s, x1s, mod_s, N_SAMPLE, fg, y_slots)

    return (y_p.reshape(N_PROMPT, SEQ, D_MODEL).astype(x_prompt.dtype),
            y_s.reshape(N_SAMPLE, 1, D_MODEL).astype(x_sample.dtype),
            p_re[None], p_im[None], c_p[None], n_p[None], m_p[None],
            s_re[None], s_im[None], c_s[None],
            n_s.reshape(1, N_SAMPLE, HEADS, DH), m_s[:, :HEADS][None])
```

```python
import functools
import math

import jax
import jax.numpy as jnp
from jax import lax
from jax.experimental import pallas as pl
from jax.experimental.pallas import tpu as pltpu

F32 = jnp.float32
BF16 = jnp.bfloat16

D_MODEL = 1024
SEQ = 2048
N_PROMPT = 8
N_SAMPLE = 128
S5_WIDTH = 512
S5_GROUP = 16
S5_GROUPS = 32
S5_STATE = 64
HEADS = 4
DH = 128
MWIDTH = HEADS * DH
N_EXPERTS = 32
TOP_K = 4
D_FF = 1024
SWIGLU_LIMIT = 7.0
SWIGLU_ALPHA = 1.702
EPS = 1e-6

LANES = 128
S5_CHUNK = 16
S5_SUPER = LANES // S5_GROUP
N_SUPER = S5_GROUPS // S5_SUPER
SUPER_STATE = 2 * S5_SUPER * S5_STATE
MCHUNK = 128
VMEM_LIMIT = 56 * 1024 * 1024


def _cparams(*sem):
    return pltpu.CompilerParams(dimension_semantics=sem, vmem_limit_bytes=VMEM_LIMIT)


def _dot(a, b):
    return jnp.dot(a, b, preferred_element_type=F32)


def _dot_dims(a, b, dims):
    return lax.dot_general(a, b, (dims, ((), ())), preferred_element_type=F32)


def _split_bf16(a):
    hi = a.astype(BF16)
    lo = (a - hi.astype(F32)).astype(BF16)
    return hi, lo


def _dot_hp(a, b, dims=((1,), (0,))):
    ah, al = _split_bf16(a)
    bh, bl = _split_bf16(b)
    return _dot_dims(ah, bh, dims) + (_dot_dims(al, bh, dims) + _dot_dims(ah, bl, dims))


def _dot_exact_rhs(a, b_exact, dims=((1,), (0,))):
    ah, al = _split_bf16(a)
    return _dot_dims(ah, b_exact, dims) + _dot_dims(al, b_exact, dims)


def _log_sigmoid(x):
    return -(jnp.maximum(-x, 0.0) + jnp.log1p(jnp.exp(-jnp.abs(x))))


def _sigmoid(x):
    return 0.5 * (1.0 + jnp.tanh(0.5 * x))


def _gelu_tanh(x):
    c = math.sqrt(2.0 / math.pi)
    return 0.5 * x * (1.0 + jnp.tanh(c * (x + 0.044715 * (x * x * x))))


def _adaln_body(c_ref, w_ref, b_ref, o_ref):
    c = c_ref[...]
    s = c * _sigmoid(c)
    o_ref[...] = _dot_hp(s, w_ref[...]) + b_ref[...]


def _adaln(c_all, w_ada, b_ada):
    n = c_all.shape[0]
    tn = 1024
    return pl.pallas_call(
        _adaln_body,
        grid=(6 * D_MODEL // tn,),
        in_specs=[pl.BlockSpec((n, D_MODEL), lambda j: (0, 0)),
                  pl.BlockSpec((D_MODEL, tn), lambda j: (0, j)),
                  pl.BlockSpec((1, tn), lambda j: (0, j))],
        out_specs=pl.BlockSpec((n, tn), lambda j: (0, j)),
        out_shape=jax.ShapeDtypeStruct((n, 6 * D_MODEL), F32),
        compiler_params=_cparams("parallel"),
        name="adaln",
    )(c_all, w_ada, b_ada.reshape(1, -1))


def _inproj_body(x_ref, g1_ref, sc_ref, sh_ref, wu_ref, wq_ref, wg_ref, wif_ref, wift_ref, bif_ref, bift_ref,
                 u_ref, q_ref, gg_ref, if_ref, ift_ref):
    x = x_ref[...]
    ms = jnp.mean(x * x, axis=-1, keepdims=True)
    h = x * lax.rsqrt(ms + EPS) * g1_ref[...] * (1.0 + sc_ref[...]) + sh_ref[...]
    hb = h.astype(BF16)
    u_ref[...] = _dot(hb, wu_ref[...])
    q_ref[...] = _dot(hb, wq_ref[...]).astype(BF16)
    gg_ref[...] = _dot(hb, wg_ref[...]).astype(BF16)
    gc = _dot_hp(h, wif_ref[...]) + bif_ref[...]
    lane = lax.broadcasted_iota(jnp.int32, gc.shape, 1)
    if_ref[...] = jnp.where(lane < HEADS, gc, _log_sigmoid(gc))
    gr = _dot_hp(wift_ref[...], h, ((1,), (1,))) + bift_ref[...]
    sub = lax.broadcasted_iota(jnp.int32, gr.shape, 0)
    ift_ref[...] = jnp.where(sub < HEADS, gr, _log_sigmoid(gr))


def _inproj(x2d, mod3, rows_per_mod, tm, g1, wparts):
    t = x2d.shape[0]
    wu, wq, wg, wif, wift, bif, bift = wparts
    mrows = mod3.shape[1]

    def mod_spec(j):
        return pl.BlockSpec((None, mrows, D_MODEL), lambda i: ((i * tm) // rows_per_mod, 0, j))

    const = lambda shape: pl.BlockSpec(shape, lambda i: (0,) * len(shape), pipeline_mode=pl.Buffered(1))
    return pl.pallas_call(
        _inproj_body,
        grid=(t // tm,),
        in_specs=[pl.BlockSpec((tm, D_MODEL), lambda i: (i, 0)),
                  const((1, D_MODEL)), mod_spec(1), mod_spec(0),
                  const(wu.shape), const(wq.shape), const(wg.shape), const(wif.shape), const(wift.shape),
                  const(bif.shape), const(bift.shape)],
        out_specs=[pl.BlockSpec((tm, S5_WIDTH), lambda i: (i, 0)),
                   pl.BlockSpec((tm, 4 * MWIDTH), lambda i: (i, 0)),
                   pl.BlockSpec((tm, 2 * D_MODEL), lambda i: (i, 0)),
                   pl.BlockSpec((tm, LANES), lambda i: (i, 0)),
                   pl.BlockSpec((8, tm), lambda i: (0, i))],
        out_shape=[jax.ShapeDtypeStruct((t, S5_WIDTH), F32),
                   jax.ShapeDtypeStruct((t, 4 * MWIDTH), BF16),
                   jax.ShapeDtypeStruct((t, 2 * D_MODEL), BF16),
                   jax.ShapeDtypeStruct((t, LANES), F32),
                   jax.ShapeDtypeStruct((8, t), F32)],
        compiler_params=_cparams("parallel"),
        name="inproj",
    )(x2d, g1, mod3, mod3, wu, wq, wg, wif, wift, bif, bift)


def _s5_tables(lam_re, lam_im, log_dt, b_re, b_im, c_re, c_im, d_s5):
    hi = lax.Precision.HIGHEST
    dt = jnp.exp(log_dt.astype(F32))[:, None]
    lr, li = lam_re.astype(F32), lam_im.astype(F32)
    dpow = jnp.arange(S5_CHUNK + 1, dtype=F32)[:, None, None]
    mag = jnp.exp(dpow * (lr * dt))
    pw_re, pw_im = mag * jnp.cos(dpow * (li * dt)), mag * jnp.sin(dpow * (li * dt))
    ab_re, ab_im = pw_re[1], pw_im[1]
    den = lr * lr + li * li
    nr, ni = ab_re - 1.0, ab_im
    coef_re = (nr * lr + ni * li) / den
    coef_im = (ni * lr - nr * li) / den
    br, bi = b_re.astype(F32), b_im.astype(F32)
    bb_re = coef_re[..., None] * br - coef_im[..., None] * bi
    bb_im = coef_re[..., None] * bi + coef_im[..., None] * br
    cr, ci = c_re.astype(F32), c_im.astype(F32)
    cl_re = cr[None] * pw_re[:, :, None, :] - ci[None] * pw_im[:, :, None, :]
    cl_im = cr[None] * pw_im[:, :, None, :] + ci[None] * pw_re[:, :, None, :]
    kd = (jnp.einsum('dgop,gpi->dgio', cl_re[:S5_CHUNK], bb_re, precision=hi)
          - jnp.einsum('dgop,gpi->dgio', cl_im[:S5_CHUNK], bb_im, precision=hi))
    kd = kd.at[0].add(d_s5.astype(F32)[:, :, None] * jnp.eye(S5_GROUP, dtype=F32)[None])
    kext = jnp.concatenate([jnp.zeros_like(kd[:1]), kd], axis=0)
    kc = jnp.stack([kext[S5_CHUNK - 1::-1], kext[S5_CHUNK:0:-1]], axis=3)
    kc = kc.reshape(S5_CHUNK, N_SUPER, S5_SUPER * S5_GROUP, 2 * S5_GROUP).transpose(1, 0, 2, 3)
    kcomp = jnp.pad(kc.reshape(N_SUPER, S5_CHUNK * LANES, 2 * S5_GROUP), ((0, 0), (0, 0), (0, LANES - 2 * S5_GROUP)))
    rp_re, rp_im = pw_re[S5_CHUNK - 1::-1], pw_im[S5_CHUNK - 1::-1]
    f_re = rp_re[..., None] * bb_re[None] - rp_im[..., None] * bb_im[None]
    f_im = rp_re[..., None] * bb_im[None] + rp_im[..., None] * bb_re[None]
    fcat = jnp.stack([f_re, f_im], axis=2).transpose(0, 1, 4, 2, 3)
    fcomp = fcat.reshape(S5_CHUNK, N_SUPER, LANES, 2 * S5_STATE).transpose(1, 0, 2, 3)
    fcomp = fcomp.reshape(N_SUPER, S5_CHUNK * LANES, 2 * S5_STATE)
    ecat = jnp.stack([cl_re[1:], -cl_im[1:]], axis=0)
    er = ecat.transpose(2, 0, 4, 1, 3).reshape(N_SUPER, S5_SUPER, 2, S5_STATE, S5_CHUNK * S5_GROUP)
    ecomp = er.transpose(0, 2, 1, 3, 4).reshape(N_SUPER, SUPER_STATE, S5_CHUNK * S5_GROUP)

    def lay(re, im):
        z = jnp.stack([re, im], axis=0).reshape(2, N_SUPER, S5_SUPER * S5_STATE)
        return z.transpose(1, 0, 2).reshape(N_SUPER, 1, SUPER_STATE)

    a16 = lay(pw_re[S5_CHUNK], pw_im[S5_CHUNK])
    a1 = lay(ab_re, ab_im)
    return kcomp, fcomp, ecomp, a16, a1


def _expand_groups(comp, row_div, col_blk, n_cols, col_div):
    r, c = comp.shape
    ci = lax.broadcasted_iota(jnp.int32, (c, n_cols), 0)
    co = lax.broadcasted_iota(jnp.int32, (c, n_cols), 1)
    sel = jnp.logical_and(ci // col_blk == co // (S5_SUPER * col_blk), ci % col_blk == co % col_blk)
    rep = _dot(comp.astype(BF16), sel.astype(BF16))
    ro = lax.broadcasted_iota(jnp.int32, (r, n_cols), 0)
    cc = lax.broadcasted_iota(jnp.int32, (r, n_cols), 1)
    same = (ro // row_div) % S5_SUPER == (cc // col_div) % S5_SUPER
    return jnp.where(same, rep, 0.0).astype(BF16)


S5_NB = 4


def _s5_prompt_body(u_ref, kc_ref, fc_ref, ec_ref, a_ref, y_ref, xe_ref, k0_ref, e0_ref, f1_ref,
                    ucat, v_sc, xp_sc, kp_ref, f_ref, e_ref):
    n = SEQ // S5_CHUNK
    half = SUPER_STATE // 2

    @pl.when(pl.program_id(1) == 0)
    def _():
        kp_ref[...] = _expand_groups(kc_ref[...], S5_GROUP, S5_GROUP, 2 * LANES, S5_GROUP)
        f_ref[...] = _expand_groups(fc_ref[...], S5_GROUP, S5_STATE, SUPER_STATE, S5_STATE)
        e_ref[...] = _expand_groups(ec_ref[...], S5_STATE, S5_GROUP, S5_CHUNK * LANES, S5_GROUP)
        k0_ref[...] = kp_ref[(S5_CHUNK - 2) * LANES:(S5_CHUNK - 1) * LANES, :LANES]
        e0_ref[...] = e_ref[:, :LANES]
        f1_ref[...] = f_ref[(S5_CHUNK - 1) * LANES:, :]

    for bl in range(S5_NB):
        for j in range(S5_CHUNK):
            ucat[bl * n:(bl + 1) * n, j * LANES:(j + 1) * LANES] = (
                u_ref[pl.ds(bl * SEQ + j, n, stride=S5_CHUNK), :].astype(BF16))
    v_sc[...] = _dot(ucat[...], f_ref[...])
    a_r, a_i = a_ref[:, :half], a_ref[:, half:]
    xr = [jnp.zeros((1, half), F32) for _ in range(S5_NB)]
    xi = [jnp.zeros((1, half), F32) for _ in range(S5_NB)]
    for c in range(n):
        for bl in range(S5_NB):
            row = bl * n + c
            xp_sc[row:row + 1, :half] = xr[bl]
            xp_sc[row:row + 1, half:] = xi[bl]
            vr, vi = v_sc[row:row + 1, :half], v_sc[row:row + 1, half:]
            xr[bl], xi[bl] = a_r * xr[bl] - a_i * xi[bl] + vr, a_r * xi[bl] + a_i * xr[bl] + vi
    for bl in range(S5_NB):
        xe_ref[bl:bl + 1, :half] = xr[bl]
        xe_ref[bl:bl + 1, half:] = xi[bl]
    ye = _dot(xp_sc[...].astype(BF16), e_ref[...])
    for m in range(S5_CHUNK // 2):
        rows = (2 * m + 2) * LANES
        yp = _dot(ucat[:, :rows], kp_ref[(S5_CHUNK - 2 - 2 * m) * LANES:, :]) + ye[:, 2 * m * LANES:(2 * m + 2) * LANES]
        for bl in range(S5_NB):
            y_ref[pl.ds(bl * SEQ + 2 * m, n, stride=S5_CHUNK), :] = yp[bl * n:(bl + 1) * n, :LANES]
            y_ref[pl.ds(bl * SEQ + 2 * m + 1, n, stride=S5_CHUNK), :] = yp[bl * n:(bl + 1) * n, LANES:]


def _s5_prompt(u, kcomp, fcomp, ecomp, a16):
    n = S5_NB * SEQ // S5_CHUNK
    per_super = lambda shape: pl.BlockSpec((None,) + shape, lambda s, b: (s, 0, 0))
    return pl.pallas_call(
        _s5_prompt_body,
        grid=(N_SUPER, N_PROMPT // S5_NB),
        in_specs=[pl.BlockSpec((S5_NB * SEQ, LANES), lambda s, b: (b, s)),
                  per_super(kcomp.shape[1:]), per_super(fcomp.shape[1:]), per_super(ecomp.shape[1:]),
                  per_super((1, SUPER_STATE))],
        out_specs=[pl.BlockSpec((S5_NB * SEQ, LANES), lambda s, b: (b, s)),
                   pl.BlockSpec((None, None, S5_NB, SUPER_STATE), lambda s, b: (s, b, 0, 0)),
                   per_super((LANES, LANES)), per_super((SUPER_STATE, LANES)), per_super((LANES, SUPER_STATE))],
        out_shape=[jax.ShapeDtypeStruct(u.shape, F32),
                   jax.ShapeDtypeStruct((N_SUPER, N_PROMPT // S5_NB, S5_NB, SUPER_STATE), F32),
                   jax.ShapeDtypeStruct((N_SUPER, LANES, LANES), BF16),
                   jax.ShapeDtypeStruct((N_SUPER, SUPER_STATE, LANES), BF16),
                   jax.ShapeDtypeStruct((N_SUPER, LANES, SUPER_STATE), BF16)],
        scratch_shapes=[pltpu.VMEM((n, S5_CHUNK * LANES), BF16),
                        pltpu.VMEM((n, SUPER_STATE), F32),
                        pltpu.VMEM((n, SUPER_STATE), F32),
                        pltpu.VMEM((S5_CHUNK * LANES, 2 * LANES), BF16),
                        pltpu.VMEM((S5_CHUNK * LANES, SUPER_STATE), BF16),
                        pltpu.VMEM((SUPER_STATE, S5_CHUNK * LANES), BF16)],
        compiler_params=_cparams("parallel", "arbitrary"),
        name="s5_prompt",
    )(u, kcomp, fcomp, ecomp, a16)


def _s5_step_body(u_ref, x0_ref, k0_ref, e0_ref, f1_ref, a_ref, y_ref, xn_ref):
    half = SUPER_STATE // 2
    ub = u_ref[...].astype(BF16)
    x0 = x0_ref[...]
    bu = _dot(ub, f1_ref[...])
    a_r, a_i = a_ref[:, :half], a_ref[:, half:]
    x0r, x0i = x0[:, :half], x0[:, half:]
    xn_ref[:, :half] = a_r * x0r - a_i * x0i + bu[:, :half]
    xn_ref[:, half:] = a_r * x0i + a_i * x0r + bu[:, half:]
    y_ref[...] = _dot(x0.astype(BF16), e0_ref[...]) + _dot(ub, k0_ref[...])


def _s5_step(u, x0, k0, e0, f1, a1):
    t = u.shape[0]
    per = lambda a: pl.BlockSpec((None,) + a.shape[1:], lambda s: (s,) + (0,) * (a.ndim - 1))
    return pl.pallas_call(
        _s5_step_body,
        grid=(N_SUPER,),
        in_specs=[pl.BlockSpec((t, LANES), lambda s: (0, s)), per(x0), per(k0), per(e0), per(f1), per(a1)],
        out_specs=[pl.BlockSpec((t, LANES), lambda s: (0, s)), per(x0)],
        out_shape=[jax.ShapeDtypeStruct(u.shape, F32), jax.ShapeDtypeStruct(x0.shape, F32)],
        compiler_params=_cparams("parallel"),
        name="s5_step",
    )(u, x0, k0, e0, f1, a1)


MSEQ = 2


def _mlstm_prompt_body(q_all, gc_all, *rest):
    gr_refs, (mg_ref, hm_all, c_out, n_out, m_out, c_all, n_all, m_all) = rest[:MSEQ], rest[MSEQ:]
    ci = pl.program_id(1)

    @pl.when(ci == 0)
    def _():
        c_all[...] = jnp.zeros_like(c_all)
        n_all[...] = jnp.zeros_like(n_all)
        m_all[...] = jnp.zeros_like(m_all)

    for sq in range(MSEQ):
        _mlstm_chunk(q_all.at[sq], gc_all.at[sq], gr_refs[sq], mg_ref, hm_all.at[sq],
                     c_all.at[sq], n_all.at[sq], m_all.at[sq])

    @pl.when(ci == pl.num_programs(1) - 1)
    def _():
        c_out[...] = c_all[...]
        n_out[...] = n_all[...]
        m_out[...] = m_all[...]


def _mlstm_chunk(q_ref, gc_ref, gr_ref, mg_ref, hm_ref, c_sc, n_sc, m_sc):
    lc = MCHUNK
    scale = DH ** -0.5
    row = lax.broadcasted_iota(jnp.int32, (lc, lc), 0)
    col = lax.broadcasted_iota(jnp.int32, (lc, lc), 1)
    causal = col <= row
    tri = causal.astype(BF16)
    ones = jnp.ones((lc, DH), BF16)
    gc = gc_ref[...]
    gr = gr_ref[...]
    gc_hi, gc_lo = _split_bf16(gc)
    bcol_all = _dot(tri, gc_hi) + _dot(tri, gc_lo)
    brow_all = _dot_exact_rhs(gr, tri, ((1,), (1,)))
    for hd in range(HEADS):
        q = q_ref[:, hd * DH:(hd + 1) * DH]
        k = q_ref[:, MWIDTH + hd * DH:MWIDTH + (hd + 1) * DH]
        v = q_ref[:, 2 * MWIDTH + hd * DH:2 * MWIDTH + (hd + 1) * DH]
        o = q_ref[:, 3 * MWIDTH + hd * DH:3 * MWIDTH + (hd + 1) * DH]
        i_col = gc[:, hd:hd + 1]
        b_col = bcol_all[:, HEADS + hd:HEADS + hd + 1]
        i_row = gr[hd:hd + 1, :]
        b_row = brow_all[HEADS + hd:HEADS + hd + 1, :]
        m_prev = m_sc[hd:hd + 1, :][:, :1]
        dm = jnp.where(causal, b_col - b_row + i_row, -jnp.inf)
        m_inter = b_col + m_prev
        m_t = jnp.maximum(m_inter, jnp.max(dm, axis=-1, keepdims=True))
        w_inter = jnp.exp(m_inter - m_t)
        w = jnp.exp(dm - m_t)
        s = _dot_dims(q, k, ((1,), (1,))) * scale * w
        sb = s.astype(BF16)
        c_prev = c_sc[hd]
        n_prev = n_sc[hd]
        num = w_inter * _dot(q, c_prev.astype(BF16)) + _dot(sb, v)
        den = w_inter * _dot(q, n_prev.astype(BF16)) + _dot(sb, ones)
        h = num / jnp.maximum(jnp.abs(den), jnp.exp(-m_t))
        hn = h * lax.rsqrt(jnp.mean(h * h, axis=-1, keepdims=True) + EPS)
        hm_ref[:, hd * DH:(hd + 1) * DH] = (hn * mg_ref[:, hd * DH:(hd + 1) * DH] * _sigmoid(o.astype(F32))).astype(BF16)
        m_last = m_t[lc - 1:lc, :]
        b_last = b_col[lc - 1:lc, :]
        decay = w_inter[lc - 1:lc, :]
        w_last = jnp.exp(b_last - b_col + i_col - m_last) * scale
        kw = (k.astype(F32) * w_last).astype(BF16)
        c_sc[hd] = decay * c_prev + _dot_dims(kw, v, ((0,), (0,)))
        n_sc[hd] = decay * n_prev + _dot_dims(kw, ones, ((0,), (0,)))
        m_sc[hd:hd + 1, :] = jnp.broadcast_to(m_last, (1, LANES))


def _mlstm_prompt(qkvo, gcol, grow, mnorm_g):
    nc = SEQ // MCHUNK
    q3 = qkvo.reshape(N_PROMPT, SEQ, 4 * MWIDTH)
    g3 = gcol.reshape(N_PROMPT, SEQ, LANES)
    seq_blk = lambda w: pl.BlockSpec((MSEQ, MCHUNK, w), lambda b, c: (b, c, 0))
    state = lambda shape: pl.BlockSpec((MSEQ,) + shape, lambda b, c: (b,) + (0,) * len(shape))
    gr_specs = [pl.BlockSpec((8, MCHUNK), functools.partial(lambda b, c, sq: (0, (b * MSEQ + sq) * nc + c), sq=sq))
                for sq in range(MSEQ)]
    hm, c_p, n_p, m_p = pl.pallas_call(
        _mlstm_prompt_body,
        grid=(N_PROMPT // MSEQ, nc),
        in_specs=[seq_blk(4 * MWIDTH), seq_blk(LANES)] + gr_specs + [pl.BlockSpec((1, MWIDTH), lambda b, c: (0, 0))],
        out_specs=[seq_blk(MWIDTH), state((HEADS, DH, DH)), state((HEADS, DH, DH)), state((8, LANES))],
        out_shape=[jax.ShapeDtypeStruct((N_PROMPT, SEQ, MWIDTH), BF16),
                   jax.ShapeDtypeStruct((N_PROMPT, HEADS, DH, DH), F32),
                   jax.ShapeDtypeStruct((N_PROMPT, HEADS, DH, DH), F32),
                   jax.ShapeDtypeStruct((N_PROMPT, 8, LANES), F32)],
        scratch_shapes=[pltpu.VMEM((MSEQ, HEADS, DH, DH), F32),
                        pltpu.VMEM((MSEQ, HEADS, DH, DH), F32),
                        pltpu.VMEM((MSEQ, 8, LANES), F32)],
        compiler_params=_cparams("parallel", "arbitrary"),
        name="mlstm_prompt",
    )(q3, g3, *([grow] * MSEQ), mnorm_g)
    return hm.reshape(N_PROMPT * SEQ, MWIDTH), c_p, n_p, m_p


MSTEP_TOK = 8


def _mlstm_step_body(q_ref, qkt_ref, ig_ref, lf_ref, m0_ref, c0_ref, n0_ref, mg_ref, hm_ref, c_out, n_out, m_out, qc_sc):
    scale = DH ** -0.5
    ig = ig_ref[...]
    m_inter = lf_ref[...] + m0_ref[...]
    m_t = jnp.maximum(m_inter, ig)
    w_inter_all = jnp.exp(m_inter - m_t)
    w_all = jnp.exp(ig - m_t) * scale
    floor_all = jnp.exp(-m_t)
    m_out[...] = m_t
    for hd in range(HEADS):
        sl = slice(hd * DH, (hd + 1) * DH)
        q = q_ref[:, sl].astype(F32)
        k = q_ref[:, MWIDTH + hd * DH:MWIDTH + (hd + 1) * DH].astype(F32)
        v = q_ref[:, 2 * MWIDTH + hd * DH:2 * MWIDTH + (hd + 1) * DH].astype(F32)
        o = q_ref[:, 3 * MWIDTH + hd * DH:3 * MWIDTH + (hd + 1) * DH].astype(F32)
        n0 = n0_ref[:, sl]
        wi = w_inter_all[:, hd:hd + 1]
        wk = w_all[:, hd:hd + 1]
        s = jnp.sum(q * k, axis=-1, keepdims=True) * wk
        qn = jnp.sum(q * n0, axis=-1, keepdims=True)
        for j in range(MSTEP_TOK):
            c0 = c0_ref[j, hd]
            q_col = qkt_ref[hd, :, j:j + 1]
            k_col = qkt_ref[HEADS + hd, :, j:j + 1]
            qc_sc[j:j + 1, sl] = jnp.sum(q_col * c0, axis=0, keepdims=True)
            c_out[j, hd] = wi[j:j + 1, :] * c0 + (wk[j:j + 1, :] * k_col) * v[j:j + 1, :]
        num = wi * qc_sc[:, sl] + s * v
        den = wi * qn + s
        h = num / jnp.maximum(jnp.abs(den), floor_all[:, hd:hd + 1])
        hn = h * lax.rsqrt(jnp.mean(h * h, axis=-1, keepdims=True) + EPS)
        hm_ref[:, sl] = hn * mg_ref[:, sl] * _sigmoid(o)
        n_out[:, sl] = wi * n0 + wk * k


def _mlstm_step(qkvo, qkt, ig, lf, m0, c0, n0, mnorm_g):
    t = qkvo.shape[0]
    tk = MSTEP_TOK
    return pl.pallas_call(
        _mlstm_step_body,
        grid=(t // tk,),
        in_specs=[pl.BlockSpec((tk, 4 * MWIDTH), lambda i: (i, 0)),
                  pl.BlockSpec((None, 2 * HEADS, DH, tk), lambda i: (i, 0, 0, 0)),
                  pl.BlockSpec((tk, LANES), lambda i: (i, 0)),
                  pl.BlockSpec((tk, LANES), lambda i: (i, 0)),
                  pl.BlockSpec((tk, LANES), lambda i: (i, 0)),
                  pl.BlockSpec((tk, HEADS, DH, DH), lambda i: (i, 0, 0, 0)),
                  pl.BlockSpec((tk, MWIDTH), lambda i: (i, 0)),
                  pl.BlockSpec((1, MWIDTH), lambda i: (0, 0))],
        out_specs=[pl.BlockSpec((tk, MWIDTH), lambda i: (i, 0)),
                   pl.BlockSpec((tk, HEADS, DH, DH), lambda i: (i, 0, 0, 0)),
                   pl.BlockSpec((tk, MWIDTH), lambda i: (i, 0)),
                   pl.BlockSpec((tk, LANES), lambda i: (i, 0))],
        out_shape=[jax.ShapeDtypeStruct((t, MWIDTH), F32),
                   jax.ShapeDtypeStruct((t, HEADS, DH, DH), F32),
                   jax.ShapeDtypeStruct((t, MWIDTH), F32),
                   jax.ShapeDtypeStruct((t, LANES), F32)],
        scratch_shapes=[pltpu.VMEM((tk, MWIDTH), F32)],
        compiler_params=_cparams("parallel"),
        name="mlstm_step",
    )(qkvo, qkt, ig, lf, m0, c0, n0, mnorm_g)


def _post_body(x_ref, y5_ref, hm_ref, gg_ref, g1_ref, sc_ref, sh_ref, n2_ref, wglu_ref, wmo_ref, wo_ref, rw_ref, rb_ref,
               cnt0_ref, x1_ref, h2_ref, eid_ref, wgt_ref, rank_ref, cnt_ref, run_sc, *, n_real, n_steps):
    i = pl.program_id(0)

    @pl.when(i == 0)
    def _():
        run_sc[...] = cnt0_ref[...]

    if n_steps > n_real:
        @pl.when(i >= n_real)
        def _():
            h2_ref[...] = jnp.zeros_like(h2_ref)

        pl.when(i < n_real)(functools.partial(
            _post_tile, x_ref, y5_ref, hm_ref, gg_ref, g1_ref, sc_ref, sh_ref, n2_ref, wglu_ref, wmo_ref, wo_ref,
            rw_ref, rb_ref, x1_ref, h2_ref, eid_ref, wgt_ref, rank_ref, run_sc))
    else:
        _post_tile(x_ref, y5_ref, hm_ref, gg_ref, g1_ref, sc_ref, sh_ref, n2_ref, wglu_ref, wmo_ref, wo_ref,
                   rw_ref, rb_ref, x1_ref, h2_ref, eid_ref, wgt_ref, rank_ref, run_sc)
    cnt_ref[...] = run_sc[...]


def _post_tile(x_ref, y5_ref, hm_ref, gg_ref, g1_ref, sc_ref, sh_ref, n2_ref, wglu_ref, wmo_ref, wo_ref, rw_ref, rb_ref,
               x1_ref, h2_ref, eid_ref, wgt_ref, rank_ref, run_sc):
    g5 = _gelu_tanh(y5_ref[...]).astype(BF16)
    glu = _dot(g5, wglu_ref[...])
    br_s5 = glu[:, :D_MODEL] * _sigmoid(glu[:, D_MODEL:])
    br_m = _dot(hm_ref[...].astype(BF16), wmo_ref[...])
    gg = gg_ref[...].astype(F32)
    merged = _sigmoid(gg[:, :D_MODEL]) * br_s5 + _sigmoid(gg[:, D_MODEL:]) * br_m
    x1 = x_ref[...] + g1_ref[...] * _dot(merged.astype(BF16), wo_ref[...])
    x1_ref[...] = x1
    ms = jnp.mean(x1 * x1, axis=-1, keepdims=True)
    h2 = x1 * lax.rsqrt(ms + EPS) * n2_ref[...] * (1.0 + sc_ref[...]) + sh_ref[...]
    h2_ref[...] = h2
    logits = _dot_hp(h2, rw_ref[...]) + rb_ref[...]
    tm = logits.shape[0]
    lane = lax.broadcasted_iota(jnp.int32, logits.shape, 1)
    l = jnp.where(lane < N_EXPERTS, logits, -jnp.inf)
    vals, hots, idxs = [], [], []
    for _ in range(TOP_K):
        mx = jnp.max(l, axis=-1, keepdims=True)
        idx = jnp.min(jnp.where(l == mx, lane, LANES), axis=-1, keepdims=True)
        hot = lane == idx
        vals.append(mx)
        hots.append(hot)
        idxs.append(idx)
        l = jnp.where(hot, -jnp.inf, l)
    ex = [jnp.exp(vk - vals[0]) for vk in vals]
    tot = ex[0] + ex[1] + ex[2] + ex[3]
    chosen = jnp.where(hots[0] | hots[1] | hots[2] | hots[3], 1.0, 0.0)
    r_io = lax.broadcasted_iota(jnp.int32, (tm, tm), 0)
    c_io = lax.broadcasted_iota(jnp.int32, (tm, tm), 1)
    earlier = (c_io < r_io).astype(BF16)
    before = run_sc[...] + _dot(earlier, chosen.astype(BF16))
    eid = jnp.zeros(logits.shape, jnp.int32)
    wgt = jnp.zeros(logits.shape, F32)
    rank = jnp.zeros(logits.shape, F32)
    for k in range(TOP_K):
        eid = jnp.where(lane == k, idxs[k], eid)
        wgt = jnp.where(lane == k, ex[k] / tot, wgt)
        rank = jnp.where(lane == k, jnp.sum(jnp.where(hots[k], before, 0.0), axis=-1, keepdims=True), rank)
    eid_ref[...] = eid
    wgt_ref[...] = wgt
    rank_ref[...] = rank.astype(jnp.int32)
    run_sc[...] += jnp.sum(chosen, axis=0, keepdims=True)


def _post(x2d, y5, hm, gg, mod3, rows_per_mod, tm, n2, wglu, wmo, wo, rw, rb, cnt0, h2_rows):
    t = x2d.shape[0]
    mrows = mod3.shape[1]
    n_real = t // tm
    n_steps = -(-h2_rows // tm)
    last = n_real - 1

    def mod_spec(j):
        return pl.BlockSpec((None, mrows, D_MODEL), lambda i: ((jnp.minimum(i, last) * tm) // rows_per_mod, 0, j))

    const = lambda shape: pl.BlockSpec(shape, lambda i: (0,) * len(shape))
    tile = lambda w: pl.BlockSpec((tm, w), lambda i: (jnp.minimum(i, last), 0))
    return pl.pallas_call(
        functools.partial(_post_body, n_real=n_real, n_steps=n_steps),
        grid=(n_steps,),
        in_specs=[tile(D_MODEL), tile(S5_WIDTH), tile(MWIDTH), tile(2 * D_MODEL),
                  mod_spec(2), mod_spec(4), mod_spec(3), const((1, D_MODEL)),
                  const(wglu.shape), const(wmo.shape), const(wo.shape), const(rw.shape), const(rb.shape),
                  const((1, LANES))],
        out_specs=[tile(D_MODEL), pl.BlockSpec((tm, D_MODEL), lambda i: (i, 0)), tile(LANES), tile(LANES), tile(LANES),
                   const((1, LANES))],
        out_shape=[jax.ShapeDtypeStruct((t, D_MODEL), F32),
                   jax.ShapeDtypeStruct((h2_rows, D_MODEL), F32),
                   jax.ShapeDtypeStruct((t, LANES), jnp.int32),
                   jax.ShapeDtypeStruct((t, LANES), F32),
                   jax.ShapeDtypeStruct((t, LANES), jnp.int32),
                   jax.ShapeDtypeStruct((1, LANES), F32)],
        scratch_shapes=[pltpu.VMEM((1, LANES), F32)],
        compiler_params=_cparams("arbitrary"),
        name="post",
    )(x2d, y5, hm, gg, mod3, mod3, mod3, n2, wglu, wmo, wo, rw, rb, cnt0)


N_TOKENS = N_PROMPT * SEQ + N_SAMPLE
N_ASSIGN = N_TOKENS * TOP_K
MOE_TILE = 256
MOE_TILES = N_ASSIGN // MOE_TILE + N_EXPERTS
N_SLOT = MOE_TILES * MOE_TILE
CMB_TILE = 128


MOE_RING = 3


def _moe_body(te_ref, nt_ref, nxe_ref, gpar_ref, idx0_ref, idx1_ref, idx2_ref, h2_hbm, wup_hbm, bup_ref, wdn_hbm, bdn_ref,
              y_ref, xb0, xb1, xb2, sem, wbuf_up, wbuf_dn, wsem, wup_bf, wdn_bf):
    i = pl.program_id(0)
    nt = nt_ref[0]
    bufs = (xb0, xb1, xb2)
    last = MOE_TILES - 1

    def gather(idx_ref, slot):
        for r in range(MOE_TILE):
            pltpu.make_async_copy(h2_hbm.at[pl.ds(idx_ref[0, r], 1)], bufs[slot].at[pl.ds(r, 1)],
                                  sem.at[slot]).start(priority=r % 2)

    def wait(slot):
        pltpu.make_async_copy(h2_hbm.at[pl.ds(0, MOE_TILE)], bufs[slot], sem.at[slot]).wait()

    def weights(e, slot):
        return (pltpu.make_async_copy(wup_hbm.at[e], wbuf_up.at[slot], wsem.at[0, slot]),
                pltpu.make_async_copy(wdn_hbm.at[e], wbuf_dn.at[slot], wsem.at[1, slot]))

    wslot = gpar_ref[i]

    @pl.when(i == 0)
    def _():
        for c in weights(te_ref[0], wslot):
            c.start(priority=1)
        gather(idx0_ref, 0)
        gather(idx1_ref, 1)

    new_expert = jnp.logical_or(i == 0, te_ref[i] != te_ref[jnp.maximum(i - 1, 0)])
    live = i < nt

    @pl.when(jnp.logical_and(new_expert, live))
    def _():
        for c in weights(0, wslot):
            c.wait()
        wup_bf[...] = wbuf_up[wslot].astype(BF16)
        wdn_bf[...] = wbuf_dn[wslot].astype(BF16)
        nxe = nxe_ref[i]

        @pl.when(nxe >= 0)
        def _():
            for c in weights(nxe, 1 - wslot):
                c.start(priority=1)

    def step(slot):
        wait(slot)
        gather(idx2_ref, (slot + 2) % MOE_RING)
        gu = _dot(bufs[slot][...].astype(BF16), wup_bf[...]) + bup_ref[...]
        g = jnp.minimum(gu[:, :D_FF], SWIGLU_LIMIT)
        up = jnp.clip(gu[:, D_FF:], -SWIGLU_LIMIT, SWIGLU_LIMIT)
        act = (up + 1.0) * g * _sigmoid(SWIGLU_ALPHA * g)
        y_ref[...] = _dot(act.astype(BF16), wdn_bf[...]) + bdn_ref[...]

    for slot in range(MOE_RING):
        pl.when(jnp.logical_and(live, i % MOE_RING == slot))(functools.partial(step, slot))

    @pl.when(jnp.logical_not(live))
    def _():
        y_ref[...] = jnp.zeros_like(y_ref)

    drain = jnp.logical_and(jnp.logical_not(live), i < nt + 2)
    for slot in range(MOE_RING):
        pl.when(jnp.logical_and(drain, i % MOE_RING == slot))(functools.partial(wait, slot))
    pl.when(jnp.logical_and(i == last, last - 1 < nt))(functools.partial(wait, (last + 1) % MOE_RING))
    pl.when(jnp.logical_and(i == last, last < nt))(functools.partial(wait, (last + 2) % MOE_RING))


def _moe(tile_expert, n_tiles, next_expert, group_parity, slot_token, h2_all, wup, bup, wdn, bdn):
    idx3 = slot_token.reshape(MOE_TILES, 1, MOE_TILE)
    ahead = lambda k: pl.BlockSpec((None, 1, MOE_TILE),
                                   lambda i, *_: (jnp.minimum(i + k, MOE_TILES - 1), 0, 0), memory_space=pltpu.SMEM)
    grid_spec = pltpu.PrefetchScalarGridSpec(
        num_scalar_prefetch=4,
        grid=(MOE_TILES,),
        in_specs=[ahead(0), ahead(1), ahead(2),
                  pl.BlockSpec(memory_space=pl.ANY),
                  pl.BlockSpec(memory_space=pl.ANY),
                  pl.BlockSpec((None, 1, 2 * D_FF), lambda i, te, *_: (te[i], 0, 0)),
                  pl.BlockSpec(memory_space=pl.ANY),
                  pl.BlockSpec((None, 1, D_MODEL), lambda i, te, *_: (te[i], 0, 0))],
        out_specs=pl.BlockSpec((MOE_TILE, D_MODEL), lambda i, *_: (i, 0)),
        scratch_shapes=[pltpu.VMEM((MOE_TILE, D_MODEL), F32), pltpu.VMEM((MOE_TILE, D_MODEL), F32),
                        pltpu.VMEM((MOE_TILE, D_MODEL), F32),
                        pltpu.SemaphoreType.DMA((MOE_RING,)),
                        pltpu.VMEM((2, D_MODEL, 2 * D_FF), F32), pltpu.VMEM((2, D_FF, D_MODEL), F32),
                        pltpu.SemaphoreType.DMA((2, 2)),
                        pltpu.VMEM((D_MODEL, 2 * D_FF), BF16), pltpu.VMEM((D_FF, D_MODEL), BF16)])
    return pl.pallas_call(
        _moe_body,
        grid_spec=grid_spec,
        out_shape=jax.ShapeDtypeStruct((N_SLOT, D_MODEL), F32),
        compiler_params=_cparams("arbitrary"),
        name="moe",
    )(tile_expert, n_tiles, next_expert, group_parity, idx3, idx3, idx3, h2_all, wup, bup, wdn, bdn)


def _combine_body(cur_ref, nxt_ref, wgt_ref, x1_ref, g2_ref, fg_ref, y_hbm, o_ref, yb0, yb1, sem, *, n_steps):
    i = pl.program_id(0)
    tm = CMB_TILE

    def gather(idx_ref, buf, s):
        for r in range(tm):
            for k in range(TOP_K):
                pltpu.make_async_copy(y_hbm.at[pl.ds(idx_ref[0, r * TOP_K + k], 1)],
                                      buf.at[pl.ds(k * tm + r, 1)], sem.at[s]).start(priority=k % 2)

    def wait(buf, s):
        pltpu.make_async_copy(y_hbm.at[pl.ds(0, TOP_K * tm)], buf, sem.at[s]).wait()

    @pl.when(i == 0)
    def _():
        gather(cur_ref, yb0, 0)

    def step(cur, nxt, s_cur, s_nxt):
        wait(cur, s_cur)
        gather(nxt_ref, nxt, s_nxt)
        wgt = wgt_ref[...]
        acc = wgt[:, 0:1] * cur[0:tm, :]
        for k in range(1, TOP_K):
            acc = acc + wgt[:, k:k + 1] * cur[k * tm:(k + 1) * tm, :]
        xo = x1_ref[...] + g2_ref[...] * acc
        ms = jnp.mean(xo * xo, axis=-1, keepdims=True)
        o_ref[...] = xo * lax.rsqrt(ms + EPS) * fg_ref[...]

    pl.when(i % 2 == 0)(functools.partial(step, yb0, yb1, 0, 1))
    pl.when(i % 2 == 1)(functools.partial(step, yb1, yb0, 1, 0))
    last_slot = n_steps % 2
    pl.when(i == n_steps - 1)(functools.partial(wait, yb1 if last_slot else yb0, last_slot))


def _combine(pos, wgt, x1, mod3, rows_per_mod, fg, y_slots):
    t = x1.shape[0]
    tm = CMB_TILE
    n = t // tm
    mrows = mod3.shape[1]
    assert mrows in (1, tm)
    pos3 = pos.reshape(n, 1, tm * TOP_K)
    return pl.pallas_call(
        functools.partial(_combine_body, n_steps=n),
        grid=(n,),
        in_specs=[pl.BlockSpec((None, 1, tm * TOP_K), lambda i: (i, 0, 0), memory_space=pltpu.SMEM),
                  pl.BlockSpec((None, 1, tm * TOP_K), lambda i: (jnp.minimum(i + 1, n - 1), 0, 0),
                               memory_space=pltpu.SMEM),
                  pl.BlockSpec((tm, LANES), lambda i: (i, 0)),
                  pl.BlockSpec((tm, D_MODEL), lambda i: (i, 0)),
                  pl.BlockSpec((None, mrows, D_MODEL), lambda i: ((i * tm) // rows_per_mod, 0, 5)),
                  pl.BlockSpec((1, D_MODEL), lambda i: (0, 0)),
                  pl.BlockSpec(memory_space=pl.ANY)],
        out_specs=pl.BlockSpec((tm, D_MODEL), lambda i: (i, 0)),
        out_shape=jax.ShapeDtypeStruct((t, D_MODEL), F32),
        scratch_shapes=[pltpu.VMEM((TOP_K * tm, D_MODEL), F32), pltpu.VMEM((TOP_K * tm, D_MODEL), F32),
                        pltpu.SemaphoreType.DMA((2,))],
        compiler_params=_cparams("arbitrary"),
        name="combine",
    )(pos3, pos3, wgt, x1, mod3, fg, y_slots)


def _routing(eid_p, rank_p, eid_s, rank_s, counts):
    cnt = counts[0, :N_EXPERTS].astype(jnp.int32)
    ntile = (cnt + MOE_TILE - 1) // MOE_TILE
    tile_end = jnp.cumsum(ntile)
    poff = (tile_end - ntile) * MOE_TILE
    total = tile_end[-1]
    j = jnp.arange(MOE_TILES, dtype=jnp.int32)
    te = jnp.sum((j[:, None] >= tile_end[None, :]).astype(jnp.int32), axis=1)
    te_last = jnp.sum(((total - 1) >= tile_end).astype(jnp.int32))
    tile_expert = jnp.where(j < total, te, te_last).astype(jnp.int32)
    ex = jnp.arange(N_EXPERTS, dtype=jnp.int32)
    used = ntile > 0
    later = jnp.where(used[None, :] & (ex[None, :] > ex[:, None]), ex[None, :], N_EXPERTS)
    nxt_e = jnp.min(later, axis=1)
    nxt_e = jnp.where(nxt_e < N_EXPERTS, nxt_e, -1).astype(jnp.int32)
    par_e = ((jnp.cumsum(used.astype(jnp.int32)) - 1) % 2).astype(jnp.int32)
    pick = tile_expert[:, None] == ex[None, :]
    next_expert = jnp.sum(jnp.where(pick, nxt_e[None, :], 0), axis=1).astype(jnp.int32)
    group_parity = jnp.sum(jnp.where(pick, par_e[None, :], 0), axis=1).astype(jnp.int32)

    def pos_of(eid, rank):
        e = eid[:, :TOP_K]
        off = jnp.sum(jnp.where(e[:, :, None] == ex[None, None, :], poff[None, None, :], 0), axis=-1)
        return off + rank[:, :TOP_K]

    pos_p = pos_of(eid_p, rank_p)
    pos_s = pos_of(eid_s, rank_s)
    tok = jnp.concatenate([jnp.repeat(jnp.arange(N_PROMPT * SEQ, dtype=jnp.int32), TOP_K),
                           jnp.repeat(N_PROMPT * SEQ + jnp.arange(N_SAMPLE, dtype=jnp.int32), TOP_K)])
    pos_all = jnp.concatenate([pos_p.reshape(-1), pos_s.reshape(-1)])
    padc = jnp.concatenate([ntile * MOE_TILE - cnt, (N_SLOT - total * MOE_TILE).reshape(1)])
    pad_end = jnp.cumsum(padc)
    pad_base = jnp.concatenate([poff + cnt, (total * MOE_TILE).reshape(1)])
    jp = jnp.arange(N_SLOT - N_ASSIGN, dtype=jnp.int32)
    pe = jnp.sum((jp[:, None] >= pad_end[None, :]).astype(jnp.int32), axis=1)
    hit = pe[:, None] == jnp.arange(N_EXPERTS + 1, dtype=jnp.int32)[None, :]
    pad_slot = jp + jnp.sum(jnp.where(hit, (pad_base - (pad_end - padc))[None, :], 0), axis=1)
    keys = jnp.concatenate([pos_all, pad_slot.astype(jnp.int32)])
    vals = jnp.concatenate([tok, jnp.zeros((N_SLOT - N_ASSIGN,), jnp.int32)])
    slot_token = lax.sort((keys, vals), num_keys=1)[1]
    return tile_expert, total.reshape(1).astype(jnp.int32), next_expert, group_parity, slot_token, pos_p, pos_s


def _unpack_s5_state(x):
    n = x.shape[1]
    z = x.reshape(N_SUPER, n, 2, S5_SUPER, S5_STATE).transpose(2, 1, 0, 3, 4).reshape(2, n, S5_GROUPS, S5_STATE)
    return z[0], z[1]


def _pack_s5_state(re, im):
    n = re.shape[0]
    z = jnp.stack([re, im], axis=0).reshape(2, n, N_SUPER, S5_SUPER, S5_STATE)
    return z.transpose(2, 1, 0, 3, 4).reshape(N_SUPER, n, SUPER_STATE)


def kernel(x_prompt, x_sample, c_prompt, c_sample, state_s5_re, state_s5_im, state_mlstm_C, state_mlstm_n, state_mlstm_m, norm1_g, norm2_g, final_norm_g, w_ada, b_ada, w_in, s5_lambda_re, s5_lambda_im, s5_log_dt, s5_B_re, s5_B_im, s5_C_re, s5_C_im, s5_D, s5_w_glu, mlstm_b_i, mlstm_b_f, mlstm_norm_g, mlstm_w_out, w_out, router_w, router_b, expert_w_up, expert_b_up, expert_w_down, expert_b_down):
    assert w_in.shape[0] == 1, "single layer"
    tp = N_PROMPT * SEQ
    xp = x_prompt.reshape(tp, D_MODEL).astype(F32)
    xs = x_sample.reshape(N_SAMPLE, D_MODEL).astype(F32)

    w = w_in[0]
    c0 = S5_WIDTH
    c1 = c0 + 4 * MWIDTH
    c2 = c1 + 2 * HEADS
    wu = w[:, :c0].astype(BF16)
    wq = w[:, c0:c1].astype(BF16)
    wif = jnp.pad(w[:, c1:c2].astype(F32), ((0, 0), (0, LANES - 2 * HEADS)))
    wift = w[:, c1:c2].astype(F32).T
    wg = w[:, c2:].astype(BF16)
    bvec = jnp.concatenate([mlstm_b_i[0], mlstm_b_f[0]]).astype(F32)
    bif = jnp.pad(bvec, (0, LANES - 2 * HEADS)).reshape(1, LANES)
    bift = bvec.reshape(2 * HEADS, 1)
    wparts = (wu, wq, wg, wif, wift, bif, bift)
    g1 = norm1_g[0].reshape(1, D_MODEL).astype(F32)
    n2 = norm2_g[0].reshape(1, D_MODEL).astype(F32)
    fg = final_norm_g.reshape(1, D_MODEL).astype(F32)
    mg = mlstm_norm_g[0].reshape(1, MWIDTH).astype(F32)
    wglu = s5_w_glu[0].astype(BF16)
    wmo = mlstm_w_out[0].astype(BF16)
    wo = w_out[0].astype(BF16)
    rw = jnp.pad(router_w[0].astype(F32), ((0, 0), (0, LANES - N_EXPERTS)))
    rb = jnp.pad(router_b[0].astype(F32), (0, LANES - N_EXPERTS)).reshape(1, LANES)
    wup = expert_w_up[0].astype(F32)
    wdn = expert_w_down[0].astype(F32)
    bup = expert_b_up[0].astype(F32).reshape(N_EXPERTS, 1, 2 * D_FF)
    bdn = expert_b_down[0].astype(F32).reshape(N_EXPERTS, 1, D_MODEL)
    kcomp, fcomp, ecomp, a16, a1 = _s5_tables(
        s5_lambda_re[0], s5_lambda_im[0], s5_log_dt[0], s5_B_re[0], s5_B_im[0], s5_C_re[0], s5_C_im[0], s5_D[0])

    c_all = jnp.concatenate([c_prompt, c_sample], axis=0).astype(F32)
    mod = _adaln(c_all, w_ada[0].astype(F32), b_ada[0].astype(F32))
    mod_p = mod[:N_PROMPT].reshape(N_PROMPT, 1, 6 * D_MODEL)
    mod_s = mod[N_PROMPT:].reshape(1, N_SAMPLE, 6 * D_MODEL)

    u, qkvo, gg, gcol, grow = _inproj(xp, mod_p, SEQ, 1024, g1, wparts)
    y5, xend, k0, e0, f1 = _s5_prompt(u, kcomp, fcomp, ecomp, a16)
    hm, c_p, n_p, m_p = _mlstm_prompt(qkvo, gcol, grow, mg)
    cnt0 = jnp.zeros((1, LANES), F32)
    x1, h2_all, eid_p, wgt_p, rank_p, cnt_p = _post(xp, y5, hm, gg, mod_p, SEQ, 512, n2, wglu, wmo, wo, rw, rb,
                                                    cnt0, N_TOKENS)
    p_re, p_im = _unpack_s5_state(xend.reshape(N_SUPER, N_PROMPT, SUPER_STATE))
    n_p = n_p[..., 0]
    m_p = m_p[:, :HEADS, 0]

    us, qs, ggs, gcs, _ = _inproj(xs, mod_s, N_SAMPLE, N_SAMPLE, g1, wparts)
    x0 = _pack_s5_state(state_s5_re[0].astype(F32), state_s5_im[0].astype(F32))
    y5s, xns = _s5_step(us, x0, k0, e0, f1, a1)
    s_re, s_im = _unpack_s5_state(xns)
    qkt = qs[:, :2 * MWIDTH].astype(F32).reshape(N_SAMPLE // MSTEP_TOK, MSTEP_TOK, 2 * HEADS, DH).transpose(0, 2, 3, 1)
    pad_heads = lambda a: jnp.pad(a, ((0, 0), (0, LANES - HEADS)))
    m0 = pad_heads(state_mlstm_m[0].astype(F32))
    hms, c_s, n_s, m_s = _mlstm_step(qs, qkt, pad_heads(gcs[:, :HEADS]), pad_heads(gcs[:, HEADS:2 * HEADS]), m0,
                                     state_mlstm_C[0].astype(F32),
                                     state_mlstm_n[0].astype(F32).reshape(N_SAMPLE, MWIDTH), mg)
    x1s, h2s, eid_s, wgt_s, rank_s, cnt_all = _post(xs, y5s, hms, ggs, mod_s, N_SAMPLE, N_SAMPLE, n2, wglu, wmo, wo, rw, rb,
                                                    cnt_p, N_SAMPLE)

    h2_all = lax.dynamic_update_slice(h2_all, h2s, (tp, 0))
    tile_expert, n_tiles, next_expert, group_parity, slot_token, pos_p, pos_s = _routing(
        eid_p, rank_p, eid_s, rank_s, cnt_all)
    y_slots = _moe(tile_expert, n_tiles, next_expert, group_parity, slot_token, h2_all, wup, bup, wdn, bdn)
    y_p = _combine(pos_p, wgt_p, x1, mod_p, SEQ, fg, y_slots)
    y_s = _combine(pos_s, wgt_s, x1s, mod_s, N_SAMPLE, fg, y_slots)

    return (y_p.reshape(N_PROMPT, SEQ, D_MODEL).astype(x_prompt.dtype),
            y_s.reshape(N_SAMPLE, 1, D_MODEL).astype(x_sample.dtype),
            p_re[None], p_im[None], c_p[None], n_p[None], m_p[None],
            s_re[None], s_im[None], c_s[None],
            n_s.reshape(1, N_SAMPLE, HEADS, DH), m_s[:, :HEADS][None])
```

```python
import functools
import math

import jax
import jax.numpy as jnp
from jax import lax
from jax.experimental import pallas as pl
from jax.experimental.pallas import tpu as pltpu

F32 = jnp.float32
BF16 = jnp.bfloat16

D_MODEL = 1024
SEQ = 2048
N_PROMPT = 8
N_SAMPLE = 128
S5_WIDTH = 512
S5_GROUP = 16
S5_GROUPS = 32
S5_STATE = 64
HEADS = 4
DH = 128
MWIDTH = HEADS * DH
N_EXPERTS = 32
TOP_K = 4
D_FF = 1024
SWIGLU_LIMIT = 7.0
SWIGLU_ALPHA = 1.702
EPS = 1e-6

LANES = 128
S5_CHUNK = 16
S5_SUPER = LANES // S5_GROUP
N_SUPER = S5_GROUPS // S5_SUPER
SUPER_STATE = 2 * S5_SUPER * S5_STATE
MCHUNK = 128
VMEM_LIMIT = 56 * 1024 * 1024


def _cparams(*sem):
    return pltpu.CompilerParams(dimension_semantics=sem, vmem_limit_bytes=VMEM_LIMIT)


def _dot(a, b):
    return jnp.dot(a, b, preferred_element_type=F32)


def _dot_dims(a, b, dims):
    return lax.dot_general(a, b, (dims, ((), ())), preferred_element_type=F32)


def _split_bf16(a):
    hi = a.astype(BF16)
    lo = (a - hi.astype(F32)).astype(BF16)
    return hi, lo


def _dot_hp(a, b, dims=((1,), (0,))):
    ah, al = _split_bf16(a)
    bh, bl = _split_bf16(b)
    return _dot_dims(ah, bh, dims) + (_dot_dims(al, bh, dims) + _dot_dims(ah, bl, dims))


def _dot_exact_rhs(a, b_exact, dims=((1,), (0,))):
    ah, al = _split_bf16(a)
    return _dot_dims(ah, b_exact, dims) + _dot_dims(al, b_exact, dims)


def _log_sigmoid(x):
    return -(jnp.maximum(-x, 0.0) + jnp.log1p(jnp.exp(-jnp.abs(x))))


def _sigmoid(x):
    return 0.5 * (1.0 + jnp.tanh(0.5 * x))


def _gelu_tanh(x):
    c = math.sqrt(2.0 / math.pi)
    return 0.5 * x * (1.0 + jnp.tanh(c * (x + 0.044715 * (x * x * x))))


def _adaln_body(c_ref, w_ref, b_ref, o_ref):
    c = c_ref[...]
    s = c * _sigmoid(c)
    o_ref[...] = _dot_hp(s, w_ref[...]) + b_ref[...]


def _adaln(c_all, w_ada, b_ada):
    n = c_all.shape[0]
    tn = 1024
    return pl.pallas_call(
        _adaln_body,
        grid=(6 * D_MODEL // tn,),
        in_specs=[pl.BlockSpec((n, D_MODEL), lambda j: (0, 0)),
                  pl.BlockSpec((D_MODEL, tn), lambda j: (0, j)),
                  pl.BlockSpec((1, tn), lambda j: (0, j))],
        out_specs=pl.BlockSpec((n, tn), lambda j: (0, j)),
        out_shape=jax.ShapeDtypeStruct((n, 6 * D_MODEL), F32),
        compiler_params=_cparams("parallel"),
        name="adaln",
    )(c_all, w_ada, b_ada.reshape(1, -1))


def _inproj_body(x_ref, g1_ref, sc_ref, sh_ref, wu_ref, wq_ref, wg_ref, wif_ref, bif_ref,
                 u_ref, q_ref, gg_ref, if_ref, ift_ref):
    x = x_ref[...]
    ms = jnp.mean(x * x, axis=-1, keepdims=True)
    h = x * lax.rsqrt(ms + EPS) * g1_ref[...] * (1.0 + sc_ref[...]) + sh_ref[...]
    hb = h.astype(BF16)
    u_ref[...] = _dot(hb, wu_ref[...])
    q_ref[...] = _dot(hb, wq_ref[...]).astype(BF16)
    gg_ref[...] = _dot(hb, wg_ref[...]).astype(BF16)
    gc = _dot_hp(h, wif_ref[...]) + bif_ref[...]
    lane = lax.broadcasted_iota(jnp.int32, gc.shape, 1)
    gates = jnp.where(lane < HEADS, gc, _log_sigmoid(gc))
    if_ref[...] = gates
    ift_ref[...] = gates.T[:2 * HEADS, :]


def _inproj(x2d, mod3, rows_per_mod, tm, g1, wparts):
    t = x2d.shape[0]
    wu, wq, wg, wif, bif = wparts
    mrows = mod3.shape[1]

    def mod_spec(j):
        return pl.BlockSpec((None, mrows, D_MODEL), lambda i: ((i * tm) // rows_per_mod, 0, j))

    const = lambda shape: pl.BlockSpec(shape, lambda i: (0,) * len(shape), pipeline_mode=pl.Buffered(1))
    return pl.pallas_call(
        _inproj_body,
        grid=(t // tm,),
        in_specs=[pl.BlockSpec((tm, D_MODEL), lambda i: (i, 0)),
                  const((1, D_MODEL)), mod_spec(1), mod_spec(0),
                  const(wu.shape), const(wq.shape), const(wg.shape), const(wif.shape), const(bif.shape)],
        out_specs=[pl.BlockSpec((tm, S5_WIDTH), lambda i: (i, 0)),
                   pl.BlockSpec((tm, 4 * MWIDTH), lambda i: (i, 0)),
                   pl.BlockSpec((tm, 2 * D_MODEL), lambda i: (i, 0)),
                   pl.BlockSpec((tm, LANES), lambda i: (i, 0)),
                   pl.BlockSpec((8, tm), lambda i: (0, i))],
        out_shape=[jax.ShapeDtypeStruct((t, S5_WIDTH), F32),
                   jax.ShapeDtypeStruct((t, 4 * MWIDTH), BF16),
                   jax.ShapeDtypeStruct((t, 2 * D_MODEL), BF16),
                   jax.ShapeDtypeStruct((t, LANES), F32),
                   jax.ShapeDtypeStruct((8, t), F32)],
        compiler_params=_cparams("parallel"),
        name="inproj",
    )(x2d, g1, mod3, mod3, wu, wq, wg, wif, bif)


def _s5_tables(lam_re, lam_im, log_dt, b_re, b_im, c_re, c_im, d_s5):
    hi = lax.Precision.HIGHEST
    dt = jnp.exp(log_dt.astype(F32))[:, None]
    lr, li = lam_re.astype(F32), lam_im.astype(F32)
    dpow = jnp.arange(S5_CHUNK + 1, dtype=F32)[:, None, None]
    mag = jnp.exp(dpow * (lr * dt))
    pw_re, pw_im = mag * jnp.cos(dpow * (li * dt)), mag * jnp.sin(dpow * (li * dt))
    ab_re, ab_im = pw_re[1], pw_im[1]
    den = lr * lr + li * li
    nr, ni = ab_re - 1.0, ab_im
    coef_re = (nr * lr + ni * li) / den
    coef_im = (ni * lr - nr * li) / den
    br, bi = b_re.astype(F32), b_im.astype(F32)
    bb_re = coef_re[..., None] * br - coef_im[..., None] * bi
    bb_im = coef_re[..., None] * bi + coef_im[..., None] * br
    cr, ci = c_re.astype(F32), c_im.astype(F32)
    cl_re = cr[None] * pw_re[:, :, None, :] - ci[None] * pw_im[:, :, None, :]
    cl_im = cr[None] * pw_im[:, :, None, :] + ci[None] * pw_re[:, :, None, :]
    ddiag = (d_s5.astype(F32).reshape(N_SUPER, LANES, 1) * jnp.eye(LANES, dtype=F32)[None])
    rp_re, rp_im = pw_re[S5_CHUNK - 1::-1], pw_im[S5_CHUNK - 1::-1]
    f_re = rp_re[..., None] * bb_re[None] - rp_im[..., None] * bb_im[None]
    f_im = rp_re[..., None] * bb_im[None] + rp_im[..., None] * bb_re[None]
    fcat = jnp.stack([f_re, f_im], axis=2).transpose(0, 1, 4, 2, 3)
    fcomp = fcat.reshape(S5_CHUNK, N_SUPER, LANES, 2 * S5_STATE).transpose(1, 0, 2, 3)
    fcomp = fcomp.reshape(N_SUPER, S5_CHUNK * LANES, 2 * S5_STATE)
    ecat = jnp.stack([cl_re[:S5_CHUNK], -cl_im[:S5_CHUNK]], axis=0)
    er = ecat.transpose(2, 0, 4, 1, 3).reshape(N_SUPER, S5_SUPER, 2, S5_STATE, S5_CHUNK * S5_GROUP)
    ecomp = er.transpose(0, 2, 1, 3, 4).reshape(N_SUPER, SUPER_STATE, S5_CHUNK * S5_GROUP)

    def lay(re, im):
        z = jnp.stack([re, im], axis=0).reshape(2, N_SUPER, S5_SUPER * S5_STATE)
        return z.transpose(1, 0, 2).reshape(N_SUPER, 1, SUPER_STATE)

    a16 = lay(pw_re[S5_CHUNK], pw_im[S5_CHUNK])
    a1 = lay(ab_re, ab_im)
    return ddiag, fcomp, ecomp, a16, a1


def _expand_groups(comp, row_div, col_blk, n_cols, col_div):
    r, c = comp.shape
    ci = lax.broadcasted_iota(jnp.int32, (c, n_cols), 0)
    co = lax.broadcasted_iota(jnp.int32, (c, n_cols), 1)
    sel = jnp.logical_and(ci // col_blk == co // (S5_SUPER * col_blk), ci % col_blk == co % col_blk)
    rep = _dot(comp.astype(BF16), sel.astype(BF16))
    ro = lax.broadcasted_iota(jnp.int32, (r, n_cols), 0)
    cc = lax.broadcasted_iota(jnp.int32, (r, n_cols), 1)
    same = (ro // row_div) % S5_SUPER == (cc // col_div) % S5_SUPER
    return jnp.where(same, rep, 0.0).astype(BF16)


S5_NB = 4


def _s5_prompt_body(u_ref, dd_ref, fc_ref, ec_ref, a_ref, a1_ref, y_ref, xe_ref, k0_ref, e0_ref, f1_ref,
                    ucat, v_sc, xp_sc, kp_ref, f_ref, e_ref):
    n = SEQ // S5_CHUNK
    half = SUPER_STATE // 2

    @pl.when(pl.program_id(1) == 0)
    def _():
        f_ref[...] = _expand_groups(fc_ref[...], S5_GROUP, S5_STATE, SUPER_STATE, S5_STATE)
        e_ref[...] = _expand_groups(ec_ref[...], S5_STATE, S5_GROUP, S5_CHUNK * LANES, S5_GROUP)
        f1 = f_ref[(S5_CHUNK - 1) * LANES:, :]
        kall = _dot(f1, e_ref[...])
        for r in range(S5_CHUNK):
            for hf in range(2):
                d = S5_CHUNK - 2 - r + hf
                blk = jnp.zeros((LANES, LANES), F32) if d < 0 else kall[:, d * LANES:(d + 1) * LANES]
                if d == 0:
                    blk = blk + dd_ref[...]
                kp_ref[r * LANES:(r + 1) * LANES, hf * LANES:(hf + 1) * LANES] = blk.astype(BF16)
        k0_ref[...] = kp_ref[(S5_CHUNK - 2) * LANES:(S5_CHUNK - 1) * LANES, :LANES]
        e0_ref[...] = e_ref[:, LANES:2 * LANES]
        f1_ref[...] = f1

    for bl in range(S5_NB):
        for j in range(S5_CHUNK):
            ucat[bl * n:(bl + 1) * n, j * LANES:(j + 1) * LANES] = (
                u_ref[pl.ds(bl * SEQ + j, n, stride=S5_CHUNK), :].astype(BF16))
    v_sc[...] = _dot(ucat[...], f_ref[...])
    a_r, a_i = a_ref[:, :half], a_ref[:, half:]
    xr = [jnp.zeros((1, half), F32) for _ in range(S5_NB)]
    xi = [jnp.zeros((1, half), F32) for _ in range(S5_NB)]
    for c in range(n):
        for bl in range(S5_NB):
            row = bl * n + c
            xp_sc[row:row + 1, :half] = xr[bl]
            xp_sc[row:row + 1, half:] = xi[bl]
            vr, vi = v_sc[row:row + 1, :half], v_sc[row:row + 1, half:]
            xr[bl], xi[bl] = a_r * xr[bl] - a_i * xi[bl] + vr, a_r * xi[bl] + a_i * xr[bl] + vi
    for bl in range(S5_NB):
        xe_ref[bl:bl + 1, :half] = xr[bl]
        xe_ref[bl:bl + 1, half:] = xi[bl]
    l_r, l_i = a1_ref[:, :half], a1_ref[:, half:]
    xpr, xpi = xp_sc[:, :half], xp_sc[:, half:]
    z = jnp.concatenate([l_r * xpr - l_i * xpi, l_r * xpi + l_i * xpr], axis=1).astype(BF16)
    ye = _dot(z, e_ref[...])
    for m in range(S5_CHUNK // 2):
        rows = (2 * m + 2) * LANES
        yp = _dot(ucat[:, :rows], kp_ref[(S5_CHUNK - 2 - 2 * m) * LANES:, :]) + ye[:, 2 * m * LANES:(2 * m + 2) * LANES]
        for bl in range(S5_NB):
            y_ref[pl.ds(bl * SEQ + 2 * m, n, stride=S5_CHUNK), :] = yp[bl * n:(bl + 1) * n, :LANES]
            y_ref[pl.ds(bl * SEQ + 2 * m + 1, n, stride=S5_CHUNK), :] = yp[bl * n:(bl + 1) * n, LANES:]


def _s5_prompt(u, ddiag, fcomp, ecomp, a16, a1):
    n = S5_NB * SEQ // S5_CHUNK
    per_super = lambda shape: pl.BlockSpec((None,) + shape, lambda s, b: (s, 0, 0))
    return pl.pallas_call(
        _s5_prompt_body,
        grid=(N_SUPER, N_PROMPT // S5_NB),
        in_specs=[pl.BlockSpec((S5_NB * SEQ, LANES), lambda s, b: (b, s)),
                  per_super(ddiag.shape[1:]), per_super(fcomp.shape[1:]), per_super(ecomp.shape[1:]),
                  per_super((1, SUPER_STATE)), per_super((1, SUPER_STATE))],
        out_specs=[pl.BlockSpec((S5_NB * SEQ, LANES), lambda s, b: (b, s)),
                   pl.BlockSpec((None, None, S5_NB, SUPER_STATE), lambda s, b: (s, b, 0, 0)),
                   per_super((LANES, LANES)), per_super((SUPER_STATE, LANES)), per_super((LANES, SUPER_STATE))],
        out_shape=[jax.ShapeDtypeStruct(u.shape, F32),
                   jax.ShapeDtypeStruct((N_SUPER, N_PROMPT // S5_NB, S5_NB, SUPER_STATE), F32),
                   jax.ShapeDtypeStruct((N_SUPER, LANES, LANES), BF16),
                   jax.ShapeDtypeStruct((N_SUPER, SUPER_STATE, LANES), BF16),
                   jax.ShapeDtypeStruct((N_SUPER, LANES, SUPER_STATE), BF16)],
        scratch_shapes=[pltpu.VMEM((n, S5_CHUNK * LANES), BF16),
                        pltpu.VMEM((n, SUPER_STATE), F32),
                        pltpu.VMEM((n, SUPER_STATE), F32),
                        pltpu.VMEM((S5_CHUNK * LANES, 2 * LANES), BF16),
                        pltpu.VMEM((S5_CHUNK * LANES, SUPER_STATE), BF16),
                        pltpu.VMEM((SUPER_STATE, S5_CHUNK * LANES), BF16)],
        compiler_params=_cparams("parallel", "arbitrary"),
        name="s5_prompt",
    )(u, ddiag, fcomp, ecomp, a16, a1)


def _s5_step_body(u_ref, x0_ref, k0_ref, e0_ref, f1_ref, a_ref, y_ref, xn_ref):
    half = SUPER_STATE // 2
    ub = u_ref[...].astype(BF16)
    x0 = x0_ref[...]
    bu = _dot(ub, f1_ref[...])
    a_r, a_i = a_ref[:, :half], a_ref[:, half:]
    x0r, x0i = x0[:, :half], x0[:, half:]
    xn_ref[:, :half] = a_r * x0r - a_i * x0i + bu[:, :half]
    xn_ref[:, half:] = a_r * x0i + a_i * x0r + bu[:, half:]
    y_ref[...] = _dot(x0.astype(BF16), e0_ref[...]) + _dot(ub, k0_ref[...])


def _s5_step(u, x0, k0, e0, f1, a1):
    t = u.shape[0]
    per = lambda a: pl.BlockSpec((None,) + a.shape[1:], lambda s: (s,) + (0,) * (a.ndim - 1))
    return pl.pallas_call(
        _s5_step_body,
        grid=(N_SUPER,),
        in_specs=[pl.BlockSpec((t, LANES), lambda s: (0, s)), per(x0), per(k0), per(e0), per(f1), per(a1)],
        out_specs=[pl.BlockSpec((t, LANES), lambda s: (0, s)), per(x0)],
        out_shape=[jax.ShapeDtypeStruct(u.shape, F32), jax.ShapeDtypeStruct(x0.shape, F32)],
        compiler_params=_cparams("parallel"),
        name="s5_step",
    )(u, x0, k0, e0, f1, a1)


MSEQ = 2


def _mlstm_prompt_body(q_all, gc_all, *rest):
    gr_refs, (mg_ref, hm_all, c_out, n_out, m_out, c_all, n_all, m_all) = rest[:MSEQ], rest[MSEQ:]
    ci = pl.program_id(1)

    @pl.when(ci == 0)
    def _():
        c_all[...] = jnp.zeros_like(c_all)
        n_all[...] = jnp.zeros_like(n_all)
        m_all[...] = jnp.zeros_like(m_all)

    for sq in range(MSEQ):
        _mlstm_chunk(q_all.at[sq], gc_all.at[sq], gr_refs[sq], mg_ref, hm_all.at[sq],
                     c_all.at[sq], n_all.at[sq], m_all.at[sq])

    @pl.when(ci == pl.num_programs(1) - 1)
    def _():
        c_out[...] = c_all[...]
        n_out[...] = n_all[...]
        m_out[...] = m_all[...]


def _mlstm_chunk(q_ref, gc_ref, gr_ref, mg_ref, hm_ref, c_sc, n_sc, m_sc):
    lc = MCHUNK
    scale = DH ** -0.5
    row = lax.broadcasted_iota(jnp.int32, (lc, lc), 0)
    col = lax.broadcasted_iota(jnp.int32, (lc, lc), 1)
    causal = col <= row
    tri = causal.astype(BF16)
    ones = jnp.ones((lc, DH), BF16)
    gc = gc_ref[...]
    gr = gr_ref[...]
    gc_hi, gc_lo = _split_bf16(gc)
    bcol_all = _dot(tri, gc_hi) + _dot(tri, gc_lo)
    brow_all = _dot_exact_rhs(gr, tri, ((1,), (1,)))
    for hd in range(HEADS):
        q = q_ref[:, hd * DH:(hd + 1) * DH]
        k = q_ref[:, MWIDTH + hd * DH:MWIDTH + (hd + 1) * DH]
        v = q_ref[:, 2 * MWIDTH + hd * DH:2 * MWIDTH + (hd + 1) * DH]
        o = q_ref[:, 3 * MWIDTH + hd * DH:3 * MWIDTH + (hd + 1) * DH]
        i_col = gc[:, hd:hd + 1]
        b_col = bcol_all[:, HEADS + hd:HEADS + hd + 1]
        i_row = gr[hd:hd + 1, :]
        b_row = brow_all[HEADS + hd:HEADS + hd + 1, :]
        m_prev = m_sc[hd:hd + 1, :][:, :1]
        dm = jnp.where(causal, b_col - b_row + i_row, -jnp.inf)
        m_inter = b_col + m_prev
        m_t = jnp.maximum(m_inter, jnp.max(dm, axis=-1, keepdims=True))
        w_inter = jnp.exp(m_inter - m_t)
        w = jnp.exp(dm - m_t)
        s = _dot_dims(q, k, ((1,), (1,))) * scale * w
        sb = s.astype(BF16)
        c_prev = c_sc[hd]
        n_prev = n_sc[hd]
        num = w_inter * _dot(q, c_prev.astype(BF16)) + _dot(sb, v)
        den = w_inter * _dot(q, n_prev.astype(BF16)) + _dot(sb, ones)
        h = num / jnp.maximum(jnp.abs(den), jnp.exp(-m_t))
        hn = h * lax.rsqrt(jnp.mean(h * h, axis=-1, keepdims=True) + EPS)
        hm_ref[:, hd * DH:(hd + 1) * DH] = (hn * mg_ref[:, hd * DH:(hd + 1) * DH] * _sigmoid(o.astype(F32))).astype(BF16)
        m_last = m_t[lc - 1:lc, :]
        b_last = b_col[lc - 1:lc, :]
        decay = w_inter[lc - 1:lc, :]
        w_last = jnp.exp(b_last - b_col + i_col - m_last) * scale
        kw = (k.astype(F32) * w_last).astype(BF16)
        c_sc[hd] = decay * c_prev + _dot_dims(kw, v, ((0,), (0,)))
        n_sc[hd] = decay * n_prev + _dot_dims(kw, ones, ((0,), (0,)))
        m_sc[hd:hd + 1, :] = jnp.broadcast_to(m_last, (1, LANES))


def _mlstm_prompt(qkvo, gcol, grow, mnorm_g):
    nc = SEQ // MCHUNK
    q3 = qkvo.reshape(N_PROMPT, SEQ, 4 * MWIDTH)
    g3 = gcol.reshape(N_PROMPT, SEQ, LANES)
    seq_blk = lambda w: pl.BlockSpec((MSEQ, MCHUNK, w), lambda b, c: (b, c, 0))
    state = lambda shape: pl.BlockSpec((MSEQ,) + shape, lambda b, c: (b,) + (0,) * len(shape))
    gr_specs = [pl.BlockSpec((8, MCHUNK), functools.partial(lambda b, c, sq: (0, (b * MSEQ + sq) * nc + c), sq=sq))
                for sq in range(MSEQ)]
    hm, c_p, n_p, m_p = pl.pallas_call(
        _mlstm_prompt_body,
        grid=(N_PROMPT // MSEQ, nc),
        in_specs=[seq_blk(4 * MWIDTH), seq_blk(LANES)] + gr_specs + [pl.BlockSpec((1, MWIDTH), lambda b, c: (0, 0))],
        out_specs=[seq_blk(MWIDTH), state((HEADS, DH, DH)), state((HEADS, DH, DH)), state((8, LANES))],
        out_shape=[jax.ShapeDtypeStruct((N_PROMPT, SEQ, MWIDTH), BF16),
                   jax.ShapeDtypeStruct((N_PROMPT, HEADS, DH, DH), F32),
                   jax.ShapeDtypeStruct((N_PROMPT, HEADS, DH, DH), F32),
                   jax.ShapeDtypeStruct((N_PROMPT, 8, LANES), F32)],
        scratch_shapes=[pltpu.VMEM((MSEQ, HEADS, DH, DH), F32),
                        pltpu.VMEM((MSEQ, HEADS, DH, DH), F32),
                        pltpu.VMEM((MSEQ, 8, LANES), F32)],
        compiler_params=_cparams("parallel", "arbitrary"),
        name="mlstm_prompt",
    )(q3, g3, *([grow] * MSEQ), mnorm_g)
    return hm.reshape(N_PROMPT * SEQ, MWIDTH), c_p, n_p, m_p


MSTEP_TOK = 8


def _mlstm_step_body(q_ref, qkt_ref, ig_ref, lf_ref, m0_ref, c0_ref, n0_ref, mg_ref, hm_ref, c_out, n_out, m_out, qc_sc):
    scale = DH ** -0.5
    ig = ig_ref[...]
    m_inter = lf_ref[...] + m0_ref[...]
    m_t = jnp.maximum(m_inter, ig)
    w_inter_all = jnp.exp(m_inter - m_t)
    w_all = jnp.exp(ig - m_t) * scale
    floor_all = jnp.exp(-m_t)
    m_out[...] = m_t
    for hd in range(HEADS):
        sl = slice(hd * DH, (hd + 1) * DH)
        q = q_ref[:, sl].astype(F32)
        k = q_ref[:, MWIDTH + hd * DH:MWIDTH + (hd + 1) * DH].astype(F32)
        v = q_ref[:, 2 * MWIDTH + hd * DH:2 * MWIDTH + (hd + 1) * DH].astype(F32)
        o = q_ref[:, 3 * MWIDTH + hd * DH:3 * MWIDTH + (hd + 1) * DH].astype(F32)
        n0 = n0_ref[:, sl]
        wi = w_inter_all[:, hd:hd + 1]
        wk = w_all[:, hd:hd + 1]
        s = jnp.sum(q * k, axis=-1, keepdims=True) * wk
        qn = jnp.sum(q * n0, axis=-1, keepdims=True)
        for j in range(MSTEP_TOK):
            c0 = c0_ref[j, hd]
            q_col = qkt_ref[hd, :, j:j + 1]
            k_col = qkt_ref[HEADS + hd, :, j:j + 1]
            qc_sc[j:j + 1, sl] = jnp.sum(q_col * c0, axis=0, keepdims=True)
            c_out[j, hd] = wi[j:j + 1, :] * c0 + (wk[j:j + 1, :] * k_col) * v[j:j + 1, :]
        num = wi * qc_sc[:, sl] + s * v
        den = wi * qn + s
        h = num / jnp.maximum(jnp.abs(den), floor_all[:, hd:hd + 1])
        hn = h * lax.rsqrt(jnp.mean(h * h, axis=-1, keepdims=True) + EPS)
        hm_ref[:, sl] = hn * mg_ref[:, sl] * _sigmoid(o)
        n_out[:, sl] = wi * n0 + wk * k


def _mlstm_step(qkvo, qkt, ig, lf, m0, c0, n0, mnorm_g):
    t = qkvo.shape[0]
    tk = MSTEP_TOK
    return pl.pallas_call(
        _mlstm_step_body,
        grid=(t // tk,),
        in_specs=[pl.BlockSpec((tk, 4 * MWIDTH), lambda i: (i, 0)),
                  pl.BlockSpec((None, 2 * HEADS, DH, tk), lambda i: (i, 0, 0, 0)),
                  pl.BlockSpec((tk, LANES), lambda i: (i, 0)),
                  pl.BlockSpec((tk, LANES), lambda i: (i, 0)),
                  pl.BlockSpec((tk, LANES), lambda i: (i, 0)),
                  pl.BlockSpec((tk, HEADS, DH, DH), lambda i: (i, 0, 0, 0)),
                  pl.BlockSpec((tk, MWIDTH), lambda i: (i, 0)),
                  pl.BlockSpec((1, MWIDTH), lambda i: (0, 0))],
        out_specs=[pl.BlockSpec((tk, MWIDTH), lambda i: (i, 0)),
                   pl.BlockSpec((tk, HEADS, DH, DH), lambda i: (i, 0, 0, 0)),
                   pl.BlockSpec((tk, MWIDTH), lambda i: (i, 0)),
                   pl.BlockSpec((tk, LANES), lambda i: (i, 0))],
        out_shape=[jax.ShapeDtypeStruct((t, MWIDTH), F32),
                   jax.ShapeDtypeStruct((t, HEADS, DH, DH), F32),
                   jax.ShapeDtypeStruct((t, MWIDTH), F32),
                   jax.ShapeDtypeStruct((t, LANES), F32)],
        scratch_shapes=[pltpu.VMEM((tk, MWIDTH), F32)],
        compiler_params=_cparams("parallel"),
        name="mlstm_step",
    )(qkvo, qkt, ig, lf, m0, c0, n0, mnorm_g)


def _post_body(x_ref, y5_ref, hm_ref, gg_ref, g1_ref, sc_ref, sh_ref, n2_ref, wglu_ref, wmo_ref, wo_ref, rw_ref, rb_ref,
               cnt0_ref, x1_ref, h2_ref, eid_ref, wgt_ref, rank_ref, cnt_ref, run_sc, *, n_real, n_steps):
    i = pl.program_id(0)

    @pl.when(i == 0)
    def _():
        run_sc[...] = cnt0_ref[...]

    if n_steps > n_real:
        @pl.when(i >= n_real)
        def _():
            h2_ref[...] = jnp.zeros_like(h2_ref)

        pl.when(i < n_real)(functools.partial(
            _post_tile, x_ref, y5_ref, hm_ref, gg_ref, g1_ref, sc_ref, sh_ref, n2_ref, wglu_ref, wmo_ref, wo_ref,
            rw_ref, rb_ref, x1_ref, h2_ref, eid_ref, wgt_ref, rank_ref, run_sc))
    else:
        _post_tile(x_ref, y5_ref, hm_ref, gg_ref, g1_ref, sc_ref, sh_ref, n2_ref, wglu_ref, wmo_ref, wo_ref,
                   rw_ref, rb_ref, x1_ref, h2_ref, eid_ref, wgt_ref, rank_ref, run_sc)
    cnt_ref[...] = run_sc[...]


def _post_tile(x_ref, y5_ref, hm_ref, gg_ref, g1_ref, sc_ref, sh_ref, n2_ref, wglu_ref, wmo_ref, wo_ref, rw_ref, rb_ref,
               x1_ref, h2_ref, eid_ref, wgt_ref, rank_ref, run_sc):
    g5 = _gelu_tanh(y5_ref[...]).astype(BF16)
    glu = _dot(g5, wglu_ref[...])
    br_s5 = glu[:, :D_MODEL] * _sigmoid(glu[:, D_MODEL:])
    br_m = _dot(hm_ref[...].astype(BF16), wmo_ref[...])
    gg = gg_ref[...].astype(F32)
    merged = _sigmoid(gg[:, :D_MODEL]) * br_s5 + _sigmoid(gg[:, D_MODEL:]) * br_m
    x1 = x_ref[...] + g1_ref[...] * _dot(merged.astype(BF16), wo_ref[...])
    x1_ref[...] = x1
    ms = jnp.mean(x1 * x1, axis=-1, keepdims=True)
    h2 = x1 * lax.rsqrt(ms + EPS) * n2_ref[...] * (1.0 + sc_ref[...]) + sh_ref[...]
    h2_ref[...] = h2
    logits = _dot_hp(h2, rw_ref[...]) + rb_ref[...]
    tm = logits.shape[0]
    lane = lax.broadcasted_iota(jnp.int32, logits.shape, 1)
    l = jnp.where(lane < N_EXPERTS, logits, -jnp.inf)
    vals, hots, idxs = [], [], []
    for _ in range(TOP_K):
        mx = jnp.max(l, axis=-1, keepdims=True)
        idx = jnp.min(jnp.where(l == mx, lane, LANES), axis=-1, keepdims=True)
        hot = lane == idx
        vals.append(mx)
        hots.append(hot)
        idxs.append(idx)
        l = jnp.where(hot, -jnp.inf, l)
    ex = [jnp.exp(vk - vals[0]) for vk in vals]
    tot = ex[0] + ex[1] + ex[2] + ex[3]
    chosen = jnp.where(hots[0] | hots[1] | hots[2] | hots[3], 1.0, 0.0)
    r_io = lax.broadcasted_iota(jnp.int32, (tm, tm), 0)
    c_io = lax.broadcasted_iota(jnp.int32, (tm, tm), 1)
    earlier = (c_io < r_io).astype(BF16)
    before = run_sc[...] + _dot(earlier, chosen.astype(BF16))
    eid = jnp.zeros(logits.shape, jnp.int32)
    wgt = jnp.zeros(logits.shape, F32)
    rank = jnp.zeros(logits.shape, F32)
    for k in range(TOP_K):
        eid = jnp.where(lane == k, idxs[k], eid)
        wgt = jnp.where(lane == k, ex[k] / tot, wgt)
        rank = jnp.where(lane == k, jnp.sum(jnp.where(hots[k], before, 0.0), axis=-1, keepdims=True), rank)
    eid_ref[...] = eid
    wgt_ref[...] = wgt
    rank_ref[...] = rank.astype(jnp.int32)
    run_sc[...] += jnp.sum(chosen, axis=0, keepdims=True)


def _post(x2d, y5, hm, gg, mod3, rows_per_mod, tm, n2, wglu, wmo, wo, rw, rb, cnt0, h2_rows):
    t = x2d.shape[0]
    mrows = mod3.shape[1]
    n_real = t // tm
    n_steps = -(-h2_rows // tm)
    last = n_real - 1

    def mod_spec(j):
        return pl.BlockSpec((None, mrows, D_MODEL), lambda i: ((jnp.minimum(i, last) * tm) // rows_per_mod, 0, j))

    const = lambda shape: pl.BlockSpec(shape, lambda i: (0,) * len(shape))
    tile = lambda w: pl.BlockSpec((tm, w), lambda i: (jnp.minimum(i, last), 0))
    return pl.pallas_call(
        functools.partial(_post_body, n_real=n_real, n_steps=n_steps),
        grid=(n_steps,),
        in_specs=[tile(D_MODEL), tile(S5_WIDTH), tile(MWIDTH), tile(2 * D_MODEL),
                  mod_spec(2), mod_spec(4), mod_spec(3), const((1, D_MODEL)),
                  const(wglu.shape), const(wmo.shape), const(wo.shape), const(rw.shape), const(rb.shape),
                  const((1, LANES))],
        out_specs=[tile(D_MODEL), pl.BlockSpec((tm, D_MODEL), lambda i: (i, 0)), tile(LANES), tile(LANES), tile(LANES),
                   const((1, LANES))],
        out_shape=[jax.ShapeDtypeStruct((t, D_MODEL), F32),
                   jax.ShapeDtypeStruct((h2_rows, D_MODEL), F32),
                   jax.ShapeDtypeStruct((t, LANES), jnp.int32),
                   jax.ShapeDtypeStruct((t, LANES), F32),
                   jax.ShapeDtypeStruct((t, LANES), jnp.int32),
                   jax.ShapeDtypeStruct((1, LANES), F32)],
        scratch_shapes=[pltpu.VMEM((1, LANES), F32)],
        compiler_params=_cparams("arbitrary"),
        name="post",
    )(x2d, y5, hm, gg, mod3, mod3, mod3, n2, wglu, wmo, wo, rw, rb, cnt0)


N_TOKENS = N_PROMPT * SEQ + N_SAMPLE
N_ASSIGN = N_TOKENS * TOP_K
MOE_TILE = 256
MOE_TILES = N_ASSIGN // MOE_TILE + N_EXPERTS
N_SLOT = MOE_TILES * MOE_TILE
CMB_TILE = 128


MOE_RING = 3


def _moe_body(te_ref, nt_ref, nxe_ref, gpar_ref, idx0_ref, idx1_ref, idx2_ref, h2_hbm, wup_hbm, bup_ref, wdn_hbm, bdn_ref,
              y_ref, xb0, xb1, xb2, sem, wbuf_up, wbuf_dn, wsem, wup_bf, wdn_bf):
    i = pl.program_id(0)
    nt = nt_ref[0]
    bufs = (xb0, xb1, xb2)
    last = MOE_TILES - 1

    def gather(idx_ref, slot):
        for r in range(MOE_TILE):
            pltpu.make_async_copy(h2_hbm.at[pl.ds(idx_ref[0, r], 1)], bufs[slot].at[pl.ds(r, 1)],
                                  sem.at[slot]).start()

    def wait(slot):
        pltpu.make_async_copy(h2_hbm.at[pl.ds(0, MOE_TILE)], bufs[slot], sem.at[slot]).wait()

    def weights(e, slot):
        return (pltpu.make_async_copy(wup_hbm.at[e], wbuf_up.at[slot], wsem.at[0, slot]),
                pltpu.make_async_copy(wdn_hbm.at[e], wbuf_dn.at[slot], wsem.at[1, slot]))

    wslot = gpar_ref[i]

    @pl.when(i == 0)
    def _():
        for c in weights(te_ref[0], wslot):
            c.start(priority=1)
        gather(idx0_ref, 0)
        gather(idx1_ref, 1)

    new_expert = jnp.logical_or(i == 0, te_ref[i] != te_ref[jnp.maximum(i - 1, 0)])
    live = i < nt

    @pl.when(jnp.logical_and(new_expert, live))
    def _():
        for c in weights(0, wslot):
            c.wait()
        wup_bf[...] = wbuf_up[wslot].astype(BF16)
        wdn_bf[...] = wbuf_dn[wslot].astype(BF16)
        nxe = nxe_ref[i]

        @pl.when(nxe >= 0)
        def _():
            for c in weights(nxe, 1 - wslot):
                c.start(priority=1)

    def step(slot):
        wait(slot)
        gather(idx2_ref, (slot + 2) % MOE_RING)
        gu = _dot(bufs[slot][...].astype(BF16), wup_bf[...]) + bup_ref[...]
        g = jnp.minimum(gu[:, :D_FF], SWIGLU_LIMIT)
        up = jnp.clip(gu[:, D_FF:], -SWIGLU_LIMIT, SWIGLU_LIMIT)
        act = (up + 1.0) * g * _sigmoid(SWIGLU_ALPHA * g)
        y_ref[...] = _dot(act.astype(BF16), wdn_bf[...]) + bdn_ref[...]

    for slot in range(MOE_RING):
        pl.when(jnp.logical_and(live, i % MOE_RING == slot))(functools.partial(step, slot))

    @pl.when(jnp.logical_not(live))
    def _():
        y_ref[...] = jnp.zeros_like(y_ref)

    drain = jnp.logical_and(jnp.logical_not(live), i < nt + 2)
    for slot in range(MOE_RING):
        pl.when(jnp.logical_and(drain, i % MOE_RING == slot))(functools.partial(wait, slot))
    pl.when(jnp.logical_and(i == last, last - 1 < nt))(functools.partial(wait, (last + 1) % MOE_RING))
    pl.when(jnp.logical_and(i == last, last < nt))(functools.partial(wait, (last + 2) % MOE_RING))


def _moe(tile_expert, n_tiles, next_expert, group_parity, slot_token, h2_all, wup, bup, wdn, bdn):
    idx3 = slot_token.reshape(MOE_TILES, 1, MOE_TILE)
    ahead = lambda k: pl.BlockSpec((None, 1, MOE_TILE),
                                   lambda i, *_: (jnp.minimum(i + k, MOE_TILES - 1), 0, 0), memory_space=pltpu.SMEM)
    grid_spec = pltpu.PrefetchScalarGridSpec(
        num_scalar_prefetch=4,
        grid=(MOE_TILES,),
        in_specs=[ahead(0), ahead(1), ahead(2),
                  pl.BlockSpec(memory_space=pl.ANY),
                  pl.BlockSpec(memory_space=pl.ANY),
                  pl.BlockSpec((None, 1, 2 * D_FF), lambda i, te, *_: (te[i], 0, 0)),
                  pl.BlockSpec(memory_space=pl.ANY),
                  pl.BlockSpec((None, 1, D_MODEL), lambda i, te, *_: (te[i], 0, 0))],
        out_specs=pl.BlockSpec((MOE_TILE, D_MODEL), lambda i, *_: (i, 0)),
        scratch_shapes=[pltpu.VMEM((MOE_TILE, D_MODEL), F32), pltpu.VMEM((MOE_TILE, D_MODEL), F32),
                        pltpu.VMEM((MOE_TILE, D_MODEL), F32),
                        pltpu.SemaphoreType.DMA((MOE_RING,)),
                        pltpu.VMEM((2, D_MODEL, 2 * D_FF), F32), pltpu.VMEM((2, D_FF, D_MODEL), F32),
                        pltpu.SemaphoreType.DMA((2, 2)),
                        pltpu.VMEM((D_MODEL, 2 * D_FF), BF16), pltpu.VMEM((D_FF, D_MODEL), BF16)])
    return pl.pallas_call(
        _moe_body,
        grid_spec=grid_spec,
        out_shape=jax.ShapeDtypeStruct((N_SLOT, D_MODEL), F32),
        compiler_params=_cparams("arbitrary"),
        name="moe",
    )(tile_expert, n_tiles, next_expert, group_parity, idx3, idx3, idx3, h2_all, wup, bup, wdn, bdn)


def _combine_body(cur_ref, nxt_ref, wgt_ref, x1_ref, g2_ref, fg_ref, y_hbm, o_ref, yb0, yb1, sem, *, n_steps):
    i = pl.program_id(0)
    tm = CMB_TILE

    def gather(idx_ref, buf, s):
        for r in range(tm):
            for k in range(TOP_K):
                pltpu.make_async_copy(y_hbm.at[pl.ds(idx_ref[0, r * TOP_K + k], 1)],
                                      buf.at[pl.ds(k * tm + r, 1)], sem.at[s]).start(priority=k % 2)

    def wait(buf, s):
        pltpu.make_async_copy(y_hbm.at[pl.ds(0, TOP_K * tm)], buf, sem.at[s]).wait()

    @pl.when(i == 0)
    def _():
        gather(cur_ref, yb0, 0)

    def step(cur, nxt, s_cur, s_nxt):
        wait(cur, s_cur)
        gather(nxt_ref, nxt, s_nxt)
        wgt = wgt_ref[...]
        acc = wgt[:, 0:1] * cur[0:tm, :]
        for k in range(1, TOP_K):
            acc = acc + wgt[:, k:k + 1] * cur[k * tm:(k + 1) * tm, :]
        xo = x1_ref[...] + g2_ref[...] * acc
        ms = jnp.mean(xo * xo, axis=-1, keepdims=True)
        o_ref[...] = xo * lax.rsqrt(ms + EPS) * fg_ref[...]

    pl.when(i % 2 == 0)(functools.partial(step, yb0, yb1, 0, 1))
    pl.when(i % 2 == 1)(functools.partial(step, yb1, yb0, 1, 0))
    last_slot = n_steps % 2
    pl.when(i == n_steps - 1)(functools.partial(wait, yb1 if last_slot else yb0, last_slot))


def _combine(pos, wgt, x1, mod3, rows_per_mod, fg, y_slots):
    t = x1.shape[0]
    tm = CMB_TILE
    n = t // tm
    mrows = mod3.shape[1]
    assert mrows in (1, tm)
    pos3 = pos.reshape(n, 1, tm * TOP_K)
    return pl.pallas_call(
        functools.partial(_combine_body, n_steps=n),
        grid=(n,),
        in_specs=[pl.BlockSpec((None, 1, tm * TOP_K), lambda i: (i, 0, 0), memory_space=pltpu.SMEM),
                  pl.BlockSpec((None, 1, tm * TOP_K), lambda i: (jnp.minimum(i + 1, n - 1), 0, 0),
                               memory_space=pltpu.SMEM),
                  pl.BlockSpec((tm, LANES), lambda i: (i, 0)),
                  pl.BlockSpec((tm, D_MODEL), lambda i: (i, 0)),
                  pl.BlockSpec((None, mrows, D_MODEL), lambda i: ((i * tm) // rows_per_mod, 0, 5)),
                  pl.BlockSpec((1, D_MODEL), lambda i: (0, 0)),
                  pl.BlockSpec(memory_space=pl.ANY)],
        out_specs=pl.BlockSpec((tm, D_MODEL), lambda i: (i, 0)),
        out_shape=jax.ShapeDtypeStruct((t, D_MODEL), F32),
        scratch_shapes=[pltpu.VMEM((TOP_K * tm, D_MODEL), F32), pltpu.VMEM((TOP_K * tm, D_MODEL), F32),
                        pltpu.SemaphoreType.DMA((2,))],
        compiler_params=_cparams("arbitrary"),
        name="combine",
    )(pos3, pos3, wgt, x1, mod3, fg, y_slots)


def _routing(eid_p, rank_p, eid_s, rank_s, counts):
    cnt = counts[0, :N_EXPERTS].astype(jnp.int32)
    ntile = (cnt + MOE_TILE - 1) // MOE_TILE
    tile_end = jnp.cumsum(ntile)
    poff = (tile_end - ntile) * MOE_TILE
    total = tile_end[-1]
    j = jnp.arange(MOE_TILES, dtype=jnp.int32)
    te = jnp.sum((j[:, None] >= tile_end[None, :]).astype(jnp.int32), axis=1)
    te_last = jnp.sum(((total - 1) >= tile_end).astype(jnp.int32))
    tile_expert = jnp.where(j < total, te, te_last).astype(jnp.int32)
    ex = jnp.arange(N_EXPERTS, dtype=jnp.int32)
    used = ntile > 0
    later = jnp.where(used[None, :] & (ex[None, :] > ex[:, None]), ex[None, :], N_EXPERTS)
    nxt_e = jnp.min(later, axis=1)
    nxt_e = jnp.where(nxt_e < N_EXPERTS, nxt_e, -1).astype(jnp.int32)
    par_e = ((jnp.cumsum(used.astype(jnp.int32)) - 1) % 2).astype(jnp.int32)
    pick = tile_expert[:, None] == ex[None, :]
    next_expert = jnp.sum(jnp.where(pick, nxt_e[None, :], 0), axis=1).astype(jnp.int32)
    group_parity = jnp.sum(jnp.where(pick, par_e[None, :], 0), axis=1).astype(jnp.int32)

    def pos_of(eid, rank):
        e = eid[:, :TOP_K]
        off = jnp.sum(jnp.where(e[:, :, None] == ex[None, None, :], poff[None, None, :], 0), axis=-1)
        return off + rank[:, :TOP_K]

    pos_p = pos_of(eid_p, rank_p)
    pos_s = pos_of(eid_s, rank_s)
    tok = jnp.concatenate([jnp.repeat(jnp.arange(N_PROMPT * SEQ, dtype=jnp.int32), TOP_K),
                           jnp.repeat(N_PROMPT * SEQ + jnp.arange(N_SAMPLE, dtype=jnp.int32), TOP_K)])
    pos_all = jnp.concatenate([pos_p.reshape(-1), pos_s.reshape(-1)])
    padc = jnp.concatenate([ntile * MOE_TILE - cnt, (N_SLOT - total * MOE_TILE).reshape(1)])
    pad_end = jnp.cumsum(padc)
    pad_base = jnp.concatenate([poff + cnt, (total * MOE_TILE).reshape(1)])
    jp = jnp.arange(N_SLOT - N_ASSIGN, dtype=jnp.int32)
    pe = jnp.sum((jp[:, None] >= pad_end[None, :]).astype(jnp.int32), axis=1)
    hit = pe[:, None] == jnp.arange(N_EXPERTS + 1, dtype=jnp.int32)[None, :]
    pad_slot = jp + jnp.sum(jnp.where(hit, (pad_base - (pad_end - padc))[None, :], 0), axis=1)
    keys = jnp.concatenate([pos_all, pad_slot.astype(jnp.int32)])
    vals = jnp.concatenate([tok, jnp.zeros((N_SLOT - N_ASSIGN,), jnp.int32)])
    slot_token = lax.sort((keys, vals), num_keys=1)[1]
    return tile_expert, total.reshape(1).astype(jnp.int32), next_expert, group_parity, slot_token, pos_p, pos_s


def _unpack_s5_state(x):
    n = x.shape[1]
    z = x.reshape(N_SUPER, n, 2, S5_SUPER, S5_STATE).transpose(2, 1, 0, 3, 4).reshape(2, n, S5_GROUPS, S5_STATE)
    return z[0], z[1]


def _pack_s5_state(re, im):
    n = re.shape[0]
    z = jnp.stack([re, im], axis=0).reshape(2, n, N_SUPER, S5_SUPER, S5_STATE)
    return z.transpose(2, 1, 0, 3, 4).reshape(N_SUPER, n, SUPER_STATE)


def kernel(x_prompt, x_sample, c_prompt, c_sample, state_s5_re, state_s5_im, state_mlstm_C, state_mlstm_n, state_mlstm_m, norm1_g, norm2_g, final_norm_g, w_ada, b_ada, w_in, s5_lambda_re, s5_lambda_im, s5_log_dt, s5_B_re, s5_B_im, s5_C_re, s5_C_im, s5_D, s5_w_glu, mlstm_b_i, mlstm_b_f, mlstm_norm_g, mlstm_w_out, w_out, router_w, router_b, expert_w_up, expert_b_up, expert_w_down, expert_b_down):
    assert w_in.shape[0] == 1, "single layer"
    tp = N_PROMPT * SEQ
    xp = x_prompt.reshape(tp, D_MODEL).astype(F32)
    xs = x_sample.reshape(N_SAMPLE, D_MODEL).astype(F32)

    w = w_in[0]
    c0 = S5_WIDTH
    c1 = c0 + 4 * MWIDTH
    c2 = c1 + 2 * HEADS
    wu = w[:, :c0].astype(BF16)
    wq = w[:, c0:c1].astype(BF16)
    wif = jnp.pad(w[:, c1:c2].astype(F32), ((0, 0), (0, LANES - 2 * HEADS)))
    wg = w[:, c2:].astype(BF16)
    bvec = jnp.concatenate([mlstm_b_i[0], mlstm_b_f[0]]).astype(F32)
    bif = jnp.pad(bvec, (0, LANES - 2 * HEADS)).reshape(1, LANES)
    wparts = (wu, wq, wg, wif, bif)
    g1 = norm1_g[0].reshape(1, D_MODEL).astype(F32)
    n2 = norm2_g[0].reshape(1, D_MODEL).astype(F32)
    fg = final_norm_g.reshape(1, D_MODEL).astype(F32)
    mg = mlstm_norm_g[0].reshape(1, MWIDTH).astype(F32)
    wglu = s5_w_glu[0].astype(BF16)
    wmo = mlstm_w_out[0].astype(BF16)
    wo = w_out[0].astype(BF16)
    rw = jnp.pad(router_w[0].astype(F32), ((0, 0), (0, LANES - N_EXPERTS)))
    rb = jnp.pad(router_b[0].astype(F32), (0, LANES - N_EXPERTS)).reshape(1, LANES)
    wup = expert_w_up[0].astype(F32)
    wdn = expert_w_down[0].astype(F32)
    bup = expert_b_up[0].astype(F32).reshape(N_EXPERTS, 1, 2 * D_FF)
    bdn = expert_b_down[0].astype(F32).reshape(N_EXPERTS, 1, D_MODEL)
    ddiag, fcomp, ecomp, a16, a1 = _s5_tables(
        s5_lambda_re[0], s5_lambda_im[0], s5_log_dt[0], s5_B_re[0], s5_B_im[0], s5_C_re[0], s5_C_im[0], s5_D[0])

    c_all = jnp.concatenate([c_prompt, c_sample], axis=0).astype(F32)
    mod = _adaln(c_all, w_ada[0].astype(F32), b_ada[0].astype(F32))
    mod_p = mod[:N_PROMPT].reshape(N_PROMPT, 1, 6 * D_MODEL)
    mod_s = mod[N_PROMPT:].reshape(1, N_SAMPLE, 6 * D_MODEL)

    u, qkvo, gg, gcol, grow = _inproj(xp, mod_p, SEQ, 1024, g1, wparts)
    y5, xend, k0, e0, f1 = _s5_prompt(u, ddiag, fcomp, ecomp, a16, a1)
    hm, c_p, n_p, m_p = _mlstm_prompt(qkvo, gcol, grow, mg)
    cnt0 = jnp.zeros((1, LANES), F32)
    x1, h2_all, eid_p, wgt_p, rank_p, cnt_p = _post(xp, y5, hm, gg, mod_p, SEQ, 512, n2, wglu, wmo, wo, rw, rb,
                                                    cnt0, N_TOKENS)
    p_re, p_im = _unpack_s5_state(xend.reshape(N_SUPER, N_PROMPT, SUPER_STATE))
    n_p = n_p[..., 0]
    m_p = m_p[:, :HEADS, 0]

    us, qs, ggs, gcs, _ = _inproj(xs, mod_s, N_SAMPLE, N_SAMPLE, g1, wparts)
    x0 = _pack_s5_state(state_s5_re[0].astype(F32), state_s5_im[0].astype(F32))
    y5s, xns = _s5_step(us, x0, k0, e0, f1, a1)
    s_re, s_im = _unpack_s5_state(xns)
    qkt = qs[:, :2 * MWIDTH].astype(F32).reshape(N_SAMPLE // MSTEP_TOK, MSTEP_TOK, 2 * HEADS, DH).transpose(0, 2, 3, 1)
    pad_heads = lambda a: jnp.pad(a, ((0, 0), (0, LANES - HEADS)))
    m0 = pad_heads(state_mlstm_m[0].astype(F32))
    hms, c_s, n_s, m_s = _mlstm_step(qs, qkt, pad_heads(gcs[:, :HEADS]), pad_heads(gcs[:, HEADS:2 * HEADS]), m0,
                                     state_mlstm_C[0].astype(F32),
                                     state_mlstm_n[0].astype(F32).reshape(N_SAMPLE, MWIDTH), mg)
    x1s, h2s, eid_s, wgt_s, rank_s, cnt_all = _post(xs, y5s, hms, ggs, mod_s, N_SAMPLE, N_SAMPLE, n2, wglu, wmo, wo, rw, rb,
                                                    cnt_p, N_SAMPLE)

    h2_all = lax.dynamic_update_slice(h2_all, h2s, (tp, 0))
    tile_expert, n_tiles, next_expert, group_parity, slot_token, pos_p, pos_s = _routing(
        eid_p, rank_p, eid_s, rank_s, cnt_all)
    y_slots = _moe(tile_expert, n_tiles, next_expert, group_parity, slot_token, h2_all, wup, bup, wdn, bdn)
    y_p = _combine(pos_p, wgt_p, x1, mod_p, SEQ, fg, y_slots)
    y_s = _combine(pos_s, wgt_s, x1s, mod_s, N_SAMPLE, fg, y_slots)

    return (y_p.reshape(N_PROMPT, SEQ, D_MODEL).astype(x_prompt.dtype),
            y_s.reshape(N_SAMPLE, 1, D_MODEL).astype(x_sample.dtype),
            p_re[None], p_im[None], c_p[None], n_p[None], m_p[None],
            s_re[None], s_im[None], c_s[None],
            n_s.reshape(1, N_SAMPLE, HEADS, DH), m_s[:, :HEADS][None])
```

```python
import functools
import math

import jax
import jax.numpy as jnp
from jax import lax
from jax.experimental import pallas as pl
from jax.experimental.pallas import tpu as pltpu

F32 = jnp.float32
BF16 = jnp.bfloat16

D_MODEL = 1024
SEQ = 2048
N_PROMPT = 8
N_SAMPLE = 128
S5_WIDTH = 512
S5_GROUP = 16
S5_GROUPS = 32
S5_STATE = 64
HEADS = 4
DH = 128
MWIDTH = HEADS * DH
N_EXPERTS = 32
TOP_K = 4
D_FF = 1024
SWIGLU_LIMIT = 7.0
SWIGLU_ALPHA = 1.702
EPS = 1e-6

LANES = 128
S5_CHUNK = 16
S5_SUPER = LANES // S5_GROUP
N_SUPER = S5_GROUPS // S5_SUPER
SUPER_STATE = 2 * S5_SUPER * S5_STATE
MCHUNK = 128
VMEM_LIMIT = 56 * 1024 * 1024


def _cparams(*sem):
    return pltpu.CompilerParams(dimension_semantics=sem, vmem_limit_bytes=VMEM_LIMIT)


def _dot(a, b):
    return jnp.dot(a, b, preferred_element_type=F32)


def _dot_dims(a, b, dims):
    return lax.dot_general(a, b, (dims, ((), ())), preferred_element_type=F32)


def _split_bf16(a):
    hi = a.astype(BF16)
    lo = (a - hi.astype(F32)).astype(BF16)
    return hi, lo


def _dot_hp(a, b, dims=((1,), (0,))):
    ah, al = _split_bf16(a)
    bh, bl = _split_bf16(b)
    return _dot_dims(ah, bh, dims) + (_dot_dims(al, bh, dims) + _dot_dims(ah, bl, dims))


def _dot_exact_rhs(a, b_exact, dims=((1,), (0,))):
    ah, al = _split_bf16(a)
    return _dot_dims(ah, b_exact, dims) + _dot_dims(al, b_exact, dims)


def _log_sigmoid(x):
    return -(jnp.maximum(-x, 0.0) + jnp.log1p(jnp.exp(-jnp.abs(x))))


def _sigmoid(x):
    return 0.5 * (1.0 + jnp.tanh(0.5 * x))


def _gelu_tanh(x):
    c = math.sqrt(2.0 / math.pi)
    return 0.5 * x * (1.0 + jnp.tanh(c * (x + 0.044715 * (x * x * x))))


def _adaln_body(c_ref, w_ref, b_ref, o_ref):
    c = c_ref[...]
    s = c * _sigmoid(c)
    o_ref[...] = _dot_hp(s, w_ref[...]) + b_ref[...]


def _adaln(c_all, w_ada, b_ada):
    n = c_all.shape[0]
    tn = 1024
    return pl.pallas_call(
        _adaln_body,
        grid=(6 * D_MODEL // tn,),
        in_specs=[pl.BlockSpec((n, D_MODEL), lambda j: (0, 0)),
                  pl.BlockSpec((D_MODEL, tn), lambda j: (0, j)),
                  pl.BlockSpec((1, tn), lambda j: (0, j))],
        out_specs=pl.BlockSpec((n, tn), lambda j: (0, j)),
        out_shape=jax.ShapeDtypeStruct((n, 6 * D_MODEL), F32),
        compiler_params=_cparams("parallel"),
        name="adaln",
    )(c_all, w_ada, b_ada.reshape(1, -1))


def _inproj_body(x_ref, g1_ref, sc_ref, sh_ref, wu_ref, wq_ref, wg_ref, wif_ref, bif_ref,
                 u_ref, q_ref, gg_ref, if_ref, ift_ref):
    x = x_ref[...]
    ms = jnp.mean(x * x, axis=-1, keepdims=True)
    h = x * lax.rsqrt(ms + EPS) * g1_ref[...] * (1.0 + sc_ref[...]) + sh_ref[...]
    hb = h.astype(BF16)
    u_ref[...] = _dot(hb, wu_ref[...])
    q_ref[...] = _dot(hb, wq_ref[...]).astype(BF16)
    gg_ref[...] = _dot(hb, wg_ref[...]).astype(BF16)
    gc = _dot_hp(h, wif_ref[...]) + bif_ref[...]
    lane = lax.broadcasted_iota(jnp.int32, gc.shape, 1)
    gates = jnp.where(lane < HEADS, gc, _log_sigmoid(gc))
    if_ref[...] = gates
    ift_ref[...] = gates.T[:2 * HEADS, :]


def _inproj(x2d, mod3, rows_per_mod, tm, g1, wparts):
    t = x2d.shape[0]
    wu, wq, wg, wif, bif = wparts
    mrows = mod3.shape[1]

    def mod_spec(j):
        return pl.BlockSpec((None, mrows, D_MODEL), lambda i: ((i * tm) // rows_per_mod, 0, j))

    const = lambda shape: pl.BlockSpec(shape, lambda i: (0,) * len(shape), pipeline_mode=pl.Buffered(1))
    return pl.pallas_call(
        _inproj_body,
        grid=(t // tm,),
        in_specs=[pl.BlockSpec((tm, D_MODEL), lambda i: (i, 0)),
                  const((1, D_MODEL)), mod_spec(1), mod_spec(0),
                  const(wu.shape), const(wq.shape), const(wg.shape), const(wif.shape), const(bif.shape)],
        out_specs=[pl.BlockSpec((tm, S5_WIDTH), lambda i: (i, 0)),
                   pl.BlockSpec((tm, 4 * MWIDTH), lambda i: (i, 0)),
                   pl.BlockSpec((tm, 2 * D_MODEL), lambda i: (i, 0)),
                   pl.BlockSpec((tm, LANES), lambda i: (i, 0)),
                   pl.BlockSpec((8, tm), lambda i: (0, i))],
        out_shape=[jax.ShapeDtypeStruct((t, S5_WIDTH), F32),
                   jax.ShapeDtypeStruct((t, 4 * MWIDTH), BF16),
                   jax.ShapeDtypeStruct((t, 2 * D_MODEL), BF16),
                   jax.ShapeDtypeStruct((t, LANES), F32),
                   jax.ShapeDtypeStruct((8, t), F32)],
        compiler_params=_cparams("parallel"),
        name="inproj",
    )(x2d, g1, mod3, mod3, wu, wq, wg, wif, bif)


def _s5_tables(lam_re, lam_im, log_dt, b_re, b_im, c_re, c_im, d_s5):
    hi = lax.Precision.HIGHEST
    dt = jnp.exp(log_dt.astype(F32))[:, None]
    lr, li = lam_re.astype(F32), lam_im.astype(F32)
    dpow = jnp.arange(S5_CHUNK + 1, dtype=F32)[:, None, None]
    mag = jnp.exp(dpow * (lr * dt))
    pw_re, pw_im = mag * jnp.cos(dpow * (li * dt)), mag * jnp.sin(dpow * (li * dt))
    ab_re, ab_im = pw_re[1], pw_im[1]
    den = lr * lr + li * li
    nr, ni = ab_re - 1.0, ab_im
    coef_re = (nr * lr + ni * li) / den
    coef_im = (ni * lr - nr * li) / den
    br, bi = b_re.astype(F32), b_im.astype(F32)
    bb_re = coef_re[..., None] * br - coef_im[..., None] * bi
    bb_im = coef_re[..., None] * bi + coef_im[..., None] * br
    cr, ci = c_re.astype(F32), c_im.astype(F32)
    cl_re = cr[None] * pw_re[:, :, None, :] - ci[None] * pw_im[:, :, None, :]
    cl_im = cr[None] * pw_im[:, :, None, :] + ci[None] * pw_re[:, :, None, :]
    ddiag = (d_s5.astype(F32).reshape(N_SUPER, LANES, 1) * jnp.eye(LANES, dtype=F32)[None])
    rp_re, rp_im = pw_re[S5_CHUNK - 1::-1], pw_im[S5_CHUNK - 1::-1]
    f_re = rp_re[..., None] * bb_re[None] - rp_im[..., None] * bb_im[None]
    f_im = rp_re[..., None] * bb_im[None] + rp_im[..., None] * bb_re[None]
    fcat = jnp.stack([f_re, f_im], axis=2).transpose(0, 1, 4, 2, 3)
    fcomp = fcat.reshape(S5_CHUNK, N_SUPER, LANES, 2 * S5_STATE).transpose(1, 0, 2, 3)
    fcomp = fcomp.reshape(N_SUPER, S5_CHUNK * LANES, 2 * S5_STATE)
    ecat = jnp.stack([cl_re[:S5_CHUNK], -cl_im[:S5_CHUNK]], axis=0)
    er = ecat.transpose(2, 0, 4, 1, 3).reshape(N_SUPER, S5_SUPER, 2, S5_STATE, S5_CHUNK * S5_GROUP)
    ecomp = er.transpose(0, 2, 1, 3, 4).reshape(N_SUPER, SUPER_STATE, S5_CHUNK * S5_GROUP)

    def lay(re, im):
        z = jnp.stack([re, im], axis=0).reshape(2, N_SUPER, S5_SUPER * S5_STATE)
        return z.transpose(1, 0, 2).reshape(N_SUPER, 1, SUPER_STATE)

    a16 = lay(pw_re[S5_CHUNK], pw_im[S5_CHUNK])
    a1 = lay(ab_re, ab_im)
    return ddiag, fcomp, ecomp, a16, a1


def _expand_groups(comp, row_div, col_blk, n_cols, col_div):
    r, c = comp.shape
    ci = lax.broadcasted_iota(jnp.int32, (c, n_cols), 0)
    co = lax.broadcasted_iota(jnp.int32, (c, n_cols), 1)
    sel = jnp.logical_and(ci // col_blk == co // (S5_SUPER * col_blk), ci % col_blk == co % col_blk)
    rep = _dot(comp.astype(BF16), sel.astype(BF16))
    ro = lax.broadcasted_iota(jnp.int32, (r, n_cols), 0)
    cc = lax.broadcasted_iota(jnp.int32, (r, n_cols), 1)
    same = (ro // row_div) % S5_SUPER == (cc // col_div) % S5_SUPER
    return jnp.where(same, rep, 0.0).astype(BF16)


S5_NB = 4


def _s5_prompt_body(u_ref, dd_ref, fc_ref, ec_ref, a_ref, a1_ref, y_ref, xe_ref, k0_ref, e0_ref, f1_ref,
                    ucat, v_sc, xp_sc, kp_ref, f_ref, e_ref):
    n = SEQ // S5_CHUNK
    half = SUPER_STATE // 2

    @pl.when(pl.program_id(1) == 0)
    def _():
        f_ref[...] = _expand_groups(fc_ref[...], S5_GROUP, S5_STATE, SUPER_STATE, S5_STATE)
        e_ref[...] = _expand_groups(ec_ref[...], S5_STATE, S5_GROUP, S5_CHUNK * LANES, S5_GROUP)
        f1 = f_ref[(S5_CHUNK - 1) * LANES:, :]
        kall = _dot(f1, e_ref[...])
        for r in range(S5_CHUNK):
            for hf in range(2):
                d = S5_CHUNK - 2 - r + hf
                blk = jnp.zeros((LANES, LANES), F32) if d < 0 else kall[:, d * LANES:(d + 1) * LANES]
                if d == 0:
                    blk = blk + dd_ref[...]
                kp_ref[r * LANES:(r + 1) * LANES, hf * LANES:(hf + 1) * LANES] = blk.astype(BF16)
        k0_ref[...] = kp_ref[(S5_CHUNK - 2) * LANES:(S5_CHUNK - 1) * LANES, :LANES]
        e0_ref[...] = e_ref[:, LANES:2 * LANES]
        f1_ref[...] = f1

    for bl in range(S5_NB):
        for j in range(S5_CHUNK):
            ucat[bl * n:(bl + 1) * n, j * LANES:(j + 1) * LANES] = (
                u_ref[pl.ds(bl * SEQ + j, n, stride=S5_CHUNK), :].astype(BF16))
    v_sc[...] = _dot(ucat[...], f_ref[...])
    a_r, a_i = a_ref[:, :half], a_ref[:, half:]
    xr = [jnp.zeros((1, half), F32) for _ in range(S5_NB)]
    xi = [jnp.zeros((1, half), F32) for _ in range(S5_NB)]
    for c in range(n):
        for bl in range(S5_NB):
            row = bl * n + c
            xp_sc[row:row + 1, :half] = xr[bl]
            xp_sc[row:row + 1, half:] = xi[bl]
            vr, vi = v_sc[row:row + 1, :half], v_sc[row:row + 1, half:]
            xr[bl], xi[bl] = a_r * xr[bl] - a_i * xi[bl] + vr, a_r * xi[bl] + a_i * xr[bl] + vi
    for bl in range(S5_NB):
        xe_ref[bl:bl + 1, :half] = xr[bl]
        xe_ref[bl:bl + 1, half:] = xi[bl]
    l_r, l_i = a1_ref[:, :half], a1_ref[:, half:]
    xpr, xpi = xp_sc[:, :half], xp_sc[:, half:]
    z = jnp.concatenate([l_r * xpr - l_i * xpi, l_r * xpi + l_i * xpr], axis=1).astype(BF16)
    ye = _dot(z, e_ref[...])
    for m in range(S5_CHUNK // 2):
        rows = (2 * m + 2) * LANES
        yp = _dot(ucat[:, :rows], kp_ref[(S5_CHUNK - 2 - 2 * m) * LANES:, :]) + ye[:, 2 * m * LANES:(2 * m + 2) * LANES]
        for bl in range(S5_NB):
            y_ref[pl.ds(bl * SEQ + 2 * m, n, stride=S5_CHUNK), :] = yp[bl * n:(bl + 1) * n, :LANES]
            y_ref[pl.ds(bl * SEQ + 2 * m + 1, n, stride=S5_CHUNK), :] = yp[bl * n:(bl + 1) * n, LANES:]


def _s5_prompt(u, ddiag, fcomp, ecomp, a16, a1):
    n = S5_NB * SEQ // S5_CHUNK
    per_super = lambda shape: pl.BlockSpec((None,) + shape, lambda s, b: (s, 0, 0))
    return pl.pallas_call(
        _s5_prompt_body,
        grid=(N_SUPER, N_PROMPT // S5_NB),
        in_specs=[pl.BlockSpec((S5_NB * SEQ, LANES), lambda s, b: (b, s)),
                  per_super(ddiag.shape[1:]), per_super(fcomp.shape[1:]), per_super(ecomp.shape[1:]),
                  per_super((1, SUPER_STATE)), per_super((1, SUPER_STATE))],
        out_specs=[pl.BlockSpec((S5_NB * SEQ, LANES), lambda s, b: (b, s)),
                   pl.BlockSpec((None, None, S5_NB, SUPER_STATE), lambda s, b: (s, b, 0, 0)),
                   per_super((LANES, LANES)), per_super((SUPER_STATE, LANES)), per_super((LANES, SUPER_STATE))],
        out_shape=[jax.ShapeDtypeStruct(u.shape, F32),
                   jax.ShapeDtypeStruct((N_SUPER, N_PROMPT // S5_NB, S5_NB, SUPER_STATE), F32),
                   jax.ShapeDtypeStruct((N_SUPER, LANES, LANES), BF16),
                   jax.ShapeDtypeStruct((N_SUPER, SUPER_STATE, LANES), BF16),
                   jax.ShapeDtypeStruct((N_SUPER, LANES, SUPER_STATE), BF16)],
        scratch_shapes=[pltpu.VMEM((n, S5_CHUNK * LANES), BF16),
                        pltpu.VMEM((n, SUPER_STATE), F32),
                        pltpu.VMEM((n, SUPER_STATE), F32),
                        pltpu.VMEM((S5_CHUNK * LANES, 2 * LANES), BF16),
                        pltpu.VMEM((S5_CHUNK * LANES, SUPER_STATE), BF16),
                        pltpu.VMEM((SUPER_STATE, S5_CHUNK * LANES), BF16)],
        compiler_params=_cparams("parallel", "arbitrary"),
        name="s5_prompt",
    )(u, ddiag, fcomp, ecomp, a16, a1)


def _s5_step_body(u_ref, x0_ref, k0_ref, e0_ref, f1_ref, a_ref, y_ref, xn_ref):
    half = SUPER_STATE // 2
    ub = u_ref[...].astype(BF16)
    x0 = x0_ref[...]
    bu = _dot(ub, f1_ref[...])
    a_r, a_i = a_ref[:, :half], a_ref[:, half:]
    x0r, x0i = x0[:, :half], x0[:, half:]
    xn_ref[:, :half] = a_r * x0r - a_i * x0i + bu[:, :half]
    xn_ref[:, half:] = a_r * x0i + a_i * x0r + bu[:, half:]
    y_ref[...] = _dot(x0.astype(BF16), e0_ref[...]) + _dot(ub, k0_ref[...])


def _s5_step(u, x0, k0, e0, f1, a1):
    t = u.shape[0]
    per = lambda a: pl.BlockSpec((None,) + a.shape[1:], lambda s: (s,) + (0,) * (a.ndim - 1))
    return pl.pallas_call(
        _s5_step_body,
        grid=(N_SUPER,),
        in_specs=[pl.BlockSpec((t, LANES), lambda s: (0, s)), per(x0), per(k0), per(e0), per(f1), per(a1)],
        out_specs=[pl.BlockSpec((t, LANES), lambda s: (0, s)), per(x0)],
        out_shape=[jax.ShapeDtypeStruct(u.shape, F32), jax.ShapeDtypeStruct(x0.shape, F32)],
        compiler_params=_cparams("parallel"),
        name="s5_step",
    )(u, x0, k0, e0, f1, a1)


MSEQ = 4


def _mlstm_prompt_body(q_all, gc_all, *rest):
    gr_refs, (mg_ref, hm_all, c_out, n_out, m_out, c_all, n_all, m_all) = rest[:MSEQ], rest[MSEQ:]
    ci = pl.program_id(1)

    @pl.when(ci == 0)
    def _():
        c_all[...] = jnp.zeros_like(c_all)
        n_all[...] = jnp.zeros_like(n_all)
        m_all[...] = jnp.zeros_like(m_all)

    for sq in range(MSEQ):
        _mlstm_chunk(q_all.at[sq], gc_all.at[sq], gr_refs[sq], mg_ref, hm_all.at[sq],
                     c_all.at[sq], n_all.at[sq], m_all.at[sq])

    @pl.when(ci == pl.num_programs(1) - 1)
    def _():
        c_out[...] = c_all[...]
        n_out[...] = n_all[...]
        m_out[...] = m_all[...]


def _mlstm_chunk(q_ref, gc_ref, gr_ref, mg_ref, hm_ref, c_sc, n_sc, m_sc):
    lc = MCHUNK
    scale = DH ** -0.5
    row = lax.broadcasted_iota(jnp.int32, (lc, lc), 0)
    col = lax.broadcasted_iota(jnp.int32, (lc, lc), 1)
    causal = col <= row
    tri = causal.astype(BF16)
    ones = jnp.ones((lc, DH), BF16)
    gc = gc_ref[...]
    gr = gr_ref[...]
    gc_hi, gc_lo = _split_bf16(gc)
    bcol_all = _dot(tri, gc_hi) + _dot(tri, gc_lo)
    brow_all = _dot_exact_rhs(gr, tri, ((1,), (1,)))
    for hd in range(HEADS):
        q = q_ref[:, hd * DH:(hd + 1) * DH]
        k = q_ref[:, MWIDTH + hd * DH:MWIDTH + (hd + 1) * DH]
        v = q_ref[:, 2 * MWIDTH + hd * DH:2 * MWIDTH + (hd + 1) * DH]
        o = q_ref[:, 3 * MWIDTH + hd * DH:3 * MWIDTH + (hd + 1) * DH]
        i_col = gc[:, hd:hd + 1]
        b_col = bcol_all[:, HEADS + hd:HEADS + hd + 1]
        i_row = gr[hd:hd + 1, :]
        b_row = brow_all[HEADS + hd:HEADS + hd + 1, :]
        m_prev = m_sc[hd:hd + 1, :][:, :1]
        dm = jnp.where(causal, b_col - b_row + i_row, -jnp.inf)
        m_inter = b_col + m_prev
        m_t = jnp.maximum(m_inter, jnp.max(dm, axis=-1, keepdims=True))
        w_inter = jnp.exp(m_inter - m_t)
        w = jnp.exp(dm - m_t)
        s = _dot_dims(q, k, ((1,), (1,))) * scale * w
        sb = s.astype(BF16)
        c_prev = c_sc[hd]
        n_prev = n_sc[hd]
        num = w_inter * _dot(q, c_prev.astype(BF16)) + _dot(sb, v)
        den = w_inter * _dot(q, n_prev.astype(BF16)) + _dot(sb, ones)
        h = num / jnp.maximum(jnp.abs(den), jnp.exp(-m_t))
        hn = h * lax.rsqrt(jnp.mean(h * h, axis=-1, keepdims=True) + EPS)
        hm_ref[:, hd * DH:(hd + 1) * DH] = (hn * mg_ref[:, hd * DH:(hd + 1) * DH] * _sigmoid(o.astype(F32))).astype(BF16)
        m_last = m_t[lc - 1:lc, :]
        b_last = b_col[lc - 1:lc, :]
        decay = w_inter[lc - 1:lc, :]
        w_last = jnp.exp(b_last - b_col + i_col - m_last) * scale
        kw = (k.astype(F32) * w_last).astype(BF16)
        c_sc[hd] = decay * c_prev + _dot_dims(kw, v, ((0,), (0,)))
        n_sc[hd] = decay * n_prev + _dot_dims(kw, ones, ((0,), (0,)))
        m_sc[hd:hd + 1, :] = jnp.broadcast_to(m_last, (1, LANES))


def _mlstm_prompt(qkvo, gcol, grow, mnorm_g):
    nc = SEQ // MCHUNK
    q3 = qkvo.reshape(N_PROMPT, SEQ, 4 * MWIDTH)
    g3 = gcol.reshape(N_PROMPT, SEQ, LANES)
    seq_blk = lambda w: pl.BlockSpec((MSEQ, MCHUNK, w), lambda b, c: (b, c, 0))
    state = lambda shape: pl.BlockSpec((MSEQ,) + shape, lambda b, c: (b,) + (0,) * len(shape))
    gr_specs = [pl.BlockSpec((8, MCHUNK), functools.partial(lambda b, c, sq: (0, (b * MSEQ + sq) * nc + c), sq=sq))
                for sq in range(MSEQ)]
    hm, c_p, n_p, m_p = pl.pallas_call(
        _mlstm_prompt_body,
        grid=(N_PROMPT // MSEQ, nc),
        in_specs=[seq_blk(4 * MWIDTH), seq_blk(LANES)] + gr_specs + [pl.BlockSpec((1, MWIDTH), lambda b, c: (0, 0))],
        out_specs=[seq_blk(MWIDTH), state((HEADS, DH, DH)), state((HEADS, DH, DH)), state((8, LANES))],
        out_shape=[jax.ShapeDtypeStruct((N_PROMPT, SEQ, MWIDTH), BF16),
                   jax.ShapeDtypeStruct((N_PROMPT, HEADS, DH, DH), F32),
                   jax.ShapeDtypeStruct((N_PROMPT, HEADS, DH, DH), F32),
                   jax.ShapeDtypeStruct((N_PROMPT, 8, LANES), F32)],
        scratch_shapes=[pltpu.VMEM((MSEQ, HEADS, DH, DH), F32),
                        pltpu.VMEM((MSEQ, HEADS, DH, DH), F32),
                        pltpu.VMEM((MSEQ, 8, LANES), F32)],
        compiler_params=_cparams("parallel", "arbitrary"),
        name="mlstm_prompt",
    )(q3, g3, *([grow] * MSEQ), mnorm_g)
    return hm.reshape(N_PROMPT * SEQ, MWIDTH), c_p, n_p, m_p


MSTEP_TOK = 8


def _mlstm_step_body(q_ref, qkt_ref, ig_ref, lf_ref, m0_ref, c0_ref, n0_ref, mg_ref, hm_ref, c_out, n_out, m_out, qc_sc):
    scale = DH ** -0.5
    ig = ig_ref[...]
    m_inter = lf_ref[...] + m0_ref[...]
    m_t = jnp.maximum(m_inter, ig)
    w_inter_all = jnp.exp(m_inter - m_t)
    w_all = jnp.exp(ig - m_t) * scale
    floor_all = jnp.exp(-m_t)
    m_out[...] = m_t
    for hd in range(HEADS):
        sl = slice(hd * DH, (hd + 1) * DH)
        q = q_ref[:, sl].astype(F32)
        k = q_ref[:, MWIDTH + hd * DH:MWIDTH + (hd + 1) * DH].astype(F32)
        v = q_ref[:, 2 * MWIDTH + hd * DH:2 * MWIDTH + (hd + 1) * DH].astype(F32)
        o = q_ref[:, 3 * MWIDTH + hd * DH:3 * MWIDTH + (hd + 1) * DH].astype(F32)
        n0 = n0_ref[:, sl]
        wi = w_inter_all[:, hd:hd + 1]
        wk = w_all[:, hd:hd + 1]
        s = jnp.sum(q * k, axis=-1, keepdims=True) * wk
        qn = jnp.sum(q * n0, axis=-1, keepdims=True)
        for j in range(MSTEP_TOK):
            c0 = c0_ref[j, hd]
            q_col = qkt_ref[hd, :, j:j + 1]
            k_col = qkt_ref[HEADS + hd, :, j:j + 1]
            qc_sc[j:j + 1, sl] = jnp.sum(q_col * c0, axis=0, keepdims=True)
            c_out[j, hd] = wi[j:j + 1, :] * c0 + (wk[j:j + 1, :] * k_col) * v[j:j + 1, :]
        num = wi * qc_sc[:, sl] + s * v
        den = wi * qn + s
        h = num / jnp.maximum(jnp.abs(den), floor_all[:, hd:hd + 1])
        hn = h * lax.rsqrt(jnp.mean(h * h, axis=-1, keepdims=True) + EPS)
        hm_ref[:, sl] = hn * mg_ref[:, sl] * _sigmoid(o)
        n_out[:, sl] = wi * n0 + wk * k


def _mlstm_step(qkvo, qkt, ig, lf, m0, c0, n0, mnorm_g):
    t = qkvo.shape[0]
    tk = MSTEP_TOK
    return pl.pallas_call(
        _mlstm_step_body,
        grid=(t // tk,),
        in_specs=[pl.BlockSpec((tk, 4 * MWIDTH), lambda i: (i, 0)),
                  pl.BlockSpec((None, 2 * HEADS, DH, tk), lambda i: (i, 0, 0, 0)),
                  pl.BlockSpec((tk, LANES), lambda i: (i, 0)),
                  pl.BlockSpec((tk, LANES), lambda i: (i, 0)),
                  pl.BlockSpec((tk, LANES), lambda i: (i, 0)),
                  pl.BlockSpec((tk, HEADS, DH, DH), lambda i: (i, 0, 0, 0)),
                  pl.BlockSpec((tk, MWIDTH), lambda i: (i, 0)),
                  pl.BlockSpec((1, MWIDTH), lambda i: (0, 0))],
        out_specs=[pl.BlockSpec((tk, MWIDTH), lambda i: (i, 0)),
                   pl.BlockSpec((tk, HEADS, DH, DH), lambda i: (i, 0, 0, 0)),
                   pl.BlockSpec((tk, MWIDTH), lambda i: (i, 0)),
                   pl.BlockSpec((tk, LANES), lambda i: (i, 0))],
        out_shape=[jax.ShapeDtypeStruct((t, MWIDTH), F32),
                   jax.ShapeDtypeStruct((t, HEADS, DH, DH), F32),
                   jax.ShapeDtypeStruct((t, MWIDTH), F32),
                   jax.ShapeDtypeStruct((t, LANES), F32)],
        scratch_shapes=[pltpu.VMEM((tk, MWIDTH), F32)],
        compiler_params=_cparams("parallel"),
        name="mlstm_step",
    )(qkvo, qkt, ig, lf, m0, c0, n0, mnorm_g)


def _post_body(x_ref, y5_ref, hm_ref, gg_ref, g1_ref, sc_ref, sh_ref, n2_ref, wglu_ref, wmo_ref, wo_ref, rw_ref, rb_ref,
               cnt0_ref, x1_ref, h2_ref, eid_ref, wgt_ref, rank_ref, cnt_ref, run_sc, *, n_real, n_steps):
    i = pl.program_id(0)

    @pl.when(i == 0)
    def _():
        run_sc[...] = cnt0_ref[...]

    if n_steps > n_real:
        @pl.when(i >= n_real)
        def _():
            h2_ref[...] = jnp.zeros_like(h2_ref)

        pl.when(i < n_real)(functools.partial(
            _post_tile, x_ref, y5_ref, hm_ref, gg_ref, g1_ref, sc_ref, sh_ref, n2_ref, wglu_ref, wmo_ref, wo_ref,
            rw_ref, rb_ref, x1_ref, h2_ref, eid_ref, wgt_ref, rank_ref, run_sc))
    else:
        _post_tile(x_ref, y5_ref, hm_ref, gg_ref, g1_ref, sc_ref, sh_ref, n2_ref, wglu_ref, wmo_ref, wo_ref,
                   rw_ref, rb_ref, x1_ref, h2_ref, eid_ref, wgt_ref, rank_ref, run_sc)
    cnt_ref[...] = run_sc[...]


def _post_tile(x_ref, y5_ref, hm_ref, gg_ref, g1_ref, sc_ref, sh_ref, n2_ref, wglu_ref, wmo_ref, wo_ref, rw_ref, rb_ref,
               x1_ref, h2_ref, eid_ref, wgt_ref, rank_ref, run_sc):
    g5 = _gelu_tanh(y5_ref[...]).astype(BF16)
    glu = _dot(g5, wglu_ref[...])
    br_s5 = glu[:, :D_MODEL] * _sigmoid(glu[:, D_MODEL:])
    br_m = _dot(hm_ref[...].astype(BF16), wmo_ref[...])
    gg = gg_ref[...].astype(F32)
    merged = _sigmoid(gg[:, :D_MODEL]) * br_s5 + _sigmoid(gg[:, D_MODEL:]) * br_m
    x1 = x_ref[...] + g1_ref[...] * _dot(merged.astype(BF16), wo_ref[...])
    x1_ref[...] = x1
    ms = jnp.mean(x1 * x1, axis=-1, keepdims=True)
    h2 = x1 * lax.rsqrt(ms + EPS) * n2_ref[...] * (1.0 + sc_ref[...]) + sh_ref[...]
    h2_ref[...] = h2
    logits = _dot_hp(h2, rw_ref[...]) + rb_ref[...]
    tm = logits.shape[0]
    lane = lax.broadcasted_iota(jnp.int32, logits.shape, 1)
    l = jnp.where(lane < N_EXPERTS, logits, -jnp.inf)
    vals, hots, idxs = [], [], []
    for _ in range(TOP_K):
        mx = jnp.max(l, axis=-1, keepdims=True)
        idx = jnp.min(jnp.where(l == mx, lane, LANES), axis=-1, keepdims=True)
        hot = lane == idx
        vals.append(mx)
        hots.append(hot)
        idxs.append(idx)
        l = jnp.where(hot, -jnp.inf, l)
    ex = [jnp.exp(vk - vals[0]) for vk in vals]
    tot = ex[0] + ex[1] + ex[2] + ex[3]
    chosen = jnp.where(hots[0] | hots[1] | hots[2] | hots[3], 1.0, 0.0)
    r_io = lax.broadcasted_iota(jnp.int32, (tm, tm), 0)
    c_io = lax.broadcasted_iota(jnp.int32, (tm, tm), 1)
    earlier = (c_io < r_io).astype(BF16)
    before = run_sc[...] + _dot(earlier, chosen.astype(BF16))
    eid = jnp.zeros(logits.shape, jnp.int32)
    wgt = jnp.zeros(logits.shape, F32)
    rank = jnp.zeros(logits.shape, F32)
    for k in range(TOP_K):
        eid = jnp.where(lane == k, idxs[k], eid)
        wgt = jnp.where(lane == k, ex[k] / tot, wgt)
        rank = jnp.where(lane == k, jnp.sum(jnp.where(hots[k], before, 0.0), axis=-1, keepdims=True), rank)
    eid_ref[...] = eid
    wgt_ref[...] = wgt
    rank_ref[...] = rank.astype(jnp.int32)
    run_sc[...] += jnp.sum(chosen, axis=0, keepdims=True)


def _post(x2d, y5, hm, gg, mod3, rows_per_mod, tm, n2, wglu, wmo, wo, rw, rb, cnt0, h2_rows):
    t = x2d.shape[0]
    mrows = mod3.shape[1]
    n_real = t // tm
    n_steps = -(-h2_rows // tm)
    last = n_real - 1

    def mod_spec(j):
        return pl.BlockSpec((None, mrows, D_MODEL), lambda i: ((jnp.minimum(i, last) * tm) // rows_per_mod, 0, j))

    const = lambda shape: pl.BlockSpec(shape, lambda i: (0,) * len(shape))
    tile = lambda w: pl.BlockSpec((tm, w), lambda i: (jnp.minimum(i, last), 0))
    return pl.pallas_call(
        functools.partial(_post_body, n_real=n_real, n_steps=n_steps),
        grid=(n_steps,),
        in_specs=[tile(D_MODEL), tile(S5_WIDTH), tile(MWIDTH), tile(2 * D_MODEL),
                  mod_spec(2), mod_spec(4), mod_spec(3), const((1, D_MODEL)),
                  const(wglu.shape), const(wmo.shape), const(wo.shape), const(rw.shape), const(rb.shape),
                  const((1, LANES))],
        out_specs=[tile(D_MODEL), pl.BlockSpec((tm, D_MODEL), lambda i: (i, 0)), tile(LANES), tile(LANES), tile(LANES),
                   const((1, LANES))],
        out_shape=[jax.ShapeDtypeStruct((t, D_MODEL), F32),
                   jax.ShapeDtypeStruct((h2_rows, D_MODEL), F32),
                   jax.ShapeDtypeStruct((t, LANES), jnp.int32),
                   jax.ShapeDtypeStruct((t, LANES), F32),
                   jax.ShapeDtypeStruct((t, LANES), jnp.int32),
                   jax.ShapeDtypeStruct((1, LANES), F32)],
        scratch_shapes=[pltpu.VMEM((1, LANES), F32)],
        compiler_params=_cparams("arbitrary"),
        name="post",
    )(x2d, y5, hm, gg, mod3, mod3, mod3, n2, wglu, wmo, wo, rw, rb, cnt0)


N_TOKENS = N_PROMPT * SEQ + N_SAMPLE
N_ASSIGN = N_TOKENS * TOP_K
MOE_TILE = 256
MOE_TILES = N_ASSIGN // MOE_TILE + N_EXPERTS
N_SLOT = MOE_TILES * MOE_TILE
CMB_TILE = 256


MOE_RING = 3


def _moe_body(te_ref, nt_ref, nxe_ref, gpar_ref, idx01_ref, idx2_ref, h2_hbm, wup_hbm, wdn_hbm, bias_ref,
              y_ref, xb0, xb1, xb2, sem, wbuf_up, wbuf_dn, wsem, wup_bf, wdn_bf):
    i = pl.program_id(0)
    nt = nt_ref[0]
    bufs = (xb0, xb1, xb2)
    last = MOE_TILES - 1

    def gather(idx_ref, row, slot):
        for r in range(MOE_TILE):
            pltpu.make_async_copy(h2_hbm.at[pl.ds(idx_ref[row, r], 1)], bufs[slot].at[pl.ds(r, 1)],
                                  sem.at[slot]).start()

    def wait(slot):
        pltpu.make_async_copy(h2_hbm.at[pl.ds(0, MOE_TILE)], bufs[slot], sem.at[slot]).wait()

    def weights(e, slot):
        return (pltpu.make_async_copy(wup_hbm.at[e], wbuf_up.at[slot], wsem.at[0, slot]),
                pltpu.make_async_copy(wdn_hbm.at[e], wbuf_dn.at[slot], wsem.at[1, slot]))

    wslot = gpar_ref[i]

    @pl.when(i == 0)
    def _():
        for c in weights(te_ref[0], wslot):
            c.start(priority=1)
        gather(idx01_ref, 0, 0)
        gather(idx01_ref, 1, 1)

    new_expert = jnp.logical_or(i == 0, te_ref[i] != te_ref[jnp.maximum(i - 1, 0)])
    live = i < nt

    @pl.when(jnp.logical_and(new_expert, live))
    def _():
        for c in weights(0, wslot):
            c.wait()
        wup_bf[...] = wbuf_up[wslot].astype(BF16)
        wdn_bf[...] = wbuf_dn[wslot].astype(BF16)
        nxe = nxe_ref[i]

        @pl.when(nxe >= 0)
        def _():
            for c in weights(nxe, 1 - wslot):
                c.start(priority=1)

    def step(slot):
        wait(slot)
        gather(idx2_ref, 0, (slot + 2) % MOE_RING)
        gu = _dot(bufs[slot][...].astype(BF16), wup_bf[...]) + bias_ref[:, :2 * D_FF]
        g = jnp.minimum(gu[:, :D_FF], SWIGLU_LIMIT)
        up = jnp.clip(gu[:, D_FF:], -SWIGLU_LIMIT, SWIGLU_LIMIT)
        act = (up + 1.0) * g * _sigmoid(SWIGLU_ALPHA * g)
        y_ref[...] = _dot(act.astype(BF16), wdn_bf[...]) + bias_ref[:, 2 * D_FF:]

    for slot in range(MOE_RING):
        pl.when(jnp.logical_and(live, i % MOE_RING == slot))(functools.partial(step, slot))

    @pl.when(jnp.logical_not(live))
    def _():
        y_ref[...] = jnp.zeros_like(y_ref)

    drain = jnp.logical_and(jnp.logical_not(live), i < nt + 2)
    for slot in range(MOE_RING):
        pl.when(jnp.logical_and(drain, i % MOE_RING == slot))(functools.partial(wait, slot))
    pl.when(jnp.logical_and(i == last, last - 1 < nt))(functools.partial(wait, (last + 1) % MOE_RING))
    pl.when(jnp.logical_and(i == last, last < nt))(functools.partial(wait, (last + 2) % MOE_RING))


def _moe(tile_expert, n_tiles, next_expert, group_parity, slot_token, h2_all, wup, wdn, bias):
    idx3 = slot_token.reshape(MOE_TILES, 1, MOE_TILE)
    idx01 = slot_token[:2 * MOE_TILE].reshape(1, 2, MOE_TILE)
    grid_spec = pltpu.PrefetchScalarGridSpec(
        num_scalar_prefetch=4,
        grid=(MOE_TILES,),
        in_specs=[pl.BlockSpec((None, 2, MOE_TILE), lambda i, *_: (0, 0, 0), memory_space=pltpu.SMEM),
                  pl.BlockSpec((None, 1, MOE_TILE), lambda i, *_: (jnp.minimum(i + 2, MOE_TILES - 1), 0, 0),
                               memory_space=pltpu.SMEM),
                  pl.BlockSpec(memory_space=pl.ANY),
                  pl.BlockSpec(memory_space=pl.ANY),
                  pl.BlockSpec(memory_space=pl.ANY),
                  pl.BlockSpec((None, 1, 2 * D_FF + D_MODEL), lambda i, te, *_: (te[i], 0, 0))],
        out_specs=pl.BlockSpec((MOE_TILE, D_MODEL), lambda i, *_: (i, 0)),
        scratch_shapes=[pltpu.VMEM((MOE_TILE, D_MODEL), F32), pltpu.VMEM((MOE_TILE, D_MODEL), F32),
                        pltpu.VMEM((MOE_TILE, D_MODEL), F32),
                        pltpu.SemaphoreType.DMA((MOE_RING,)),
                        pltpu.VMEM((2, D_MODEL, 2 * D_FF), F32), pltpu.VMEM((2, D_FF, D_MODEL), F32),
                        pltpu.SemaphoreType.DMA((2, 2)),
                        pltpu.VMEM((D_MODEL, 2 * D_FF), BF16), pltpu.VMEM((D_FF, D_MODEL), BF16)])
    return pl.pallas_call(
        _moe_body,
        grid_spec=grid_spec,
        out_shape=jax.ShapeDtypeStruct((N_SLOT, D_MODEL), F32),
        compiler_params=_cparams("arbitrary"),
        name="moe",
    )(tile_expert, n_tiles, next_expert, group_parity, idx01, idx3, h2_all, wup, wdn, bias)


def _combine_body(cur_ref, nxt_ref, wgt_ref, x1_ref, g2_ref, fg_ref, y_hbm, o_ref, yb0, yb1, sem, *, n_steps, tm):
    i = pl.program_id(0)

    def gather(idx_ref, buf, s):
        for r in range(tm):
            for k in range(TOP_K):
                pltpu.make_async_copy(y_hbm.at[pl.ds(idx_ref[0, r * TOP_K + k], 1)],
                                      buf.at[pl.ds(k * tm + r, 1)], sem.at[s]).start(priority=k % 2)

    def wait(buf, s):
        pltpu.make_async_copy(y_hbm.at[pl.ds(0, TOP_K * tm)], buf, sem.at[s]).wait()

    @pl.when(i == 0)
    def _():
        gather(cur_ref, yb0, 0)

    def step(cur, nxt, s_cur, s_nxt):
        wait(cur, s_cur)
        gather(nxt_ref, nxt, s_nxt)
        wgt = wgt_ref[...]
        acc = wgt[:, 0:1] * cur[0:tm, :]
        for k in range(1, TOP_K):
            acc = acc + wgt[:, k:k + 1] * cur[k * tm:(k + 1) * tm, :]
        xo = x1_ref[...] + g2_ref[...] * acc
        ms = jnp.mean(xo * xo, axis=-1, keepdims=True)
        o_ref[...] = xo * lax.rsqrt(ms + EPS) * fg_ref[...]

    pl.when(i % 2 == 0)(functools.partial(step, yb0, yb1, 0, 1))
    pl.when(i % 2 == 1)(functools.partial(step, yb1, yb0, 1, 0))
    last_slot = n_steps % 2
    pl.when(i == n_steps - 1)(functools.partial(wait, yb1 if last_slot else yb0, last_slot))


def _combine(pos, wgt, x1, mod3, rows_per_mod, fg, y_slots, tm):
    t = x1.shape[0]
    n = t // tm
    mrows = mod3.shape[1]
    assert mrows in (1, tm) and rows_per_mod % tm == 0
    pos3 = pos.reshape(n, 1, tm * TOP_K)
    return pl.pallas_call(
        functools.partial(_combine_body, n_steps=n, tm=tm),
        grid=(n,),
        in_specs=[pl.BlockSpec((None, 1, tm * TOP_K), lambda i: (i, 0, 0), memory_space=pltpu.SMEM),
                  pl.BlockSpec((None, 1, tm * TOP_K), lambda i: (jnp.minimum(i + 1, n - 1), 0, 0),
                               memory_space=pltpu.SMEM),
                  pl.BlockSpec((tm, LANES), lambda i: (i, 0)),
                  pl.BlockSpec((tm, D_MODEL), lambda i: (i, 0)),
                  pl.BlockSpec((None, mrows, D_MODEL), lambda i: ((i * tm) // rows_per_mod, 0, 5)),
                  pl.BlockSpec((1, D_MODEL), lambda i: (0, 0)),
                  pl.BlockSpec(memory_space=pl.ANY)],
        out_specs=pl.BlockSpec((tm, D_MODEL), lambda i: (i, 0)),
        out_shape=jax.ShapeDtypeStruct((t, D_MODEL), F32),
        scratch_shapes=[pltpu.VMEM((TOP_K * tm, D_MODEL), F32), pltpu.VMEM((TOP_K * tm, D_MODEL), F32),
                        pltpu.SemaphoreType.DMA((2,))],
        compiler_params=_cparams("arbitrary"),
        name="combine",
    )(pos3, pos3, wgt, x1, mod3, fg, y_slots)


def _routing(eid_p, rank_p, eid_s, rank_s, counts):
    cnt = counts[0, :N_EXPERTS].astype(jnp.int32)
    ntile = (cnt + MOE_TILE - 1) // MOE_TILE
    tile_end = jnp.cumsum(ntile)
    poff = (tile_end - ntile) * MOE_TILE
    total = tile_end[-1]
    j = jnp.arange(MOE_TILES, dtype=jnp.int32)
    te = jnp.sum((j[:, None] >= tile_end[None, :]).astype(jnp.int32), axis=1)
    te_last = jnp.sum(((total - 1) >= tile_end).astype(jnp.int32))
    tile_expert = jnp.where(j < total, te, te_last).astype(jnp.int32)
    ex = jnp.arange(N_EXPERTS, dtype=jnp.int32)
    used = ntile > 0
    later = jnp.where(used[None, :] & (ex[None, :] > ex[:, None]), ex[None, :], N_EXPERTS)
    nxt_e = jnp.min(later, axis=1)
    nxt_e = jnp.where(nxt_e < N_EXPERTS, nxt_e, -1).astype(jnp.int32)
    par_e = ((jnp.cumsum(used.astype(jnp.int32)) - 1) % 2).astype(jnp.int32)
    pick = tile_expert[:, None] == ex[None, :]
    next_expert = jnp.sum(jnp.where(pick, nxt_e[None, :], 0), axis=1).astype(jnp.int32)
    group_parity = jnp.sum(jnp.where(pick, par_e[None, :], 0), axis=1).astype(jnp.int32)

    def pos_of(eid, rank):
        e = eid[:, :TOP_K]
        off = jnp.sum(jnp.where(e[:, :, None] == ex[None, None, :], poff[None, None, :], 0), axis=-1)
        return off + rank[:, :TOP_K]

    pos_p = pos_of(eid_p, rank_p)
    pos_s = pos_of(eid_s, rank_s)
    tok = jnp.concatenate([jnp.repeat(jnp.arange(N_PROMPT * SEQ, dtype=jnp.int32), TOP_K),
                           jnp.repeat(N_PROMPT * SEQ + jnp.arange(N_SAMPLE, dtype=jnp.int32), TOP_K)])
    pos_all = jnp.concatenate([pos_p.reshape(-1), pos_s.reshape(-1)])
    padc = jnp.concatenate([ntile * MOE_TILE - cnt, (N_SLOT - total * MOE_TILE).reshape(1)])
    pad_end = jnp.cumsum(padc)
    pad_base = jnp.concatenate([poff + cnt, (total * MOE_TILE).reshape(1)])
    jp = jnp.arange(N_SLOT - N_ASSIGN, dtype=jnp.int32)
    pe = jnp.sum((jp[:, None] >= pad_end[None, :]).astype(jnp.int32), axis=1)
    hit = pe[:, None] == jnp.arange(N_EXPERTS + 1, dtype=jnp.int32)[None, :]
    pad_slot = jp + jnp.sum(jnp.where(hit, (pad_base - (pad_end - padc))[None, :], 0), axis=1)
    keys = jnp.concatenate([pos_all, pad_slot.astype(jnp.int32)])
    vals = jnp.concatenate([tok, jnp.zeros((N_SLOT - N_ASSIGN,), jnp.int32)])
    slot_token = lax.sort((keys, vals), num_keys=1)[1]
    return tile_expert, total.reshape(1).astype(jnp.int32), next_expert, group_parity, slot_token, pos_p, pos_s


def _unpack_s5_state(x):
    n = x.shape[1]
    z = x.reshape(N_SUPER, n, 2, S5_SUPER, S5_STATE).transpose(2, 1, 0, 3, 4).reshape(2, n, S5_GROUPS, S5_STATE)
    return z[0], z[1]


def _pack_s5_state(re, im):
    n = re.shape[0]
    z = jnp.stack([re, im], axis=0).reshape(2, n, N_SUPER, S5_SUPER, S5_STATE)
    return z.transpose(2, 1, 0, 3, 4).reshape(N_SUPER, n, SUPER_STATE)


def kernel(x_prompt, x_sample, c_prompt, c_sample, state_s5_re, state_s5_im, state_mlstm_C, state_mlstm_n, state_mlstm_m, norm1_g, norm2_g, final_norm_g, w_ada, b_ada, w_in, s5_lambda_re, s5_lambda_im, s5_log_dt, s5_B_re, s5_B_im, s5_C_re, s5_C_im, s5_D, s5_w_glu, mlstm_b_i, mlstm_b_f, mlstm_norm_g, mlstm_w_out, w_out, router_w, router_b, expert_w_up, expert_b_up, expert_w_down, expert_b_down):
    assert w_in.shape[0] == 1, "single layer"
    tp = N_PROMPT * SEQ
    xp = x_prompt.reshape(tp, D_MODEL).astype(F32)
    xs = x_sample.reshape(N_SAMPLE, D_MODEL).astype(F32)

    w = w_in[0]
    c0 = S5_WIDTH
    c1 = c0 + 4 * MWIDTH
    c2 = c1 + 2 * HEADS
    wu = w[:, :c0].astype(BF16)
    wq = w[:, c0:c1].astype(BF16)
    wif = jnp.pad(w[:, c1:c2].astype(F32), ((0, 0), (0, LANES - 2 * HEADS)))
    wg = w[:, c2:].astype(BF16)
    bvec = jnp.concatenate([mlstm_b_i[0], mlstm_b_f[0]]).astype(F32)
    bif = jnp.pad(bvec, (0, LANES - 2 * HEADS)).reshape(1, LANES)
    wparts = (wu, wq, wg, wif, bif)
    g1 = norm1_g[0].reshape(1, D_MODEL).astype(F32)
    n2 = norm2_g[0].reshape(1, D_MODEL).astype(F32)
    fg = final_norm_g.reshape(1, D_MODEL).astype(F32)
    mg = mlstm_norm_g[0].reshape(1, MWIDTH).astype(F32)
    wglu = s5_w_glu[0].astype(BF16)
    wmo = mlstm_w_out[0].astype(BF16)
    wo = w_out[0].astype(BF16)
    rw = jnp.pad(router_w[0].astype(F32), ((0, 0), (0, LANES - N_EXPERTS)))
    rb = jnp.pad(router_b[0].astype(F32), (0, LANES - N_EXPERTS)).reshape(1, LANES)
    wup = expert_w_up[0].astype(F32)
    wdn = expert_w_down[0].astype(F32)
    ebias = jnp.concatenate([expert_b_up[0].astype(F32), expert_b_down[0].astype(F32)], axis=1)
    ebias = ebias.reshape(N_EXPERTS, 1, 2 * D_FF + D_MODEL)
    ddiag, fcomp, ecomp, a16, a1 = _s5_tables(
        s5_lambda_re[0], s5_lambda_im[0], s5_log_dt[0], s5_B_re[0], s5_B_im[0], s5_C_re[0], s5_C_im[0], s5_D[0])

    c_all = jnp.concatenate([c_prompt, c_sample], axis=0).astype(F32)
    mod = _adaln(c_all, w_ada[0].astype(F32), b_ada[0].astype(F32))
    mod_p = mod[:N_PROMPT].reshape(N_PROMPT, 1, 6 * D_MODEL)
    mod_s = mod[N_PROMPT:].reshape(1, N_SAMPLE, 6 * D_MODEL)

    u, qkvo, gg, gcol, grow = _inproj(xp, mod_p, SEQ, 1024, g1, wparts)
    y5, xend, k0, e0, f1 = _s5_prompt(u, ddiag, fcomp, ecomp, a16, a1)
    hm, c_p, n_p, m_p = _mlstm_prompt(qkvo, gcol, grow, mg)
    cnt0 = jnp.zeros((1, LANES), F32)
    x1, h2_all, eid_p, wgt_p, rank_p, cnt_p = _post(xp, y5, hm, gg, mod_p, SEQ, 512, n2, wglu, wmo, wo, rw, rb,
                                                    cnt0, N_TOKENS)
    p_re, p_im = _unpack_s5_state(xend.reshape(N_SUPER, N_PROMPT, SUPER_STATE))
    n_p = n_p[..., 0]
    m_p = m_p[:, :HEADS, 0]

    us, qs, ggs, gcs, _ = _inproj(xs, mod_s, N_SAMPLE, N_SAMPLE, g1, wparts)
    x0 = _pack_s5_state(state_s5_re[0].astype(F32), state_s5_im[0].astype(F32))
    y5s, xns = _s5_step(us, x0, k0, e0, f1, a1)
    s_re, s_im = _unpack_s5_state(xns)
    qkt = qs[:, :2 * MWIDTH].astype(F32).reshape(N_SAMPLE // MSTEP_TOK, MSTEP_TOK, 2 * HEADS, DH).transpose(0, 2, 3, 1)
    pad_heads = lambda a: jnp.pad(a, ((0, 0), (0, LANES - HEADS)))
    m0 = pad_heads(state_mlstm_m[0].astype(F32))
    hms, c_s, n_s, m_s = _mlstm_step(qs, qkt, pad_heads(gcs[:, :HEADS]), pad_heads(gcs[:, HEADS:2 * HEADS]), m0,
                                     state_mlstm_C[0].astype(F32),
                                     state_mlstm_n[0].astype(F32).reshape(N_SAMPLE, MWIDTH), mg)
    x1s, h2s, eid_s, wgt_s, rank_s, cnt_all = _post(xs, y5s, hms, ggs, mod_s, N_SAMPLE, N_SAMPLE, n2, wglu, wmo, wo, rw, rb,
                                                    cnt_p, N_SAMPLE)

    h2_all = lax.dynamic_update_slice(h2_all, h2s, (tp, 0))
    tile_expert, n_tiles, next_expert, group_parity, slot_token, pos_p, pos_s = _routing(
        eid_p, rank_p, eid_s, rank_s, cnt_all)
    y_slots = _moe(tile_expert, n_tiles, next_expert, group_parity, slot_token, h2_all, wup, wdn, ebias)
    y_p = _combine(pos_p, wgt_p, x1, mod_p, SEQ, fg, y_slots, CMB_TILE)
    y_s = _combine(pos_s, wgt_s, x1s, mod_s, N_SAMPLE, fg, y_slots, N_SAMPLE)

    return (y_p.reshape(N_PROMPT, SEQ, D_MODEL).astype(x_prompt.dtype),
            y_s.reshape(N_SAMPLE, 1, D_MODEL).astype(x_sample.dtype),
            p_re[None], p_im[None], c_p[None], n_p[None], m_p[None],
            s_re[None], s_im[None], c_s[None],
            n_s.reshape(1, N_SAMPLE, HEADS, DH), m_s[:, :HEADS][None])
```

```python
import functools
import math

import jax
import jax.numpy as jnp
from jax import lax
from jax.experimental import pallas as pl
from jax.experimental.pallas import tpu as pltpu

F32 = jnp.float32
BF16 = jnp.bfloat16

D_MODEL = 1024
SEQ = 2048
N_PROMPT = 8
N_SAMPLE = 128
S5_WIDTH = 512
S5_GROUP = 16
S5_GROUPS = 32
S5_STATE = 64
HEADS = 4
DH = 128
MWIDTH = HEADS * DH
N_EXPERTS = 32
TOP_K = 4
D_FF = 1024
SWIGLU_LIMIT = 7.0
SWIGLU_ALPHA = 1.702
EPS = 1e-6

LANES = 128
S5_CHUNK = 16
S5_SUPER = LANES // S5_GROUP
N_SUPER = S5_GROUPS // S5_SUPER
SUPER_STATE = 2 * S5_SUPER * S5_STATE
MCHUNK = 128
VMEM_LIMIT = 56 * 1024 * 1024


def _cparams(*sem):
    return pltpu.CompilerParams(dimension_semantics=sem, vmem_limit_bytes=VMEM_LIMIT)


def _dot(a, b):
    return jnp.dot(a, b, preferred_element_type=F32)


def _dot_dims(a, b, dims):
    return lax.dot_general(a, b, (dims, ((), ())), preferred_element_type=F32)


def _split_bf16(a):
    hi = a.astype(BF16)
    lo = (a - hi.astype(F32)).astype(BF16)
    return hi, lo


def _dot_hp(a, b, dims=((1,), (0,))):
    ah, al = _split_bf16(a)
    bh, bl = _split_bf16(b)
    return _dot_dims(ah, bh, dims) + (_dot_dims(al, bh, dims) + _dot_dims(ah, bl, dims))


def _dot_exact_rhs(a, b_exact, dims=((1,), (0,))):
    ah, al = _split_bf16(a)
    return _dot_dims(ah, b_exact, dims) + _dot_dims(al, b_exact, dims)


def _log_sigmoid(x):
    return -(jnp.maximum(-x, 0.0) + jnp.log1p(jnp.exp(-jnp.abs(x))))


def _sigmoid(x):
    return 0.5 * (1.0 + jnp.tanh(0.5 * x))


def _gelu_tanh(x):
    c = math.sqrt(2.0 / math.pi)
    return 0.5 * x * (1.0 + jnp.tanh(c * (x + 0.044715 * (x * x * x))))


def _adaln_body(c_ref, w_ref, b_ref, o_ref):
    c = c_ref[...]
    s = c * _sigmoid(c)
    o_ref[...] = _dot_hp(s, w_ref[...]) + b_ref[...]


def _adaln(c_all, w_ada, b_ada):
    n = c_all.shape[0]
    tn = 1024
    return pl.pallas_call(
        _adaln_body,
        grid=(6 * D_MODEL // tn,),
        in_specs=[pl.BlockSpec((n, D_MODEL), lambda j: (0, 0)),
                  pl.BlockSpec((D_MODEL, tn), lambda j: (0, j)),
                  pl.BlockSpec((1, tn), lambda j: (0, j))],
        out_specs=pl.BlockSpec((n, tn), lambda j: (0, j)),
        out_shape=jax.ShapeDtypeStruct((n, 6 * D_MODEL), F32),
        compiler_params=_cparams("parallel"),
        name="adaln",
    )(c_all, w_ada, b_ada.reshape(1, -1))


def _inproj_body(x_ref, g1_ref, sc_ref, sh_ref, wu_ref, wq_ref, wg_ref, wif_ref, bif_ref,
                 u_ref, q_ref, gg_ref, if_ref, ift_ref):
    x = x_ref[...]
    ms = jnp.mean(x * x, axis=-1, keepdims=True)
    h = x * lax.rsqrt(ms + EPS) * g1_ref[...] * (1.0 + sc_ref[...]) + sh_ref[...]
    hb = h.astype(BF16)
    u_ref[...] = _dot(hb, wu_ref[...])
    q_ref[...] = _dot(hb, wq_ref[...]).astype(BF16)
    gg_ref[...] = _dot(hb, wg_ref[...]).astype(BF16)
    gc = _dot_hp(h, wif_ref[...]) + bif_ref[...]
    lane = lax.broadcasted_iota(jnp.int32, gc.shape, 1)
    gates = jnp.where(lane < HEADS, gc, _log_sigmoid(gc))
    if_ref[...] = gates
    ift_ref[...] = gates.T[:2 * HEADS, :]


def _inproj(x2d, mod3, rows_per_mod, tm, g1, wparts):
    t = x2d.shape[0]
    wu, wq, wg, wif, bif = wparts
    mrows = mod3.shape[1]

    def mod_spec(j):
        return pl.BlockSpec((None, mrows, D_MODEL), lambda i: ((i * tm) // rows_per_mod, 0, j))

    const = lambda shape: pl.BlockSpec(shape, lambda i: (0,) * len(shape), pipeline_mode=pl.Buffered(1))
    return pl.pallas_call(
        _inproj_body,
        grid=(t // tm,),
        in_specs=[pl.BlockSpec((tm, D_MODEL), lambda i: (i, 0)),
                  const((1, D_MODEL)), mod_spec(1), mod_spec(0),
                  const(wu.shape), const(wq.shape), const(wg.shape), const(wif.shape), const(bif.shape)],
        out_specs=[pl.BlockSpec((tm, S5_WIDTH), lambda i: (i, 0)),
                   pl.BlockSpec((tm, 4 * MWIDTH), lambda i: (i, 0)),
                   pl.BlockSpec((tm, 2 * D_MODEL), lambda i: (i, 0)),
                   pl.BlockSpec((tm, LANES), lambda i: (i, 0)),
                   pl.BlockSpec((8, tm), lambda i: (0, i))],
        out_shape=[jax.ShapeDtypeStruct((t, S5_WIDTH), F32),
                   jax.ShapeDtypeStruct((t, 4 * MWIDTH), BF16),
                   jax.ShapeDtypeStruct((t, 2 * D_MODEL), BF16),
                   jax.ShapeDtypeStruct((t, LANES), F32),
                   jax.ShapeDtypeStruct((8, t), F32)],
        compiler_params=_cparams("parallel"),
        name="inproj",
    )(x2d, g1, mod3, mod3, wu, wq, wg, wif, bif)


def _s5_tables(lam_re, lam_im, log_dt, b_re, b_im, c_re, c_im, d_s5):
    hi = lax.Precision.HIGHEST
    dt = jnp.exp(log_dt.astype(F32))[:, None]
    lr, li = lam_re.astype(F32), lam_im.astype(F32)
    dpow = jnp.arange(S5_CHUNK + 1, dtype=F32)[:, None, None]
    mag = jnp.exp(dpow * (lr * dt))
    pw_re, pw_im = mag * jnp.cos(dpow * (li * dt)), mag * jnp.sin(dpow * (li * dt))
    ab_re, ab_im = pw_re[1], pw_im[1]
    den = lr * lr + li * li
    nr, ni = ab_re - 1.0, ab_im
    coef_re = (nr * lr + ni * li) / den
    coef_im = (ni * lr - nr * li) / den
    br, bi = b_re.astype(F32), b_im.astype(F32)
    bb_re = coef_re[..., None] * br - coef_im[..., None] * bi
    bb_im = coef_re[..., None] * bi + coef_im[..., None] * br
    cr, ci = c_re.astype(F32), c_im.astype(F32)
    cl_re = cr[None] * pw_re[:, :, None, :] - ci[None] * pw_im[:, :, None, :]
    cl_im = cr[None] * pw_im[:, :, None, :] + ci[None] * pw_re[:, :, None, :]
    ddiag = (d_s5.astype(F32).reshape(N_SUPER, LANES, 1) * jnp.eye(LANES, dtype=F32)[None])
    rp_re, rp_im = pw_re[S5_CHUNK - 1::-1], pw_im[S5_CHUNK - 1::-1]
    f_re = rp_re[..., None] * bb_re[None] - rp_im[..., None] * bb_im[None]
    f_im = rp_re[..., None] * bb_im[None] + rp_im[..., None] * bb_re[None]
    fcat = jnp.stack([f_re, f_im], axis=2).transpose(0, 1, 4, 2, 3)
    fcomp = fcat.reshape(S5_CHUNK, N_SUPER, LANES, 2 * S5_STATE).transpose(1, 0, 2, 3)
    fcomp = fcomp.reshape(N_SUPER, S5_CHUNK * LANES, 2 * S5_STATE)
    ecat = jnp.stack([cl_re[:S5_CHUNK], -cl_im[:S5_CHUNK]], axis=0)
    er = ecat.transpose(2, 0, 4, 1, 3).reshape(N_SUPER, S5_SUPER, 2, S5_STATE, S5_CHUNK * S5_GROUP)
    ecomp = er.transpose(0, 2, 1, 3, 4).reshape(N_SUPER, SUPER_STATE, S5_CHUNK * S5_GROUP)

    def lay(re, im):
        z = jnp.stack([re, im], axis=0).reshape(2, N_SUPER, S5_SUPER * S5_STATE)
        return z.transpose(1, 0, 2).reshape(N_SUPER, 1, SUPER_STATE)

    a16 = lay(pw_re[S5_CHUNK], pw_im[S5_CHUNK])
    a1 = lay(ab_re, ab_im)
    return ddiag, fcomp, ecomp, a16, a1


def _expand_groups(comp, row_div, col_blk, n_cols, col_div):
    r, c = comp.shape
    ci = lax.broadcasted_iota(jnp.int32, (c, n_cols), 0)
    co = lax.broadcasted_iota(jnp.int32, (c, n_cols), 1)
    sel = jnp.logical_and(ci // col_blk == co // (S5_SUPER * col_blk), ci % col_blk == co % col_blk)
    rep = _dot(comp.astype(BF16), sel.astype(BF16))
    ro = lax.broadcasted_iota(jnp.int32, (r, n_cols), 0)
    cc = lax.broadcasted_iota(jnp.int32, (r, n_cols), 1)
    same = (ro // row_div) % S5_SUPER == (cc // col_div) % S5_SUPER
    return jnp.where(same, rep, 0.0).astype(BF16)


S5_NB = 4


def _s5_prompt_body(u_ref, dd_ref, fc_ref, ec_ref, a_ref, a1_ref, y_ref, xe_ref, k0_ref, e0_ref, f1_ref,
                    ucat, v_sc, xp_sc, kp_ref, f_ref, e_ref):
    n = SEQ // S5_CHUNK
    half = SUPER_STATE // 2

    @pl.when(pl.program_id(1) == 0)
    def _():
        f_ref[...] = _expand_groups(fc_ref[...], S5_GROUP, S5_STATE, SUPER_STATE, S5_STATE)
        e_ref[...] = _expand_groups(ec_ref[...], S5_STATE, S5_GROUP, S5_CHUNK * LANES, S5_GROUP)
        f1 = f_ref[(S5_CHUNK - 1) * LANES:, :]
        kall = _dot(f1, e_ref[...])
        for r in range(S5_CHUNK):
            for hf in range(2):
                d = S5_CHUNK - 2 - r + hf
                blk = jnp.zeros((LANES, LANES), F32) if d < 0 else kall[:, d * LANES:(d + 1) * LANES]
                if d == 0:
                    blk = blk + dd_ref[...]
                kp_ref[r * LANES:(r + 1) * LANES, hf * LANES:(hf + 1) * LANES] = blk.astype(BF16)
        k0_ref[...] = kp_ref[(S5_CHUNK - 2) * LANES:(S5_CHUNK - 1) * LANES, :LANES]
        e0_ref[...] = e_ref[:, LANES:2 * LANES]
        f1_ref[...] = f1

    for bl in range(S5_NB):
        for j in range(S5_CHUNK):
            ucat[bl * n:(bl + 1) * n, j * LANES:(j + 1) * LANES] = (
                u_ref[pl.ds(bl * SEQ + j, n, stride=S5_CHUNK), :].astype(BF16))
    v_sc[...] = _dot(ucat[...], f_ref[...])
    a_r, a_i = a_ref[:, :half], a_ref[:, half:]
    xr = [jnp.zeros((1, half), F32) for _ in range(S5_NB)]
    xi = [jnp.zeros((1, half), F32) for _ in range(S5_NB)]
    for c in range(n):
        for bl in range(S5_NB):
            row = bl * n + c
            xp_sc[row:row + 1, :half] = xr[bl]
            xp_sc[row:row + 1, half:] = xi[bl]
            vr, vi = v_sc[row:row + 1, :half], v_sc[row:row + 1, half:]
            xr[bl], xi[bl] = a_r * xr[bl] - a_i * xi[bl] + vr, a_r * xi[bl] + a_i * xr[bl] + vi
    for bl in range(S5_NB):
        xe_ref[bl:bl + 1, :half] = xr[bl]
        xe_ref[bl:bl + 1, half:] = xi[bl]
    l_r, l_i = a1_ref[:, :half], a1_ref[:, half:]
    xpr, xpi = xp_sc[:, :half], xp_sc[:, half:]
    z = jnp.concatenate([l_r * xpr - l_i * xpi, l_r * xpi + l_i * xpr], axis=1).astype(BF16)
    ye = _dot(z, e_ref[...])
    for m in range(S5_CHUNK // 2):
        rows = (2 * m + 2) * LANES
        yp = _dot(ucat[:, :rows], kp_ref[(S5_CHUNK - 2 - 2 * m) * LANES:, :]) + ye[:, 2 * m * LANES:(2 * m + 2) * LANES]
        for bl in range(S5_NB):
            y_ref[pl.ds(bl * SEQ + 2 * m, n, stride=S5_CHUNK), :] = yp[bl * n:(bl + 1) * n, :LANES]
            y_ref[pl.ds(bl * SEQ + 2 * m + 1, n, stride=S5_CHUNK), :] = yp[bl * n:(bl + 1) * n, LANES:]


def _s5_prompt(u, ddiag, fcomp, ecomp, a16, a1):
    n = S5_NB * SEQ // S5_CHUNK
    per_super = lambda shape: pl.BlockSpec((None,) + shape, lambda s, b: (s, 0, 0))
    return pl.pallas_call(
        _s5_prompt_body,
        grid=(N_SUPER, N_PROMPT // S5_NB),
        in_specs=[pl.BlockSpec((S5_NB * SEQ, LANES), lambda s, b: (b, s)),
                  per_super(ddiag.shape[1:]), per_super(fcomp.shape[1:]), per_super(ecomp.shape[1:]),
                  per_super((1, SUPER_STATE)), per_super((1, SUPER_STATE))],
        out_specs=[pl.BlockSpec((S5_NB * SEQ, LANES), lambda s, b: (b, s)),
                   pl.BlockSpec((None, None, S5_NB, SUPER_STATE), lambda s, b: (s, b, 0, 0)),
                   per_super((LANES, LANES)), per_super((SUPER_STATE, LANES)), per_super((LANES, SUPER_STATE))],
        out_shape=[jax.ShapeDtypeStruct(u.shape, F32),
                   jax.ShapeDtypeStruct((N_SUPER, N_PROMPT // S5_NB, S5_NB, SUPER_STATE), F32),
                   jax.ShapeDtypeStruct((N_SUPER, LANES, LANES), BF16),
                   jax.ShapeDtypeStruct((N_SUPER, SUPER_STATE, LANES), BF16),
                   jax.ShapeDtypeStruct((N_SUPER, LANES, SUPER_STATE), BF16)],
        scratch_shapes=[pltpu.VMEM((n, S5_CHUNK * LANES), BF16),
                        pltpu.VMEM((n, SUPER_STATE), F32),
                        pltpu.VMEM((n, SUPER_STATE), F32),
                        pltpu.VMEM((S5_CHUNK * LANES, 2 * LANES), BF16),
                        pltpu.VMEM((S5_CHUNK * LANES, SUPER_STATE), BF16),
                        pltpu.VMEM((SUPER_STATE, S5_CHUNK * LANES), BF16)],
        compiler_params=_cparams("parallel", "arbitrary"),
        name="s5_prompt",
    )(u, ddiag, fcomp, ecomp, a16, a1)


def _s5_step_body(u_ref, x0_ref, k0_ref, e0_ref, f1_ref, a_ref, y_ref, xn_ref):
    half = SUPER_STATE // 2
    ub = u_ref[...].astype(BF16)
    x0 = x0_ref[...]
    bu = _dot(ub, f1_ref[...])
    a_r, a_i = a_ref[:, :half], a_ref[:, half:]
    x0r, x0i = x0[:, :half], x0[:, half:]
    xn_ref[:, :half] = a_r * x0r - a_i * x0i + bu[:, :half]
    xn_ref[:, half:] = a_r * x0i + a_i * x0r + bu[:, half:]
    y_ref[...] = _dot(x0.astype(BF16), e0_ref[...]) + _dot(ub, k0_ref[...])


def _s5_step(u, x0, k0, e0, f1, a1):
    t = u.shape[0]
    per = lambda a: pl.BlockSpec((None,) + a.shape[1:], lambda s: (s,) + (0,) * (a.ndim - 1))
    return pl.pallas_call(
        _s5_step_body,
        grid=(N_SUPER,),
        in_specs=[pl.BlockSpec((t, LANES), lambda s: (0, s)), per(x0), per(k0), per(e0), per(f1), per(a1)],
        out_specs=[pl.BlockSpec((t, LANES), lambda s: (0, s)), per(x0)],
        out_shape=[jax.ShapeDtypeStruct(u.shape, F32), jax.ShapeDtypeStruct(x0.shape, F32)],
        compiler_params=_cparams("parallel"),
        name="s5_step",
    )(u, x0, k0, e0, f1, a1)


MSEQ = 2


def _mlstm_prompt_body(q_all, gc_all, *rest):
    gr_refs, (mg_ref, hm_all, c_out, n_out, m_out, c_all, n_all, m_all) = rest[:MSEQ], rest[MSEQ:]
    ci = pl.program_id(1)

    @pl.when(ci == 0)
    def _():
        c_all[...] = jnp.zeros_like(c_all)
        n_all[...] = jnp.zeros_like(n_all)
        m_all[...] = jnp.zeros_like(m_all)

    for sq in range(MSEQ):
        _mlstm_chunk(q_all.at[sq], gc_all.at[sq], gr_refs[sq], mg_ref, hm_all.at[sq],
                     c_all.at[sq], n_all.at[sq], m_all.at[sq])

    @pl.when(ci == pl.num_programs(1) - 1)
    def _():
        c_out[...] = c_all[...]
        n_out[...] = n_all[...]
        m_out[...] = m_all[...]


def _mlstm_chunk(q_ref, gc_ref, gr_ref, mg_ref, hm_ref, c_sc, n_sc, m_sc):
    lc = MCHUNK
    scale = DH ** -0.5
    row = lax.broadcasted_iota(jnp.int32, (lc, lc), 0)
    col = lax.broadcasted_iota(jnp.int32, (lc, lc), 1)
    causal = col <= row
    tri = causal.astype(BF16)
    ones = jnp.ones((lc, DH), BF16)
    gc = gc_ref[...]
    gr = gr_ref[...]
    gc_hi, gc_lo = _split_bf16(gc)
    bcol_all = _dot(tri, gc_hi) + _dot(tri, gc_lo)
    brow_all = _dot_exact_rhs(gr, tri, ((1,), (1,)))
    for hd in range(HEADS):
        q = q_ref[:, hd * DH:(hd + 1) * DH]
        k = q_ref[:, MWIDTH + hd * DH:MWIDTH + (hd + 1) * DH]
        v = q_ref[:, 2 * MWIDTH + hd * DH:2 * MWIDTH + (hd + 1) * DH]
        o = q_ref[:, 3 * MWIDTH + hd * DH:3 * MWIDTH + (hd + 1) * DH]
        i_col = gc[:, hd:hd + 1]
        b_col = bcol_all[:, HEADS + hd:HEADS + hd + 1]
        i_row = gr[hd:hd + 1, :]
        b_row = brow_all[HEADS + hd:HEADS + hd + 1, :]
        m_prev = m_sc[hd:hd + 1, :][:, :1]
        dm = jnp.where(causal, b_col - b_row + i_row, -jnp.inf)
        m_inter = b_col + m_prev
        m_t = jnp.maximum(m_inter, jnp.max(dm, axis=-1, keepdims=True))
        w_inter = jnp.exp(m_inter - m_t)
        w = jnp.exp(dm - m_t)
        s = _dot_dims(q, k, ((1,), (1,))) * scale * w
        sb = s.astype(BF16)
        c_prev = c_sc[hd]
        n_prev = n_sc[hd]
        num = w_inter * _dot(q, c_prev.astype(BF16)) + _dot(sb, v)
        den = w_inter * _dot(q, n_prev.astype(BF16)) + _dot(sb, ones)
        h = num / jnp.maximum(jnp.abs(den), jnp.exp(-m_t))
        hn = h * lax.rsqrt(jnp.mean(h * h, axis=-1, keepdims=True) + EPS)
        hm_ref[:, hd * DH:(hd + 1) * DH] = (hn * mg_ref[:, hd * DH:(hd + 1) * DH] * _sigmoid(o.astype(F32))).astype(BF16)
        m_last = m_t[lc - 1:lc, :]
        b_last = b_col[lc - 1:lc, :]
        decay = w_inter[lc - 1:lc, :]
        w_last = jnp.exp(b_last - b_col + i_col - m_last) * scale
        kw = (k.astype(F32) * w_last).astype(BF16)
        c_sc[hd] = decay * c_prev + _dot_dims(kw, v, ((0,), (0,)))
        n_sc[hd] = decay * n_prev + _dot_dims(kw, ones, ((0,), (0,)))
        m_sc[hd:hd + 1, :] = jnp.broadcast_to(m_last, (1, LANES))


def _mlstm_prompt(qkvo, gcol, grow, mnorm_g):
    nc = SEQ // MCHUNK
    q3 = qkvo.reshape(N_PROMPT, SEQ, 4 * MWIDTH)
    g3 = gcol.reshape(N_PROMPT, SEQ, LANES)
    seq_blk = lambda w: pl.BlockSpec((MSEQ, MCHUNK, w), lambda b, c: (b, c, 0))
    state = lambda shape: pl.BlockSpec((MSEQ,) + shape, lambda b, c: (b,) + (0,) * len(shape))
    gr_specs = [pl.BlockSpec((8, MCHUNK), functools.partial(lambda b, c, sq: (0, (b * MSEQ + sq) * nc + c), sq=sq))
                for sq in range(MSEQ)]
    hm, c_p, n_p, m_p = pl.pallas_call(
        _mlstm_prompt_body,
        grid=(N_PROMPT // MSEQ, nc),
        in_specs=[seq_blk(4 * MWIDTH), seq_blk(LANES)] + gr_specs + [pl.BlockSpec((1, MWIDTH), lambda b, c: (0, 0))],
        out_specs=[seq_blk(MWIDTH), state((HEADS, DH, DH)), state((HEADS, DH, DH)), state((8, LANES))],
        out_shape=[jax.ShapeDtypeStruct((N_PROMPT, SEQ, MWIDTH), BF16),
                   jax.ShapeDtypeStruct((N_PROMPT, HEADS, DH, DH), F32),
                   jax.ShapeDtypeStruct((N_PROMPT, HEADS, DH, DH), F32),
                   jax.ShapeDtypeStruct((N_PROMPT, 8, LANES), F32)],
        scratch_shapes=[pltpu.VMEM((MSEQ, HEADS, DH, DH), F32),
                        pltpu.VMEM((MSEQ, HEADS, DH, DH), F32),
                        pltpu.VMEM((MSEQ, 8, LANES), F32)],
        compiler_params=_cparams("parallel", "arbitrary"),
        name="mlstm_prompt",
    )(q3, g3, *([grow] * MSEQ), mnorm_g)
    return hm.reshape(N_PROMPT * SEQ, MWIDTH), c_p, n_p, m_p


MSTEP_TOK = 8


def _mlstm_step_body(q_ref, qkt_ref, ig_ref, lf_ref, m0_ref, c0_ref, n0_ref, mg_ref, hm_ref, c_out, n_out, m_out, qc_sc):
    scale = DH ** -0.5
    ig = ig_ref[...]
    m_inter = lf_ref[...] + m0_ref[...]
    m_t = jnp.maximum(m_inter, ig)
    w_inter_all = jnp.exp(m_inter - m_t)
    w_all = jnp.exp(ig - m_t) * scale
    floor_all = jnp.exp(-m_t)
    m_out[...] = m_t
    for hd in range(HEADS):
        sl = slice(hd * DH, (hd + 1) * DH)
        q = q_ref[:, sl].astype(F32)
        k = q_ref[:, MWIDTH + hd * DH:MWIDTH + (hd + 1) * DH].astype(F32)
        v = q_ref[:, 2 * MWIDTH + hd * DH:2 * MWIDTH + (hd + 1) * DH].astype(F32)
        o = q_ref[:, 3 * MWIDTH + hd * DH:3 * MWIDTH + (hd + 1) * DH].astype(F32)
        n0 = n0_ref[:, sl]
        wi = w_inter_all[:, hd:hd + 1]
        wk = w_all[:, hd:hd + 1]
        s = jnp.sum(q * k, axis=-1, keepdims=True) * wk
        qn = jnp.sum(q * n0, axis=-1, keepdims=True)
        for j in range(MSTEP_TOK):
            c0 = c0_ref[j, hd]
            q_col = qkt_ref[hd, :, j:j + 1]
            k_col = qkt_ref[HEADS + hd, :, j:j + 1]
            qc_sc[j:j + 1, sl] = jnp.sum(q_col * c0, axis=0, keepdims=True)
            c_out[j, hd] = wi[j:j + 1, :] * c0 + (wk[j:j + 1, :] * k_col) * v[j:j + 1, :]
        num = wi * qc_sc[:, sl] + s * v
        den = wi * qn + s
        h = num / jnp.maximum(jnp.abs(den), floor_all[:, hd:hd + 1])
        hn = h * lax.rsqrt(jnp.mean(h * h, axis=-1, keepdims=True) + EPS)
        hm_ref[:, sl] = hn * mg_ref[:, sl] * _sigmoid(o)
        n_out[:, sl] = wi * n0 + wk * k


def _mlstm_step(qkvo, qkt, ig, lf, m0, c0, n0, mnorm_g):
    t = qkvo.shape[0]
    tk = MSTEP_TOK
    return pl.pallas_call(
        _mlstm_step_body,
        grid=(t // tk,),
        in_specs=[pl.BlockSpec((tk, 4 * MWIDTH), lambda i: (i, 0)),
                  pl.BlockSpec((None, 2 * HEADS, DH, tk), lambda i: (i, 0, 0, 0)),
                  pl.BlockSpec((tk, LANES), lambda i: (i, 0)),
                  pl.BlockSpec((tk, LANES), lambda i: (i, 0)),
                  pl.BlockSpec((tk, LANES), lambda i: (i, 0)),
                  pl.BlockSpec((tk, HEADS, DH, DH), lambda i: (i, 0, 0, 0)),
                  pl.BlockSpec((tk, MWIDTH), lambda i: (i, 0)),
                  pl.BlockSpec((1, MWIDTH), lambda i: (0, 0))],
        out_specs=[pl.BlockSpec((tk, MWIDTH), lambda i: (i, 0)),
                   pl.BlockSpec((tk, HEADS, DH, DH), lambda i: (i, 0, 0, 0)),
                   pl.BlockSpec((tk, MWIDTH), lambda i: (i, 0)),
                   pl.BlockSpec((tk, LANES), lambda i: (i, 0))],
        out_shape=[jax.ShapeDtypeStruct((t, MWIDTH), F32),
                   jax.ShapeDtypeStruct((t, HEADS, DH, DH), F32),
                   jax.ShapeDtypeStruct((t, MWIDTH), F32),
                   jax.ShapeDtypeStruct((t, LANES), F32)],
        scratch_shapes=[pltpu.VMEM((tk, MWIDTH), F32)],
        compiler_params=_cparams("parallel"),
        name="mlstm_step",
    )(qkvo, qkt, ig, lf, m0, c0, n0, mnorm_g)


def _post_body(x_ref, y5_ref, hm_ref, gg_ref, g1_ref, sc_ref, sh_ref, n2_ref, wglu_ref, wmo_ref, wo_ref, rw_ref, rb_ref,
               cnt0_ref, x1_ref, h2_ref, eid_ref, wgt_ref, rank_ref, cnt_ref, run_sc, *, n_real, n_steps):
    i = pl.program_id(0)

    @pl.when(i == 0)
    def _():
        run_sc[...] = cnt0_ref[...]

    if n_steps > n_real:
        @pl.when(i >= n_real)
        def _():
            h2_ref[...] = jnp.zeros_like(h2_ref)

        pl.when(i < n_real)(functools.partial(
            _post_tile, x_ref, y5_ref, hm_ref, gg_ref, g1_ref, sc_ref, sh_ref, n2_ref, wglu_ref, wmo_ref, wo_ref,
            rw_ref, rb_ref, x1_ref, h2_ref, eid_ref, wgt_ref, rank_ref, run_sc))
    else:
        _post_tile(x_ref, y5_ref, hm_ref, gg_ref, g1_ref, sc_ref, sh_ref, n2_ref, wglu_ref, wmo_ref, wo_ref,
                   rw_ref, rb_ref, x1_ref, h2_ref, eid_ref, wgt_ref, rank_ref, run_sc)
    cnt_ref[...] = run_sc[...]


def _post_tile(x_ref, y5_ref, hm_ref, gg_ref, g1_ref, sc_ref, sh_ref, n2_ref, wglu_ref, wmo_ref, wo_ref, rw_ref, rb_ref,
               x1_ref, h2_ref, eid_ref, wgt_ref, rank_ref, run_sc):
    g5 = _gelu_tanh(y5_ref[...]).astype(BF16)
    glu = _dot(g5, wglu_ref[...])
    br_s5 = glu[:, :D_MODEL] * _sigmoid(glu[:, D_MODEL:])
    br_m = _dot(hm_ref[...].astype(BF16), wmo_ref[...])
    gg = gg_ref[...].astype(F32)
    merged = _sigmoid(gg[:, :D_MODEL]) * br_s5 + _sigmoid(gg[:, D_MODEL:]) * br_m
    x1 = x_ref[...] + g1_ref[...] * _dot(merged.astype(BF16), wo_ref[...])
    x1_ref[...] = x1
    ms = jnp.mean(x1 * x1, axis=-1, keepdims=True)
    h2 = x1 * lax.rsqrt(ms + EPS) * n2_ref[...] * (1.0 + sc_ref[...]) + sh_ref[...]
    h2_ref[...] = h2
    logits = _dot_hp(h2, rw_ref[...]) + rb_ref[...]
    tm = logits.shape[0]
    lane = lax.broadcasted_iota(jnp.int32, logits.shape, 1)
    l = jnp.where(lane < N_EXPERTS, logits, -jnp.inf)
    vals, hots, idxs = [], [], []
    for _ in range(TOP_K):
        mx = jnp.max(l, axis=-1, keepdims=True)
        idx = jnp.min(jnp.where(l == mx, lane, LANES), axis=-1, keepdims=True)
        hot = lane == idx
        vals.append(mx)
        hots.append(hot)
        idxs.append(idx)
        l = jnp.where(hot, -jnp.inf, l)
    ex = [jnp.exp(vk - vals[0]) for vk in vals]
    tot = ex[0] + ex[1] + ex[2] + ex[3]
    chosen = jnp.where(hots[0] | hots[1] | hots[2] | hots[3], 1.0, 0.0)
    r_io = lax.broadcasted_iota(jnp.int32, (tm, tm), 0)
    c_io = lax.broadcasted_iota(jnp.int32, (tm, tm), 1)
    earlier = (c_io < r_io).astype(BF16)
    before = run_sc[...] + _dot(earlier, chosen.astype(BF16))
    eid = jnp.zeros(logits.shape, jnp.int32)
    wgt = jnp.zeros(logits.shape, F32)
    rank = jnp.zeros(logits.shape, F32)
    for k in range(TOP_K):
        eid = jnp.where(lane == k, idxs[k], eid)
        wgt = jnp.where(lane == k, ex[k] / tot, wgt)
        rank = jnp.where(lane == k, jnp.sum(jnp.where(hots[k], before, 0.0), axis=-1, keepdims=True), rank)
    eid_ref[...] = eid
    wgt_ref[...] = wgt
    rank_ref[...] = rank.astype(jnp.int32)
    run_sc[...] += jnp.sum(chosen, axis=0, keepdims=True)


def _post(x2d, y5, hm, gg, mod3, rows_per_mod, tm, n2, wglu, wmo, wo, rw, rb, cnt0, h2_rows):
    t = x2d.shape[0]
    mrows = mod3.shape[1]
    n_real = t // tm
    n_steps = -(-h2_rows // tm)
    last = n_real - 1

    def mod_spec(j):
        return pl.BlockSpec((None, mrows, D_MODEL), lambda i: ((jnp.minimum(i, last) * tm) // rows_per_mod, 0, j))

    const = lambda shape: pl.BlockSpec(shape, lambda i: (0,) * len(shape))
    tile = lambda w: pl.BlockSpec((tm, w), lambda i: (jnp.minimum(i, last), 0))
    return pl.pallas_call(
        functools.partial(_post_body, n_real=n_real, n_steps=n_steps),
        grid=(n_steps,),
        in_specs=[tile(D_MODEL), tile(S5_WIDTH), tile(MWIDTH), tile(2 * D_MODEL),
                  mod_spec(2), mod_spec(4), mod_spec(3), const((1, D_MODEL)),
                  const(wglu.shape), const(wmo.shape), const(wo.shape), const(rw.shape), const(rb.shape),
                  const((1, LANES))],
        out_specs=[tile(D_MODEL), pl.BlockSpec((tm, D_MODEL), lambda i: (i, 0)), tile(LANES), tile(LANES), tile(LANES),
                   const((1, LANES))],
        out_shape=[jax.ShapeDtypeStruct((t, D_MODEL), F32),
                   jax.ShapeDtypeStruct((h2_rows, D_MODEL), F32),
                   jax.ShapeDtypeStruct((t, LANES), jnp.int32),
                   jax.ShapeDtypeStruct((t, LANES), F32),
                   jax.ShapeDtypeStruct((t, LANES), jnp.int32),
                   jax.ShapeDtypeStruct((1, LANES), F32)],
        scratch_shapes=[pltpu.VMEM((1, LANES), F32)],
        compiler_params=_cparams("arbitrary"),
        name="post",
    )(x2d, y5, hm, gg, mod3, mod3, mod3, n2, wglu, wmo, wo, rw, rb, cnt0)


N_TOKENS = N_PROMPT * SEQ + N_SAMPLE
N_ASSIGN = N_TOKENS * TOP_K
MOE_TILE = 256
MOE_TILES = N_ASSIGN // MOE_TILE + N_EXPERTS
N_SLOT = MOE_TILES * MOE_TILE
CMB_TILE = 512


MOE_RING = 3


def _moe_body(te_ref, nt_ref, nxe_ref, gpar_ref, idx01_ref, idx2_ref, h2_hbm, wup_hbm, wdn_hbm, bias_ref,
              y_ref, xb0, xb1, xb2, sem, wbuf_up, wbuf_dn, wsem, wup_bf, wdn_bf):
    i = pl.program_id(0)
    nt = nt_ref[0]
    bufs = (xb0, xb1, xb2)
    last = MOE_TILES - 1

    def gather(idx_ref, row, slot):
        for r in range(MOE_TILE):
            pltpu.make_async_copy(h2_hbm.at[pl.ds(idx_ref[row, r], 1)], bufs[slot].at[pl.ds(r, 1)],
                                  sem.at[slot]).start()

    def wait(slot):
        pltpu.make_async_copy(h2_hbm.at[pl.ds(0, MOE_TILE)], bufs[slot], sem.at[slot]).wait()

    def weights(e, slot):
        return (pltpu.make_async_copy(wup_hbm.at[e], wbuf_up.at[slot], wsem.at[0, slot]),
                pltpu.make_async_copy(wdn_hbm.at[e], wbuf_dn.at[slot], wsem.at[1, slot]))

    wslot = gpar_ref[i]

    @pl.when(i == 0)
    def _():
        for c in weights(te_ref[0], wslot):
            c.start(priority=1)
        gather(idx01_ref, 0, 0)
        gather(idx01_ref, 1, 1)

    new_expert = jnp.logical_or(i == 0, te_ref[i] != te_ref[jnp.maximum(i - 1, 0)])
    live = i < nt

    @pl.when(jnp.logical_and(new_expert, live))
    def _():
        for c in weights(0, wslot):
            c.wait()
        wup_bf[...] = wbuf_up[wslot].astype(BF16)
        wdn_bf[...] = wbuf_dn[wslot].astype(BF16)
        nxe = nxe_ref[i]

        @pl.when(nxe >= 0)
        def _():
            for c in weights(nxe, 1 - wslot):
                c.start(priority=1)

    def step(slot):
        wait(slot)
        gather(idx2_ref, 0, (slot + 2) % MOE_RING)
        gu = _dot(bufs[slot][...].astype(BF16), wup_bf[...]) + bias_ref[:, :2 * D_FF]
        g = jnp.minimum(gu[:, :D_FF], SWIGLU_LIMIT)
        up = jnp.clip(gu[:, D_FF:], -SWIGLU_LIMIT, SWIGLU_LIMIT)
        act = (up + 1.0) * g * _sigmoid(SWIGLU_ALPHA * g)
        y_ref[...] = _dot(act.astype(BF16), wdn_bf[...]) + bias_ref[:, 2 * D_FF:]

    for slot in range(MOE_RING):
        pl.when(jnp.logical_and(live, i % MOE_RING == slot))(functools.partial(step, slot))

    @pl.when(jnp.logical_not(live))
    def _():
        y_ref[...] = jnp.zeros_like(y_ref)

    drain = jnp.logical_and(jnp.logical_not(live), i < nt + 2)
    for slot in range(MOE_RING):
        pl.when(jnp.logical_and(drain, i % MOE_RING == slot))(functools.partial(wait, slot))
    pl.when(jnp.logical_and(i == last, last - 1 < nt))(functools.partial(wait, (last + 1) % MOE_RING))
    pl.when(jnp.logical_and(i == last, last < nt))(functools.partial(wait, (last + 2) % MOE_RING))


def _moe(tile_expert, n_tiles, next_expert, group_parity, slot_token, h2_all, wup, wdn, bias):
    idx3 = slot_token.reshape(MOE_TILES, 1, MOE_TILE)
    idx01 = slot_token[:2 * MOE_TILE].reshape(1, 2, MOE_TILE)
    grid_spec = pltpu.PrefetchScalarGridSpec(
        num_scalar_prefetch=4,
        grid=(MOE_TILES,),
        in_specs=[pl.BlockSpec((None, 2, MOE_TILE), lambda i, *_: (0, 0, 0), memory_space=pltpu.SMEM),
                  pl.BlockSpec((None, 1, MOE_TILE), lambda i, *_: (jnp.minimum(i + 2, MOE_TILES - 1), 0, 0),
                               memory_space=pltpu.SMEM),
                  pl.BlockSpec(memory_space=pl.ANY),
                  pl.BlockSpec(memory_space=pl.ANY),
                  pl.BlockSpec(memory_space=pl.ANY),
                  pl.BlockSpec((None, 1, 2 * D_FF + D_MODEL), lambda i, te, *_: (te[i], 0, 0))],
        out_specs=pl.BlockSpec((MOE_TILE, D_MODEL), lambda i, *_: (i, 0)),
        scratch_shapes=[pltpu.VMEM((MOE_TILE, D_MODEL), F32), pltpu.VMEM((MOE_TILE, D_MODEL), F32),
                        pltpu.VMEM((MOE_TILE, D_MODEL), F32),
                        pltpu.SemaphoreType.DMA((MOE_RING,)),
                        pltpu.VMEM((2, D_MODEL, 2 * D_FF), F32), pltpu.VMEM((2, D_FF, D_MODEL), F32),
                        pltpu.SemaphoreType.DMA((2, 2)),
                        pltpu.VMEM((D_MODEL, 2 * D_FF), BF16), pltpu.VMEM((D_FF, D_MODEL), BF16)])
    return pl.pallas_call(
        _moe_body,
        grid_spec=grid_spec,
        out_shape=jax.ShapeDtypeStruct((N_SLOT, D_MODEL), F32),
        compiler_params=_cparams("arbitrary"),
        name="moe",
    )(tile_expert, n_tiles, next_expert, group_parity, idx01, idx3, h2_all, wup, wdn, bias)


def _combine_body(cur_ref, nxt_ref, wgt_ref, x1_ref, g2_ref, fg_ref, y_hbm, o_ref, yb0, yb1, sem, *, n_steps, tm):
    i = pl.program_id(0)

    def gather(idx_ref, buf, s):
        for r in range(tm):
            for k in range(TOP_K):
                pltpu.make_async_copy(y_hbm.at[pl.ds(idx_ref[0, r * TOP_K + k], 1)],
                                      buf.at[pl.ds(k * tm + r, 1)], sem.at[s]).start(priority=k % 2)

    def wait(buf, s):
        pltpu.make_async_copy(y_hbm.at[pl.ds(0, TOP_K * tm)], buf, sem.at[s]).wait()

    @pl.when(i == 0)
    def _():
        gather(cur_ref, yb0, 0)

    def step(cur, nxt, s_cur, s_nxt):
        wait(cur, s_cur)
        gather(nxt_ref, nxt, s_nxt)
        wgt = wgt_ref[...]
        acc = wgt[:, 0:1] * cur[0:tm, :]
        for k in range(1, TOP_K):
            acc = acc + wgt[:, k:k + 1] * cur[k * tm:(k + 1) * tm, :]
        xo = x1_ref[...] + g2_ref[...] * acc
        ms = jnp.mean(xo * xo, axis=-1, keepdims=True)
        o_ref[...] = xo * lax.rsqrt(ms + EPS) * fg_ref[...]

    pl.when(i % 2 == 0)(functools.partial(step, yb0, yb1, 0, 1))
    pl.when(i % 2 == 1)(functools.partial(step, yb1, yb0, 1, 0))
    last_slot = n_steps % 2
    pl.when(i == n_steps - 1)(functools.partial(wait, yb1 if last_slot else yb0, last_slot))


def _combine(pos, wgt, x1, mod3, rows_per_mod, fg, y_slots, tm):
    t = x1.shape[0]
    n = t // tm
    mrows = mod3.shape[1]
    assert mrows in (1, tm) and rows_per_mod % tm == 0
    pos3 = pos.reshape(n, 1, tm * TOP_K)
    return pl.pallas_call(
        functools.partial(_combine_body, n_steps=n, tm=tm),
        grid=(n,),
        in_specs=[pl.BlockSpec((None, 1, tm * TOP_K), lambda i: (i, 0, 0), memory_space=pltpu.SMEM),
                  pl.BlockSpec((None, 1, tm * TOP_K), lambda i: (jnp.minimum(i + 1, n - 1), 0, 0),
                               memory_space=pltpu.SMEM),
                  pl.BlockSpec((tm, LANES), lambda i: (i, 0)),
                  pl.BlockSpec((tm, D_MODEL), lambda i: (i, 0)),
                  pl.BlockSpec((None, mrows, D_MODEL), lambda i: ((i * tm) // rows_per_mod, 0, 5)),
                  pl.BlockSpec((1, D_MODEL), lambda i: (0, 0)),
                  pl.BlockSpec(memory_space=pl.ANY)],
        out_specs=pl.BlockSpec((tm, D_MODEL), lambda i: (i, 0)),
        out_shape=jax.ShapeDtypeStruct((t, D_MODEL), F32),
        scratch_shapes=[pltpu.VMEM((TOP_K * tm, D_MODEL), F32), pltpu.VMEM((TOP_K * tm, D_MODEL), F32),
                        pltpu.SemaphoreType.DMA((2,))],
        compiler_params=_cparams("arbitrary"),
        name="combine",
    )(pos3, pos3, wgt, x1, mod3, fg, y_slots)


def _routing(eid_p, rank_p, eid_s, rank_s, counts):
    cnt = counts[0, :N_EXPERTS].astype(jnp.int32)
    ntile = (cnt + MOE_TILE - 1) // MOE_TILE
    tile_end = jnp.cumsum(ntile)
    poff = (tile_end - ntile) * MOE_TILE
    total = tile_end[-1]
    j = jnp.arange(MOE_TILES, dtype=jnp.int32)
    te = jnp.sum((j[:, None] >= tile_end[None, :]).astype(jnp.int32), axis=1)
    te_last = jnp.sum(((total - 1) >= tile_end).astype(jnp.int32))
    tile_expert = jnp.where(j < total, te, te_last).astype(jnp.int32)
    ex = jnp.arange(N_EXPERTS, dtype=jnp.int32)
    used = ntile > 0
    later = jnp.where(used[None, :] & (ex[None, :] > ex[:, None]), ex[None, :], N_EXPERTS)
    nxt_e = jnp.min(later, axis=1)
    nxt_e = jnp.where(nxt_e < N_EXPERTS, nxt_e, -1).astype(jnp.int32)
    par_e = ((jnp.cumsum(used.astype(jnp.int32)) - 1) % 2).astype(jnp.int32)
    pick = tile_expert[:, None] == ex[None, :]
    next_expert = jnp.sum(jnp.where(pick, nxt_e[None, :], 0), axis=1).astype(jnp.int32)
    group_parity = jnp.sum(jnp.where(pick, par_e[None, :], 0), axis=1).astype(jnp.int32)

    def pos_of(eid, rank):
        e = eid[:, :TOP_K]
        off = jnp.sum(jnp.where(e[:, :, None] == ex[None, None, :], poff[None, None, :], 0), axis=-1)
        return off + rank[:, :TOP_K]

    pos_p = pos_of(eid_p, rank_p)
    pos_s = pos_of(eid_s, rank_s)
    tok = jnp.concatenate([jnp.repeat(jnp.arange(N_PROMPT * SEQ, dtype=jnp.int32), TOP_K),
                           jnp.repeat(N_PROMPT * SEQ + jnp.arange(N_SAMPLE, dtype=jnp.int32), TOP_K)])
    pos_all = jnp.concatenate([pos_p.reshape(-1), pos_s.reshape(-1)])
    padc = jnp.concatenate([ntile * MOE_TILE - cnt, (N_SLOT - total * MOE_TILE).reshape(1)])
    pad_end = jnp.cumsum(padc)
    pad_base = jnp.concatenate([poff + cnt, (total * MOE_TILE).reshape(1)])
    jp = jnp.arange(N_SLOT - N_ASSIGN, dtype=jnp.int32)
    pe = jnp.sum((jp[:, None] >= pad_end[None, :]).astype(jnp.int32), axis=1)
    hit = pe[:, None] == jnp.arange(N_EXPERTS + 1, dtype=jnp.int32)[None, :]
    pad_slot = jp + jnp.sum(jnp.where(hit, (pad_base - (pad_end - padc))[None, :], 0), axis=1)
    keys = jnp.concatenate([pos_all, pad_slot.astype(jnp.int32)])
    vals = jnp.concatenate([tok, jnp.zeros((N_SLOT - N_ASSIGN,), jnp.int32)])
    tok_bits = 15
    assert N_TOKENS <= (1 << tok_bits) and N_SLOT <= (1 << (32 - tok_bits))
    packed = lax.sort((keys.astype(jnp.uint32) << tok_bits) | vals.astype(jnp.uint32))
    slot_token = (packed & ((1 << tok_bits) - 1)).astype(jnp.int32)
    return tile_expert, total.reshape(1).astype(jnp.int32), next_expert, group_parity, slot_token, pos_p, pos_s


def _unpack_s5_state(x):
    n = x.shape[1]
    z = x.reshape(N_SUPER, n, 2, S5_SUPER, S5_STATE).transpose(2, 1, 0, 3, 4).reshape(2, n, S5_GROUPS, S5_STATE)
    return z[0], z[1]


def _pack_s5_state(re, im):
    n = re.shape[0]
    z = jnp.stack([re, im], axis=0).reshape(2, n, N_SUPER, S5_SUPER, S5_STATE)
    return z.transpose(2, 1, 0, 3, 4).reshape(N_SUPER, n, SUPER_STATE)


def kernel(x_prompt, x_sample, c_prompt, c_sample, state_s5_re, state_s5_im, state_mlstm_C, state_mlstm_n, state_mlstm_m, norm1_g, norm2_g, final_norm_g, w_ada, b_ada, w_in, s5_lambda_re, s5_lambda_im, s5_log_dt, s5_B_re, s5_B_im, s5_C_re, s5_C_im, s5_D, s5_w_glu, mlstm_b_i, mlstm_b_f, mlstm_norm_g, mlstm_w_out, w_out, router_w, router_b, expert_w_up, expert_b_up, expert_w_down, expert_b_down):
    assert w_in.shape[0] == 1, "single layer"
    tp = N_PROMPT * SEQ
    xp = x_prompt.reshape(tp, D_MODEL).astype(F32)
    xs = x_sample.reshape(N_SAMPLE, D_MODEL).astype(F32)

    w = w_in[0]
    c0 = S5_WIDTH
    c1 = c0 + 4 * MWIDTH
    c2 = c1 + 2 * HEADS
    wu = w[:, :c0].astype(BF16)
    wq = w[:, c0:c1].astype(BF16)
    wif = jnp.pad(w[:, c1:c2].astype(F32), ((0, 0), (0, LANES - 2 * HEADS)))
    wg = w[:, c2:].astype(BF16)
    bvec = jnp.concatenate([mlstm_b_i[0], mlstm_b_f[0]]).astype(F32)
    bif = jnp.pad(bvec, (0, LANES - 2 * HEADS)).reshape(1, LANES)
    wparts = (wu, wq, wg, wif, bif)
    g1 = norm1_g[0].reshape(1, D_MODEL).astype(F32)
    n2 = norm2_g[0].reshape(1, D_MODEL).astype(F32)
    fg = final_norm_g.reshape(1, D_MODEL).astype(F32)
    mg = mlstm_norm_g[0].reshape(1, MWIDTH).astype(F32)
    wglu = s5_w_glu[0].astype(BF16)
    wmo = mlstm_w_out[0].astype(BF16)
    wo = w_out[0].astype(BF16)
    rw = jnp.pad(router_w[0].astype(F32), ((0, 0), (0, LANES - N_EXPERTS)))
    rb = jnp.pad(router_b[0].astype(F32), (0, LANES - N_EXPERTS)).reshape(1, LANES)
    wup = expert_w_up[0].astype(F32)
    wdn = expert_w_down[0].astype(F32)
    ebias = jnp.concatenate([expert_b_up[0].astype(F32), expert_b_down[0].astype(F32)], axis=1)
    ebias = ebias.reshape(N_EXPERTS, 1, 2 * D_FF + D_MODEL)
    ddiag, fcomp, ecomp, a16, a1 = _s5_tables(
        s5_lambda_re[0], s5_lambda_im[0], s5_log_dt[0], s5_B_re[0], s5_B_im[0], s5_C_re[0], s5_C_im[0], s5_D[0])

    c_all = jnp.concatenate([c_prompt, c_sample], axis=0).astype(F32)
    mod = _adaln(c_all, w_ada[0].astype(F32), b_ada[0].astype(F32))
    mod_p = mod[:N_PROMPT].reshape(N_PROMPT, 1, 6 * D_MODEL)
    mod_s = mod[N_PROMPT:].reshape(1, N_SAMPLE, 6 * D_MODEL)

    u, qkvo, gg, gcol, grow = _inproj(xp, mod_p, SEQ, 1024, g1, wparts)
    y5, xend, k0, e0, f1 = _s5_prompt(u, ddiag, fcomp, ecomp, a16, a1)
    hm, c_p, n_p, m_p = _mlstm_prompt(qkvo, gcol, grow, mg)
    cnt0 = jnp.zeros((1, LANES), F32)
    x1, h2_all, eid_p, wgt_p, rank_p, cnt_p = _post(xp, y5, hm, gg, mod_p, SEQ, 512, n2, wglu, wmo, wo, rw, rb,
                                                    cnt0, N_TOKENS)
    p_re, p_im = _unpack_s5_state(xend.reshape(N_SUPER, N_PROMPT, SUPER_STATE))
    n_p = n_p[..., 0]
    m_p = m_p[:, :HEADS, 0]

    us, qs, ggs, gcs, _ = _inproj(xs, mod_s, N_SAMPLE, N_SAMPLE, g1, wparts)
    x0 = _pack_s5_state(state_s5_re[0].astype(F32), state_s5_im[0].astype(F32))
    y5s, xns = _s5_step(us, x0, k0, e0, f1, a1)
    s_re, s_im = _unpack_s5_state(xns)
    qkt = qs[:, :2 * MWIDTH].astype(F32).reshape(N_SAMPLE // MSTEP_TOK, MSTEP_TOK, 2 * HEADS, DH).transpose(0, 2, 3, 1)
    pad_heads = lambda a: jnp.pad(a, ((0, 0), (0, LANES - HEADS)))
    m0 = pad_heads(state_mlstm_m[0].astype(F32))
    hms, c_s, n_s, m_s = _mlstm_step(qs, qkt, pad_heads(gcs[:, :HEADS]), pad_heads(gcs[:, HEADS:2 * HEADS]), m0,
                                     state_mlstm_C[0].astype(F32),
                                     state_mlstm_n[0].astype(F32).reshape(N_SAMPLE, MWIDTH), mg)
    x1s, h2s, eid_s, wgt_s, rank_s, cnt_all = _post(xs, y5s, hms, ggs, mod_s, N_SAMPLE, N_SAMPLE, n2, wglu, wmo, wo, rw, rb,
                                                    cnt_p, N_SAMPLE)

    h2_all = lax.dynamic_update_slice(h2_all, h2s, (tp, 0))
    tile_expert, n_tiles, next_expert, group_parity, slot_token, pos_p, pos_s = _routing(
        eid_p, rank_p, eid_s, rank_s, cnt_all)
    y_slots = _moe(tile_expert, n_tiles, next_expert, group_parity, slot_token, h2_all, wup, wdn, ebias)
    y_p = _combine(pos_p, wgt_p, x1, mod_p, SEQ, fg, y_slots, CMB_TILE)
    y_s = _combine(pos_s, wgt_s, x1s, mod_s, N_SAMPLE, fg, y_slots, N_SAMPLE)

    return (y_p.reshape(N_PROMPT, SEQ, D_MODEL).astype(x_prompt.dtype),
            y_s.reshape(N_SAMPLE, 1, D_MODEL).astype(x_sample.dtype),
            p_re[None], p_im[None], c_p[None], n_p[None], m_p[None],
            s_re[None], s_im[None], c_s[None],
            n_s.reshape(1, N_SAMPLE, HEADS, DH), m_s[:, :HEADS][None])
```

```python
import functools
import math

import jax
import jax.numpy as jnp
from jax import lax
from jax.experimental import pallas as pl
from jax.experimental.pallas import tpu as pltpu

F32 = jnp.float32
BF16 = jnp.bfloat16

D_MODEL = 1024
SEQ = 2048
N_PROMPT = 8
N_SAMPLE = 128
S5_WIDTH = 512
S5_GROUP = 16
S5_GROUPS = 32
S5_STATE = 64
HEADS = 4
DH = 128
MWIDTH = HEADS * DH
N_EXPERTS = 32
TOP_K = 4
D_FF = 1024
SWIGLU_LIMIT = 7.0
SWIGLU_ALPHA = 1.702
EPS = 1e-6

LANES = 128
S5_CHUNK = 16
S5_SUPER = LANES // S5_GROUP
N_SUPER = S5_GROUPS // S5_SUPER
SUPER_STATE = 2 * S5_SUPER * S5_STATE
MCHUNK = 128
VMEM_LIMIT = 56 * 1024 * 1024


def _cparams(*sem):
    return pltpu.CompilerParams(dimension_semantics=sem, vmem_limit_bytes=VMEM_LIMIT)


def _dot(a, b):
    return jnp.dot(a, b, preferred_element_type=F32)


def _dot_dims(a, b, dims):
    return lax.dot_general(a, b, (dims, ((), ())), preferred_element_type=F32)


def _split_bf16(a):
    hi = a.astype(BF16)
    lo = (a - hi.astype(F32)).astype(BF16)
    return hi, lo


def _dot_hp(a, b, dims=((1,), (0,))):
    ah, al = _split_bf16(a)
    bh, bl = _split_bf16(b)
    return _dot_dims(ah, bh, dims) + (_dot_dims(al, bh, dims) + _dot_dims(ah, bl, dims))


def _dot_exact_rhs(a, b_exact, dims=((1,), (0,))):
    ah, al = _split_bf16(a)
    return _dot_dims(ah, b_exact, dims) + _dot_dims(al, b_exact, dims)


def _log_sigmoid(x):
    return -(jnp.maximum(-x, 0.0) + jnp.log1p(jnp.exp(-jnp.abs(x))))


def _sigmoid(x):
    return 0.5 * (1.0 + jnp.tanh(0.5 * x))


def _gelu_tanh(x):
    c = math.sqrt(2.0 / math.pi)
    return 0.5 * x * (1.0 + jnp.tanh(c * (x + 0.044715 * (x * x * x))))


def _adaln_body(c_ref, w_ref, b_ref, o_ref):
    c = c_ref[...]
    s = c * _sigmoid(c)
    o_ref[...] = _dot_hp(s, w_ref[...]) + b_ref[...]


def _adaln(c_all, w_ada, b_ada):
    n = c_all.shape[0]
    tn = 1024
    return pl.pallas_call(
        _adaln_body,
        grid=(6 * D_MODEL // tn,),
        in_specs=[pl.BlockSpec((n, D_MODEL), lambda j: (0, 0)),
                  pl.BlockSpec((D_MODEL, tn), lambda j: (0, j)),
                  pl.BlockSpec((1, tn), lambda j: (0, j))],
        out_specs=pl.BlockSpec((n, tn), lambda j: (0, j)),
        out_shape=jax.ShapeDtypeStruct((n, 6 * D_MODEL), F32),
        compiler_params=_cparams("parallel"),
        name="adaln",
    )(c_all, w_ada, b_ada.reshape(1, -1))


def _inproj_body(x_ref, g1_ref, sc_ref, sh_ref, wu_ref, wq_ref, wg_ref, wif_ref, bif_ref,
                 u_ref, q_ref, gg_ref, if_ref, ift_ref):
    x = x_ref[...]
    ms = jnp.mean(x * x, axis=-1, keepdims=True)
    h = x * lax.rsqrt(ms + EPS) * g1_ref[...] * (1.0 + sc_ref[...]) + sh_ref[...]
    hb = h.astype(BF16)
    u_ref[...] = _dot(hb, wu_ref[...])
    q_ref[...] = _dot(hb, wq_ref[...]).astype(BF16)
    gg_ref[...] = _dot(hb, wg_ref[...]).astype(BF16)
    gc = _dot_hp(h, wif_ref[...]) + bif_ref[...]
    lane = lax.broadcasted_iota(jnp.int32, gc.shape, 1)
    gates = jnp.where(lane < HEADS, gc, _log_sigmoid(gc))
    if_ref[...] = gates
    ift_ref[...] = gates.T[:2 * HEADS, :]


def _inproj(x2d, mod3, rows_per_mod, tm, g1, wparts):
    t = x2d.shape[0]
    wu, wq, wg, wif, bif = wparts
    mrows = mod3.shape[1]

    def mod_spec(j):
        return pl.BlockSpec((None, mrows, D_MODEL), lambda i: ((i * tm) // rows_per_mod, 0, j))

    const = lambda shape: pl.BlockSpec(shape, lambda i: (0,) * len(shape), pipeline_mode=pl.Buffered(1))
    return pl.pallas_call(
        _inproj_body,
        grid=(t // tm,),
        in_specs=[pl.BlockSpec((tm, D_MODEL), lambda i: (i, 0)),
                  const((1, D_MODEL)), mod_spec(1), mod_spec(0),
                  const(wu.shape), const(wq.shape), const(wg.shape), const(wif.shape), const(bif.shape)],
        out_specs=[pl.BlockSpec((tm, S5_WIDTH), lambda i: (i, 0)),
                   pl.BlockSpec((tm, 4 * MWIDTH), lambda i: (i, 0)),
                   pl.BlockSpec((tm, 2 * D_MODEL), lambda i: (i, 0)),
                   pl.BlockSpec((tm, LANES), lambda i: (i, 0)),
                   pl.BlockSpec((8, tm), lambda i: (0, i))],
        out_shape=[jax.ShapeDtypeStruct((t, S5_WIDTH), F32),
                   jax.ShapeDtypeStruct((t, 4 * MWIDTH), BF16),
                   jax.ShapeDtypeStruct((t, 2 * D_MODEL), BF16),
                   jax.ShapeDtypeStruct((t, LANES), F32),
                   jax.ShapeDtypeStruct((8, t), F32)],
        compiler_params=_cparams("parallel"),
        name="inproj",
    )(x2d, g1, mod3, mod3, wu, wq, wg, wif, bif)


def _s5_tables(lam_re, lam_im, log_dt, b_re, b_im, c_re, c_im, d_s5):
    hi = lax.Precision.HIGHEST
    dt = jnp.exp(log_dt.astype(F32))[:, None]
    lr, li = lam_re.astype(F32), lam_im.astype(F32)
    dpow = jnp.arange(S5_CHUNK + 1, dtype=F32)[:, None, None]
    mag = jnp.exp(dpow * (lr * dt))
    pw_re, pw_im = mag * jnp.cos(dpow * (li * dt)), mag * jnp.sin(dpow * (li * dt))
    ab_re, ab_im = pw_re[1], pw_im[1]
    den = lr * lr + li * li
    nr, ni = ab_re - 1.0, ab_im
    coef_re = (nr * lr + ni * li) / den
    coef_im = (ni * lr - nr * li) / den
    br, bi = b_re.astype(F32), b_im.astype(F32)
    bb_re = coef_re[..., None] * br - coef_im[..., None] * bi
    bb_im = coef_re[..., None] * bi + coef_im[..., None] * br
    cr, ci = c_re.astype(F32), c_im.astype(F32)
    cl_re = cr[None] * pw_re[:, :, None, :] - ci[None] * pw_im[:, :, None, :]
    cl_im = cr[None] * pw_im[:, :, None, :] + ci[None] * pw_re[:, :, None, :]
    ddiag = (d_s5.astype(F32).reshape(N_SUPER, LANES, 1) * jnp.eye(LANES, dtype=F32)[None])
    rp_re, rp_im = pw_re[S5_CHUNK - 1::-1], pw_im[S5_CHUNK - 1::-1]
    f_re = rp_re[..., None] * bb_re[None] - rp_im[..., None] * bb_im[None]
    f_im = rp_re[..., None] * bb_im[None] + rp_im[..., None] * bb_re[None]
    fcat = jnp.stack([f_re, f_im], axis=2).transpose(0, 1, 4, 2, 3)
    fcomp = fcat.reshape(S5_CHUNK, N_SUPER, LANES, 2 * S5_STATE).transpose(1, 0, 2, 3)
    fcomp = fcomp.reshape(N_SUPER, S5_CHUNK * LANES, 2 * S5_STATE)
    ecat = jnp.stack([cl_re[:S5_CHUNK], -cl_im[:S5_CHUNK]], axis=0)
    er = ecat.transpose(2, 0, 4, 1, 3).reshape(N_SUPER, S5_SUPER, 2, S5_STATE, S5_CHUNK * S5_GROUP)
    ecomp = er.transpose(0, 2, 1, 3, 4).reshape(N_SUPER, SUPER_STATE, S5_CHUNK * S5_GROUP)

    def lay(re, im):
        z = jnp.stack([re, im], axis=0).reshape(2, N_SUPER, S5_SUPER * S5_STATE)
        return z.transpose(1, 0, 2).reshape(N_SUPER, 1, SUPER_STATE)

    a16 = lay(pw_re[S5_CHUNK], pw_im[S5_CHUNK])
    a1 = lay(ab_re, ab_im)
    return ddiag, fcomp, ecomp, a16, a1


def _expand_groups(comp, row_div, col_blk, n_cols, col_div):
    r, c = comp.shape
    ci = lax.broadcasted_iota(jnp.int32, (c, n_cols), 0)
    co = lax.broadcasted_iota(jnp.int32, (c, n_cols), 1)
    sel = jnp.logical_and(ci // col_blk == co // (S5_SUPER * col_blk), ci % col_blk == co % col_blk)
    rep = _dot(comp.astype(BF16), sel.astype(BF16))
    ro = lax.broadcasted_iota(jnp.int32, (r, n_cols), 0)
    cc = lax.broadcasted_iota(jnp.int32, (r, n_cols), 1)
    same = (ro // row_div) % S5_SUPER == (cc // col_div) % S5_SUPER
    return jnp.where(same, rep, 0.0).astype(BF16)


S5_NB = 4


def _s5_prompt_body(u_ref, dd_ref, fc_ref, ec_ref, a_ref, a1_ref, y_ref, xe_ref, k0_ref, e0_ref, f1_ref,
                    ucat, v_sc, xp_sc, kp_ref, f_ref, e_ref):
    n = SEQ // S5_CHUNK
    half = SUPER_STATE // 2

    @pl.when(pl.program_id(1) == 0)
    def _():
        f_ref[...] = _expand_groups(fc_ref[...], S5_GROUP, S5_STATE, SUPER_STATE, S5_STATE)
        e_ref[...] = _expand_groups(ec_ref[...], S5_STATE, S5_GROUP, S5_CHUNK * LANES, S5_GROUP)
        f1 = f_ref[(S5_CHUNK - 1) * LANES:, :]
        kall = _dot(f1, e_ref[...])
        for r in range(S5_CHUNK):
            for hf in range(2):
                d = S5_CHUNK - 2 - r + hf
                blk = jnp.zeros((LANES, LANES), F32) if d < 0 else kall[:, d * LANES:(d + 1) * LANES]
                if d == 0:
                    blk = blk + dd_ref[...]
                kp_ref[r * LANES:(r + 1) * LANES, hf * LANES:(hf + 1) * LANES] = blk.astype(BF16)
        k0_ref[...] = kp_ref[(S5_CHUNK - 2) * LANES:(S5_CHUNK - 1) * LANES, :LANES]
        e0_ref[...] = e_ref[:, LANES:2 * LANES]
        f1_ref[...] = f1

    for bl in range(S5_NB):
        for j in range(S5_CHUNK):
            ucat[bl * n:(bl + 1) * n, j * LANES:(j + 1) * LANES] = (
                u_ref[pl.ds(bl * SEQ + j, n, stride=S5_CHUNK), :].astype(BF16))
    v_sc[...] = _dot(ucat[...], f_ref[...])
    a_r, a_i = a_ref[:, :half], a_ref[:, half:]
    xr = [jnp.zeros((1, half), F32) for _ in range(S5_NB)]
    xi = [jnp.zeros((1, half), F32) for _ in range(S5_NB)]
    for c in range(n):
        for bl in range(S5_NB):
            row = bl * n + c
            xp_sc[row:row + 1, :half] = xr[bl]
            xp_sc[row:row + 1, half:] = xi[bl]
            vr, vi = v_sc[row:row + 1, :half], v_sc[row:row + 1, half:]
            xr[bl], xi[bl] = a_r * xr[bl] - a_i * xi[bl] + vr, a_r * xi[bl] + a_i * xr[bl] + vi
    for bl in range(S5_NB):
        xe_ref[bl:bl + 1, :half] = xr[bl]
        xe_ref[bl:bl + 1, half:] = xi[bl]
    l_r, l_i = a1_ref[:, :half], a1_ref[:, half:]
    xpr, xpi = xp_sc[:, :half], xp_sc[:, half:]
    z = jnp.concatenate([l_r * xpr - l_i * xpi, l_r * xpi + l_i * xpr], axis=1).astype(BF16)
    ye = _dot(z, e_ref[...])
    for m in range(S5_CHUNK // 2):
        rows = (2 * m + 2) * LANES
        yp = _dot(ucat[:, :rows], kp_ref[(S5_CHUNK - 2 - 2 * m) * LANES:, :]) + ye[:, 2 * m * LANES:(2 * m + 2) * LANES]
        for bl in range(S5_NB):
            y_ref[pl.ds(bl * SEQ + 2 * m, n, stride=S5_CHUNK), :] = yp[bl * n:(bl + 1) * n, :LANES]
            y_ref[pl.ds(bl * SEQ + 2 * m + 1, n, stride=S5_CHUNK), :] = yp[bl * n:(bl + 1) * n, LANES:]


def _s5_prompt(u, ddiag, fcomp, ecomp, a16, a1):
    n = S5_NB * SEQ // S5_CHUNK
    per_super = lambda shape: pl.BlockSpec((None,) + shape, lambda s, b: (s, 0, 0))
    return pl.pallas_call(
        _s5_prompt_body,
        grid=(N_SUPER, N_PROMPT // S5_NB),
        in_specs=[pl.BlockSpec((S5_NB * SEQ, LANES), lambda s, b: (b, s)),
                  per_super(ddiag.shape[1:]), per_super(fcomp.shape[1:]), per_super(ecomp.shape[1:]),
                  per_super((1, SUPER_STATE)), per_super((1, SUPER_STATE))],
        out_specs=[pl.BlockSpec((S5_NB * SEQ, LANES), lambda s, b: (b, s)),
                   pl.BlockSpec((None, None, S5_NB, SUPER_STATE), lambda s, b: (s, b, 0, 0)),
                   per_super((LANES, LANES)), per_super((SUPER_STATE, LANES)), per_super((LANES, SUPER_STATE))],
        out_shape=[jax.ShapeDtypeStruct(u.shape, F32),
                   jax.ShapeDtypeStruct((N_SUPER, N_PROMPT // S5_NB, S5_NB, SUPER_STATE), F32),
                   jax.ShapeDtypeStruct((N_SUPER, LANES, LANES), BF16),
                   jax.ShapeDtypeStruct((N_SUPER, SUPER_STATE, LANES), BF16),
                   jax.ShapeDtypeStruct((N_SUPER, LANES, SUPER_STATE), BF16)],
        scratch_shapes=[pltpu.VMEM((n, S5_CHUNK * LANES), BF16),
                        pltpu.VMEM((n, SUPER_STATE), F32),
                        pltpu.VMEM((n, SUPER_STATE), F32),
                        pltpu.VMEM((S5_CHUNK * LANES, 2 * LANES), BF16),
                        pltpu.VMEM((S5_CHUNK * LANES, SUPER_STATE), BF16),
                        pltpu.VMEM((SUPER_STATE, S5_CHUNK * LANES), BF16)],
        compiler_params=_cparams("parallel", "arbitrary"),
        name="s5_prompt",
    )(u, ddiag, fcomp, ecomp, a16, a1)


def _s5_step_body(u_ref, x0_ref, k0_ref, e0_ref, f1_ref, a_ref, y_ref, xn_ref):
    half = SUPER_STATE // 2
    ub = u_ref[...].astype(BF16)
    x0 = x0_ref[...]
    bu = _dot(ub, f1_ref[...])
    a_r, a_i = a_ref[:, :half], a_ref[:, half:]
    x0r, x0i = x0[:, :half], x0[:, half:]
    xn_ref[:, :half] = a_r * x0r - a_i * x0i + bu[:, :half]
    xn_ref[:, half:] = a_r * x0i + a_i * x0r + bu[:, half:]
    y_ref[...] = _dot(x0.astype(BF16), e0_ref[...]) + _dot(ub, k0_ref[...])


def _s5_step(u, x0, k0, e0, f1, a1):
    t = u.shape[0]
    per = lambda a: pl.BlockSpec((None,) + a.shape[1:], lambda s: (s,) + (0,) * (a.ndim - 1))
    return pl.pallas_call(
        _s5_step_body,
        grid=(N_SUPER,),
        in_specs=[pl.BlockSpec((t, LANES), lambda s: (0, s)), per(x0), per(k0), per(e0), per(f1), per(a1)],
        out_specs=[pl.BlockSpec((t, LANES), lambda s: (0, s)), per(x0)],
        out_shape=[jax.ShapeDtypeStruct(u.shape, F32), jax.ShapeDtypeStruct(x0.shape, F32)],
        compiler_params=_cparams("parallel"),
        name="s5_step",
    )(u, x0, k0, e0, f1, a1)


MSEQ = 2


def _mlstm_prompt_body(q_all, gc_all, *rest):
    gr_refs, (mg_ref, hm_all, c_out, n_out, m_out, c_all, n_all, m_all) = rest[:MSEQ], rest[MSEQ:]
    ci = pl.program_id(1)

    @pl.when(ci == 0)
    def _():
        c_all[...] = jnp.zeros_like(c_all)
        n_all[...] = jnp.zeros_like(n_all)
        m_all[...] = jnp.zeros_like(m_all)

    for sq in range(MSEQ):
        _mlstm_chunk(q_all.at[sq], gc_all.at[sq], gr_refs[sq], mg_ref, hm_all.at[sq],
                     c_all.at[sq], n_all.at[sq], m_all.at[sq])

    @pl.when(ci == pl.num_programs(1) - 1)
    def _():
        c_out[...] = c_all[...]
        n_out[...] = n_all[...]
        m_out[...] = m_all[...]


def _mlstm_chunk(q_ref, gc_ref, gr_ref, mg_ref, hm_ref, c_sc, n_sc, m_sc):
    lc = MCHUNK
    scale = DH ** -0.5
    row = lax.broadcasted_iota(jnp.int32, (lc, lc), 0)
    col = lax.broadcasted_iota(jnp.int32, (lc, lc), 1)
    causal = col <= row
    tri = causal.astype(BF16)
    ones = jnp.ones((lc, DH), BF16)
    gc = gc_ref[...]
    gr = gr_ref[...]
    gc_hi, gc_lo = _split_bf16(gc)
    bcol_all = _dot(tri, gc_hi) + _dot(tri, gc_lo)
    brow_all = _dot_exact_rhs(gr, tri, ((1,), (1,)))
    for hd in range(HEADS):
        q = q_ref[:, hd * DH:(hd + 1) * DH]
        k = q_ref[:, MWIDTH + hd * DH:MWIDTH + (hd + 1) * DH]
        v = q_ref[:, 2 * MWIDTH + hd * DH:2 * MWIDTH + (hd + 1) * DH]
        o = q_ref[:, 3 * MWIDTH + hd * DH:3 * MWIDTH + (hd + 1) * DH]
        i_col = gc[:, hd:hd + 1]
        b_col = bcol_all[:, HEADS + hd:HEADS + hd + 1]
        i_row = gr[hd:hd + 1, :]
        b_row = brow_all[HEADS + hd:HEADS + hd + 1, :]
        m_prev = m_sc[hd:hd + 1, :][:, :1]
        dm = jnp.where(causal, b_col - b_row + i_row, -jnp.inf)
        m_inter = b_col + m_prev
        m_t = jnp.maximum(m_inter, jnp.max(dm, axis=-1, keepdims=True))
        w_inter = jnp.exp(m_inter - m_t)
        w = jnp.exp(dm - m_t)
        s = _dot_dims(q, k, ((1,), (1,))) * scale * w
        sb = s.astype(BF16)
        c_prev = c_sc[hd]
        n_prev = n_sc[hd]
        num = w_inter * _dot(q, c_prev.astype(BF16)) + _dot(sb, v)
        den = w_inter * _dot(q, n_prev.astype(BF16)) + _dot(sb, ones)
        h = num / jnp.maximum(jnp.abs(den), jnp.exp(-m_t))
        hn = h * lax.rsqrt(jnp.mean(h * h, axis=-1, keepdims=True) + EPS)
        hm_ref[:, hd * DH:(hd + 1) * DH] = (hn * mg_ref[:, hd * DH:(hd + 1) * DH] * _sigmoid(o.astype(F32))).astype(BF16)
        m_last = m_t[lc - 1:lc, :]
        b_last = b_col[lc - 1:lc, :]
        decay = w_inter[lc - 1:lc, :]
        w_last = jnp.exp(b_last - b_col + i_col - m_last) * scale
        kw = (k.astype(F32) * w_last).astype(BF16)
        c_sc[hd] = decay * c_prev + _dot_dims(kw, v, ((0,), (0,)))
        n_sc[hd] = decay * n_prev + _dot_dims(kw, ones, ((0,), (0,)))
        m_sc[hd:hd + 1, :] = jnp.broadcast_to(m_last, (1, LANES))


def _mlstm_prompt(qkvo, gcol, grow, mnorm_g):
    nc = SEQ // MCHUNK
    q3 = qkvo.reshape(N_PROMPT, SEQ, 4 * MWIDTH)
    g3 = gcol.reshape(N_PROMPT, SEQ, LANES)
    seq_blk = lambda w: pl.BlockSpec((MSEQ, MCHUNK, w), lambda b, c: (b, c, 0))
    state = lambda shape: pl.BlockSpec((MSEQ,) + shape, lambda b, c: (b,) + (0,) * len(shape))
    gr_specs = [pl.BlockSpec((8, MCHUNK), functools.partial(lambda b, c, sq: (0, (b * MSEQ + sq) * nc + c), sq=sq))
                for sq in range(MSEQ)]
    hm, c_p, n_p, m_p = pl.pallas_call(
        _mlstm_prompt_body,
        grid=(N_PROMPT // MSEQ, nc),
        in_specs=[seq_blk(4 * MWIDTH), seq_blk(LANES)] + gr_specs + [pl.BlockSpec((1, MWIDTH), lambda b, c: (0, 0))],
        out_specs=[seq_blk(MWIDTH), state((HEADS, DH, DH)), state((HEADS, DH, DH)), state((8, LANES))],
        out_shape=[jax.ShapeDtypeStruct((N_PROMPT, SEQ, MWIDTH), BF16),
                   jax.ShapeDtypeStruct((N_PROMPT, HEADS, DH, DH), F32),
                   jax.ShapeDtypeStruct((N_PROMPT, HEADS, DH, DH), F32),
                   jax.ShapeDtypeStruct((N_PROMPT, 8, LANES), F32)],
        scratch_shapes=[pltpu.VMEM((MSEQ, HEADS, DH, DH), F32),
                        pltpu.VMEM((MSEQ, HEADS, DH, DH), F32),
                        pltpu.VMEM((MSEQ, 8, LANES), F32)],
        compiler_params=_cparams("parallel", "arbitrary"),
        name="mlstm_prompt",
    )(q3, g3, *([grow] * MSEQ), mnorm_g)
    return hm.reshape(N_PROMPT * SEQ, MWIDTH), c_p, n_p, m_p


MSTEP_TOK = 8


def _mlstm_step_body(q_ref, qkt_ref, ig_ref, lf_ref, m0_ref, c0_ref, n0_ref, mg_ref, hm_ref, c_out, n_out, m_out, qc_sc):
    scale = DH ** -0.5
    ig = ig_ref[...]
    m_inter = lf_ref[...] + m0_ref[...]
    m_t = jnp.maximum(m_inter, ig)
    w_inter_all = jnp.exp(m_inter - m_t)
    w_all = jnp.exp(ig - m_t) * scale
    floor_all = jnp.exp(-m_t)
    m_out[...] = m_t
    for hd in range(HEADS):
        sl = slice(hd * DH, (hd + 1) * DH)
        q = q_ref[:, sl].astype(F32)
        k = q_ref[:, MWIDTH + hd * DH:MWIDTH + (hd + 1) * DH].astype(F32)
        v = q_ref[:, 2 * MWIDTH + hd * DH:2 * MWIDTH + (hd + 1) * DH].astype(F32)
        o = q_ref[:, 3 * MWIDTH + hd * DH:3 * MWIDTH + (hd + 1) * DH].astype(F32)
        n0 = n0_ref[:, sl]
        wi = w_inter_all[:, hd:hd + 1]
        wk = w_all[:, hd:hd + 1]
        s = jnp.sum(q * k, axis=-1, keepdims=True) * wk
        qn = jnp.sum(q * n0, axis=-1, keepdims=True)
        for j in range(MSTEP_TOK):
            c0 = c0_ref[j, hd]
            q_col = qkt_ref[hd, :, j:j + 1]
            k_col = qkt_ref[HEADS + hd, :, j:j + 1]
            qc_sc[j:j + 1, sl] = jnp.sum(q_col * c0, axis=0, keepdims=True)
            c_out[j, hd] = wi[j:j + 1, :] * c0 + (wk[j:j + 1, :] * k_col) * v[j:j + 1, :]
        num = wi * qc_sc[:, sl] + s * v
        den = wi * qn + s
        h = num / jnp.maximum(jnp.abs(den), floor_all[:, hd:hd + 1])
        hn = h * lax.rsqrt(jnp.mean(h * h, axis=-1, keepdims=True) + EPS)
        hm_ref[:, sl] = hn * mg_ref[:, sl] * _sigmoid(o)
        n_out[:, sl] = wi * n0 + wk * k


def _mlstm_step(qkvo, qkt, ig, lf, m0, c0, n0, mnorm_g):
    t = qkvo.shape[0]
    tk = MSTEP_TOK
    return pl.pallas_call(
        _mlstm_step_body,
        grid=(t // tk,),
        in_specs=[pl.BlockSpec((tk, 4 * MWIDTH), lambda i: (i, 0)),
                  pl.BlockSpec((None, 2 * HEADS, DH, tk), lambda i: (i, 0, 0, 0)),
                  pl.BlockSpec((tk, LANES), lambda i: (i, 0)),
                  pl.BlockSpec((tk, LANES), lambda i: (i, 0)),
                  pl.BlockSpec((tk, LANES), lambda i: (i, 0)),
                  pl.BlockSpec((tk, HEADS, DH, DH), lambda i: (i, 0, 0, 0)),
                  pl.BlockSpec((tk, MWIDTH), lambda i: (i, 0)),
                  pl.BlockSpec((1, MWIDTH), lambda i: (0, 0))],
        out_specs=[pl.BlockSpec((tk, MWIDTH), lambda i: (i, 0)),
                   pl.BlockSpec((tk, HEADS, DH, DH), lambda i: (i, 0, 0, 0)),
                   pl.BlockSpec((tk, MWIDTH), lambda i: (i, 0)),
                   pl.BlockSpec((tk, LANES), lambda i: (i, 0))],
        out_shape=[jax.ShapeDtypeStruct((t, MWIDTH), F32),
                   jax.ShapeDtypeStruct((t, HEADS, DH, DH), F32),
                   jax.ShapeDtypeStruct((t, MWIDTH), F32),
                   jax.ShapeDtypeStruct((t, LANES), F32)],
        scratch_shapes=[pltpu.VMEM((tk, MWIDTH), F32)],
        compiler_params=_cparams("parallel"),
        name="mlstm_step",
    )(qkvo, qkt, ig, lf, m0, c0, n0, mnorm_g)


def _post_body(x_ref, y5_ref, hm_ref, gg_ref, g1_ref, sc_ref, sh_ref, n2_ref, wglu_ref, wmo_ref, wo_ref, rw_ref, rb_ref,
               cnt0_ref, x1_ref, h2_ref, eid_ref, wgt_ref, rank_ref, cnt_ref, run_sc, *, n_real, n_steps):
    i = pl.program_id(0)

    @pl.when(i == 0)
    def _():
        run_sc[...] = cnt0_ref[...]

    if n_steps > n_real:
        @pl.when(i >= n_real)
        def _():
            h2_ref[...] = jnp.zeros_like(h2_ref)

        pl.when(i < n_real)(functools.partial(
            _post_tile, x_ref, y5_ref, hm_ref, gg_ref, g1_ref, sc_ref, sh_ref, n2_ref, wglu_ref, wmo_ref, wo_ref,
            rw_ref, rb_ref, x1_ref, h2_ref, eid_ref, wgt_ref, rank_ref, run_sc))
    else:
        _post_tile(x_ref, y5_ref, hm_ref, gg_ref, g1_ref, sc_ref, sh_ref, n2_ref, wglu_ref, wmo_ref, wo_ref,
                   rw_ref, rb_ref, x1_ref, h2_ref, eid_ref, wgt_ref, rank_ref, run_sc)
    cnt_ref[...] = run_sc[...]


def _post_tile(x_ref, y5_ref, hm_ref, gg_ref, g1_ref, sc_ref, sh_ref, n2_ref, wglu_ref, wmo_ref, wo_ref, rw_ref, rb_ref,
               x1_ref, h2_ref, eid_ref, wgt_ref, rank_ref, run_sc):
    g5 = _gelu_tanh(y5_ref[...]).astype(BF16)
    glu = _dot(g5, wglu_ref[...])
    br_s5 = glu[:, :D_MODEL] * _sigmoid(glu[:, D_MODEL:])
    br_m = _dot(hm_ref[...].astype(BF16), wmo_ref[...])
    gg = gg_ref[...].astype(F32)
    merged = _sigmoid(gg[:, :D_MODEL]) * br_s5 + _sigmoid(gg[:, D_MODEL:]) * br_m
    x1 = x_ref[...] + g1_ref[...] * _dot(merged.astype(BF16), wo_ref[...])
    x1_ref[...] = x1
    ms = jnp.mean(x1 * x1, axis=-1, keepdims=True)
    h2 = x1 * lax.rsqrt(ms + EPS) * n2_ref[...] * (1.0 + sc_ref[...]) + sh_ref[...]
    h2_ref[...] = h2
    logits = _dot_hp(h2, rw_ref[...]) + rb_ref[...]
    tm = logits.shape[0]
    lane = lax.broadcasted_iota(jnp.int32, logits.shape, 1)
    l = jnp.where(lane < N_EXPERTS, logits, -jnp.inf)
    vals, hots, idxs = [], [], []
    for _ in range(TOP_K):
        mx = jnp.max(l, axis=-1, keepdims=True)
        idx = jnp.min(jnp.where(l == mx, lane, LANES), axis=-1, keepdims=True)
        hot = lane == idx
        vals.append(mx)
        hots.append(hot)
        idxs.append(idx)
        l = jnp.where(hot, -jnp.inf, l)
    ex = [jnp.exp(vk - vals[0]) for vk in vals]
    tot = ex[0] + ex[1] + ex[2] + ex[3]
    chosen = jnp.where(hots[0] | hots[1] | hots[2] | hots[3], 1.0, 0.0)
    r_io = lax.broadcasted_iota(jnp.int32, (tm, tm), 0)
    c_io = lax.broadcasted_iota(jnp.int32, (tm, tm), 1)
    earlier = (c_io < r_io).astype(BF16)
    before = run_sc[...] + _dot(earlier, chosen.astype(BF16))
    eid = jnp.zeros(logits.shape, jnp.int32)
    wgt = jnp.zeros(logits.shape, F32)
    rank = jnp.zeros(logits.shape, F32)
    for k in range(TOP_K):
        eid = jnp.where(lane == k, idxs[k], eid)
        wgt = jnp.where(lane == k, ex[k] / tot, wgt)
        rank = jnp.where(lane == k, jnp.sum(jnp.where(hots[k], before, 0.0), axis=-1, keepdims=True), rank)
    eid_ref[...] = eid
    wgt_ref[...] = wgt
    rank_ref[...] = rank.astype(jnp.int32)
    run_sc[...] += jnp.sum(chosen, axis=0, keepdims=True)


def _post(x2d, y5, hm, gg, mod3, rows_per_mod, tm, n2, wglu, wmo, wo, rw, rb, cnt0, h2_rows):
    t = x2d.shape[0]
    mrows = mod3.shape[1]
    n_real = t // tm
    n_steps = -(-h2_rows // tm)
    last = n_real - 1

    def mod_spec(j):
        return pl.BlockSpec((None, mrows, D_MODEL), lambda i: ((jnp.minimum(i, last) * tm) // rows_per_mod, 0, j))

    const = lambda shape: pl.BlockSpec(shape, lambda i: (0,) * len(shape))
    tile = lambda w: pl.BlockSpec((tm, w), lambda i: (jnp.minimum(i, last), 0))
    return pl.pallas_call(
        functools.partial(_post_body, n_real=n_real, n_steps=n_steps),
        grid=(n_steps,),
        in_specs=[tile(D_MODEL), tile(S5_WIDTH), tile(MWIDTH), tile(2 * D_MODEL),
                  mod_spec(2), mod_spec(4), mod_spec(3), const((1, D_MODEL)),
                  const(wglu.shape), const(wmo.shape), const(wo.shape), const(rw.shape), const(rb.shape),
                  const((1, LANES))],
        out_specs=[tile(D_MODEL), pl.BlockSpec((tm, D_MODEL), lambda i: (i, 0)), tile(LANES), tile(LANES), tile(LANES),
                   const((1, LANES))],
        out_shape=[jax.ShapeDtypeStruct((t, D_MODEL), F32),
                   jax.ShapeDtypeStruct((h2_rows, D_MODEL), F32),
                   jax.ShapeDtypeStruct((t, LANES), jnp.int32),
                   jax.ShapeDtypeStruct((t, LANES), F32),
                   jax.ShapeDtypeStruct((t, LANES), jnp.int32),
                   jax.ShapeDtypeStruct((1, LANES), F32)],
        scratch_shapes=[pltpu.VMEM((1, LANES), F32)],
        compiler_params=_cparams("arbitrary"),
        name="post",
    )(x2d, y5, hm, gg, mod3, mod3, mod3, n2, wglu, wmo, wo, rw, rb, cnt0)


N_TOKENS = N_PROMPT * SEQ + N_SAMPLE
N_ASSIGN = N_TOKENS * TOP_K
MOE_TILE = 256
MOE_TILES = N_ASSIGN // MOE_TILE + N_EXPERTS
N_SLOT = MOE_TILES * MOE_TILE
CMB_TILE = 512


MOE_RING = 3
MOE_SHORT = 64


def _moe_body(te_ref, nt_ref, nxe_ref, gpar_ref, rows_ref, idx0_ref, idx1_ref, idx2_ref, h2_hbm, wup_hbm, bup_ref,
              wdn_hbm, bdn_ref, y_ref, xb0, xb1, xb2, sem, wbuf_up, wbuf_dn, wsem, wup_bf, wdn_bf):
    i = pl.program_id(0)
    nt = nt_ref[0]
    bufs = (xb0, xb1, xb2)
    last = MOE_TILES - 1

    def gather(idx_ref, slot):
        for r in range(MOE_TILE):
            pltpu.make_async_copy(h2_hbm.at[pl.ds(idx_ref[0, r], 1)], bufs[slot].at[pl.ds(r, 1)],
                                  sem.at[slot]).start()

    def wait(slot):
        pltpu.make_async_copy(h2_hbm.at[pl.ds(0, MOE_TILE)], bufs[slot], sem.at[slot]).wait()

    def weights(e, slot):
        return (pltpu.make_async_copy(wup_hbm.at[e], wbuf_up.at[slot], wsem.at[0, slot]),
                pltpu.make_async_copy(wdn_hbm.at[e], wbuf_dn.at[slot], wsem.at[1, slot]))

    wslot = gpar_ref[i]

    @pl.when(i == 0)
    def _():
        for c in weights(te_ref[0], wslot):
            c.start(priority=1)
        gather(idx0_ref, 0)
        gather(idx1_ref, 1)

    new_expert = jnp.logical_or(i == 0, te_ref[i] != te_ref[jnp.maximum(i - 1, 0)])
    live = i < nt

    @pl.when(jnp.logical_and(new_expert, live))
    def _():
        for c in weights(0, wslot):
            c.wait()
        wup_bf[...] = wbuf_up[wslot].astype(BF16)
        wdn_bf[...] = wbuf_dn[wslot].astype(BF16)
        nxe = nxe_ref[i]

        @pl.when(nxe >= 0)
        def _():
            for c in weights(nxe, 1 - wslot):
                c.start(priority=1)

    def step(slot, rows):
        wait(slot)
        gather(idx2_ref, (slot + 2) % MOE_RING)
        gu = _dot(bufs[slot][:rows, :].astype(BF16), wup_bf[...]) + bup_ref[...]
        g = jnp.minimum(gu[:, :D_FF], SWIGLU_LIMIT)
        up = jnp.clip(gu[:, D_FF:], -SWIGLU_LIMIT, SWIGLU_LIMIT)
        act = (up + 1.0) * g * _sigmoid(SWIGLU_ALPHA * g)
        y_ref[:rows, :] = _dot(act.astype(BF16), wdn_bf[...]) + bdn_ref[...]
        if rows < MOE_TILE:
            y_ref[rows:, :] = jnp.zeros((MOE_TILE - rows, D_MODEL), F32)

    short = rows_ref[i] <= MOE_SHORT
    for slot in range(MOE_RING):
        mine = jnp.logical_and(live, i % MOE_RING == slot)
        pl.when(jnp.logical_and(mine, jnp.logical_not(short)))(functools.partial(step, slot, MOE_TILE))
        pl.when(jnp.logical_and(mine, short))(functools.partial(step, slot, MOE_SHORT))

    @pl.when(jnp.logical_not(live))
    def _():
        y_ref[...] = jnp.zeros_like(y_ref)

    drain = jnp.logical_and(jnp.logical_not(live), i < nt + 2)
    for slot in range(MOE_RING):
        pl.when(jnp.logical_and(drain, i % MOE_RING == slot))(functools.partial(wait, slot))
    pl.when(jnp.logical_and(i == last, last - 1 < nt))(functools.partial(wait, (last + 1) % MOE_RING))
    pl.when(jnp.logical_and(i == last, last < nt))(functools.partial(wait, (last + 2) % MOE_RING))


def _moe(tile_expert, n_tiles, next_expert, group_parity, tile_rows, slot_token, h2_all, wup, bup, wdn, bdn):
    idx3 = slot_token.reshape(MOE_TILES, 1, MOE_TILE)
    ahead = lambda k: pl.BlockSpec((None, 1, MOE_TILE),
                                   lambda i, *_: (jnp.minimum(i + k, MOE_TILES - 1), 0, 0), memory_space=pltpu.SMEM)
    grid_spec = pltpu.PrefetchScalarGridSpec(
        num_scalar_prefetch=5,
        grid=(MOE_TILES,),
        in_specs=[ahead(0), ahead(1), ahead(2),
                  pl.BlockSpec(memory_space=pl.ANY),
                  pl.BlockSpec(memory_space=pl.ANY),
                  pl.BlockSpec((None, 1, 2 * D_FF), lambda i, te, *_: (te[i], 0, 0)),
                  pl.BlockSpec(memory_space=pl.ANY),
                  pl.BlockSpec((None, 1, D_MODEL), lambda i, te, *_: (te[i], 0, 0))],
        out_specs=pl.BlockSpec((MOE_TILE, D_MODEL), lambda i, *_: (i, 0)),
        scratch_shapes=[pltpu.VMEM((MOE_TILE, D_MODEL), F32), pltpu.VMEM((MOE_TILE, D_MODEL), F32),
                        pltpu.VMEM((MOE_TILE, D_MODEL), F32),
                        pltpu.SemaphoreType.DMA((MOE_RING,)),
                        pltpu.VMEM((2, D_MODEL, 2 * D_FF), F32), pltpu.VMEM((2, D_FF, D_MODEL), F32),
                        pltpu.SemaphoreType.DMA((2, 2)),
                        pltpu.VMEM((D_MODEL, 2 * D_FF), BF16), pltpu.VMEM((D_FF, D_MODEL), BF16)])
    return pl.pallas_call(
        _moe_body,
        grid_spec=grid_spec,
        out_shape=jax.ShapeDtypeStruct((N_SLOT, D_MODEL), F32),
        compiler_params=_cparams("arbitrary"),
        name="moe",
    )(tile_expert, n_tiles, next_expert, group_parity, tile_rows, idx3, idx3, idx3, h2_all, wup, bup, wdn, bdn)


def _combine_body(cur_ref, nxt_ref, wgt_ref, x1_ref, g2_ref, fg_ref, y_hbm, o_ref, yb0, yb1, sem, *, n_steps, tm):
    i = pl.program_id(0)

    def gather(idx_ref, buf, s):
        for r in range(tm):
            for k in range(TOP_K):
                pltpu.make_async_copy(y_hbm.at[pl.ds(idx_ref[0, r * TOP_K + k], 1)],
                                      buf.at[pl.ds(k * tm + r, 1)], sem.at[s]).start(priority=k % 2)

    def wait(buf, s):
        pltpu.make_async_copy(y_hbm.at[pl.ds(0, TOP_K * tm)], buf, sem.at[s]).wait()

    @pl.when(i == 0)
    def _():
        gather(cur_ref, yb0, 0)

    def step(cur, nxt, s_cur, s_nxt):
        wait(cur, s_cur)
        gather(nxt_ref, nxt, s_nxt)
        wgt = wgt_ref[...]
        acc = wgt[:, 0:1] * cur[0:tm, :]
        for k in range(1, TOP_K):
            acc = acc + wgt[:, k:k + 1] * cur[k * tm:(k + 1) * tm, :]
        xo = x1_ref[...] + g2_ref[...] * acc
        ms = jnp.mean(xo * xo, axis=-1, keepdims=True)
        o_ref[...] = xo * lax.rsqrt(ms + EPS) * fg_ref[...]

    pl.when(i % 2 == 0)(functools.partial(step, yb0, yb1, 0, 1))
    pl.when(i % 2 == 1)(functools.partial(step, yb1, yb0, 1, 0))
    last_slot = n_steps % 2
    pl.when(i == n_steps - 1)(functools.partial(wait, yb1 if last_slot else yb0, last_slot))


def _combine(pos, wgt, x1, mod3, rows_per_mod, fg, y_slots, tm):
    t = x1.shape[0]
    n = t // tm
    mrows = mod3.shape[1]
    assert mrows in (1, tm) and rows_per_mod % tm == 0
    pos3 = pos.reshape(n, 1, tm * TOP_K)
    return pl.pallas_call(
        functools.partial(_combine_body, n_steps=n, tm=tm),
        grid=(n,),
        in_specs=[pl.BlockSpec((None, 1, tm * TOP_K), lambda i: (i, 0, 0), memory_space=pltpu.SMEM),
                  pl.BlockSpec((None, 1, tm * TOP_K), lambda i: (jnp.minimum(i + 1, n - 1), 0, 0),
                               memory_space=pltpu.SMEM),
                  pl.BlockSpec((tm, LANES), lambda i: (i, 0)),
                  pl.BlockSpec((tm, D_MODEL), lambda i: (i, 0)),
                  pl.BlockSpec((None, mrows, D_MODEL), lambda i: ((i * tm) // rows_per_mod, 0, 5)),
                  pl.BlockSpec((1, D_MODEL), lambda i: (0, 0)),
                  pl.BlockSpec(memory_space=pl.ANY)],
        out_specs=pl.BlockSpec((tm, D_MODEL), lambda i: (i, 0)),
        out_shape=jax.ShapeDtypeStruct((t, D_MODEL), F32),
        scratch_shapes=[pltpu.VMEM((TOP_K * tm, D_MODEL), F32), pltpu.VMEM((TOP_K * tm, D_MODEL), F32),
                        pltpu.SemaphoreType.DMA((2,))],
        compiler_params=_cparams("arbitrary"),
        name="combine",
    )(pos3, pos3, wgt, x1, mod3, fg, y_slots)


def _routing(eid_p, rank_p, eid_s, rank_s, counts):
    cnt = counts[0, :N_EXPERTS].astype(jnp.int32)
    ntile = (cnt + MOE_TILE - 1) // MOE_TILE
    tile_end = jnp.cumsum(ntile)
    poff = (tile_end - ntile) * MOE_TILE
    total = tile_end[-1]
    j = jnp.arange(MOE_TILES, dtype=jnp.int32)
    te = jnp.sum((j[:, None] >= tile_end[None, :]).astype(jnp.int32), axis=1)
    te_last = jnp.sum(((total - 1) >= tile_end).astype(jnp.int32))
    tile_expert = jnp.where(j < total, te, te_last).astype(jnp.int32)
    ex = jnp.arange(N_EXPERTS, dtype=jnp.int32)
    used = ntile > 0
    later = jnp.where(used[None, :] & (ex[None, :] > ex[:, None]), ex[None, :], N_EXPERTS)
    nxt_e = jnp.min(later, axis=1)
    nxt_e = jnp.where(nxt_e < N_EXPERTS, nxt_e, -1).astype(jnp.int32)
    par_e = ((jnp.cumsum(used.astype(jnp.int32)) - 1) % 2).astype(jnp.int32)
    pick = tile_expert[:, None] == ex[None, :]
    next_expert = jnp.sum(jnp.where(pick, nxt_e[None, :], 0), axis=1).astype(jnp.int32)
    group_parity = jnp.sum(jnp.where(pick, par_e[None, :], 0), axis=1).astype(jnp.int32)
    left = jnp.sum(jnp.where(pick, (cnt + poff)[None, :], 0), axis=1) - j * MOE_TILE
    tile_rows = jnp.where(j < total, jnp.clip(left, 0, MOE_TILE), 0).astype(jnp.int32)

    def pos_of(eid, rank):
        e = eid[:, :TOP_K]
        off = jnp.sum(jnp.where(e[:, :, None] == ex[None, None, :], poff[None, None, :], 0), axis=-1)
        return off + rank[:, :TOP_K]

    pos_p = pos_of(eid_p, rank_p)
    pos_s = pos_of(eid_s, rank_s)
    tok = jnp.concatenate([jnp.repeat(jnp.arange(N_PROMPT * SEQ, dtype=jnp.int32), TOP_K),
                           jnp.repeat(N_PROMPT * SEQ + jnp.arange(N_SAMPLE, dtype=jnp.int32), TOP_K)])
    pos_all = jnp.concatenate([pos_p.reshape(-1), pos_s.reshape(-1)])
    padc = jnp.concatenate([ntile * MOE_TILE - cnt, (N_SLOT - total * MOE_TILE).reshape(1)])
    pad_end = jnp.cumsum(padc)
    pad_base = jnp.concatenate([poff + cnt, (total * MOE_TILE).reshape(1)])
    jp = jnp.arange(N_SLOT - N_ASSIGN, dtype=jnp.int32)
    pe = jnp.sum((jp[:, None] >= pad_end[None, :]).astype(jnp.int32), axis=1)
    hit = pe[:, None] == jnp.arange(N_EXPERTS + 1, dtype=jnp.int32)[None, :]
    pad_slot = jp + jnp.sum(jnp.where(hit, (pad_base - (pad_end - padc))[None, :], 0), axis=1)
    keys = jnp.concatenate([pos_all, pad_slot.astype(jnp.int32)])
    vals = jnp.concatenate([tok, jnp.zeros((N_SLOT - N_ASSIGN,), jnp.int32)])
    tok_bits = 15
    assert N_TOKENS <= (1 << tok_bits) and N_SLOT <= (1 << (32 - tok_bits))
    packed = lax.sort((keys.astype(jnp.uint32) << tok_bits) | vals.astype(jnp.uint32))
    slot_token = (packed & ((1 << tok_bits) - 1)).astype(jnp.int32)
    return (tile_expert, total.reshape(1).astype(jnp.int32), next_expert, group_parity, tile_rows, slot_token,
            pos_p, pos_s)


def _unpack_s5_state(x):
    n = x.shape[1]
    z = x.reshape(N_SUPER, n, 2, S5_SUPER, S5_STATE).transpose(2, 1, 0, 3, 4).reshape(2, n, S5_GROUPS, S5_STATE)
    return z[0], z[1]


def _pack_s5_state(re, im):
    n = re.shape[0]
    z = jnp.stack([re, im], axis=0).reshape(2, n, N_SUPER, S5_SUPER, S5_STATE)
    return z.transpose(2, 1, 0, 3, 4).reshape(N_SUPER, n, SUPER_STATE)


def kernel(x_prompt, x_sample, c_prompt, c_sample, state_s5_re, state_s5_im, state_mlstm_C, state_mlstm_n, state_mlstm_m, norm1_g, norm2_g, final_norm_g, w_ada, b_ada, w_in, s5_lambda_re, s5_lambda_im, s5_log_dt, s5_B_re, s5_B_im, s5_C_re, s5_C_im, s5_D, s5_w_glu, mlstm_b_i, mlstm_b_f, mlstm_norm_g, mlstm_w_out, w_out, router_w, router_b, expert_w_up, expert_b_up, expert_w_down, expert_b_down):
    assert w_in.shape[0] == 1, "single layer"
    tp = N_PROMPT * SEQ
    xp = x_prompt.reshape(tp, D_MODEL).astype(F32)
    xs = x_sample.reshape(N_SAMPLE, D_MODEL).astype(F32)

    w = w_in[0]
    c0 = S5_WIDTH
    c1 = c0 + 4 * MWIDTH
    c2 = c1 + 2 * HEADS
    wu = w[:, :c0].astype(BF16)
    wq = w[:, c0:c1].astype(BF16)
    wif = jnp.pad(w[:, c1:c2].astype(F32), ((0, 0), (0, LANES - 2 * HEADS)))
    wg = w[:, c2:].astype(BF16)
    bvec = jnp.concatenate([mlstm_b_i[0], mlstm_b_f[0]]).astype(F32)
    bif = jnp.pad(bvec, (0, LANES - 2 * HEADS)).reshape(1, LANES)
    wparts = (wu, wq, wg, wif, bif)
    g1 = norm1_g[0].reshape(1, D_MODEL).astype(F32)
    n2 = norm2_g[0].reshape(1, D_MODEL).astype(F32)
    fg = final_norm_g.reshape(1, D_MODEL).astype(F32)
    mg = mlstm_norm_g[0].reshape(1, MWIDTH).astype(F32)
    wglu = s5_w_glu[0].astype(BF16)
    wmo = mlstm_w_out[0].astype(BF16)
    wo = w_out[0].astype(BF16)
    rw = jnp.pad(router_w[0].astype(F32), ((0, 0), (0, LANES - N_EXPERTS)))
    rb = jnp.pad(router_b[0].astype(F32), (0, LANES - N_EXPERTS)).reshape(1, LANES)
    wup = expert_w_up[0].astype(F32)
    wdn = expert_w_down[0].astype(F32)
    bup = expert_b_up[0].astype(F32).reshape(N_EXPERTS, 1, 2 * D_FF)
    bdn = expert_b_down[0].astype(F32).reshape(N_EXPERTS, 1, D_MODEL)
    ddiag, fcomp, ecomp, a16, a1 = _s5_tables(
        s5_lambda_re[0], s5_lambda_im[0], s5_log_dt[0], s5_B_re[0], s5_B_im[0], s5_C_re[0], s5_C_im[0], s5_D[0])

    c_all = jnp.concatenate([c_prompt, c_sample], axis=0).astype(F32)
    mod = _adaln(c_all, w_ada[0].astype(F32), b_ada[0].astype(F32))
    mod_p = mod[:N_PROMPT].reshape(N_PROMPT, 1, 6 * D_MODEL)
    mod_s = mod[N_PROMPT:].reshape(1, N_SAMPLE, 6 * D_MODEL)

    u, qkvo, gg, gcol, grow = _inproj(xp, mod_p, SEQ, 1024, g1, wparts)
    y5, xend, k0, e0, f1 = _s5_prompt(u, ddiag, fcomp, ecomp, a16, a1)
    hm, c_p, n_p, m_p = _mlstm_prompt(qkvo, gcol, grow, mg)
    cnt0 = jnp.zeros((1, LANES), F32)
    x1, h2_all, eid_p, wgt_p, rank_p, cnt_p = _post(xp, y5, hm, gg, mod_p, SEQ, 512, n2, wglu, wmo, wo, rw, rb,
                                                    cnt0, N_TOKENS)
    p_re, p_im = _unpack_s5_state(xend.reshape(N_SUPER, N_PROMPT, SUPER_STATE))
    n_p = n_p[..., 0]
    m_p = m_p[:, :HEADS, 0]

    us, qs, ggs, gcs, _ = _inproj(xs, mod_s, N_SAMPLE, N_SAMPLE, g1, wparts)
    x0 = _pack_s5_state(state_s5_re[0].astype(F32), state_s5_im[0].astype(F32))
    y5s, xns = _s5_step(us, x0, k0, e0, f1, a1)
    s_re, s_im = _unpack_s5_state(xns)
    qkt = qs[:, :2 * MWIDTH].astype(F32).reshape(N_SAMPLE // MSTEP_TOK, MSTEP_TOK, 2 * HEADS, DH).transpose(0, 2, 3, 1)
    pad_heads = lambda a: jnp.pad(a, ((0, 0), (0, LANES - HEADS)))
    m0 = pad_heads(state_mlstm_m[0].astype(F32))
    hms, c_s, n_s, m_s = _mlstm_step(qs, qkt, pad_heads(gcs[:, :HEADS]), pad_heads(gcs[:, HEADS:2 * HEADS]), m0,
                                     state_mlstm_C[0].astype(F32),
                                     state_mlstm_n[0].astype(F32).reshape(N_SAMPLE, MWIDTH), mg)
    x1s, h2s, eid_s, wgt_s, rank_s, cnt_all = _post(xs, y5s, hms, ggs, mod_s, N_SAMPLE, N_SAMPLE, n2, wglu, wmo, wo, rw, rb,
                                                    cnt_p, N_SAMPLE)

    h2_all = lax.dynamic_update_slice(h2_all, h2s, (tp, 0))
    tile_expert, n_tiles, next_expert, group_parity, tile_rows, slot_token, pos_p, pos_s = _routing(
        eid_p, rank_p, eid_s, rank_s, cnt_all)
    y_slots = _moe(tile_expert, n_tiles, next_expert, group_parity, tile_rows, slot_token, h2_all, wup, bup, wdn, bdn)
    y_p = _combine(pos_p, wgt_p, x1, mod_p, SEQ, fg, y_slots, CMB_TILE)
    y_s = _combine(pos_s, wgt_s, x1s, mod_s, N_SAMPLE, fg, y_slots, N_SAMPLE)

    return (y_p.reshape(N_PROMPT, SEQ, D_MODEL).astype(x_prompt.dtype),
            y_s.reshape(N_SAMPLE, 1, D_MODEL).astype(x_sample.dtype),
            p_re[None], p_im[None], c_p[None], n_p[None], m_p[None],
            s_re[None], s_im[None], c_s[None],
            n_s.reshape(1, N_SAMPLE, HEADS, DH), m_s[:, :HEADS][None])
```

```python
import functools
import math

import jax
import jax.numpy as jnp
from jax import lax
from jax.experimental import pallas as pl
from jax.experimental.pallas import tpu as pltpu

F32 = jnp.float32
BF16 = jnp.bfloat16

D_MODEL = 1024
SEQ = 2048
N_PROMPT = 8
N_SAMPLE = 128
S5_WIDTH = 512
S5_GROUP = 16
S5_GROUPS = 32
S5_STATE = 64
HEADS = 4
DH = 128
MWIDTH = HEADS * DH
N_EXPERTS = 32
TOP_K = 4
D_FF = 1024
SWIGLU_LIMIT = 7.0
SWIGLU_ALPHA = 1.702
EPS = 1e-6

LANES = 128
S5_CHUNK = 16
S5_SUPER = LANES // S5_GROUP
N_SUPER = S5_GROUPS // S5_SUPER
SUPER_STATE = 2 * S5_SUPER * S5_STATE
MCHUNK = 128
VMEM_LIMIT = 56 * 1024 * 1024


def _cparams(*sem):
    return pltpu.CompilerParams(dimension_semantics=sem, vmem_limit_bytes=VMEM_LIMIT)


def _dot(a, b):
    return jnp.dot(a, b, preferred_element_type=F32)


def _dot_dims(a, b, dims):
    return lax.dot_general(a, b, (dims, ((), ())), preferred_element_type=F32)


def _split_bf16(a):
    hi = a.astype(BF16)
    lo = (a - hi.astype(F32)).astype(BF16)
    return hi, lo


def _dot_hp(a, b, dims=((1,), (0,))):
    ah, al = _split_bf16(a)
    bh, bl = _split_bf16(b)
    return _dot_dims(ah, bh, dims) + (_dot_dims(al, bh, dims) + _dot_dims(ah, bl, dims))


def _dot_exact_rhs(a, b_exact, dims=((1,), (0,))):
    ah, al = _split_bf16(a)
    return _dot_dims(ah, b_exact, dims) + _dot_dims(al, b_exact, dims)


def _log_sigmoid(x):
    return -(jnp.maximum(-x, 0.0) + jnp.log1p(jnp.exp(-jnp.abs(x))))


def _sigmoid(x):
    return 0.5 * (1.0 + jnp.tanh(0.5 * x))


def _gelu_tanh(x):
    c = math.sqrt(2.0 / math.pi)
    return 0.5 * x * (1.0 + jnp.tanh(c * (x + 0.044715 * (x * x * x))))


def _adaln_body(c_ref, w_ref, b_ref, o_ref):
    c = c_ref[...]
    s = c * _sigmoid(c)
    o_ref[...] = _dot_hp(s, w_ref[...]) + b_ref[...]


def _adaln(c_all, w_ada, b_ada):
    n = c_all.shape[0]
    tn = 1024
    return pl.pallas_call(
        _adaln_body,
        grid=(6 * D_MODEL // tn,),
        in_specs=[pl.BlockSpec((n, D_MODEL), lambda j: (0, 0)),
                  pl.BlockSpec((D_MODEL, tn), lambda j: (0, j)),
                  pl.BlockSpec((1, tn), lambda j: (0, j))],
        out_specs=pl.BlockSpec((n, tn), lambda j: (0, j)),
        out_shape=jax.ShapeDtypeStruct((n, 6 * D_MODEL), F32),
        compiler_params=_cparams("parallel"),
        name="adaln",
    )(c_all, w_ada, b_ada.reshape(1, -1))


def _inproj_body(x_ref, g1_ref, sc_ref, sh_ref, wu_ref, wq_ref, wg_ref, wif_ref, bif_ref,
                 u_ref, q_ref, gg_ref, if_ref, ift_ref):
    x = x_ref[...]
    ms = jnp.mean(x * x, axis=-1, keepdims=True)
    h = x * lax.rsqrt(ms + EPS) * g1_ref[...] * (1.0 + sc_ref[...]) + sh_ref[...]
    hb = h.astype(BF16)
    u_ref[...] = _dot(hb, wu_ref[...])
    q_ref[...] = _dot(hb, wq_ref[...]).astype(BF16)
    gg_ref[...] = _dot(hb, wg_ref[...]).astype(BF16)
    gc = _dot_hp(h, wif_ref[...]) + bif_ref[...]
    lane = lax.broadcasted_iota(jnp.int32, gc.shape, 1)
    gates = jnp.where(lane < HEADS, gc, _log_sigmoid(gc))
    if_ref[...] = gates
    ift_ref[...] = gates.T[:2 * HEADS, :]


def _inproj(x2d, mod3, rows_per_mod, tm, g1, wparts):
    t = x2d.shape[0]
    wu, wq, wg, wif, bif = wparts
    mrows = mod3.shape[1]

    def mod_spec(j):
        return pl.BlockSpec((None, mrows, D_MODEL), lambda i: ((i * tm) // rows_per_mod, 0, j))

    const = lambda shape: pl.BlockSpec(shape, lambda i: (0,) * len(shape), pipeline_mode=pl.Buffered(1))
    return pl.pallas_call(
        _inproj_body,
        grid=(t // tm,),
        in_specs=[pl.BlockSpec((tm, D_MODEL), lambda i: (i, 0)),
                  const((1, D_MODEL)), mod_spec(1), mod_spec(0),
                  const(wu.shape), const(wq.shape), const(wg.shape), const(wif.shape), const(bif.shape)],
        out_specs=[pl.BlockSpec((tm, S5_WIDTH), lambda i: (i, 0)),
                   pl.BlockSpec((tm, 4 * MWIDTH), lambda i: (i, 0)),
                   pl.BlockSpec((tm, 2 * D_MODEL), lambda i: (i, 0)),
                   pl.BlockSpec((tm, LANES), lambda i: (i, 0)),
                   pl.BlockSpec((8, tm), lambda i: (0, i))],
        out_shape=[jax.ShapeDtypeStruct((t, S5_WIDTH), F32),
                   jax.ShapeDtypeStruct((t, 4 * MWIDTH), BF16),
                   jax.ShapeDtypeStruct((t, 2 * D_MODEL), BF16),
                   jax.ShapeDtypeStruct((t, LANES), F32),
                   jax.ShapeDtypeStruct((8, t), F32)],
        compiler_params=_cparams("parallel"),
        name="inproj",
    )(x2d, g1, mod3, mod3, wu, wq, wg, wif, bif)


def _s5_tables(lam_re, lam_im, log_dt, b_re, b_im, c_re, c_im, d_s5):
    hi = lax.Precision.HIGHEST
    dt = jnp.exp(log_dt.astype(F32))[:, None]
    lr, li = lam_re.astype(F32), lam_im.astype(F32)
    dpow = jnp.arange(S5_CHUNK + 1, dtype=F32)[:, None, None]
    mag = jnp.exp(dpow * (lr * dt))
    pw_re, pw_im = mag * jnp.cos(dpow * (li * dt)), mag * jnp.sin(dpow * (li * dt))
    ab_re, ab_im = pw_re[1], pw_im[1]
    den = lr * lr + li * li
    nr, ni = ab_re - 1.0, ab_im
    coef_re = (nr * lr + ni * li) / den
    coef_im = (ni * lr - nr * li) / den
    br, bi = b_re.astype(F32), b_im.astype(F32)
    bb_re = coef_re[..., None] * br - coef_im[..., None] * bi
    bb_im = coef_re[..., None] * bi + coef_im[..., None] * br
    cr, ci = c_re.astype(F32), c_im.astype(F32)
    cl_re = cr[None] * pw_re[:, :, None, :] - ci[None] * pw_im[:, :, None, :]
    cl_im = cr[None] * pw_im[:, :, None, :] + ci[None] * pw_re[:, :, None, :]
    ddiag = (d_s5.astype(F32).reshape(N_SUPER, LANES, 1) * jnp.eye(LANES, dtype=F32)[None])
    rp_re, rp_im = pw_re[S5_CHUNK - 1::-1], pw_im[S5_CHUNK - 1::-1]
    f_re = rp_re[..., None] * bb_re[None] - rp_im[..., None] * bb_im[None]
    f_im = rp_re[..., None] * bb_im[None] + rp_im[..., None] * bb_re[None]
    fcat = jnp.stack([f_re, f_im], axis=2).transpose(0, 1, 4, 2, 3)
    fcomp = fcat.reshape(S5_CHUNK, N_SUPER, LANES, 2 * S5_STATE).transpose(1, 0, 2, 3)
    fcomp = fcomp.reshape(N_SUPER, S5_CHUNK * LANES, 2 * S5_STATE)
    ecat = jnp.stack([cl_re[:S5_CHUNK], -cl_im[:S5_CHUNK]], axis=0)
    er = ecat.transpose(2, 0, 4, 1, 3).reshape(N_SUPER, S5_SUPER, 2, S5_STATE, S5_CHUNK * S5_GROUP)
    ecomp = er.transpose(0, 2, 1, 3, 4).reshape(N_SUPER, SUPER_STATE, S5_CHUNK * S5_GROUP)

    def lay(re, im):
        z = jnp.stack([re, im], axis=0).reshape(2, N_SUPER, S5_SUPER * S5_STATE)
        return z.transpose(1, 0, 2).reshape(N_SUPER, 1, SUPER_STATE)

    a16 = lay(pw_re[S5_CHUNK], pw_im[S5_CHUNK])
    a1 = lay(ab_re, ab_im)
    return ddiag, fcomp, ecomp, a16, a1


def _expand_groups(comp, row_div, col_blk, n_cols, col_div):
    r, c = comp.shape
    ci = lax.broadcasted_iota(jnp.int32, (c, n_cols), 0)
    co = lax.broadcasted_iota(jnp.int32, (c, n_cols), 1)
    sel = jnp.logical_and(ci // col_blk == co // (S5_SUPER * col_blk), ci % col_blk == co % col_blk)
    rep = _dot(comp.astype(BF16), sel.astype(BF16))
    ro = lax.broadcasted_iota(jnp.int32, (r, n_cols), 0)
    cc = lax.broadcasted_iota(jnp.int32, (r, n_cols), 1)
    same = (ro // row_div) % S5_SUPER == (cc // col_div) % S5_SUPER
    return jnp.where(same, rep, 0.0).astype(BF16)


S5_NB = 4


def _s5_prompt_body(u_ref, dd_ref, fc_ref, ec_ref, a_ref, a1_ref, y_ref, xe_ref, k0_ref, e0_ref, f1_ref,
                    ucat, v_sc, xp_sc, kp_ref, f_ref, e_ref):
    n = SEQ // S5_CHUNK
    half = SUPER_STATE // 2

    @pl.when(pl.program_id(1) == 0)
    def _():
        f_ref[...] = _expand_groups(fc_ref[...], S5_GROUP, S5_STATE, SUPER_STATE, S5_STATE)
        e_ref[...] = _expand_groups(ec_ref[...], S5_STATE, S5_GROUP, S5_CHUNK * LANES, S5_GROUP)
        f1 = f_ref[(S5_CHUNK - 1) * LANES:, :]
        kall = _dot(f1, e_ref[...])
        for r in range(S5_CHUNK):
            for hf in range(2):
                d = S5_CHUNK - 2 - r + hf
                blk = jnp.zeros((LANES, LANES), F32) if d < 0 else kall[:, d * LANES:(d + 1) * LANES]
                if d == 0:
                    blk = blk + dd_ref[...]
                kp_ref[r * LANES:(r + 1) * LANES, hf * LANES:(hf + 1) * LANES] = blk.astype(BF16)
        k0_ref[...] = kp_ref[(S5_CHUNK - 2) * LANES:(S5_CHUNK - 1) * LANES, :LANES]
        e0_ref[...] = e_ref[:, LANES:2 * LANES]
        f1_ref[...] = f1

    for bl in range(S5_NB):
        for j in range(S5_CHUNK):
            ucat[bl * n:(bl + 1) * n, j * LANES:(j + 1) * LANES] = (
                u_ref[pl.ds(bl * SEQ + j, n, stride=S5_CHUNK), :].astype(BF16))
    v_sc[...] = _dot(ucat[...], f_ref[...])
    a_r, a_i = a_ref[:, :half], a_ref[:, half:]
    xr = [jnp.zeros((1, half), F32) for _ in range(S5_NB)]
    xi = [jnp.zeros((1, half), F32) for _ in range(S5_NB)]
    for c in range(n):
        for bl in range(S5_NB):
            row = bl * n + c
            xp_sc[row:row + 1, :half] = xr[bl]
            xp_sc[row:row + 1, half:] = xi[bl]
            vr, vi = v_sc[row:row + 1, :half], v_sc[row:row + 1, half:]
            xr[bl], xi[bl] = a_r * xr[bl] - a_i * xi[bl] + vr, a_r * xi[bl] + a_i * xr[bl] + vi
    for bl in range(S5_NB):
        xe_ref[bl:bl + 1, :half] = xr[bl]
        xe_ref[bl:bl + 1, half:] = xi[bl]
    l_r, l_i = a1_ref[:, :half], a1_ref[:, half:]
    xpr, xpi = xp_sc[:, :half], xp_sc[:, half:]
    z = jnp.concatenate([l_r * xpr - l_i * xpi, l_r * xpi + l_i * xpr], axis=1).astype(BF16)
    ye = _dot(z, e_ref[...])
    for m in range(S5_CHUNK // 2):
        rows = (2 * m + 2) * LANES
        yp = _dot(ucat[:, :rows], kp_ref[(S5_CHUNK - 2 - 2 * m) * LANES:, :]) + ye[:, 2 * m * LANES:(2 * m + 2) * LANES]
        for bl in range(S5_NB):
            y_ref[pl.ds(bl * SEQ + 2 * m, n, stride=S5_CHUNK), :] = yp[bl * n:(bl + 1) * n, :LANES]
            y_ref[pl.ds(bl * SEQ + 2 * m + 1, n, stride=S5_CHUNK), :] = yp[bl * n:(bl + 1) * n, LANES:]


def _s5_prompt(u, ddiag, fcomp, ecomp, a16, a1):
    n = S5_NB * SEQ // S5_CHUNK
    per_super = lambda shape: pl.BlockSpec((None,) + shape, lambda s, b: (s, 0, 0))
    return pl.pallas_call(
        _s5_prompt_body,
        grid=(N_SUPER, N_PROMPT // S5_NB),
        in_specs=[pl.BlockSpec((S5_NB * SEQ, LANES), lambda s, b: (b, s)),
                  per_super(ddiag.shape[1:]), per_super(fcomp.shape[1:]), per_super(ecomp.shape[1:]),
                  per_super((1, SUPER_STATE)), per_super((1, SUPER_STATE))],
        out_specs=[pl.BlockSpec((S5_NB * SEQ, LANES), lambda s, b: (b, s)),
                   pl.BlockSpec((None, None, S5_NB, SUPER_STATE), lambda s, b: (s, b, 0, 0)),
                   per_super((LANES, LANES)), per_super((SUPER_STATE, LANES)), per_super((LANES, SUPER_STATE))],
        out_shape=[jax.ShapeDtypeStruct(u.shape, F32),
                   jax.ShapeDtypeStruct((N_SUPER, N_PROMPT // S5_NB, S5_NB, SUPER_STATE), F32),
                   jax.ShapeDtypeStruct((N_SUPER, LANES, LANES), BF16),
                   jax.ShapeDtypeStruct((N_SUPER, SUPER_STATE, LANES), BF16),
                   jax.ShapeDtypeStruct((N_SUPER, LANES, SUPER_STATE), BF16)],
        scratch_shapes=[pltpu.VMEM((n, S5_CHUNK * LANES), BF16),
                        pltpu.VMEM((n, SUPER_STATE), F32),
                        pltpu.VMEM((n, SUPER_STATE), F32),
                        pltpu.VMEM((S5_CHUNK * LANES, 2 * LANES), BF16),
                        pltpu.VMEM((S5_CHUNK * LANES, SUPER_STATE), BF16),
                        pltpu.VMEM((SUPER_STATE, S5_CHUNK * LANES), BF16)],
        compiler_params=_cparams("parallel", "arbitrary"),
        name="s5_prompt",
    )(u, ddiag, fcomp, ecomp, a16, a1)


def _s5_step_body(u_ref, x0_ref, k0_ref, e0_ref, f1_ref, a_ref, y_ref, xn_ref):
    half = SUPER_STATE // 2
    ub = u_ref[...].astype(BF16)
    x0 = x0_ref[...]
    bu = _dot(ub, f1_ref[...])
    a_r, a_i = a_ref[:, :half], a_ref[:, half:]
    x0r, x0i = x0[:, :half], x0[:, half:]
    xn_ref[:, :half] = a_r * x0r - a_i * x0i + bu[:, :half]
    xn_ref[:, half:] = a_r * x0i + a_i * x0r + bu[:, half:]
    y_ref[...] = _dot(x0.astype(BF16), e0_ref[...]) + _dot(ub, k0_ref[...])


def _s5_step(u, x0, k0, e0, f1, a1):
    t = u.shape[0]
    per = lambda a: pl.BlockSpec((None,) + a.shape[1:], lambda s: (s,) + (0,) * (a.ndim - 1))
    return pl.pallas_call(
        _s5_step_body,
        grid=(N_SUPER,),
        in_specs=[pl.BlockSpec((t, LANES), lambda s: (0, s)), per(x0), per(k0), per(e0), per(f1), per(a1)],
        out_specs=[pl.BlockSpec((t, LANES), lambda s: (0, s)), per(x0)],
        out_shape=[jax.ShapeDtypeStruct(u.shape, F32), jax.ShapeDtypeStruct(x0.shape, F32)],
        compiler_params=_cparams("parallel"),
        name="s5_step",
    )(u, x0, k0, e0, f1, a1)


MSEQ = 2


def _mlstm_prompt_body(q_all, gc_all, *rest):
    gr_refs, (mg_ref, hm_all, c_out, n_out, m_out, c_all, n_all, m_all) = rest[:MSEQ], rest[MSEQ:]
    ci = pl.program_id(1)

    @pl.when(ci == 0)
    def _():
        c_all[...] = jnp.zeros_like(c_all)
        n_all[...] = jnp.zeros_like(n_all)
        m_all[...] = jnp.zeros_like(m_all)

    for sq in range(MSEQ):
        _mlstm_chunk(q_all.at[sq], gc_all.at[sq], gr_refs[sq], mg_ref, hm_all.at[sq],
                     c_all.at[sq], n_all.at[sq], m_all.at[sq])

    @pl.when(ci == pl.num_programs(1) - 1)
    def _():
        c_out[...] = c_all[...]
        n_out[...] = n_all[...]
        m_out[...] = m_all[...]


def _mlstm_chunk(q_ref, gc_ref, gr_ref, mg_ref, hm_ref, c_sc, n_sc, m_sc):
    lc = MCHUNK
    scale = DH ** -0.5
    row = lax.broadcasted_iota(jnp.int32, (lc, lc), 0)
    col = lax.broadcasted_iota(jnp.int32, (lc, lc), 1)
    causal = col <= row
    tri = causal.astype(BF16)
    ones = jnp.ones((lc, DH), BF16)
    gc = gc_ref[...]
    gr = gr_ref[...]
    gc_hi, gc_lo = _split_bf16(gc)
    bcol_all = _dot(tri, gc_hi) + _dot(tri, gc_lo)
    brow_all = _dot_exact_rhs(gr, tri, ((1,), (1,)))
    for hd in range(HEADS):
        q = q_ref[:, hd * DH:(hd + 1) * DH]
        k = q_ref[:, MWIDTH + hd * DH:MWIDTH + (hd + 1) * DH]
        v = q_ref[:, 2 * MWIDTH + hd * DH:2 * MWIDTH + (hd + 1) * DH]
        o = q_ref[:, 3 * MWIDTH + hd * DH:3 * MWIDTH + (hd + 1) * DH]
        i_col = gc[:, hd:hd + 1]
        b_col = bcol_all[:, HEADS + hd:HEADS + hd + 1]
        i_row = gr[hd:hd + 1, :]
        b_row = brow_all[HEADS + hd:HEADS + hd + 1, :]
        m_prev = m_sc[hd:hd + 1, :][:, :1]
        dm = jnp.where(causal, b_col - b_row + i_row, -jnp.inf)
        m_inter = b_col + m_prev
        m_t = jnp.maximum(m_inter, jnp.max(dm, axis=-1, keepdims=True))
        w_inter = jnp.exp(m_inter - m_t)
        w = jnp.exp(dm - m_t)
        s = _dot_dims(q, k, ((1,), (1,))) * scale * w
        sb = s.astype(BF16)
        c_prev = c_sc[hd]
        n_prev = n_sc[hd]
        num = w_inter * _dot(q, c_prev.astype(BF16)) + _dot(sb, v)
        den = w_inter * _dot(q, n_prev.astype(BF16)) + _dot(sb, ones)
        h = num / jnp.maximum(jnp.abs(den), jnp.exp(-m_t))
        hn = h * lax.rsqrt(jnp.mean(h * h, axis=-1, keepdims=True) + EPS)
        hm_ref[:, hd * DH:(hd + 1) * DH] = (hn * mg_ref[:, hd * DH:(hd + 1) * DH] * _sigmoid(o.astype(F32))).astype(BF16)
        m_last = m_t[lc - 1:lc, :]
        b_last = b_col[lc - 1:lc, :]
        decay = w_inter[lc - 1:lc, :]
        w_last = jnp.exp(b_last - b_col + i_col - m_last) * scale
        kw = (k.astype(F32) * w_last).astype(BF16)
        c_sc[hd] = decay * c_prev + _dot_dims(kw, v, ((0,), (0,)))
        n_sc[hd] = decay * n_prev + _dot_dims(kw, ones, ((0,), (0,)))
        m_sc[hd:hd + 1, :] = jnp.broadcast_to(m_last, (1, LANES))


def _mlstm_prompt(qkvo, gcol, grow, mnorm_g):
    nc = SEQ // MCHUNK
    q3 = qkvo.reshape(N_PROMPT, SEQ, 4 * MWIDTH)
    g3 = gcol.reshape(N_PROMPT, SEQ, LANES)
    seq_blk = lambda w: pl.BlockSpec((MSEQ, MCHUNK, w), lambda b, c: (b, c, 0))
    state = lambda shape: pl.BlockSpec((MSEQ,) + shape, lambda b, c: (b,) + (0,) * len(shape))
    gr_specs = [pl.BlockSpec((8, MCHUNK), functools.partial(lambda b, c, sq: (0, (b * MSEQ + sq) * nc + c), sq=sq))
                for sq in range(MSEQ)]
    hm, c_p, n_p, m_p = pl.pallas_call(
        _mlstm_prompt_body,
        grid=(N_PROMPT // MSEQ, nc),
        in_specs=[seq_blk(4 * MWIDTH), seq_blk(LANES)] + gr_specs + [pl.BlockSpec((1, MWIDTH), lambda b, c: (0, 0))],
        out_specs=[seq_blk(MWIDTH), state((HEADS, DH, DH)), state((HEADS, DH, DH)), state((8, LANES))],
        out_shape=[jax.ShapeDtypeStruct((N_PROMPT, SEQ, MWIDTH), BF16),
                   jax.ShapeDtypeStruct((N_PROMPT, HEADS, DH, DH), F32),
                   jax.ShapeDtypeStruct((N_PROMPT, HEADS, DH, DH), F32),
                   jax.ShapeDtypeStruct((N_PROMPT, 8, LANES), F32)],
        scratch_shapes=[pltpu.VMEM((MSEQ, HEADS, DH, DH), F32),
                        pltpu.VMEM((MSEQ, HEADS, DH, DH), F32),
                        pltpu.VMEM((MSEQ, 8, LANES), F32)],
        compiler_params=_cparams("parallel", "arbitrary"),
        name="mlstm_prompt",
    )(q3, g3, *([grow] * MSEQ), mnorm_g)
    return hm.reshape(N_PROMPT * SEQ, MWIDTH), c_p, n_p, m_p


MSTEP_TOK = 8


def _mlstm_step_body(q_ref, qkt_ref, ig_ref, lf_ref, m0_ref, c0_ref, n0_ref, mg_ref, hm_ref, c_out, n_out, m_out, qc_sc):
    scale = DH ** -0.5
    ig = ig_ref[...]
    m_inter = lf_ref[...] + m0_ref[...]
    m_t = jnp.maximum(m_inter, ig)
    w_inter_all = jnp.exp(m_inter - m_t)
    w_all = jnp.exp(ig - m_t) * scale
    floor_all = jnp.exp(-m_t)
    m_out[...] = m_t
    for hd in range(HEADS):
        sl = slice(hd * DH, (hd + 1) * DH)
        q = q_ref[:, sl].astype(F32)
        k = q_ref[:, MWIDTH + hd * DH:MWIDTH + (hd + 1) * DH].astype(F32)
        v = q_ref[:, 2 * MWIDTH + hd * DH:2 * MWIDTH + (hd + 1) * DH].astype(F32)
        o = q_ref[:, 3 * MWIDTH + hd * DH:3 * MWIDTH + (hd + 1) * DH].astype(F32)
        n0 = n0_ref[:, sl]
        wi = w_inter_all[:, hd:hd + 1]
        wk = w_all[:, hd:hd + 1]
        s = jnp.sum(q * k, axis=-1, keepdims=True) * wk
        qn = jnp.sum(q * n0, axis=-1, keepdims=True)
        for j in range(MSTEP_TOK):
            c0 = c0_ref[j, hd]
            q_col = qkt_ref[hd, :, j:j + 1]
            k_col = qkt_ref[HEADS + hd, :, j:j + 1]
            qc_sc[j:j + 1, sl] = jnp.sum(q_col * c0, axis=0, keepdims=True)
            c_out[j, hd] = wi[j:j + 1, :] * c0 + (wk[j:j + 1, :] * k_col) * v[j:j + 1, :]
        num = wi * qc_sc[:, sl] + s * v
        den = wi * qn + s
        h = num / jnp.maximum(jnp.abs(den), floor_all[:, hd:hd + 1])
        hn = h * lax.rsqrt(jnp.mean(h * h, axis=-1, keepdims=True) + EPS)
        hm_ref[:, sl] = hn * mg_ref[:, sl] * _sigmoid(o)
        n_out[:, sl] = wi * n0 + wk * k


def _mlstm_step(qkvo, qkt, ig, lf, m0, c0, n0, mnorm_g):
    t = qkvo.shape[0]
    tk = MSTEP_TOK
    return pl.pallas_call(
        _mlstm_step_body,
        grid=(t // tk,),
        in_specs=[pl.BlockSpec((tk, 4 * MWIDTH), lambda i: (i, 0)),
                  pl.BlockSpec((None, 2 * HEADS, DH, tk), lambda i: (i, 0, 0, 0)),
                  pl.BlockSpec((tk, LANES), lambda i: (i, 0)),
                  pl.BlockSpec((tk, LANES), lambda i: (i, 0)),
                  pl.BlockSpec((tk, LANES), lambda i: (i, 0)),
                  pl.BlockSpec((tk, HEADS, DH, DH), lambda i: (i, 0, 0, 0)),
                  pl.BlockSpec((tk, MWIDTH), lambda i: (i, 0)),
                  pl.BlockSpec((1, MWIDTH), lambda i: (0, 0))],
        out_specs=[pl.BlockSpec((tk, MWIDTH), lambda i: (i, 0)),
                   pl.BlockSpec((tk, HEADS, DH, DH), lambda i: (i, 0, 0, 0)),
                   pl.BlockSpec((tk, MWIDTH), lambda i: (i, 0)),
                   pl.BlockSpec((tk, LANES), lambda i: (i, 0))],
        out_shape=[jax.ShapeDtypeStruct((t, MWIDTH), F32),
                   jax.ShapeDtypeStruct((t, HEADS, DH, DH), F32),
                   jax.ShapeDtypeStruct((t, MWIDTH), F32),
                   jax.ShapeDtypeStruct((t, LANES), F32)],
        scratch_shapes=[pltpu.VMEM((tk, MWIDTH), F32)],
        compiler_params=_cparams("parallel"),
        name="mlstm_step",
    )(qkvo, qkt, ig, lf, m0, c0, n0, mnorm_g)


def _post_body(x_ref, y5_ref, hm_ref, gg_ref, g1_ref, sc_ref, sh_ref, n2_ref, wglu_ref, wmo_ref, wo_ref, rw_ref, rb_ref,
               cnt0_ref, x1_ref, h2_ref, eid_ref, wgt_ref, rank_ref, cnt_ref, run_sc, *, n_real, n_steps):
    i = pl.program_id(0)

    @pl.when(i == 0)
    def _():
        run_sc[...] = cnt0_ref[...]

    if n_steps > n_real:
        @pl.when(i >= n_real)
        def _():
            h2_ref[...] = jnp.zeros_like(h2_ref)

        pl.when(i < n_real)(functools.partial(
            _post_tile, x_ref, y5_ref, hm_ref, gg_ref, g1_ref, sc_ref, sh_ref, n2_ref, wglu_ref, wmo_ref, wo_ref,
            rw_ref, rb_ref, x1_ref, h2_ref, eid_ref, wgt_ref, rank_ref, run_sc))
    else:
        _post_tile(x_ref, y5_ref, hm_ref, gg_ref, g1_ref, sc_ref, sh_ref, n2_ref, wglu_ref, wmo_ref, wo_ref,
                   rw_ref, rb_ref, x1_ref, h2_ref, eid_ref, wgt_ref, rank_ref, run_sc)
    cnt_ref[...] = run_sc[...]


def _post_tile(x_ref, y5_ref, hm_ref, gg_ref, g1_ref, sc_ref, sh_ref, n2_ref, wglu_ref, wmo_ref, wo_ref, rw_ref, rb_ref,
               x1_ref, h2_ref, eid_ref, wgt_ref, rank_ref, run_sc):
    g5 = _gelu_tanh(y5_ref[...]).astype(BF16)
    glu = _dot(g5, wglu_ref[...])
    br_s5 = glu[:, :D_MODEL] * _sigmoid(glu[:, D_MODEL:])
    br_m = _dot(hm_ref[...].astype(BF16), wmo_ref[...])
    gg = gg_ref[...].astype(F32)
    merged = _sigmoid(gg[:, :D_MODEL]) * br_s5 + _sigmoid(gg[:, D_MODEL:]) * br_m
    x1 = x_ref[...] + g1_ref[...] * _dot(merged.astype(BF16), wo_ref[...])
    x1_ref[...] = x1
    ms = jnp.mean(x1 * x1, axis=-1, keepdims=True)
    h2 = x1 * lax.rsqrt(ms + EPS) * n2_ref[...] * (1.0 + sc_ref[...]) + sh_ref[...]
    h2_ref[...] = h2
    logits = _dot_hp(h2, rw_ref[...]) + rb_ref[...]
    tm = logits.shape[0]
    lane = lax.broadcasted_iota(jnp.int32, logits.shape, 1)
    l = jnp.where(lane < N_EXPERTS, logits, -jnp.inf)
    vals, hots, idxs = [], [], []
    for _ in range(TOP_K):
        mx = jnp.max(l, axis=-1, keepdims=True)
        idx = jnp.min(jnp.where(l == mx, lane, LANES), axis=-1, keepdims=True)
        hot = lane == idx
        vals.append(mx)
        hots.append(hot)
        idxs.append(idx)
        l = jnp.where(hot, -jnp.inf, l)
    ex = [jnp.exp(vk - vals[0]) for vk in vals]
    tot = ex[0] + ex[1] + ex[2] + ex[3]
    chosen = jnp.where(hots[0] | hots[1] | hots[2] | hots[3], 1.0, 0.0)
    r_io = lax.broadcasted_iota(jnp.int32, (tm, tm), 0)
    c_io = lax.broadcasted_iota(jnp.int32, (tm, tm), 1)
    earlier = (c_io < r_io).astype(BF16)
    before = run_sc[...] + _dot(earlier, chosen.astype(BF16))
    eid = jnp.zeros(logits.shape, jnp.int32)
    wgt = jnp.zeros(logits.shape, F32)
    rank = jnp.zeros(logits.shape, F32)
    for k in range(TOP_K):
        eid = jnp.where(lane == k, idxs[k], eid)
        wgt = jnp.where(lane == k, ex[k] / tot, wgt)
        rank = jnp.where(lane == k, jnp.sum(jnp.where(hots[k], before, 0.0), axis=-1, keepdims=True), rank)
    eid_ref[...] = eid
    wgt_ref[...] = wgt
    rank_ref[...] = rank.astype(jnp.int32)
    run_sc[...] += jnp.sum(chosen, axis=0, keepdims=True)


def _post(x2d, y5, hm, gg, mod3, rows_per_mod, tm, n2, wglu, wmo, wo, rw, rb, cnt0, h2_rows):
    t = x2d.shape[0]
    mrows = mod3.shape[1]
    n_real = t // tm
    n_steps = -(-h2_rows // tm)
    last = n_real - 1

    def mod_spec(j):
        return pl.BlockSpec((None, mrows, D_MODEL), lambda i: ((jnp.minimum(i, last) * tm) // rows_per_mod, 0, j))

    const = lambda shape: pl.BlockSpec(shape, lambda i: (0,) * len(shape))
    tile = lambda w: pl.BlockSpec((tm, w), lambda i: (jnp.minimum(i, last), 0))
    return pl.pallas_call(
        functools.partial(_post_body, n_real=n_real, n_steps=n_steps),
        grid=(n_steps,),
        in_specs=[tile(D_MODEL), tile(S5_WIDTH), tile(MWIDTH), tile(2 * D_MODEL),
                  mod_spec(2), mod_spec(4), mod_spec(3), const((1, D_MODEL)),
                  const(wglu.shape), const(wmo.shape), const(wo.shape), const(rw.shape), const(rb.shape),
                  const((1, LANES))],
        out_specs=[tile(D_MODEL), pl.BlockSpec((tm, D_MODEL), lambda i: (i, 0)), tile(LANES), tile(LANES), tile(LANES),
                   const((1, LANES))],
        out_shape=[jax.ShapeDtypeStruct((t, D_MODEL), F32),
                   jax.ShapeDtypeStruct((h2_rows, D_MODEL), F32),
                   jax.ShapeDtypeStruct((t, LANES), jnp.int32),
                   jax.ShapeDtypeStruct((t, LANES), F32),
                   jax.ShapeDtypeStruct((t, LANES), jnp.int32),
                   jax.ShapeDtypeStruct((1, LANES), F32)],
        scratch_shapes=[pltpu.VMEM((1, LANES), F32)],
        compiler_params=_cparams("arbitrary"),
        name="post",
    )(x2d, y5, hm, gg, mod3, mod3, mod3, n2, wglu, wmo, wo, rw, rb, cnt0)


N_TOKENS = N_PROMPT * SEQ + N_SAMPLE
N_ASSIGN = N_TOKENS * TOP_K
MOE_TILE = 256
MOE_TILES = N_ASSIGN // MOE_TILE + N_EXPERTS
N_SLOT = MOE_TILES * MOE_TILE
CMB_TILE = 512


MOE_RING = 3
MOE_SHORT = 64


def _moe_body(te_ref, nt_ref, nxe_ref, gpar_ref, rows_ref, idx0_ref, idx1_ref, idx2_ref, h2_hbm, wup_hbm, bup_ref,
              wdn_hbm, bdn_ref, y_ref, xb0, xb1, xb2, sem, wbuf_up, wbuf_dn, wsem, wup_bf, wdn_bf, gu_sc):
    i = pl.program_id(0)
    nt = nt_ref[0]
    bufs = (xb0, xb1, xb2)
    last = MOE_TILES - 1

    def gather(idx_ref, slot, r0=0, r1=MOE_TILE):
        for r in range(r0, r1):
            pltpu.make_async_copy(h2_hbm.at[pl.ds(idx_ref[0, r], 1)], bufs[slot].at[pl.ds(r, 1)],
                                  sem.at[slot]).start()

    def wait(slot):
        pltpu.make_async_copy(h2_hbm.at[pl.ds(0, MOE_TILE)], bufs[slot], sem.at[slot]).wait()

    def weights(e, slot):
        return (pltpu.make_async_copy(wup_hbm.at[e], wbuf_up.at[slot], wsem.at[0, slot]),
                pltpu.make_async_copy(wdn_hbm.at[e], wbuf_dn.at[slot], wsem.at[1, slot]))

    wslot = gpar_ref[i]

    @pl.when(i == 0)
    def _():
        for c in weights(te_ref[0], wslot):
            c.start(priority=1)
        gather(idx0_ref, 0)
        gather(idx1_ref, 1)

    new_expert = jnp.logical_or(i == 0, te_ref[i] != te_ref[jnp.maximum(i - 1, 0)])
    live = i < nt

    @pl.when(jnp.logical_and(new_expert, live))
    def _():
        for c in weights(0, wslot):
            c.wait()
        wup_bf[...] = wbuf_up[wslot].astype(BF16)
        wdn_bf[...] = wbuf_dn[wslot].astype(BF16)
        nxe = nxe_ref[i]

        @pl.when(nxe >= 0)
        def _():
            for c in weights(nxe, 1 - wslot):
                c.start(priority=1)

    def step(slot, rows):
        nxt = (slot + 2) % MOE_RING
        wait(slot)
        gather(idx2_ref, nxt, 0, MOE_TILE // 2)
        gu_sc[:rows, :] = _dot(bufs[slot][:rows, :].astype(BF16), wup_bf[...]) + bup_ref[...]

        @pl.when(i >= 0)
        def _():
            gather(idx2_ref, nxt, MOE_TILE // 2, MOE_TILE)
            g = jnp.minimum(gu_sc[:rows, :D_FF], SWIGLU_LIMIT)
            up = jnp.clip(gu_sc[:rows, D_FF:], -SWIGLU_LIMIT, SWIGLU_LIMIT)
            act = (up + 1.0) * g * _sigmoid(SWIGLU_ALPHA * g)
            y_ref[:rows, :] = _dot(act.astype(BF16), wdn_bf[...]) + bdn_ref[...]
            if rows < MOE_TILE:
                y_ref[rows:, :] = jnp.zeros((MOE_TILE - rows, D_MODEL), F32)

    short = rows_ref[i] <= MOE_SHORT
    for slot in range(MOE_RING):
        mine = jnp.logical_and(live, i % MOE_RING == slot)
        pl.when(jnp.logical_and(mine, jnp.logical_not(short)))(functools.partial(step, slot, MOE_TILE))
        pl.when(jnp.logical_and(mine, short))(functools.partial(step, slot, MOE_SHORT))

    @pl.when(jnp.logical_not(live))
    def _():
        y_ref[...] = jnp.zeros_like(y_ref)

    drain = jnp.logical_and(jnp.logical_not(live), i < nt + 2)
    for slot in range(MOE_RING):
        pl.when(jnp.logical_and(drain, i % MOE_RING == slot))(functools.partial(wait, slot))
    pl.when(jnp.logical_and(i == last, last - 1 < nt))(functools.partial(wait, (last + 1) % MOE_RING))
    pl.when(jnp.logical_and(i == last, last < nt))(functools.partial(wait, (last + 2) % MOE_RING))


def _moe(tile_expert, n_tiles, next_expert, group_parity, tile_rows, slot_token, h2_all, wup, bup, wdn, bdn):
    idx3 = slot_token.reshape(MOE_TILES, 1, MOE_TILE)
    ahead = lambda k: pl.BlockSpec((None, 1, MOE_TILE),
                                   lambda i, *_: (jnp.minimum(i + k, MOE_TILES - 1), 0, 0), memory_space=pltpu.SMEM)
    grid_spec = pltpu.PrefetchScalarGridSpec(
        num_scalar_prefetch=5,
        grid=(MOE_TILES,),
        in_specs=[ahead(0), ahead(1), ahead(2),
                  pl.BlockSpec(memory_space=pl.ANY),
                  pl.BlockSpec(memory_space=pl.ANY),
                  pl.BlockSpec((None, 1, 2 * D_FF), lambda i, te, *_: (te[i], 0, 0)),
                  pl.BlockSpec(memory_space=pl.ANY),
                  pl.BlockSpec((None, 1, D_MODEL), lambda i, te, *_: (te[i], 0, 0))],
        out_specs=pl.BlockSpec((MOE_TILE, D_MODEL), lambda i, *_: (i, 0)),
        scratch_shapes=[pltpu.VMEM((MOE_TILE, D_MODEL), F32), pltpu.VMEM((MOE_TILE, D_MODEL), F32),
                        pltpu.VMEM((MOE_TILE, D_MODEL), F32),
                        pltpu.SemaphoreType.DMA((MOE_RING,)),
                        pltpu.VMEM((2, D_MODEL, 2 * D_FF), F32), pltpu.VMEM((2, D_FF, D_MODEL), F32),
                        pltpu.SemaphoreType.DMA((2, 2)),
                        pltpu.VMEM((D_MODEL, 2 * D_FF), BF16), pltpu.VMEM((D_FF, D_MODEL), BF16),
                        pltpu.VMEM((MOE_TILE, 2 * D_FF), F32)])
    return pl.pallas_call(
        _moe_body,
        grid_spec=grid_spec,
        out_shape=jax.ShapeDtypeStruct((N_SLOT, D_MODEL), F32),
        compiler_params=_cparams("arbitrary"),
        name="moe",
    )(tile_expert, n_tiles, next_expert, group_parity, tile_rows, idx3, idx3, idx3, h2_all, wup, bup, wdn, bdn)


def _combine_body(cur_ref, nxt_ref, wgt_ref, x1_ref, g2_ref, fg_ref, y_hbm, o_ref, yb0, yb1, sem, *, n_steps, tm):
    i = pl.program_id(0)

    def gather(idx_ref, buf, s):
        for r in range(tm):
            for k in range(TOP_K):
                pltpu.make_async_copy(y_hbm.at[pl.ds(idx_ref[0, r * TOP_K + k], 1)],
                                      buf.at[pl.ds(k * tm + r, 1)], sem.at[s]).start(priority=k % 2)

    def wait(buf, s):
        pltpu.make_async_copy(y_hbm.at[pl.ds(0, TOP_K * tm)], buf, sem.at[s]).wait()

    @pl.when(i == 0)
    def _():
        gather(cur_ref, yb0, 0)

    def step(cur, nxt, s_cur, s_nxt):
        wait(cur, s_cur)
        gather(nxt_ref, nxt, s_nxt)
        wgt = wgt_ref[...]
        acc = wgt[:, 0:1] * cur[0:tm, :]
        for k in range(1, TOP_K):
            acc = acc + wgt[:, k:k + 1] * cur[k * tm:(k + 1) * tm, :]
        xo = x1_ref[...] + g2_ref[...] * acc
        ms = jnp.mean(xo * xo, axis=-1, keepdims=True)
        o_ref[...] = xo * lax.rsqrt(ms + EPS) * fg_ref[...]

    pl.when(i % 2 == 0)(functools.partial(step, yb0, yb1, 0, 1))
    pl.when(i % 2 == 1)(functools.partial(step, yb1, yb0, 1, 0))
    last_slot = n_steps % 2
    pl.when(i == n_steps - 1)(functools.partial(wait, yb1 if last_slot else yb0, last_slot))


def _combine(pos, wgt, x1, mod3, rows_per_mod, fg, y_slots, tm):
    t = x1.shape[0]
    n = t // tm
    mrows = mod3.shape[1]
    assert mrows in (1, tm) and rows_per_mod % tm == 0
    pos3 = pos.reshape(n, 1, tm * TOP_K)
    return pl.pallas_call(
        functools.partial(_combine_body, n_steps=n, tm=tm),
        grid=(n,),
        in_specs=[pl.BlockSpec((None, 1, tm * TOP_K), lambda i: (i, 0, 0), memory_space=pltpu.SMEM),
                  pl.BlockSpec((None, 1, tm * TOP_K), lambda i: (jnp.minimum(i + 1, n - 1), 0, 0),
                               memory_space=pltpu.SMEM),
                  pl.BlockSpec((tm, LANES), lambda i: (i, 0)),
                  pl.BlockSpec((tm, D_MODEL), lambda i: (i, 0)),
                  pl.BlockSpec((None, mrows, D_MODEL), lambda i: ((i * tm) // rows_per_mod, 0, 5)),
                  pl.BlockSpec((1, D_MODEL), lambda i: (0, 0)),
                  pl.BlockSpec(memory_space=pl.ANY)],
        out_specs=pl.BlockSpec((tm, D_MODEL), lambda i: (i, 0)),
        out_shape=jax.ShapeDtypeStruct((t, D_MODEL), F32),
        scratch_shapes=[pltpu.VMEM((TOP_K * tm, D_MODEL), F32), pltpu.VMEM((TOP_K * tm, D_MODEL), F32),
                        pltpu.SemaphoreType.DMA((2,))],
        compiler_params=_cparams("arbitrary"),
        name="combine",
    )(pos3, pos3, wgt, x1, mod3, fg, y_slots)


def _routing(eid_p, rank_p, eid_s, rank_s, counts):
    cnt = counts[0, :N_EXPERTS].astype(jnp.int32)
    ntile = (cnt + MOE_TILE - 1) // MOE_TILE
    tile_end = jnp.cumsum(ntile)
    poff = (tile_end - ntile) * MOE_TILE
    total = tile_end[-1]
    j = jnp.arange(MOE_TILES, dtype=jnp.int32)
    te = jnp.sum((j[:, None] >= tile_end[None, :]).astype(jnp.int32), axis=1)
    te_last = jnp.sum(((total - 1) >= tile_end).astype(jnp.int32))
    tile_expert = jnp.where(j < total, te, te_last).astype(jnp.int32)
    ex = jnp.arange(N_EXPERTS, dtype=jnp.int32)
    used = ntile > 0
    later = jnp.where(used[None, :] & (ex[None, :] > ex[:, None]), ex[None, :], N_EXPERTS)
    nxt_e = jnp.min(later, axis=1)
    nxt_e = jnp.where(nxt_e < N_EXPERTS, nxt_e, -1).astype(jnp.int32)
    par_e = ((jnp.cumsum(used.astype(jnp.int32)) - 1) % 2).astype(jnp.int32)
    pick = tile_expert[:, None] == ex[None, :]
    next_expert = jnp.sum(jnp.where(pick, nxt_e[None, :], 0), axis=1).astype(jnp.int32)
    group_parity = jnp.sum(jnp.where(pick, par_e[None, :], 0), axis=1).astype(jnp.int32)
    left = jnp.sum(jnp.where(pick, (cnt + poff)[None, :], 0), axis=1) - j * MOE_TILE
    tile_rows = jnp.where(j < total, jnp.clip(left, 0, MOE_TILE), 0).astype(jnp.int32)

    def pos_of(eid, rank):
        e = eid[:, :TOP_K]
        off = jnp.sum(jnp.where(e[:, :, None] == ex[None, None, :], poff[None, None, :], 0), axis=-1)
        return off + rank[:, :TOP_K]

    pos_p = pos_of(eid_p, rank_p)
    pos_s = pos_of(eid_s, rank_s)
    tok = jnp.concatenate([jnp.repeat(jnp.arange(N_PROMPT * SEQ, dtype=jnp.int32), TOP_K),
                           jnp.repeat(N_PROMPT * SEQ + jnp.arange(N_SAMPLE, dtype=jnp.int32), TOP_K)])
    pos_all = jnp.concatenate([pos_p.reshape(-1), pos_s.reshape(-1)])
    padc = jnp.concatenate([ntile * MOE_TILE - cnt, (N_SLOT - total * MOE_TILE).reshape(1)])
    pad_end = jnp.cumsum(padc)
    pad_base = jnp.concatenate([poff + cnt, (total * MOE_TILE).reshape(1)])
    jp = jnp.arange(N_SLOT - N_ASSIGN, dtype=jnp.int32)
    pe = jnp.sum((jp[:, None] >= pad_end[None, :]).astype(jnp.int32), axis=1)
    hit = pe[:, None] == jnp.arange(N_EXPERTS + 1, dtype=jnp.int32)[None, :]
    pad_slot = jp + jnp.sum(jnp.where(hit, (pad_base - (pad_end - padc))[None, :], 0), axis=1)
    keys = jnp.concatenate([pos_all, pad_slot.astype(jnp.int32)])
    vals = jnp.concatenate([tok, jnp.zeros((N_SLOT - N_ASSIGN,), jnp.int32)])
    tok_bits = 15
    assert N_TOKENS <= (1 << tok_bits) and N_SLOT <= (1 << (32 - tok_bits))
    packed = lax.sort((keys.astype(jnp.uint32) << tok_bits) | vals.astype(jnp.uint32))
    slot_token = (packed & ((1 << tok_bits) - 1)).astype(jnp.int32)
    return (tile_expert, total.reshape(1).astype(jnp.int32), next_expert, group_parity, tile_rows, slot_token,
            pos_p, pos_s)


def _unpack_s5_state(x):
    n = x.shape[1]
    z = x.reshape(N_SUPER, n, 2, S5_SUPER, S5_STATE).transpose(2, 1, 0, 3, 4).reshape(2, n, S5_GROUPS, S5_STATE)
    return z[0], z[1]


def _pack_s5_state(re, im):
    n = re.shape[0]
    z = jnp.stack([re, im], axis=0).reshape(2, n, N_SUPER, S5_SUPER, S5_STATE)
    return z.transpose(2, 1, 0, 3, 4).reshape(N_SUPER, n, SUPER_STATE)


def kernel(x_prompt, x_sample, c_prompt, c_sample, state_s5_re, state_s5_im, state_mlstm_C, state_mlstm_n, state_mlstm_m, norm1_g, norm2_g, final_norm_g, w_ada, b_ada, w_in, s5_lambda_re, s5_lambda_im, s5_log_dt, s5_B_re, s5_B_im, s5_C_re, s5_C_im, s5_D, s5_w_glu, mlstm_b_i, mlstm_b_f, mlstm_norm_g, mlstm_w_out, w_out, router_w, router_b, expert_w_up, expert_b_up, expert_w_down, expert_b_down):
    assert w_in.shape[0] == 1, "single layer"
    tp = N_PROMPT * SEQ
    xp = x_prompt.reshape(tp, D_MODEL).astype(F32)
    xs = x_sample.reshape(N_SAMPLE, D_MODEL).astype(F32)

    w = w_in[0]
    c0 = S5_WIDTH
    c1 = c0 + 4 * MWIDTH
    c2 = c1 + 2 * HEADS
    wu = w[:, :c0].astype(BF16)
    wq = w[:, c0:c1].astype(BF16)
    wif = jnp.pad(w[:, c1:c2].astype(F32), ((0, 0), (0, LANES - 2 * HEADS)))
    wg = w[:, c2:].astype(BF16)
    bvec = jnp.concatenate([mlstm_b_i[0], mlstm_b_f[0]]).astype(F32)
    bif = jnp.pad(bvec, (0, LANES - 2 * HEADS)).reshape(1, LANES)
    wparts = (wu, wq, wg, wif, bif)
    g1 = norm1_g[0].reshape(1, D_MODEL).astype(F32)
    n2 = norm2_g[0].reshape(1, D_MODEL).astype(F32)
    fg = final_norm_g.reshape(1, D_MODEL).astype(F32)
    mg = mlstm_norm_g[0].reshape(1, MWIDTH).astype(F32)
    wglu = s5_w_glu[0].astype(BF16)
    wmo = mlstm_w_out[0].astype(BF16)
    wo = w_out[0].astype(BF16)
    rw = jnp.pad(router_w[0].astype(F32), ((0, 0), (0, LANES - N_EXPERTS)))
    rb = jnp.pad(router_b[0].astype(F32), (0, LANES - N_EXPERTS)).reshape(1, LANES)
    wup = expert_w_up[0].astype(F32)
    wdn = expert_w_down[0].astype(F32)
    bup = expert_b_up[0].astype(F32).reshape(N_EXPERTS, 1, 2 * D_FF)
    bdn = expert_b_down[0].astype(F32).reshape(N_EXPERTS, 1, D_MODEL)
    ddiag, fcomp, ecomp, a16, a1 = _s5_tables(
        s5_lambda_re[0], s5_lambda_im[0], s5_log_dt[0], s5_B_re[0], s5_B_im[0], s5_C_re[0], s5_C_im[0], s5_D[0])

    c_all = jnp.concatenate([c_prompt, c_sample], axis=0).astype(F32)
    mod = _adaln(c_all, w_ada[0].astype(F32), b_ada[0].astype(F32))
    mod_p = mod[:N_PROMPT].reshape(N_PROMPT, 1, 6 * D_MODEL)
    mod_s = mod[N_PROMPT:].reshape(1, N_SAMPLE, 6 * D_MODEL)

    u, qkvo, gg, gcol, grow = _inproj(xp, mod_p, SEQ, 1024, g1, wparts)
    y5, xend, k0, e0, f1 = _s5_prompt(u, ddiag, fcomp, ecomp, a16, a1)
    hm, c_p, n_p, m_p = _mlstm_prompt(qkvo, gcol, grow, mg)
    cnt0 = jnp.zeros((1, LANES), F32)
    x1, h2_all, eid_p, wgt_p, rank_p, cnt_p = _post(xp, y5, hm, gg, mod_p, SEQ, 512, n2, wglu, wmo, wo, rw, rb,
                                                    cnt0, N_TOKENS)
    p_re, p_im = _unpack_s5_state(xend.reshape(N_SUPER, N_PROMPT, SUPER_STATE))
    n_p = n_p[..., 0]
    m_p = m_p[:, :HEADS, 0]

    us, qs, ggs, gcs, _ = _inproj(xs, mod_s, N_SAMPLE, N_SAMPLE, g1, wparts)
    x0 = _pack_s5_state(state_s5_re[0].astype(F32), state_s5_im[0].astype(F32))
    y5s, xns = _s5_step(us, x0, k0, e0, f1, a1)
    s_re, s_im = _unpack_s5_state(xns)
    qkt = qs[:, :2 * MWIDTH].astype(F32).reshape(N_SAMPLE // MSTEP_TOK, MSTEP_TOK, 2 * HEADS, DH).transpose(0, 2, 3, 1)
    pad_heads = lambda a: jnp.pad(a, ((0, 0), (0, LANES - HEADS)))
    m0 = pad_heads(state_mlstm_m[0].astype(F32))
    hms, c_s, n_s, m_s = _mlstm_step(qs, qkt, pad_heads(gcs[:, :HEADS]), pad_heads(gcs[:, HEADS:2 * HEADS]), m0,
                                     state_mlstm_C[0].astype(F32),
                                     state_mlstm_n[0].astype(F32).reshape(N_SAMPLE, MWIDTH), mg)
    x1s, h2s, eid_s, wgt_s, rank_s, cnt_all = _post(xs, y5s, hms, ggs, mod_s, N_SAMPLE, N_SAMPLE, n2, wglu, wmo, wo, rw, rb,
                                                    cnt_p, N_SAMPLE)

    h2_all = lax.dynamic_update_slice(h2_all, h2s, (tp, 0))
    tile_expert, n_tiles, next_expert, group_parity, tile_rows, slot_token, pos_p, pos_s = _routing(
        eid_p, rank_p, eid_s, rank_s, cnt_all)
    y_slots = _moe(tile_expert, n_tiles, next_expert, group_parity, tile_rows, slot_token, h2_all, wup, bup, wdn, bdn)
    y_p = _combine(pos_p, wgt_p, x1, mod_p, SEQ, fg, y_slots, CMB_TILE)
    y_s = _combine(pos_s, wgt_s, x1s, mod_s, N_SAMPLE, fg, y_slots, N_SAMPLE)

    return (y_p.reshape(N_PROMPT, SEQ, D_MODEL).astype(x_prompt.dtype),
            y_s.reshape(N_SAMPLE, 1, D_MODEL).astype(x_sample.dtype),
            p_re[None], p_im[None], c_p[None], n_p[None], m_p[None],
            s_re[None], s_im[None], c_s[None],
            n_s.reshape(1, N_SAMPLE, HEADS, DH), m_s[:, :HEADS][None])
```

```python
import functools
import math

import jax
import jax.numpy as jnp
from jax import lax
from jax.experimental import pallas as pl
from jax.experimental.pallas import tpu as pltpu

F32 = jnp.float32
BF16 = jnp.bfloat16

D_MODEL = 1024
SEQ = 2048
N_PROMPT = 8
N_SAMPLE = 128
S5_WIDTH = 512
S5_GROUP = 16
S5_GROUPS = 32
S5_STATE = 64
HEADS = 4
DH = 128
MWIDTH = HEADS * DH
N_EXPERTS = 32
TOP_K = 4
D_FF = 1024
SWIGLU_LIMIT = 7.0
SWIGLU_ALPHA = 1.702
EPS = 1e-6

LANES = 128
S5_CHUNK = 16
S5_SUPER = LANES // S5_GROUP
N_SUPER = S5_GROUPS // S5_SUPER
SUPER_STATE = 2 * S5_SUPER * S5_STATE
MCHUNK = 256
VMEM_LIMIT = 56 * 1024 * 1024


def _cparams(*sem):
    return pltpu.CompilerParams(dimension_semantics=sem, vmem_limit_bytes=VMEM_LIMIT)


def _dot(a, b):
    return jnp.dot(a, b, preferred_element_type=F32)


def _dot_dims(a, b, dims):
    return lax.dot_general(a, b, (dims, ((), ())), preferred_element_type=F32)


def _split_bf16(a):
    hi = a.astype(BF16)
    lo = (a - hi.astype(F32)).astype(BF16)
    return hi, lo


def _dot_hp(a, b, dims=((1,), (0,))):
    ah, al = _split_bf16(a)
    bh, bl = _split_bf16(b)
    return _dot_dims(ah, bh, dims) + (_dot_dims(al, bh, dims) + _dot_dims(ah, bl, dims))


def _dot_exact_rhs(a, b_exact, dims=((1,), (0,))):
    ah, al = _split_bf16(a)
    return _dot_dims(ah, b_exact, dims) + _dot_dims(al, b_exact, dims)


def _log_sigmoid(x):
    return -(jnp.maximum(-x, 0.0) + jnp.log1p(jnp.exp(-jnp.abs(x))))


def _sigmoid(x):
    return 0.5 * (1.0 + jnp.tanh(0.5 * x))


def _gelu_tanh(x):
    c = math.sqrt(2.0 / math.pi)
    return 0.5 * x * (1.0 + jnp.tanh(c * (x + 0.044715 * (x * x * x))))


def _adaln_body(c_ref, w_ref, b_ref, o_ref):
    c = c_ref[...]
    s = c * _sigmoid(c)
    o_ref[...] = _dot_hp(s, w_ref[...]) + b_ref[...]


def _adaln(c_all, w_ada, b_ada):
    n = c_all.shape[0]
    tn = 1024
    return pl.pallas_call(
        _adaln_body,
        grid=(6 * D_MODEL // tn,),
        in_specs=[pl.BlockSpec((n, D_MODEL), lambda j: (0, 0)),
                  pl.BlockSpec((D_MODEL, tn), lambda j: (0, j)),
                  pl.BlockSpec((1, tn), lambda j: (0, j))],
        out_specs=pl.BlockSpec((n, tn), lambda j: (0, j)),
        out_shape=jax.ShapeDtypeStruct((n, 6 * D_MODEL), F32),
        compiler_params=_cparams("parallel"),
        name="adaln",
    )(c_all, w_ada, b_ada.reshape(1, -1))


def _inproj_body(x_ref, g1_ref, sc_ref, sh_ref, wu_ref, wq_ref, wg_ref, wif_ref, bif_ref,
                 u_ref, q_ref, gg_ref, if_ref, ift_ref):
    x = x_ref[...]
    ms = jnp.mean(x * x, axis=-1, keepdims=True)
    h = x * lax.rsqrt(ms + EPS) * g1_ref[...] * (1.0 + sc_ref[...]) + sh_ref[...]
    hb = h.astype(BF16)
    u_ref[...] = _dot(hb, wu_ref[...])
    q_ref[...] = _dot(hb, wq_ref[...]).astype(BF16)
    gg_ref[...] = _dot(hb, wg_ref[...]).astype(BF16)
    gc = _dot_hp(h, wif_ref[...]) + bif_ref[...]
    lane = lax.broadcasted_iota(jnp.int32, gc.shape, 1)
    gates = jnp.where(lane < HEADS, gc, _log_sigmoid(gc))
    if_ref[...] = gates
    ift_ref[...] = gates.T[:2 * HEADS, :]


def _inproj(x2d, mod3, rows_per_mod, tm, g1, wparts):
    t = x2d.shape[0]
    wu, wq, wg, wif, bif = wparts
    mrows = mod3.shape[1]

    def mod_spec(j):
        return pl.BlockSpec((None, mrows, D_MODEL), lambda i: ((i * tm) // rows_per_mod, 0, j))

    const = lambda shape: pl.BlockSpec(shape, lambda i: (0,) * len(shape), pipeline_mode=pl.Buffered(1))
    return pl.pallas_call(
        _inproj_body,
        grid=(t // tm,),
        in_specs=[pl.BlockSpec((tm, D_MODEL), lambda i: (i, 0)),
                  const((1, D_MODEL)), mod_spec(1), mod_spec(0),
                  const(wu.shape), const(wq.shape), const(wg.shape), const(wif.shape), const(bif.shape)],
        out_specs=[pl.BlockSpec((tm, S5_WIDTH), lambda i: (i, 0)),
                   pl.BlockSpec((tm, 4 * MWIDTH), lambda i: (i, 0)),
                   pl.BlockSpec((tm, 2 * D_MODEL), lambda i: (i, 0)),
                   pl.BlockSpec((tm, LANES), lambda i: (i, 0)),
                   pl.BlockSpec((8, tm), lambda i: (0, i))],
        out_shape=[jax.ShapeDtypeStruct((t, S5_WIDTH), F32),
                   jax.ShapeDtypeStruct((t, 4 * MWIDTH), BF16),
                   jax.ShapeDtypeStruct((t, 2 * D_MODEL), BF16),
                   jax.ShapeDtypeStruct((t, LANES), F32),
                   jax.ShapeDtypeStruct((8, t), F32)],
        compiler_params=_cparams("parallel"),
        name="inproj",
    )(x2d, g1, mod3, mod3, wu, wq, wg, wif, bif)


def _s5_tables(lam_re, lam_im, log_dt, b_re, b_im, c_re, c_im, d_s5):
    hi = lax.Precision.HIGHEST
    dt = jnp.exp(log_dt.astype(F32))[:, None]
    lr, li = lam_re.astype(F32), lam_im.astype(F32)
    dpow = jnp.arange(S5_CHUNK + 1, dtype=F32)[:, None, None]
    mag = jnp.exp(dpow * (lr * dt))
    pw_re, pw_im = mag * jnp.cos(dpow * (li * dt)), mag * jnp.sin(dpow * (li * dt))
    ab_re, ab_im = pw_re[1], pw_im[1]
    den = lr * lr + li * li
    nr, ni = ab_re - 1.0, ab_im
    coef_re = (nr * lr + ni * li) / den
    coef_im = (ni * lr - nr * li) / den
    br, bi = b_re.astype(F32), b_im.astype(F32)
    bb_re = coef_re[..., None] * br - coef_im[..., None] * bi
    bb_im = coef_re[..., None] * bi + coef_im[..., None] * br
    cr, ci = c_re.astype(F32), c_im.astype(F32)
    cl_re = cr[None] * pw_re[:, :, None, :] - ci[None] * pw_im[:, :, None, :]
    cl_im = cr[None] * pw_im[:, :, None, :] + ci[None] * pw_re[:, :, None, :]
    ddiag = (d_s5.astype(F32).reshape(N_SUPER, LANES, 1) * jnp.eye(LANES, dtype=F32)[None])
    rp_re, rp_im = pw_re[S5_CHUNK - 1::-1], pw_im[S5_CHUNK - 1::-1]
    f_re = rp_re[..., None] * bb_re[None] - rp_im[..., None] * bb_im[None]
    f_im = rp_re[..., None] * bb_im[None] + rp_im[..., None] * bb_re[None]
    fcat = jnp.stack([f_re, f_im], axis=2).transpose(0, 1, 4, 2, 3)
    fcomp = fcat.reshape(S5_CHUNK, N_SUPER, LANES, 2 * S5_STATE).transpose(1, 0, 2, 3)
    fcomp = fcomp.reshape(N_SUPER, S5_CHUNK * LANES, 2 * S5_STATE)
    ecat = jnp.stack([cl_re[:S5_CHUNK], -cl_im[:S5_CHUNK]], axis=0)
    er = ecat.transpose(2, 0, 4, 1, 3).reshape(N_SUPER, S5_SUPER, 2, S5_STATE, S5_CHUNK * S5_GROUP)
    ecomp = er.transpose(0, 2, 1, 3, 4).reshape(N_SUPER, SUPER_STATE, S5_CHUNK * S5_GROUP)

    def lay(re, im):
        z = jnp.stack([re, im], axis=0).reshape(2, N_SUPER, S5_SUPER * S5_STATE)
        return z.transpose(1, 0, 2).reshape(N_SUPER, 1, SUPER_STATE)

    a16 = lay(pw_re[S5_CHUNK], pw_im[S5_CHUNK])
    a1 = lay(ab_re, ab_im)
    return ddiag, fcomp, ecomp, a16, a1


def _expand_groups(comp, row_div, col_blk, n_cols, col_div):
    r, c = comp.shape
    ci = lax.broadcasted_iota(jnp.int32, (c, n_cols), 0)
    co = lax.broadcasted_iota(jnp.int32, (c, n_cols), 1)
    sel = jnp.logical_and(ci // col_blk == co // (S5_SUPER * col_blk), ci % col_blk == co % col_blk)
    rep = _dot(comp.astype(BF16), sel.astype(BF16))
    ro = lax.broadcasted_iota(jnp.int32, (r, n_cols), 0)
    cc = lax.broadcasted_iota(jnp.int32, (r, n_cols), 1)
    same = (ro // row_div) % S5_SUPER == (cc // col_div) % S5_SUPER
    return jnp.where(same, rep, 0.0).astype(BF16)


S5_NB = 4


def _s5_prompt_body(u_ref, dd_ref, fc_ref, ec_ref, a_ref, a1_ref, y_ref, xe_ref, k0_ref, e0_ref, f1_ref,
                    ucat, v_sc, xp_sc, kp_ref, f_ref, e_ref):
    n = SEQ // S5_CHUNK
    half = SUPER_STATE // 2

    @pl.when(pl.program_id(1) == 0)
    def _():
        f_ref[...] = _expand_groups(fc_ref[...], S5_GROUP, S5_STATE, SUPER_STATE, S5_STATE)
        e_ref[...] = _expand_groups(ec_ref[...], S5_STATE, S5_GROUP, S5_CHUNK * LANES, S5_GROUP)
        f1 = f_ref[(S5_CHUNK - 1) * LANES:, :]
        kall = _dot(f1, e_ref[...])
        for r in range(S5_CHUNK):
            for hf in range(2):
                d = S5_CHUNK - 2 - r + hf
                blk = jnp.zeros((LANES, LANES), F32) if d < 0 else kall[:, d * LANES:(d + 1) * LANES]
                if d == 0:
                    blk = blk + dd_ref[...]
                kp_ref[r * LANES:(r + 1) * LANES, hf * LANES:(hf + 1) * LANES] = blk.astype(BF16)
        k0_ref[...] = kp_ref[(S5_CHUNK - 2) * LANES:(S5_CHUNK - 1) * LANES, :LANES]
        e0_ref[...] = e_ref[:, LANES:2 * LANES]
        f1_ref[...] = f1

    for bl in range(S5_NB):
        for j in range(S5_CHUNK):
            ucat[bl * n:(bl + 1) * n, j * LANES:(j + 1) * LANES] = (
                u_ref[pl.ds(bl * SEQ + j, n, stride=S5_CHUNK), :].astype(BF16))
    v_sc[...] = _dot(ucat[...], f_ref[...])
    a_r, a_i = a_ref[:, :half], a_ref[:, half:]
    xr = [jnp.zeros((1, half), F32) for _ in range(S5_NB)]
    xi = [jnp.zeros((1, half), F32) for _ in range(S5_NB)]
    for c in range(n):
        for bl in range(S5_NB):
            row = bl * n + c
            xp_sc[row:row + 1, :half] = xr[bl]
            xp_sc[row:row + 1, half:] = xi[bl]
            vr, vi = v_sc[row:row + 1, :half], v_sc[row:row + 1, half:]
            xr[bl], xi[bl] = a_r * xr[bl] - a_i * xi[bl] + vr, a_r * xi[bl] + a_i * xr[bl] + vi
    for bl in range(S5_NB):
        xe_ref[bl:bl + 1, :half] = xr[bl]
        xe_ref[bl:bl + 1, half:] = xi[bl]
    l_r, l_i = a1_ref[:, :half], a1_ref[:, half:]
    xpr, xpi = xp_sc[:, :half], xp_sc[:, half:]
    z = jnp.concatenate([l_r * xpr - l_i * xpi, l_r * xpi + l_i * xpr], axis=1).astype(BF16)
    ye = _dot(z, e_ref[...])
    for m in range(S5_CHUNK // 2):
        rows = (2 * m + 2) * LANES
        yp = _dot(ucat[:, :rows], kp_ref[(S5_CHUNK - 2 - 2 * m) * LANES:, :]) + ye[:, 2 * m * LANES:(2 * m + 2) * LANES]
        for bl in range(S5_NB):
            y_ref[pl.ds(bl * SEQ + 2 * m, n, stride=S5_CHUNK), :] = yp[bl * n:(bl + 1) * n, :LANES]
            y_ref[pl.ds(bl * SEQ + 2 * m + 1, n, stride=S5_CHUNK), :] = yp[bl * n:(bl + 1) * n, LANES:]


def _s5_prompt(u, ddiag, fcomp, ecomp, a16, a1):
    n = S5_NB * SEQ // S5_CHUNK
    per_super = lambda shape: pl.BlockSpec((None,) + shape, lambda s, b: (s, 0, 0))
    return pl.pallas_call(
        _s5_prompt_body,
        grid=(N_SUPER, N_PROMPT // S5_NB),
        in_specs=[pl.BlockSpec((S5_NB * SEQ, LANES), lambda s, b: (b, s)),
                  per_super(ddiag.shape[1:]), per_super(fcomp.shape[1:]), per_super(ecomp.shape[1:]),
                  per_super((1, SUPER_STATE)), per_super((1, SUPER_STATE))],
        out_specs=[pl.BlockSpec((S5_NB * SEQ, LANES), lambda s, b: (b, s)),
                   pl.BlockSpec((None, None, S5_NB, SUPER_STATE), lambda s, b: (s, b, 0, 0)),
                   per_super((LANES, LANES)), per_super((SUPER_STATE, LANES)), per_super((LANES, SUPER_STATE))],
        out_shape=[jax.ShapeDtypeStruct(u.shape, F32),
                   jax.ShapeDtypeStruct((N_SUPER, N_PROMPT // S5_NB, S5_NB, SUPER_STATE), F32),
                   jax.ShapeDtypeStruct((N_SUPER, LANES, LANES), BF16),
                   jax.ShapeDtypeStruct((N_SUPER, SUPER_STATE, LANES), BF16),
                   jax.ShapeDtypeStruct((N_SUPER, LANES, SUPER_STATE), BF16)],
        scratch_shapes=[pltpu.VMEM((n, S5_CHUNK * LANES), BF16),
                        pltpu.VMEM((n, SUPER_STATE), F32),
                        pltpu.VMEM((n, SUPER_STATE), F32),
                        pltpu.VMEM((S5_CHUNK * LANES, 2 * LANES), BF16),
                        pltpu.VMEM((S5_CHUNK * LANES, SUPER_STATE), BF16),
                        pltpu.VMEM((SUPER_STATE, S5_CHUNK * LANES), BF16)],
        compiler_params=_cparams("parallel", "arbitrary"),
        name="s5_prompt",
    )(u, ddiag, fcomp, ecomp, a16, a1)


def _s5_step_body(u_ref, x0_ref, k0_ref, e0_ref, f1_ref, a_ref, y_ref, xn_ref):
    half = SUPER_STATE // 2
    ub = u_ref[...].astype(BF16)
    x0 = x0_ref[...]
    bu = _dot(ub, f1_ref[...])
    a_r, a_i = a_ref[:, :half], a_ref[:, half:]
    x0r, x0i = x0[:, :half], x0[:, half:]
    xn_ref[:, :half] = a_r * x0r - a_i * x0i + bu[:, :half]
    xn_ref[:, half:] = a_r * x0i + a_i * x0r + bu[:, half:]
    y_ref[...] = _dot(x0.astype(BF16), e0_ref[...]) + _dot(ub, k0_ref[...])


def _s5_step(u, x0, k0, e0, f1, a1):
    t = u.shape[0]
    per = lambda a: pl.BlockSpec((None,) + a.shape[1:], lambda s: (s,) + (0,) * (a.ndim - 1))
    return pl.pallas_call(
        _s5_step_body,
        grid=(N_SUPER,),
        in_specs=[pl.BlockSpec((t, LANES), lambda s: (0, s)), per(x0), per(k0), per(e0), per(f1), per(a1)],
        out_specs=[pl.BlockSpec((t, LANES), lambda s: (0, s)), per(x0)],
        out_shape=[jax.ShapeDtypeStruct(u.shape, F32), jax.ShapeDtypeStruct(x0.shape, F32)],
        compiler_params=_cparams("parallel"),
        name="s5_step",
    )(u, x0, k0, e0, f1, a1)


MSEQ = 2


def _mlstm_prompt_body(q_all, gc_all, *rest):
    gr_refs, (mg_ref, hm_all, c_out, n_out, m_out, c_all, n_all, m_all) = rest[:MSEQ], rest[MSEQ:]
    ci = pl.program_id(1)

    @pl.when(ci == 0)
    def _():
        c_all[...] = jnp.zeros_like(c_all)
        n_all[...] = jnp.zeros_like(n_all)
        m_all[...] = jnp.zeros_like(m_all)

    for sq in range(MSEQ):
        _mlstm_chunk(q_all.at[sq], gc_all.at[sq], gr_refs[sq], mg_ref, hm_all.at[sq],
                     c_all.at[sq], n_all.at[sq], m_all.at[sq])

    @pl.when(ci == pl.num_programs(1) - 1)
    def _():
        c_out[...] = c_all[...]
        n_out[...] = n_all[...]
        m_out[...] = m_all[...]


def _mlstm_chunk(q_ref, gc_ref, gr_ref, mg_ref, hm_ref, c_sc, n_sc, m_sc):
    lc = MCHUNK
    scale = DH ** -0.5
    row = lax.broadcasted_iota(jnp.int32, (lc, lc), 0)
    col = lax.broadcasted_iota(jnp.int32, (lc, lc), 1)
    causal = col <= row
    tri = causal.astype(BF16)
    ones = jnp.ones((lc, DH), BF16)
    gc = gc_ref[...]
    gr = gr_ref[...]
    gc_hi, gc_lo = _split_bf16(gc)
    bcol_all = _dot(tri, gc_hi) + _dot(tri, gc_lo)
    brow_all = _dot_exact_rhs(gr, tri, ((1,), (1,)))
    for hd in range(HEADS):
        q = q_ref[:, hd * DH:(hd + 1) * DH]
        k = q_ref[:, MWIDTH + hd * DH:MWIDTH + (hd + 1) * DH]
        v = q_ref[:, 2 * MWIDTH + hd * DH:2 * MWIDTH + (hd + 1) * DH]
        o = q_ref[:, 3 * MWIDTH + hd * DH:3 * MWIDTH + (hd + 1) * DH]
        i_col = gc[:, hd:hd + 1]
        b_col = bcol_all[:, HEADS + hd:HEADS + hd + 1]
        i_row = gr[hd:hd + 1, :]
        b_row = brow_all[HEADS + hd:HEADS + hd + 1, :]
        m_prev = m_sc[hd:hd + 1, :][:, :1]
        dm = jnp.where(causal, b_col - b_row + i_row, -jnp.inf)
        m_inter = b_col + m_prev
        m_t = jnp.maximum(m_inter, jnp.max(dm, axis=-1, keepdims=True))
        w_inter = jnp.exp(m_inter - m_t)
        w = jnp.exp(dm - m_t)
        s = _dot_dims(q, k, ((1,), (1,))) * scale * w
        sb = s.astype(BF16)
        c_prev = c_sc[hd]
        n_prev = n_sc[hd]
        num = w_inter * _dot(q, c_prev.astype(BF16)) + _dot(sb, v)
        den = w_inter * _dot(q, n_prev.astype(BF16)) + _dot(sb, ones)
        h = num / jnp.maximum(jnp.abs(den), jnp.exp(-m_t))
        hn = h * lax.rsqrt(jnp.mean(h * h, axis=-1, keepdims=True) + EPS)
        hm_ref[:, hd * DH:(hd + 1) * DH] = (hn * mg_ref[:, hd * DH:(hd + 1) * DH] * _sigmoid(o.astype(F32))).astype(BF16)
        m_last = m_t[lc - 1:lc, :]
        b_last = b_col[lc - 1:lc, :]
        decay = w_inter[lc - 1:lc, :]
        w_last = jnp.exp(b_last - b_col + i_col - m_last) * scale
        kw = (k.astype(F32) * w_last).astype(BF16)
        c_sc[hd] = decay * c_prev + _dot_dims(kw, v, ((0,), (0,)))
        n_sc[hd] = decay * n_prev + _dot_dims(kw, ones, ((0,), (0,)))
        m_sc[hd:hd + 1, :] = jnp.broadcast_to(m_last, (1, LANES))


def _mlstm_prompt(qkvo, gcol, grow, mnorm_g):
    nc = SEQ // MCHUNK
    q3 = qkvo.reshape(N_PROMPT, SEQ, 4 * MWIDTH)
    g3 = gcol.reshape(N_PROMPT, SEQ, LANES)
    seq_blk = lambda w: pl.BlockSpec((MSEQ, MCHUNK, w), lambda b, c: (b, c, 0))
    state = lambda shape: pl.BlockSpec((MSEQ,) + shape, lambda b, c: (b,) + (0,) * len(shape))
    gr_specs = [pl.BlockSpec((8, MCHUNK), functools.partial(lambda b, c, sq: (0, (b * MSEQ + sq) * nc + c), sq=sq))
                for sq in range(MSEQ)]
    hm, c_p, n_p, m_p = pl.pallas_call(
        _mlstm_prompt_body,
        grid=(N_PROMPT // MSEQ, nc),
        in_specs=[seq_blk(4 * MWIDTH), seq_blk(LANES)] + gr_specs + [pl.BlockSpec((1, MWIDTH), lambda b, c: (0, 0))],
        out_specs=[seq_blk(MWIDTH), state((HEADS, DH, DH)), state((HEADS, DH, DH)), state((8, LANES))],
        out_shape=[jax.ShapeDtypeStruct((N_PROMPT, SEQ, MWIDTH), BF16),
                   jax.ShapeDtypeStruct((N_PROMPT, HEADS, DH, DH), F32),
                   jax.ShapeDtypeStruct((N_PROMPT, HEADS, DH, DH), F32),
                   jax.ShapeDtypeStruct((N_PROMPT, 8, LANES), F32)],
        scratch_shapes=[pltpu.VMEM((MSEQ, HEADS, DH, DH), F32),
                        pltpu.VMEM((MSEQ, HEADS, DH, DH), F32),
                        pltpu.VMEM((MSEQ, 8, LANES), F32)],
        compiler_params=_cparams("parallel", "arbitrary"),
        name="mlstm_prompt",
    )(q3, g3, *([grow] * MSEQ), mnorm_g)
    return hm.reshape(N_PROMPT * SEQ, MWIDTH), c_p, n_p, m_p


MSTEP_TOK = 8


def _mlstm_step_body(q_ref, qkt_ref, ig_ref, lf_ref, m0_ref, c0_ref, n0_ref, mg_ref, hm_ref, c_out, n_out, m_out, qc_sc):
    scale = DH ** -0.5
    ig = ig_ref[...]
    m_inter = lf_ref[...] + m0_ref[...]
    m_t = jnp.maximum(m_inter, ig)
    w_inter_all = jnp.exp(m_inter - m_t)
    w_all = jnp.exp(ig - m_t) * scale
    floor_all = jnp.exp(-m_t)
    m_out[...] = m_t
    for hd in range(HEADS):
        sl = slice(hd * DH, (hd + 1) * DH)
        q = q_ref[:, sl].astype(F32)
        k = q_ref[:, MWIDTH + hd * DH:MWIDTH + (hd + 1) * DH].astype(F32)
        v = q_ref[:, 2 * MWIDTH + hd * DH:2 * MWIDTH + (hd + 1) * DH].astype(F32)
        o = q_ref[:, 3 * MWIDTH + hd * DH:3 * MWIDTH + (hd + 1) * DH].astype(F32)
        n0 = n0_ref[:, sl]
        wi = w_inter_all[:, hd:hd + 1]
        wk = w_all[:, hd:hd + 1]
        s = jnp.sum(q * k, axis=-1, keepdims=True) * wk
        qn = jnp.sum(q * n0, axis=-1, keepdims=True)
        for j in range(MSTEP_TOK):
            c0 = c0_ref[j, hd]
            q_col = qkt_ref[hd, :, j:j + 1]
            k_col = qkt_ref[HEADS + hd, :, j:j + 1]
            qc_sc[j:j + 1, sl] = jnp.sum(q_col * c0, axis=0, keepdims=True)
            c_out[j, hd] = wi[j:j + 1, :] * c0 + (wk[j:j + 1, :] * k_col) * v[j:j + 1, :]
        num = wi * qc_sc[:, sl] + s * v
        den = wi * qn + s
        h = num / jnp.maximum(jnp.abs(den), floor_all[:, hd:hd + 1])
        hn = h * lax.rsqrt(jnp.mean(h * h, axis=-1, keepdims=True) + EPS)
        hm_ref[:, sl] = hn * mg_ref[:, sl] * _sigmoid(o)
        n_out[:, sl] = wi * n0 + wk * k


def _mlstm_step(qkvo, qkt, ig, lf, m0, c0, n0, mnorm_g):
    t = qkvo.shape[0]
    tk = MSTEP_TOK
    return pl.pallas_call(
        _mlstm_step_body,
        grid=(t // tk,),
        in_specs=[pl.BlockSpec((tk, 4 * MWIDTH), lambda i: (i, 0)),
                  pl.BlockSpec((None, 2 * HEADS, DH, tk), lambda i: (i, 0, 0, 0)),
                  pl.BlockSpec((tk, LANES), lambda i: (i, 0)),
                  pl.BlockSpec((tk, LANES), lambda i: (i, 0)),
                  pl.BlockSpec((tk, LANES), lambda i: (i, 0)),
                  pl.BlockSpec((tk, HEADS, DH, DH), lambda i: (i, 0, 0, 0)),
                  pl.BlockSpec((tk, MWIDTH), lambda i: (i, 0)),
                  pl.BlockSpec((1, MWIDTH), lambda i: (0, 0))],
        out_specs=[pl.BlockSpec((tk, MWIDTH), lambda i: (i, 0)),
                   pl.BlockSpec((tk, HEADS, DH, DH), lambda i: (i, 0, 0, 0)),
                   pl.BlockSpec((tk, MWIDTH), lambda i: (i, 0)),
                   pl.BlockSpec((tk, LANES), lambda i: (i, 0))],
        out_shape=[jax.ShapeDtypeStruct((t, MWIDTH), F32),
                   jax.ShapeDtypeStruct((t, HEADS, DH, DH), F32),
                   jax.ShapeDtypeStruct((t, MWIDTH), F32),
                   jax.ShapeDtypeStruct((t, LANES), F32)],
        scratch_shapes=[pltpu.VMEM((tk, MWIDTH), F32)],
        compiler_params=_cparams("parallel"),
        name="mlstm_step",
    )(qkvo, qkt, ig, lf, m0, c0, n0, mnorm_g)


def _post_body(x_ref, y5_ref, hm_ref, gg_ref, g1_ref, sc_ref, sh_ref, n2_ref, wglu_ref, wmo_ref, wo_ref, rw_ref, rb_ref,
               cnt0_ref, x1_ref, h2_ref, eid_ref, wgt_ref, rank_ref, cnt_ref, run_sc, *, n_real, n_steps):
    i = pl.program_id(0)

    @pl.when(i == 0)
    def _():
        run_sc[...] = cnt0_ref[...]

    if n_steps > n_real:
        @pl.when(i >= n_real)
        def _():
            h2_ref[...] = jnp.zeros_like(h2_ref)

        pl.when(i < n_real)(functools.partial(
            _post_tile, x_ref, y5_ref, hm_ref, gg_ref, g1_ref, sc_ref, sh_ref, n2_ref, wglu_ref, wmo_ref, wo_ref,
            rw_ref, rb_ref, x1_ref, h2_ref, eid_ref, wgt_ref, rank_ref, run_sc))
    else:
        _post_tile(x_ref, y5_ref, hm_ref, gg_ref, g1_ref, sc_ref, sh_ref, n2_ref, wglu_ref, wmo_ref, wo_ref,
                   rw_ref, rb_ref, x1_ref, h2_ref, eid_ref, wgt_ref, rank_ref, run_sc)
    cnt_ref[...] = run_sc[...]


def _post_tile(x_ref, y5_ref, hm_ref, gg_ref, g1_ref, sc_ref, sh_ref, n2_ref, wglu_ref, wmo_ref, wo_ref, rw_ref, rb_ref,
               x1_ref, h2_ref, eid_ref, wgt_ref, rank_ref, run_sc):
    g5 = _gelu_tanh(y5_ref[...]).astype(BF16)
    glu = _dot(g5, wglu_ref[...])
    br_s5 = glu[:, :D_MODEL] * _sigmoid(glu[:, D_MODEL:])
    br_m = _dot(hm_ref[...].astype(BF16), wmo_ref[...])
    gg = gg_ref[...].astype(F32)
    merged = _sigmoid(gg[:, :D_MODEL]) * br_s5 + _sigmoid(gg[:, D_MODEL:]) * br_m
    x1 = x_ref[...] + g1_ref[...] * _dot(merged.astype(BF16), wo_ref[...])
    x1_ref[...] = x1
    ms = jnp.mean(x1 * x1, axis=-1, keepdims=True)
    h2 = x1 * lax.rsqrt(ms + EPS) * n2_ref[...] * (1.0 + sc_ref[...]) + sh_ref[...]
    h2_ref[...] = h2
    logits = _dot_hp(h2, rw_ref[...]) + rb_ref[...]
    tm = logits.shape[0]
    lane = lax.broadcasted_iota(jnp.int32, logits.shape, 1)
    l = jnp.where(lane < N_EXPERTS, logits, -jnp.inf)
    vals, hots, idxs = [], [], []
    for _ in range(TOP_K):
        mx = jnp.max(l, axis=-1, keepdims=True)
        idx = jnp.min(jnp.where(l == mx, lane, LANES), axis=-1, keepdims=True)
        hot = lane == idx
        vals.append(mx)
        hots.append(hot)
        idxs.append(idx)
        l = jnp.where(hot, -jnp.inf, l)
    ex = [jnp.exp(vk - vals[0]) for vk in vals]
    tot = ex[0] + ex[1] + ex[2] + ex[3]
    chosen = jnp.where(hots[0] | hots[1] | hots[2] | hots[3], 1.0, 0.0)
    r_io = lax.broadcasted_iota(jnp.int32, (tm, tm), 0)
    c_io = lax.broadcasted_iota(jnp.int32, (tm, tm), 1)
    earlier = (c_io < r_io).astype(BF16)
    before = run_sc[...] + _dot(earlier, chosen.astype(BF16))
    eid = jnp.zeros(logits.shape, jnp.int32)
    wgt = jnp.zeros(logits.shape, F32)
    rank = jnp.zeros(logits.shape, F32)
    for k in range(TOP_K):
        eid = jnp.where(lane == k, idxs[k], eid)
        wgt = jnp.where(lane == k, ex[k] / tot, wgt)
        rank = jnp.where(lane == k, jnp.sum(jnp.where(hots[k], before, 0.0), axis=-1, keepdims=True), rank)
    eid_ref[...] = eid
    wgt_ref[...] = wgt
    rank_ref[...] = rank.astype(jnp.int32)
    run_sc[...] += jnp.sum(chosen, axis=0, keepdims=True)


def _post(x2d, y5, hm, gg, mod3, rows_per_mod, tm, n2, wglu, wmo, wo, rw, rb, cnt0, h2_rows):
    t = x2d.shape[0]
    mrows = mod3.shape[1]
    n_real = t // tm
    n_steps = -(-h2_rows // tm)
    last = n_real - 1

    def mod_spec(j):
        return pl.BlockSpec((None, mrows, D_MODEL), lambda i: ((jnp.minimum(i, last) * tm) // rows_per_mod, 0, j))

    const = lambda shape: pl.BlockSpec(shape, lambda i: (0,) * len(shape))
    tile = lambda w: pl.BlockSpec((tm, w), lambda i: (jnp.minimum(i, last), 0))
    return pl.pallas_call(
        functools.partial(_post_body, n_real=n_real, n_steps=n_steps),
        grid=(n_steps,),
        in_specs=[tile(D_MODEL), tile(S5_WIDTH), tile(MWIDTH), tile(2 * D_MODEL),
                  mod_spec(2), mod_spec(4), mod_spec(3), const((1, D_MODEL)),
                  const(wglu.shape), const(wmo.shape), const(wo.shape), const(rw.shape), const(rb.shape),
                  const((1, LANES))],
        out_specs=[tile(D_MODEL), pl.BlockSpec((tm, D_MODEL), lambda i: (i, 0)), tile(LANES), tile(LANES), tile(LANES),
                   const((1, LANES))],
        out_shape=[jax.ShapeDtypeStruct((t, D_MODEL), F32),
                   jax.ShapeDtypeStruct((h2_rows, D_MODEL), F32),
                   jax.ShapeDtypeStruct((t, LANES), jnp.int32),
                   jax.ShapeDtypeStruct((t, LANES), F32),
                   jax.ShapeDtypeStruct((t, LANES), jnp.int32),
                   jax.ShapeDtypeStruct((1, LANES), F32)],
        scratch_shapes=[pltpu.VMEM((1, LANES), F32)],
        compiler_params=_cparams("arbitrary"),
        name="post",
    )(x2d, y5, hm, gg, mod3, mod3, mod3, n2, wglu, wmo, wo, rw, rb, cnt0)


N_TOKENS = N_PROMPT * SEQ + N_SAMPLE
N_ASSIGN = N_TOKENS * TOP_K
MOE_TILE = 256
MOE_TILES = N_ASSIGN // MOE_TILE + N_EXPERTS
N_SLOT = MOE_TILES * MOE_TILE
CMB_TILE = 512


MOE_RING = 3
MOE_SHORT = 64


def _moe_body(te_ref, nt_ref, nxe_ref, gpar_ref, rows_ref, idx0_ref, idx1_ref, idx2_ref, h2_hbm, wup_hbm, bup_ref,
              wdn_hbm, bdn_ref, y_ref, xb0, xb1, xb2, sem, wbuf_up, wbuf_dn, wsem, wup_bf, wdn_bf):
    i = pl.program_id(0)
    nt = nt_ref[0]
    bufs = (xb0, xb1, xb2)
    last = MOE_TILES - 1

    def gather(idx_ref, slot):
        for r in range(MOE_TILE):
            pltpu.make_async_copy(h2_hbm.at[pl.ds(idx_ref[0, r], 1)], bufs[slot].at[pl.ds(r, 1)],
                                  sem.at[slot]).start()

    def wait(slot):
        pltpu.make_async_copy(h2_hbm.at[pl.ds(0, MOE_TILE)], bufs[slot], sem.at[slot]).wait()

    def weights(e, slot):
        return (pltpu.make_async_copy(wup_hbm.at[e], wbuf_up.at[slot], wsem.at[0, slot]),
                pltpu.make_async_copy(wdn_hbm.at[e], wbuf_dn.at[slot], wsem.at[1, slot]))

    wslot = gpar_ref[i]

    @pl.when(i == 0)
    def _():
        for c in weights(te_ref[0], wslot):
            c.start(priority=1)
        gather(idx0_ref, 0)
        gather(idx1_ref, 1)

    new_expert = jnp.logical_or(i == 0, te_ref[i] != te_ref[jnp.maximum(i - 1, 0)])
    live = i < nt

    @pl.when(jnp.logical_and(new_expert, live))
    def _():
        for c in weights(0, wslot):
            c.wait()
        wup_bf[...] = wbuf_up[wslot].astype(BF16)
        wdn_bf[...] = wbuf_dn[wslot].astype(BF16)
        nxe = nxe_ref[i]

        @pl.when(nxe >= 0)
        def _():
            for c in weights(nxe, 1 - wslot):
                c.start(priority=1)

    def step(slot, rows):
        wait(slot)
        gather(idx2_ref, (slot + 2) % MOE_RING)
        gu = _dot(bufs[slot][:rows, :].astype(BF16), wup_bf[...]) + bup_ref[...]
        g = jnp.minimum(gu[:, :D_FF], SWIGLU_LIMIT)
        up = jnp.clip(gu[:, D_FF:], -SWIGLU_LIMIT, SWIGLU_LIMIT)
        act = (up + 1.0) * g * _sigmoid(SWIGLU_ALPHA * g)
        y_ref[:rows, :] = _dot(act.astype(BF16), wdn_bf[...]) + bdn_ref[...]
        if rows < MOE_TILE:
            y_ref[rows:, :] = jnp.zeros((MOE_TILE - rows, D_MODEL), F32)

    short = rows_ref[i] <= MOE_SHORT
    for slot in range(MOE_RING):
        mine = jnp.logical_and(live, i % MOE_RING == slot)
        pl.when(jnp.logical_and(mine, jnp.logical_not(short)))(functools.partial(step, slot, MOE_TILE))
        pl.when(jnp.logical_and(mine, short))(functools.partial(step, slot, MOE_SHORT))

    @pl.when(jnp.logical_not(live))
    def _():
        y_ref[...] = jnp.zeros_like(y_ref)

    drain = jnp.logical_and(jnp.logical_not(live), i < nt + 2)
    for slot in range(MOE_RING):
        pl.when(jnp.logical_and(drain, i % MOE_RING == slot))(functools.partial(wait, slot))
    pl.when(jnp.logical_and(i == last, last - 1 < nt))(functools.partial(wait, (last + 1) % MOE_RING))
    pl.when(jnp.logical_and(i == last, last < nt))(functools.partial(wait, (last + 2) % MOE_RING))


def _moe(tile_expert, n_tiles, next_expert, group_parity, tile_rows, slot_token, h2_all, wup, bup, wdn, bdn):
    idx3 = slot_token.reshape(MOE_TILES, 1, MOE_TILE)
    ahead = lambda k: pl.BlockSpec((None, 1, MOE_TILE),
                                   lambda i, *_: (jnp.minimum(i + k, MOE_TILES - 1), 0, 0), memory_space=pltpu.SMEM)
    grid_spec = pltpu.PrefetchScalarGridSpec(
        num_scalar_prefetch=5,
        grid=(MOE_TILES,),
        in_specs=[ahead(0), ahead(1), ahead(2),
                  pl.BlockSpec(memory_space=pl.ANY),
                  pl.BlockSpec(memory_space=pl.ANY),
                  pl.BlockSpec((None, 1, 2 * D_FF), lambda i, te, *_: (te[i], 0, 0)),
                  pl.BlockSpec(memory_space=pl.ANY),
                  pl.BlockSpec((None, 1, D_MODEL), lambda i, te, *_: (te[i], 0, 0))],
        out_specs=pl.BlockSpec((MOE_TILE, D_MODEL), lambda i, *_: (i, 0)),
        scratch_shapes=[pltpu.VMEM((MOE_TILE, D_MODEL), F32), pltpu.VMEM((MOE_TILE, D_MODEL), F32),
                        pltpu.VMEM((MOE_TILE, D_MODEL), F32),
                        pltpu.SemaphoreType.DMA((MOE_RING,)),
                        pltpu.VMEM((2, D_MODEL, 2 * D_FF), F32), pltpu.VMEM((2, D_FF, D_MODEL), F32),
                        pltpu.SemaphoreType.DMA((2, 2)),
                        pltpu.VMEM((D_MODEL, 2 * D_FF), BF16), pltpu.VMEM((D_FF, D_MODEL), BF16)])
    return pl.pallas_call(
        _moe_body,
        grid_spec=grid_spec,
        out_shape=jax.ShapeDtypeStruct((N_SLOT, D_MODEL), F32),
        compiler_params=_cparams("arbitrary"),
        name="moe",
    )(tile_expert, n_tiles, next_expert, group_parity, tile_rows, idx3, idx3, idx3, h2_all, wup, bup, wdn, bdn)


def _combine_body(cur_ref, nxt_ref, wgt_ref, x1_ref, g2_ref, fg_ref, y_hbm, o_ref, yb0, yb1, sem, *, n_steps, tm):
    i = pl.program_id(0)

    def gather(idx_ref, buf, s):
        for r in range(tm):
            for k in range(TOP_K):
                pltpu.make_async_copy(y_hbm.at[pl.ds(idx_ref[0, r * TOP_K + k], 1)],
                                      buf.at[pl.ds(k * tm + r, 1)], sem.at[s]).start(priority=k % 2)

    def wait(buf, s):
        pltpu.make_async_copy(y_hbm.at[pl.ds(0, TOP_K * tm)], buf, sem.at[s]).wait()

    @pl.when(i == 0)
    def _():
        gather(cur_ref, yb0, 0)

    def step(cur, nxt, s_cur, s_nxt):
        wait(cur, s_cur)
        gather(nxt_ref, nxt, s_nxt)
        wgt = wgt_ref[...]
        acc = wgt[:, 0:1] * cur[0:tm, :]
        for k in range(1, TOP_K):
            acc = acc + wgt[:, k:k + 1] * cur[k * tm:(k + 1) * tm, :]
        xo = x1_ref[...] + g2_ref[...] * acc
        ms = jnp.mean(xo * xo, axis=-1, keepdims=True)
        o_ref[...] = xo * lax.rsqrt(ms + EPS) * fg_ref[...]

    pl.when(i % 2 == 0)(functools.partial(step, yb0, yb1, 0, 1))
    pl.when(i % 2 == 1)(functools.partial(step, yb1, yb0, 1, 0))
    last_slot = n_steps % 2
    pl.when(i == n_steps - 1)(functools.partial(wait, yb1 if last_slot else yb0, last_slot))


def _combine(pos, wgt, x1, mod3, rows_per_mod, fg, y_slots, tm):
    t = x1.shape[0]
    n = t // tm
    mrows = mod3.shape[1]
    assert mrows in (1, tm) and rows_per_mod % tm == 0
    pos3 = pos.reshape(n, 1, tm * TOP_K)
    return pl.pallas_call(
        functools.partial(_combine_body, n_steps=n, tm=tm),
        grid=(n,),
        in_specs=[pl.BlockSpec((None, 1, tm * TOP_K), lambda i: (i, 0, 0), memory_space=pltpu.SMEM),
                  pl.BlockSpec((None, 1, tm * TOP_K), lambda i: (jnp.minimum(i + 1, n - 1), 0, 0),
                               memory_space=pltpu.SMEM),
                  pl.BlockSpec((tm, LANES), lambda i: (i, 0)),
                  pl.BlockSpec((tm, D_MODEL), lambda i: (i, 0)),
                  pl.BlockSpec((None, mrows, D_MODEL), lambda i: ((i * tm) // rows_per_mod, 0, 5)),
                  pl.BlockSpec((1, D_MODEL), lambda i: (0, 0)),
                  pl.BlockSpec(memory_space=pl.ANY)],
        out_specs=pl.BlockSpec((tm, D_MODEL), lambda i: (i, 0)),
        out_shape=jax.ShapeDtypeStruct((t, D_MODEL), F32),
        scratch_shapes=[pltpu.VMEM((TOP_K * tm, D_MODEL), F32), pltpu.VMEM((TOP_K * tm, D_MODEL), F32),
                        pltpu.SemaphoreType.DMA((2,))],
        compiler_params=_cparams("arbitrary"),
        name="combine",
    )(pos3, pos3, wgt, x1, mod3, fg, y_slots)


def _routing(eid_p, rank_p, eid_s, rank_s, counts):
    cnt = counts[0, :N_EXPERTS].astype(jnp.int32)
    ntile = (cnt + MOE_TILE - 1) // MOE_TILE
    tile_end = jnp.cumsum(ntile)
    poff = (tile_end - ntile) * MOE_TILE
    total = tile_end[-1]
    j = jnp.arange(MOE_TILES, dtype=jnp.int32)
    te = jnp.sum((j[:, None] >= tile_end[None, :]).astype(jnp.int32), axis=1)
    te_last = jnp.sum(((total - 1) >= tile_end).astype(jnp.int32))
    tile_expert = jnp.where(j < total, te, te_last).astype(jnp.int32)
    ex = jnp.arange(N_EXPERTS, dtype=jnp.int32)
    used = ntile > 0
    later = jnp.where(used[None, :] & (ex[None, :] > ex[:, None]), ex[None, :], N_EXPERTS)
    nxt_e = jnp.min(later, axis=1)
    nxt_e = jnp.where(nxt_e < N_EXPERTS, nxt_e, -1).astype(jnp.int32)
    par_e = ((jnp.cumsum(used.astype(jnp.int32)) - 1) % 2).astype(jnp.int32)
    pick = tile_expert[:, None] == ex[None, :]
    next_expert = jnp.sum(jnp.where(pick, nxt_e[None, :], 0), axis=1).astype(jnp.int32)
    group_parity = jnp.sum(jnp.where(pick, par_e[None, :], 0), axis=1).astype(jnp.int32)
    left = jnp.sum(jnp.where(pick, (cnt + poff)[None, :], 0), axis=1) - j * MOE_TILE
    tile_rows = jnp.where(j < total, jnp.clip(left, 0, MOE_TILE), 0).astype(jnp.int32)

    def pos_of(eid, rank):
        e = eid[:, :TOP_K]
        off = jnp.sum(jnp.where(e[:, :, None] == ex[None, None, :], poff[None, None, :], 0), axis=-1)
        return off + rank[:, :TOP_K]

    pos_p = pos_of(eid_p, rank_p)
    pos_s = pos_of(eid_s, rank_s)
    tok = jnp.concatenate([jnp.repeat(jnp.arange(N_PROMPT * SEQ, dtype=jnp.int32), TOP_K),
                           jnp.repeat(N_PROMPT * SEQ + jnp.arange(N_SAMPLE, dtype=jnp.int32), TOP_K)])
    pos_all = jnp.concatenate([pos_p.reshape(-1), pos_s.reshape(-1)])
    padc = jnp.concatenate([ntile * MOE_TILE - cnt, (N_SLOT - total * MOE_TILE).reshape(1)])
    pad_end = jnp.cumsum(padc)
    pad_base = jnp.concatenate([poff + cnt, (total * MOE_TILE).reshape(1)])
    jp = jnp.arange(N_SLOT - N_ASSIGN, dtype=jnp.int32)
    pe = jnp.sum((jp[:, None] >= pad_end[None, :]).astype(jnp.int32), axis=1)
    hit = pe[:, None] == jnp.arange(N_EXPERTS + 1, dtype=jnp.int32)[None, :]
    pad_slot = jp + jnp.sum(jnp.where(hit, (pad_base - (pad_end - padc))[None, :], 0), axis=1)
    keys = jnp.concatenate([pos_all, pad_slot.astype(jnp.int32)])
    vals = jnp.concatenate([tok, jnp.zeros((N_SLOT - N_ASSIGN,), jnp.int32)])
    tok_bits = 15
    assert N_TOKENS <= (1 << tok_bits) and N_SLOT <= (1 << (32 - tok_bits))
    packed = lax.sort((keys.astype(jnp.uint32) << tok_bits) | vals.astype(jnp.uint32))
    slot_token = (packed & ((1 << tok_bits) - 1)).astype(jnp.int32)
    return (tile_expert, total.reshape(1).astype(jnp.int32), next_expert, group_parity, tile_rows, slot_token,
            pos_p, pos_s)


def _unpack_s5_state(x):
    n = x.shape[1]
    z = x.reshape(N_SUPER, n, 2, S5_SUPER, S5_STATE).transpose(2, 1, 0, 3, 4).reshape(2, n, S5_GROUPS, S5_STATE)
    return z[0], z[1]


def _pack_s5_state(re, im):
    n = re.shape[0]
    z = jnp.stack([re, im], axis=0).reshape(2, n, N_SUPER, S5_SUPER, S5_STATE)
    return z.transpose(2, 1, 0, 3, 4).reshape(N_SUPER, n, SUPER_STATE)


def kernel(x_prompt, x_sample, c_prompt, c_sample, state_s5_re, state_s5_im, state_mlstm_C, state_mlstm_n, state_mlstm_m, norm1_g, norm2_g, final_norm_g, w_ada, b_ada, w_in, s5_lambda_re, s5_lambda_im, s5_log_dt, s5_B_re, s5_B_im, s5_C_re, s5_C_im, s5_D, s5_w_glu, mlstm_b_i, mlstm_b_f, mlstm_norm_g, mlstm_w_out, w_out, router_w, router_b, expert_w_up, expert_b_up, expert_w_down, expert_b_down):
    assert w_in.shape[0] == 1, "single layer"
    tp = N_PROMPT * SEQ
    xp = x_prompt.reshape(tp, D_MODEL).astype(F32)
    xs = x_sample.reshape(N_SAMPLE, D_MODEL).astype(F32)

    w = w_in[0]
    c0 = S5_WIDTH
    c1 = c0 + 4 * MWIDTH
    c2 = c1 + 2 * HEADS
    wu = w[:, :c0].astype(BF16)
    wq = w[:, c0:c1].astype(BF16)
    wif = jnp.pad(w[:, c1:c2].astype(F32), ((0, 0), (0, LANES - 2 * HEADS)))
    wg = w[:, c2:].astype(BF16)
    bvec = jnp.concatenate([mlstm_b_i[0], mlstm_b_f[0]]).astype(F32)
    bif = jnp.pad(bvec, (0, LANES - 2 * HEADS)).reshape(1, LANES)
    wparts = (wu, wq, wg, wif, bif)
    g1 = norm1_g[0].reshape(1, D_MODEL).astype(F32)
    n2 = norm2_g[0].reshape(1, D_MODEL).astype(F32)
    fg = final_norm_g.reshape(1, D_MODEL).astype(F32)
    mg = mlstm_norm_g[0].reshape(1, MWIDTH).astype(F32)
    wglu = s5_w_glu[0].astype(BF16)
    wmo = mlstm_w_out[0].astype(BF16)
    wo = w_out[0].astype(BF16)
    rw = jnp.pad(router_w[0].astype(F32), ((0, 0), (0, LANES - N_EXPERTS)))
    rb = jnp.pad(router_b[0].astype(F32), (0, LANES - N_EXPERTS)).reshape(1, LANES)
    wup = expert_w_up[0].astype(F32)
    wdn = expert_w_down[0].astype(F32)
    bup = expert_b_up[0].astype(F32).reshape(N_EXPERTS, 1, 2 * D_FF)
    bdn = expert_b_down[0].astype(F32).reshape(N_EXPERTS, 1, D_MODEL)
    ddiag, fcomp, ecomp, a16, a1 = _s5_tables(
        s5_lambda_re[0], s5_lambda_im[0], s5_log_dt[0], s5_B_re[0], s5_B_im[0], s5_C_re[0], s5_C_im[0], s5_D[0])

    c_all = jnp.concatenate([c_prompt, c_sample], axis=0).astype(F32)
    mod = _adaln(c_all, w_ada[0].astype(F32), b_ada[0].astype(F32))
    mod_p = mod[:N_PROMPT].reshape(N_PROMPT, 1, 6 * D_MODEL)
    mod_s = mod[N_PROMPT:].reshape(1, N_SAMPLE, 6 * D_MODEL)

    u, qkvo, gg, gcol, grow = _inproj(xp, mod_p, SEQ, 1024, g1, wparts)
    y5, xend, k0, e0, f1 = _s5_prompt(u, ddiag, fcomp, ecomp, a16, a1)
    hm, c_p, n_p, m_p = _mlstm_prompt(qkvo, gcol, grow, mg)
    cnt0 = jnp.zeros((1, LANES), F32)
    x1, h2_all, eid_p, wgt_p, rank_p, cnt_p = _post(xp, y5, hm, gg, mod_p, SEQ, 512, n2, wglu, wmo, wo, rw, rb,
                                                    cnt0, N_TOKENS)
    p_re, p_im = _unpack_s5_state(xend.reshape(N_SUPER, N_PROMPT, SUPER_STATE))
    n_p = n_p[..., 0]
    m_p = m_p[:, :HEADS, 0]

    us, qs, ggs, gcs, _ = _inproj(xs, mod_s, N_SAMPLE, N_SAMPLE, g1, wparts)
    x0 = _pack_s5_state(state_s5_re[0].astype(F32), state_s5_im[0].astype(F32))
    y5s, xns = _s5_step(us, x0, k0, e0, f1, a1)
    s_re, s_im = _unpack_s5_state(xns)
    qkt = qs[:, :2 * MWIDTH].astype(F32).reshape(N_SAMPLE // MSTEP_TOK, MSTEP_TOK, 2 * HEADS, DH).transpose(0, 2, 3, 1)
    pad_heads = lambda a: jnp.pad(a, ((0, 0), (0, LANES - HEADS)))
    m0 = pad_heads(state_mlstm_m[0].astype(F32))
    hms, c_s, n_s, m_s = _mlstm_step(qs, qkt, pad_heads(gcs[:, :HEADS]), pad_heads(gcs[:, HEADS:2 * HEADS]), m0,
                                     state_mlstm_C[0].astype(F32),
                                     state_mlstm_n[0].astype(F32).reshape(N_SAMPLE, MWIDTH), mg)
    x1s, h2s, eid_s, wgt_s, rank_s, cnt_all = _post(xs, y5s, hms, ggs, mod_s, N_SAMPLE, N_SAMPLE, n2, wglu, wmo, wo, rw, rb,
                                                    cnt_p, N_SAMPLE)

    h2_all = lax.dynamic_update_slice(h2_all, h2s, (tp, 0))
    tile_expert, n_tiles, next_expert, group_parity, tile_rows, slot_token, pos_p, pos_s = _routing(
        eid_p, rank_p, eid_s, rank_s, cnt_all)
    y_slots = _moe(tile_expert, n_tiles, next_expert, group_parity, tile_rows, slot_token, h2_all, wup, bup, wdn, bdn)
    y_p = _combine(pos_p, wgt_p, x1, mod_p, SEQ, fg, y_slots, CMB_TILE)
    y_s = _combine(pos_s, wgt_s, x1s, mod_s, N_SAMPLE, fg, y_slots, N_SAMPLE)

    return (y_p.reshape(N_PROMPT, SEQ, D_MODEL).astype(x_prompt.dtype),
            y_s.reshape(N_SAMPLE, 1, D_MODEL).astype(x_sample.dtype),
            p_re[None], p_im[None], c_p[None], n_p[None], m_p[None],
            s_re[None], s_im[None], c_s[None],
            n_s.reshape(1, N_SAMPLE, HEADS, DH), m_s[:, :HEADS][None])
```

```python
import functools
import math

import jax
import jax.numpy as jnp
from jax import lax
from jax.experimental import pallas as pl
from jax.experimental.pallas import tpu as pltpu

F32 = jnp.float32
BF16 = jnp.bfloat16

D_MODEL = 1024
SEQ = 2048
N_PROMPT = 8
N_SAMPLE = 128
S5_WIDTH = 512
S5_GROUP = 16
S5_GROUPS = 32
S5_STATE = 64
HEADS = 4
DH = 128
MWIDTH = HEADS * DH
N_EXPERTS = 32
TOP_K = 4
D_FF = 1024
SWIGLU_LIMIT = 7.0
SWIGLU_ALPHA = 1.702
EPS = 1e-6

LANES = 128
S5_CHUNK = 16
S5_SUPER = LANES // S5_GROUP
N_SUPER = S5_GROUPS // S5_SUPER
SUPER_STATE = 2 * S5_SUPER * S5_STATE
MCHUNK = 128
VMEM_LIMIT = 56 * 1024 * 1024


def _cparams(*sem):
    return pltpu.CompilerParams(dimension_semantics=sem, vmem_limit_bytes=VMEM_LIMIT)


def _dot(a, b):
    return jnp.dot(a, b, preferred_element_type=F32)


def _dot_dims(a, b, dims):
    return lax.dot_general(a, b, (dims, ((), ())), preferred_element_type=F32)


def _split_bf16(a):
    hi = a.astype(BF16)
    lo = (a - hi.astype(F32)).astype(BF16)
    return hi, lo


def _dot_hp(a, b, dims=((1,), (0,))):
    ah, al = _split_bf16(a)
    bh, bl = _split_bf16(b)
    return _dot_dims(ah, bh, dims) + (_dot_dims(al, bh, dims) + _dot_dims(ah, bl, dims))


def _dot_exact_rhs(a, b_exact, dims=((1,), (0,))):
    ah, al = _split_bf16(a)
    return _dot_dims(ah, b_exact, dims) + _dot_dims(al, b_exact, dims)


def _log_sigmoid(x):
    return -(jnp.maximum(-x, 0.0) + jnp.log1p(jnp.exp(-jnp.abs(x))))


def _sigmoid(x):
    return 0.5 * (1.0 + jnp.tanh(0.5 * x))


def _gelu_tanh(x):
    c = math.sqrt(2.0 / math.pi)
    return 0.5 * x * (1.0 + jnp.tanh(c * (x + 0.044715 * (x * x * x))))


def _adaln_body(c_ref, w_ref, b_ref, o_ref):
    c = c_ref[...]
    s = c * _sigmoid(c)
    o_ref[...] = _dot_hp(s, w_ref[...]) + b_ref[...]


def _adaln(c_all, w_ada, b_ada):
    n = c_all.shape[0]
    tn = 1024
    return pl.pallas_call(
        _adaln_body,
        grid=(6 * D_MODEL // tn,),
        in_specs=[pl.BlockSpec((n, D_MODEL), lambda j: (0, 0)),
                  pl.BlockSpec((D_MODEL, tn), lambda j: (0, j)),
                  pl.BlockSpec((1, tn), lambda j: (0, j))],
        out_specs=pl.BlockSpec((n, tn), lambda j: (0, j)),
        out_shape=jax.ShapeDtypeStruct((n, 6 * D_MODEL), F32),
        compiler_params=_cparams("parallel"),
        name="adaln",
    )(c_all, w_ada, b_ada.reshape(1, -1))


def _inproj_body(x_ref, g1_ref, sc_ref, sh_ref, wu_ref, wq_ref, wg_ref, wif_ref, bif_ref,
                 u_ref, q_ref, gg_ref, if_ref, ift_ref):
    x = x_ref[...]
    ms = jnp.mean(x * x, axis=-1, keepdims=True)
    h = x * lax.rsqrt(ms + EPS) * g1_ref[...] * (1.0 + sc_ref[...]) + sh_ref[...]
    hb = h.astype(BF16)
    u_ref[...] = _dot(hb, wu_ref[...])
    q_ref[...] = _dot(hb, wq_ref[...]).astype(BF16)
    gg_ref[...] = _dot(hb, wg_ref[...]).astype(BF16)
    gc = _dot_hp(h, wif_ref[...]) + bif_ref[...]
    lane = lax.broadcasted_iota(jnp.int32, gc.shape, 1)
    gates = jnp.where(lane < HEADS, gc, _log_sigmoid(gc))
    if_ref[...] = gates
    ift_ref[...] = gates.T[:2 * HEADS, :]


def _inproj(x2d, mod3, rows_per_mod, tm, g1, wparts):
    t = x2d.shape[0]
    wu, wq, wg, wif, bif = wparts
    mrows = mod3.shape[1]

    def mod_spec(j):
        return pl.BlockSpec((None, mrows, D_MODEL), lambda i: ((i * tm) // rows_per_mod, 0, j))

    const = lambda shape: pl.BlockSpec(shape, lambda i: (0,) * len(shape), pipeline_mode=pl.Buffered(1))
    return pl.pallas_call(
        _inproj_body,
        grid=(t // tm,),
        in_specs=[pl.BlockSpec((tm, D_MODEL), lambda i: (i, 0)),
                  const((1, D_MODEL)), mod_spec(1), mod_spec(0),
                  const(wu.shape), const(wq.shape), const(wg.shape), const(wif.shape), const(bif.shape)],
        out_specs=[pl.BlockSpec((tm, S5_WIDTH), lambda i: (i, 0)),
                   pl.BlockSpec((tm, 4 * MWIDTH), lambda i: (i, 0)),
                   pl.BlockSpec((tm, 2 * D_MODEL), lambda i: (i, 0)),
                   pl.BlockSpec((tm, LANES), lambda i: (i, 0)),
                   pl.BlockSpec((8, tm), lambda i: (0, i))],
        out_shape=[jax.ShapeDtypeStruct((t, S5_WIDTH), F32),
                   jax.ShapeDtypeStruct((t, 4 * MWIDTH), BF16),
                   jax.ShapeDtypeStruct((t, 2 * D_MODEL), BF16),
                   jax.ShapeDtypeStruct((t, LANES), F32),
                   jax.ShapeDtypeStruct((8, t), F32)],
        compiler_params=_cparams("parallel"),
        name="inproj",
    )(x2d, g1, mod3, mod3, wu, wq, wg, wif, bif)


def _s5_tables(lam_re, lam_im, log_dt, b_re, b_im, c_re, c_im, d_s5):
    hi = lax.Precision.HIGHEST
    dt = jnp.exp(log_dt.astype(F32))[:, None]
    lr, li = lam_re.astype(F32), lam_im.astype(F32)
    dpow = jnp.arange(S5_CHUNK + 1, dtype=F32)[:, None, None]
    mag = jnp.exp(dpow * (lr * dt))
    pw_re, pw_im = mag * jnp.cos(dpow * (li * dt)), mag * jnp.sin(dpow * (li * dt))
    ab_re, ab_im = pw_re[1], pw_im[1]
    den = lr * lr + li * li
    nr, ni = ab_re - 1.0, ab_im
    coef_re = (nr * lr + ni * li) / den
    coef_im = (ni * lr - nr * li) / den
    br, bi = b_re.astype(F32), b_im.astype(F32)
    bb_re = coef_re[..., None] * br - coef_im[..., None] * bi
    bb_im = coef_re[..., None] * bi + coef_im[..., None] * br
    cr, ci = c_re.astype(F32), c_im.astype(F32)
    cl_re = cr[None] * pw_re[:, :, None, :] - ci[None] * pw_im[:, :, None, :]
    cl_im = cr[None] * pw_im[:, :, None, :] + ci[None] * pw_re[:, :, None, :]
    ddiag = (d_s5.astype(F32).reshape(N_SUPER, LANES, 1) * jnp.eye(LANES, dtype=F32)[None])
    rp_re, rp_im = pw_re[S5_CHUNK - 1::-1], pw_im[S5_CHUNK - 1::-1]
    f_re = rp_re[..., None] * bb_re[None] - rp_im[..., None] * bb_im[None]
    f_im = rp_re[..., None] * bb_im[None] + rp_im[..., None] * bb_re[None]
    fcat = jnp.stack([f_re, f_im], axis=2).transpose(0, 1, 4, 2, 3)
    fcomp = fcat.reshape(S5_CHUNK, N_SUPER, LANES, 2 * S5_STATE).transpose(1, 0, 2, 3)
    fcomp = fcomp.reshape(N_SUPER, S5_CHUNK * LANES, 2 * S5_STATE)
    ecat = jnp.stack([cl_re[:S5_CHUNK], -cl_im[:S5_CHUNK]], axis=0)
    er = ecat.transpose(2, 0, 4, 1, 3).reshape(N_SUPER, S5_SUPER, 2, S5_STATE, S5_CHUNK * S5_GROUP)
    ecomp = er.transpose(0, 2, 1, 3, 4).reshape(N_SUPER, SUPER_STATE, S5_CHUNK * S5_GROUP)

    def lay(re, im):
        z = jnp.stack([re, im], axis=0).reshape(2, N_SUPER, S5_SUPER * S5_STATE)
        return z.transpose(1, 0, 2).reshape(N_SUPER, 1, SUPER_STATE)

    a16 = lay(pw_re[S5_CHUNK], pw_im[S5_CHUNK])
    a1 = lay(ab_re, ab_im)
    return ddiag, fcomp, ecomp, a16, a1


def _expand_groups(comp, row_div, col_blk, n_cols, col_div):
    r, c = comp.shape
    ci = lax.broadcasted_iota(jnp.int32, (c, n_cols), 0)
    co = lax.broadcasted_iota(jnp.int32, (c, n_cols), 1)
    sel = jnp.logical_and(ci // col_blk == co // (S5_SUPER * col_blk), ci % col_blk == co % col_blk)
    rep = _dot(comp.astype(BF16), sel.astype(BF16))
    ro = lax.broadcasted_iota(jnp.int32, (r, n_cols), 0)
    cc = lax.broadcasted_iota(jnp.int32, (r, n_cols), 1)
    same = (ro // row_div) % S5_SUPER == (cc // col_div) % S5_SUPER
    return jnp.where(same, rep, 0.0).astype(BF16)


S5_NB = 4


def _s5_prompt_body(u_ref, dd_ref, fc_ref, ec_ref, a_ref, a1_ref, y_ref, xe_ref, k0_ref, e0_ref, f1_ref,
                    ucat, v_sc, xp_sc, kp_ref, f_ref, e_ref):
    n = SEQ // S5_CHUNK
    half = SUPER_STATE // 2

    @pl.when(pl.program_id(1) == 0)
    def _():
        f_ref[...] = _expand_groups(fc_ref[...], S5_GROUP, S5_STATE, SUPER_STATE, S5_STATE)
        e_ref[...] = _expand_groups(ec_ref[...], S5_STATE, S5_GROUP, S5_CHUNK * LANES, S5_GROUP)
        f1 = f_ref[(S5_CHUNK - 1) * LANES:, :]
        kall = _dot(f1, e_ref[...])
        for r in range(S5_CHUNK):
            for hf in range(2):
                d = S5_CHUNK - 2 - r + hf
                blk = jnp.zeros((LANES, LANES), F32) if d < 0 else kall[:, d * LANES:(d + 1) * LANES]
                if d == 0:
                    blk = blk + dd_ref[...]
                kp_ref[r * LANES:(r + 1) * LANES, hf * LANES:(hf + 1) * LANES] = blk.astype(BF16)
        k0_ref[...] = kp_ref[(S5_CHUNK - 2) * LANES:(S5_CHUNK - 1) * LANES, :LANES]
        e0_ref[...] = e_ref[:, LANES:2 * LANES]
        f1_ref[...] = f1

    for bl in range(S5_NB):
        for j in range(S5_CHUNK):
            ucat[bl * n:(bl + 1) * n, j * LANES:(j + 1) * LANES] = (
                u_ref[pl.ds(bl * SEQ + j, n, stride=S5_CHUNK), :].astype(BF16))
    v_sc[...] = _dot(ucat[...], f_ref[...])
    a_r, a_i = a_ref[:, :half], a_ref[:, half:]
    xr = [jnp.zeros((1, half), F32) for _ in range(S5_NB)]
    xi = [jnp.zeros((1, half), F32) for _ in range(S5_NB)]
    for c in range(n):
        for bl in range(S5_NB):
            row = bl * n + c
            xp_sc[row:row + 1, :half] = xr[bl]
            xp_sc[row:row + 1, half:] = xi[bl]
            vr, vi = v_sc[row:row + 1, :half], v_sc[row:row + 1, half:]
            xr[bl], xi[bl] = a_r * xr[bl] - a_i * xi[bl] + vr, a_r * xi[bl] + a_i * xr[bl] + vi
    for bl in range(S5_NB):
        xe_ref[bl:bl + 1, :half] = xr[bl]
        xe_ref[bl:bl + 1, half:] = xi[bl]
    l_r, l_i = a1_ref[:, :half], a1_ref[:, half:]
    xpr, xpi = xp_sc[:, :half], xp_sc[:, half:]
    z = jnp.concatenate([l_r * xpr - l_i * xpi, l_r * xpi + l_i * xpr], axis=1).astype(BF16)
    ye = _dot(z, e_ref[...])
    for m in range(S5_CHUNK // 2):
        rows = (2 * m + 2) * LANES
        yp = _dot(ucat[:, :rows], kp_ref[(S5_CHUNK - 2 - 2 * m) * LANES:, :]) + ye[:, 2 * m * LANES:(2 * m + 2) * LANES]
        for bl in range(S5_NB):
            y_ref[pl.ds(bl * SEQ + 2 * m, n, stride=S5_CHUNK), :] = yp[bl * n:(bl + 1) * n, :LANES]
            y_ref[pl.ds(bl * SEQ + 2 * m + 1, n, stride=S5_CHUNK), :] = yp[bl * n:(bl + 1) * n, LANES:]


def _s5_prompt(u, ddiag, fcomp, ecomp, a16, a1):
    n = S5_NB * SEQ // S5_CHUNK
    per_super = lambda shape: pl.BlockSpec((None,) + shape, lambda s, b: (s, 0, 0))
    return pl.pallas_call(
        _s5_prompt_body,
        grid=(N_SUPER, N_PROMPT // S5_NB),
        in_specs=[pl.BlockSpec((S5_NB * SEQ, LANES), lambda s, b: (b, s)),
                  per_super(ddiag.shape[1:]), per_super(fcomp.shape[1:]), per_super(ecomp.shape[1:]),
                  per_super((1, SUPER_STATE)), per_super((1, SUPER_STATE))],
        out_specs=[pl.BlockSpec((S5_NB * SEQ, LANES), lambda s, b: (b, s)),
                   pl.BlockSpec((None, None, S5_NB, SUPER_STATE), lambda s, b: (s, b, 0, 0)),
                   per_super((LANES, LANES)), per_super((SUPER_STATE, LANES)), per_super((LANES, SUPER_STATE))],
        out_shape=[jax.ShapeDtypeStruct(u.shape, F32),
                   jax.ShapeDtypeStruct((N_SUPER, N_PROMPT // S5_NB, S5_NB, SUPER_STATE), F32),
                   jax.ShapeDtypeStruct((N_SUPER, LANES, LANES), BF16),
                   jax.ShapeDtypeStruct((N_SUPER, SUPER_STATE, LANES), BF16),
                   jax.ShapeDtypeStruct((N_SUPER, LANES, SUPER_STATE), BF16)],
        scratch_shapes=[pltpu.VMEM((n, S5_CHUNK * LANES), BF16),
                        pltpu.VMEM((n, SUPER_STATE), F32),
                        pltpu.VMEM((n, SUPER_STATE), F32),
                        pltpu.VMEM((S5_CHUNK * LANES, 2 * LANES), BF16),
                        pltpu.VMEM((S5_CHUNK * LANES, SUPER_STATE), BF16),
                        pltpu.VMEM((SUPER_STATE, S5_CHUNK * LANES), BF16)],
        compiler_params=_cparams("parallel", "arbitrary"),
        name="s5_prompt",
    )(u, ddiag, fcomp, ecomp, a16, a1)


def _s5_step_body(u_ref, x0_ref, k0_ref, e0_ref, f1_ref, a_ref, y_ref, xn_ref):
    half = SUPER_STATE // 2
    ub = u_ref[...].astype(BF16)
    x0 = x0_ref[...]
    bu = _dot(ub, f1_ref[...])
    a_r, a_i = a_ref[:, :half], a_ref[:, half:]
    x0r, x0i = x0[:, :half], x0[:, half:]
    xn_ref[:, :half] = a_r * x0r - a_i * x0i + bu[:, :half]
    xn_ref[:, half:] = a_r * x0i + a_i * x0r + bu[:, half:]
    y_ref[...] = _dot(x0.astype(BF16), e0_ref[...]) + _dot(ub, k0_ref[...])


def _s5_step(u, x0, k0, e0, f1, a1):
    t = u.shape[0]
    per = lambda a: pl.BlockSpec((None,) + a.shape[1:], lambda s: (s,) + (0,) * (a.ndim - 1))
    return pl.pallas_call(
        _s5_step_body,
        grid=(N_SUPER,),
        in_specs=[pl.BlockSpec((t, LANES), lambda s: (0, s)), per(x0), per(k0), per(e0), per(f1), per(a1)],
        out_specs=[pl.BlockSpec((t, LANES), lambda s: (0, s)), per(x0)],
        out_shape=[jax.ShapeDtypeStruct(u.shape, F32), jax.ShapeDtypeStruct(x0.shape, F32)],
        compiler_params=_cparams("parallel"),
        name="s5_step",
    )(u, x0, k0, e0, f1, a1)


MSEQ = 2


def _mlstm_prompt_body(q_all, gc_all, *rest):
    gr_refs, (mg_ref, hm_all, c_out, n_out, m_out, c_all, n_all, m_all) = rest[:MSEQ], rest[MSEQ:]
    ci = pl.program_id(1)

    @pl.when(ci == 0)
    def _():
        c_all[...] = jnp.zeros_like(c_all)
        n_all[...] = jnp.zeros_like(n_all)
        m_all[...] = jnp.zeros_like(m_all)

    for sq in range(MSEQ):
        _mlstm_chunk(q_all.at[sq], gc_all.at[sq], gr_refs[sq], mg_ref, hm_all.at[sq],
                     c_all.at[sq], n_all.at[sq], m_all.at[sq])

    @pl.when(ci == pl.num_programs(1) - 1)
    def _():
        c_out[...] = c_all[...]
        n_out[...] = n_all[...]
        m_out[...] = m_all[...]


def _mlstm_chunk(q_ref, gc_ref, gr_ref, mg_ref, hm_ref, c_sc, n_sc, m_sc):
    lc = MCHUNK
    scale = DH ** -0.5
    row = lax.broadcasted_iota(jnp.int32, (lc, lc), 0)
    col = lax.broadcasted_iota(jnp.int32, (lc, lc), 1)
    causal = col <= row
    tri = causal.astype(BF16)
    ones = jnp.ones((lc, DH), BF16)
    gc = gc_ref[...]
    gr = gr_ref[...]
    gc_hi, gc_lo = _split_bf16(gc)
    bcol_all = _dot(tri, gc_hi) + _dot(tri, gc_lo)
    brow_all = _dot_exact_rhs(gr, tri, ((1,), (1,)))
    for hd in range(HEADS):
        q = q_ref[:, hd * DH:(hd + 1) * DH]
        k = q_ref[:, MWIDTH + hd * DH:MWIDTH + (hd + 1) * DH]
        v = q_ref[:, 2 * MWIDTH + hd * DH:2 * MWIDTH + (hd + 1) * DH]
        o = q_ref[:, 3 * MWIDTH + hd * DH:3 * MWIDTH + (hd + 1) * DH]
        i_col = gc[:, hd:hd + 1]
        b_col = bcol_all[:, HEADS + hd:HEADS + hd + 1]
        i_row = gr[hd:hd + 1, :]
        b_row = brow_all[HEADS + hd:HEADS + hd + 1, :]
        m_prev = m_sc[hd:hd + 1, :][:, :1]
        dm = jnp.where(causal, b_col - b_row + i_row, -jnp.inf)
        m_inter = b_col + m_prev
        m_t = jnp.maximum(m_inter, jnp.max(dm, axis=-1, keepdims=True))
        w_inter = jnp.exp(m_inter - m_t)
        w = jnp.exp(dm - m_t)
        s = _dot_dims(q, k, ((1,), (1,))) * scale * w
        sb = s.astype(BF16)
        c_prev = c_sc[hd]
        n_prev = n_sc[hd]
        num = w_inter * _dot(q, c_prev.astype(BF16)) + _dot(sb, v)
        den = w_inter * _dot(q, n_prev.astype(BF16)) + _dot(sb, ones)
        h = num / jnp.maximum(jnp.abs(den), jnp.exp(-m_t))
        hn = h * lax.rsqrt(jnp.mean(h * h, axis=-1, keepdims=True) + EPS)
        hm_ref[:, hd * DH:(hd + 1) * DH] = (hn * mg_ref[:, hd * DH:(hd + 1) * DH] * _sigmoid(o.astype(F32))).astype(BF16)
        m_last = m_t[lc - 1:lc, :]
        b_last = b_col[lc - 1:lc, :]
        decay = w_inter[lc - 1:lc, :]
        w_last = jnp.exp(b_last - b_col + i_col - m_last) * scale
        kw = (k.astype(F32) * w_last).astype(BF16)
        c_sc[hd] = decay * c_prev + _dot_dims(kw, v, ((0,), (0,)))
        n_sc[hd] = decay * n_prev + _dot_dims(kw, ones, ((0,), (0,)))
        m_sc[hd:hd + 1, :] = jnp.broadcast_to(m_last, (1, LANES))


def _mlstm_prompt(qkvo, gcol, grow, mnorm_g):
    nc = SEQ // MCHUNK
    q3 = qkvo.reshape(N_PROMPT, SEQ, 4 * MWIDTH)
    g3 = gcol.reshape(N_PROMPT, SEQ, LANES)
    seq_blk = lambda w: pl.BlockSpec((MSEQ, MCHUNK, w), lambda b, c: (b, c, 0))
    state = lambda shape: pl.BlockSpec((MSEQ,) + shape, lambda b, c: (b,) + (0,) * len(shape))
    gr_specs = [pl.BlockSpec((8, MCHUNK), functools.partial(lambda b, c, sq: (0, (b * MSEQ + sq) * nc + c), sq=sq))
                for sq in range(MSEQ)]
    hm, c_p, n_p, m_p = pl.pallas_call(
        _mlstm_prompt_body,
        grid=(N_PROMPT // MSEQ, nc),
        in_specs=[seq_blk(4 * MWIDTH), seq_blk(LANES)] + gr_specs + [pl.BlockSpec((1, MWIDTH), lambda b, c: (0, 0))],
        out_specs=[seq_blk(MWIDTH), state((HEADS, DH, DH)), state((HEADS, DH, DH)), state((8, LANES))],
        out_shape=[jax.ShapeDtypeStruct((N_PROMPT, SEQ, MWIDTH), BF16),
                   jax.ShapeDtypeStruct((N_PROMPT, HEADS, DH, DH), F32),
                   jax.ShapeDtypeStruct((N_PROMPT, HEADS, DH, DH), F32),
                   jax.ShapeDtypeStruct((N_PROMPT, 8, LANES), F32)],
        scratch_shapes=[pltpu.VMEM((MSEQ, HEADS, DH, DH), F32),
                        pltpu.VMEM((MSEQ, HEADS, DH, DH), F32),
                        pltpu.VMEM((MSEQ, 8, LANES), F32)],
        compiler_params=_cparams("parallel", "arbitrary"),
        name="mlstm_prompt",
    )(q3, g3, *([grow] * MSEQ), mnorm_g)
    return hm.reshape(N_PROMPT * SEQ, MWIDTH), c_p, n_p, m_p


MSTEP_TOK = 8


def _mlstm_step_body(q_ref, qkt_ref, ig_ref, lf_ref, m0_ref, c0_ref, n0_ref, mg_ref, hm_ref, c_out, n_out, m_out, qc_sc):
    scale = DH ** -0.5
    ig = ig_ref[...]
    m_inter = lf_ref[...] + m0_ref[...]
    m_t = jnp.maximum(m_inter, ig)
    w_inter_all = jnp.exp(m_inter - m_t)
    w_all = jnp.exp(ig - m_t) * scale
    floor_all = jnp.exp(-m_t)
    m_out[...] = m_t
    lane_t = lax.broadcasted_iota(jnp.int32, (DH, LANES), 1)
    ones_b = jnp.ones((LANES, LANES), BF16)
    for hd in range(HEADS):
        sl = slice(hd * DH, (hd + 1) * DH)
        q = q_ref[:, sl].astype(F32)
        k = q_ref[:, MWIDTH + hd * DH:MWIDTH + (hd + 1) * DH].astype(F32)
        v = q_ref[:, 2 * MWIDTH + hd * DH:2 * MWIDTH + (hd + 1) * DH].astype(F32)
        o = q_ref[:, 3 * MWIDTH + hd * DH:3 * MWIDTH + (hd + 1) * DH].astype(F32)
        n0 = n0_ref[:, sl]
        wi = w_inter_all[:, hd:hd + 1]
        wk = w_all[:, hd:hd + 1]
        s = jnp.sum(q * k, axis=-1, keepdims=True) * wk
        qn = jnp.sum(q * n0, axis=-1, keepdims=True)
        qt = qkt_ref[hd]
        kt = qkt_ref[HEADS + hd]
        for j in range(MSTEP_TOK):
            c0 = c0_ref[j, hd]
            q_b = _dot(jnp.where(lane_t == j, qt, 0.0).astype(BF16), ones_b)
            k_b = _dot(jnp.where(lane_t == j, kt, 0.0).astype(BF16), ones_b)
            qc_sc[j:j + 1, sl] = jnp.sum(q_b * c0, axis=0, keepdims=True)
            c_out[j, hd] = wi[j:j + 1, :] * c0 + (wk[j:j + 1, :] * k_b) * v[j:j + 1, :]
        num = wi * qc_sc[:, sl] + s * v
        den = wi * qn + s
        h = num / jnp.maximum(jnp.abs(den), floor_all[:, hd:hd + 1])
        hn = h * lax.rsqrt(jnp.mean(h * h, axis=-1, keepdims=True) + EPS)
        hm_ref[:, sl] = hn * mg_ref[:, sl] * _sigmoid(o)
        n_out[:, sl] = wi * n0 + wk * k


def _mlstm_step(qkvo, qkt, ig, lf, m0, c0, n0, mnorm_g):
    t = qkvo.shape[0]
    tk = MSTEP_TOK
    return pl.pallas_call(
        _mlstm_step_body,
        grid=(t // tk,),
        in_specs=[pl.BlockSpec((tk, 4 * MWIDTH), lambda i: (i, 0)),
                  pl.BlockSpec((None, 2 * HEADS, DH, LANES), lambda i: (i, 0, 0, 0)),
                  pl.BlockSpec((tk, LANES), lambda i: (i, 0)),
                  pl.BlockSpec((tk, LANES), lambda i: (i, 0)),
                  pl.BlockSpec((tk, LANES), lambda i: (i, 0)),
                  pl.BlockSpec((tk, HEADS, DH, DH), lambda i: (i, 0, 0, 0)),
                  pl.BlockSpec((tk, MWIDTH), lambda i: (i, 0)),
                  pl.BlockSpec((1, MWIDTH), lambda i: (0, 0))],
        out_specs=[pl.BlockSpec((tk, MWIDTH), lambda i: (i, 0)),
                   pl.BlockSpec((tk, HEADS, DH, DH), lambda i: (i, 0, 0, 0)),
                   pl.BlockSpec((tk, MWIDTH), lambda i: (i, 0)),
                   pl.BlockSpec((tk, LANES), lambda i: (i, 0))],
        out_shape=[jax.ShapeDtypeStruct((t, MWIDTH), F32),
                   jax.ShapeDtypeStruct((t, HEADS, DH, DH), F32),
                   jax.ShapeDtypeStruct((t, MWIDTH), F32),
                   jax.ShapeDtypeStruct((t, LANES), F32)],
        scratch_shapes=[pltpu.VMEM((tk, MWIDTH), F32)],
        compiler_params=_cparams("parallel"),
        name="mlstm_step",
    )(qkvo, qkt, ig, lf, m0, c0, n0, mnorm_g)


def _post_body(x_ref, y5_ref, hm_ref, gg_ref, g1_ref, sc_ref, sh_ref, n2_ref, wglu_ref, wmo_ref, wo_ref, rw_ref, rb_ref,
               cnt0_ref, x1_ref, h2_ref, eid_ref, wgt_ref, rank_ref, cnt_ref, run_sc, *, n_real, n_steps):
    i = pl.program_id(0)

    @pl.when(i == 0)
    def _():
        run_sc[...] = cnt0_ref[...]

    if n_steps > n_real:
        @pl.when(i >= n_real)
        def _():
            h2_ref[...] = jnp.zeros_like(h2_ref)

        pl.when(i < n_real)(functools.partial(
            _post_tile, x_ref, y5_ref, hm_ref, gg_ref, g1_ref, sc_ref, sh_ref, n2_ref, wglu_ref, wmo_ref, wo_ref,
            rw_ref, rb_ref, x1_ref, h2_ref, eid_ref, wgt_ref, rank_ref, run_sc))
    else:
        _post_tile(x_ref, y5_ref, hm_ref, gg_ref, g1_ref, sc_ref, sh_ref, n2_ref, wglu_ref, wmo_ref, wo_ref,
                   rw_ref, rb_ref, x1_ref, h2_ref, eid_ref, wgt_ref, rank_ref, run_sc)
    cnt_ref[...] = run_sc[...]


def _post_tile(x_ref, y5_ref, hm_ref, gg_ref, g1_ref, sc_ref, sh_ref, n2_ref, wglu_ref, wmo_ref, wo_ref, rw_ref, rb_ref,
               x1_ref, h2_ref, eid_ref, wgt_ref, rank_ref, run_sc):
    g5 = _gelu_tanh(y5_ref[...]).astype(BF16)
    glu = _dot(g5, wglu_ref[...])
    br_s5 = glu[:, :D_MODEL] * _sigmoid(glu[:, D_MODEL:])
    br_m = _dot(hm_ref[...].astype(BF16), wmo_ref[...])
    gg = gg_ref[...].astype(F32)
    merged = _sigmoid(gg[:, :D_MODEL]) * br_s5 + _sigmoid(gg[:, D_MODEL:]) * br_m
    x1 = x_ref[...] + g1_ref[...] * _dot(merged.astype(BF16), wo_ref[...])
    x1_ref[...] = x1
    ms = jnp.mean(x1 * x1, axis=-1, keepdims=True)
    h2 = x1 * lax.rsqrt(ms + EPS) * n2_ref[...] * (1.0 + sc_ref[...]) + sh_ref[...]
    h2_ref[...] = h2
    logits = _dot_hp(h2, rw_ref[...]) + rb_ref[...]
    tm = logits.shape[0]
    lane = lax.broadcasted_iota(jnp.int32, logits.shape, 1)
    l = jnp.where(lane < N_EXPERTS, logits, -jnp.inf)
    vals, hots, idxs = [], [], []
    for _ in range(TOP_K):
        mx = jnp.max(l, axis=-1, keepdims=True)
        idx = jnp.min(jnp.where(l == mx, lane, LANES), axis=-1, keepdims=True)
        hot = lane == idx
        vals.append(mx)
        hots.append(hot)
        idxs.append(idx)
        l = jnp.where(hot, -jnp.inf, l)
    ex = [jnp.exp(vk - vals[0]) for vk in vals]
    tot = ex[0] + ex[1] + ex[2] + ex[3]
    chosen = jnp.where(hots[0] | hots[1] | hots[2] | hots[3], 1.0, 0.0)
    r_io = lax.broadcasted_iota(jnp.int32, (tm, tm), 0)
    c_io = lax.broadcasted_iota(jnp.int32, (tm, tm), 1)
    earlier = (c_io < r_io).astype(BF16)
    before = run_sc[...] + _dot(earlier, chosen.astype(BF16))
    eid = jnp.zeros(logits.shape, jnp.int32)
    wgt = jnp.zeros(logits.shape, F32)
    rank = jnp.zeros(logits.shape, F32)
    for k in range(TOP_K):
        eid = jnp.where(lane == k, idxs[k], eid)
        wgt = jnp.where(lane == k, ex[k] / tot, wgt)
        rank = jnp.where(lane == k, jnp.sum(jnp.where(hots[k], before, 0.0), axis=-1, keepdims=True), rank)
    eid_ref[...] = eid
    wgt_ref[...] = wgt
    rank_ref[...] = rank.astype(jnp.int32)
    run_sc[...] += jnp.sum(chosen, axis=0, keepdims=True)


def _post(x2d, y5, hm, gg, mod3, rows_per_mod, tm, n2, wglu, wmo, wo, rw, rb, cnt0, h2_rows):
    t = x2d.shape[0]
    mrows = mod3.shape[1]
    n_real = t // tm
    n_steps = -(-h2_rows // tm)
    last = n_real - 1

    def mod_spec(j):
        return pl.BlockSpec((None, mrows, D_MODEL), lambda i: ((jnp.minimum(i, last) * tm) // rows_per_mod, 0, j))

    const = lambda shape: pl.BlockSpec(shape, lambda i: (0,) * len(shape))
    tile = lambda w: pl.BlockSpec((tm, w), lambda i: (jnp.minimum(i, last), 0))
    return pl.pallas_call(
        functools.partial(_post_body, n_real=n_real, n_steps=n_steps),
        grid=(n_steps,),
        in_specs=[tile(D_MODEL), tile(S5_WIDTH), tile(MWIDTH), tile(2 * D_MODEL),
                  mod_spec(2), mod_spec(4), mod_spec(3), const((1, D_MODEL)),
                  const(wglu.shape), const(wmo.shape), const(wo.shape), const(rw.shape), const(rb.shape),
                  const((1, LANES))],
        out_specs=[tile(D_MODEL), pl.BlockSpec((tm, D_MODEL), lambda i: (i, 0)), tile(LANES), tile(LANES), tile(LANES),
                   const((1, LANES))],
        out_shape=[jax.ShapeDtypeStruct((t, D_MODEL), F32),
                   jax.ShapeDtypeStruct((h2_rows, D_MODEL), F32),
                   jax.ShapeDtypeStruct((t, LANES), jnp.int32),
                   jax.ShapeDtypeStruct((t, LANES), F32),
                   jax.ShapeDtypeStruct((t, LANES), jnp.int32),
                   jax.ShapeDtypeStruct((1, LANES), F32)],
        scratch_shapes=[pltpu.VMEM((1, LANES), F32)],
        compiler_params=_cparams("arbitrary"),
        name="post",
    )(x2d, y5, hm, gg, mod3, mod3, mod3, n2, wglu, wmo, wo, rw, rb, cnt0)


N_TOKENS = N_PROMPT * SEQ + N_SAMPLE
N_ASSIGN = N_TOKENS * TOP_K
MOE_TILE = 256
MOE_TILES = N_ASSIGN // MOE_TILE + N_EXPERTS
N_SLOT = MOE_TILES * MOE_TILE
CMB_TILE = 512


MOE_RING = 3
MOE_SHORT = 64


def _moe_body(te_ref, nt_ref, nxe_ref, gpar_ref, rows_ref, idx0_ref, idx1_ref, idx2_ref, h2_hbm, wup_hbm, bup_ref,
              wdn_hbm, bdn_ref, y_ref, xb0, xb1, xb2, sem, wbuf_up, wbuf_dn, wsem, wup_bf, wdn_bf):
    i = pl.program_id(0)
    nt = nt_ref[0]
    bufs = (xb0, xb1, xb2)
    last = MOE_TILES - 1

    def gather(idx_ref, slot):
        for r in range(MOE_TILE):
            pltpu.make_async_copy(h2_hbm.at[pl.ds(idx_ref[0, r], 1)], bufs[slot].at[pl.ds(r, 1)],
                                  sem.at[slot]).start()

    def wait(slot):
        pltpu.make_async_copy(h2_hbm.at[pl.ds(0, MOE_TILE)], bufs[slot], sem.at[slot]).wait()

    def weights(e, slot):
        return (pltpu.make_async_copy(wup_hbm.at[e], wbuf_up.at[slot], wsem.at[0, slot]),
                pltpu.make_async_copy(wdn_hbm.at[e], wbuf_dn.at[slot], wsem.at[1, slot]))

    wslot = gpar_ref[i]

    @pl.when(i == 0)
    def _():
        for c in weights(te_ref[0], wslot):
            c.start(priority=1)
        gather(idx0_ref, 0)
        gather(idx1_ref, 1)

    new_expert = jnp.logical_or(i == 0, te_ref[i] != te_ref[jnp.maximum(i - 1, 0)])
    live = i < nt

    @pl.when(jnp.logical_and(new_expert, live))
    def _():
        for c in weights(0, wslot):
            c.wait()
        wup_bf[...] = wbuf_up[wslot].astype(BF16)
        wdn_bf[...] = wbuf_dn[wslot].astype(BF16)
        nxe = nxe_ref[i]

        @pl.when(nxe >= 0)
        def _():
            for c in weights(nxe, 1 - wslot):
                c.start(priority=1)

    def step(slot, rows):
        wait(slot)
        gather(idx2_ref, (slot + 2) % MOE_RING)
        gu = _dot(bufs[slot][:rows, :].astype(BF16), wup_bf[...]) + bup_ref[...]
        g = jnp.minimum(gu[:, :D_FF], SWIGLU_LIMIT)
        up = jnp.clip(gu[:, D_FF:], -SWIGLU_LIMIT, SWIGLU_LIMIT)
        act = (up + 1.0) * g * _sigmoid(SWIGLU_ALPHA * g)
        y_ref[:rows, :] = _dot(act.astype(BF16), wdn_bf[...]) + bdn_ref[...]
        if rows < MOE_TILE:
            y_ref[rows:, :] = jnp.zeros((MOE_TILE - rows, D_MODEL), F32)

    short = rows_ref[i] <= MOE_SHORT
    for slot in range(MOE_RING):
        mine = jnp.logical_and(live, i % MOE_RING == slot)
        pl.when(jnp.logical_and(mine, jnp.logical_not(short)))(functools.partial(step, slot, MOE_TILE))
        pl.when(jnp.logical_and(mine, short))(functools.partial(step, slot, MOE_SHORT))

    @pl.when(jnp.logical_not(live))
    def _():
        y_ref[...] = jnp.zeros_like(y_ref)

    drain = jnp.logical_and(jnp.logical_not(live), i < nt + 2)
    for slot in range(MOE_RING):
        pl.when(jnp.logical_and(drain, i % MOE_RING == slot))(functools.partial(wait, slot))
    pl.when(jnp.logical_and(i == last, last - 1 < nt))(functools.partial(wait, (last + 1) % MOE_RING))
    pl.when(jnp.logical_and(i == last, last < nt))(functools.partial(wait, (last + 2) % MOE_RING))


def _moe(tile_expert, n_tiles, next_expert, group_parity, tile_rows, slot_token, h2_all, wup, bup, wdn, bdn):
    idx3 = slot_token.reshape(MOE_TILES, 1, MOE_TILE)
    ahead = lambda k: pl.BlockSpec((None, 1, MOE_TILE),
                                   lambda i, *_: (jnp.minimum(i + k, MOE_TILES - 1), 0, 0), memory_space=pltpu.SMEM)
    grid_spec = pltpu.PrefetchScalarGridSpec(
        num_scalar_prefetch=5,
        grid=(MOE_TILES,),
        in_specs=[ahead(0), ahead(1), ahead(2),
                  pl.BlockSpec(memory_space=pl.ANY),
                  pl.BlockSpec(memory_space=pl.ANY),
                  pl.BlockSpec((None, 1, 2 * D_FF), lambda i, te, *_: (te[i], 0, 0)),
                  pl.BlockSpec(memory_space=pl.ANY),
                  pl.BlockSpec((None, 1, D_MODEL), lambda i, te, *_: (te[i], 0, 0))],
        out_specs=pl.BlockSpec((MOE_TILE, D_MODEL), lambda i, *_: (i, 0)),
        scratch_shapes=[pltpu.VMEM((MOE_TILE, D_MODEL), F32), pltpu.VMEM((MOE_TILE, D_MODEL), F32),
                        pltpu.VMEM((MOE_TILE, D_MODEL), F32),
                        pltpu.SemaphoreType.DMA((MOE_RING,)),
                        pltpu.VMEM((2, D_MODEL, 2 * D_FF), F32), pltpu.VMEM((2, D_FF, D_MODEL), F32),
                        pltpu.SemaphoreType.DMA((2, 2)),
                        pltpu.VMEM((D_MODEL, 2 * D_FF), BF16), pltpu.VMEM((D_FF, D_MODEL), BF16)])
    return pl.pallas_call(
        _moe_body,
        grid_spec=grid_spec,
        out_shape=jax.ShapeDtypeStruct((N_SLOT, D_MODEL), F32),
        compiler_params=_cparams("arbitrary"),
        name="moe",
    )(tile_expert, n_tiles, next_expert, group_parity, tile_rows, idx3, idx3, idx3, h2_all, wup, bup, wdn, bdn)


def _combine_body(cur_ref, nxt_ref, wgt_ref, x1_ref, g2_ref, fg_ref, y_hbm, o_ref, yb0, yb1, sem, *, n_steps, tm):
    i = pl.program_id(0)

    def gather(idx_ref, buf, s):
        for r in range(tm):
            for k in range(TOP_K):
                pltpu.make_async_copy(y_hbm.at[pl.ds(idx_ref[0, r * TOP_K + k], 1)],
                                      buf.at[pl.ds(k * tm + r, 1)], sem.at[s]).start(priority=k % 2)

    def wait(buf, s):
        pltpu.make_async_copy(y_hbm.at[pl.ds(0, TOP_K * tm)], buf, sem.at[s]).wait()

    @pl.when(i == 0)
    def _():
        gather(cur_ref, yb0, 0)

    def step(cur, nxt, s_cur, s_nxt):
        wait(cur, s_cur)
        gather(nxt_ref, nxt, s_nxt)
        wgt = wgt_ref[...]
        acc = wgt[:, 0:1] * cur[0:tm, :]
        for k in range(1, TOP_K):
            acc = acc + wgt[:, k:k + 1] * cur[k * tm:(k + 1) * tm, :]
        xo = x1_ref[...] + g2_ref[...] * acc
        ms = jnp.mean(xo * xo, axis=-1, keepdims=True)
        o_ref[...] = xo * lax.rsqrt(ms + EPS) * fg_ref[...]

    pl.when(i % 2 == 0)(functools.partial(step, yb0, yb1, 0, 1))
    pl.when(i % 2 == 1)(functools.partial(step, yb1, yb0, 1, 0))
    last_slot = n_steps % 2
    pl.when(i == n_steps - 1)(functools.partial(wait, yb1 if last_slot else yb0, last_slot))


def _combine(pos, wgt, x1, mod3, rows_per_mod, fg, y_slots, tm):
    t = x1.shape[0]
    n = t // tm
    mrows = mod3.shape[1]
    assert mrows in (1, tm) and rows_per_mod % tm == 0
    pos3 = pos.reshape(n, 1, tm * TOP_K)
    return pl.pallas_call(
        functools.partial(_combine_body, n_steps=n, tm=tm),
        grid=(n,),
        in_specs=[pl.BlockSpec((None, 1, tm * TOP_K), lambda i: (i, 0, 0), memory_space=pltpu.SMEM),
                  pl.BlockSpec((None, 1, tm * TOP_K), lambda i: (jnp.minimum(i + 1, n - 1), 0, 0),
                               memory_space=pltpu.SMEM),
                  pl.BlockSpec((tm, LANES), lambda i: (i, 0)),
                  pl.BlockSpec((tm, D_MODEL), lambda i: (i, 0)),
                  pl.BlockSpec((None, mrows, D_MODEL), lambda i: ((i * tm) // rows_per_mod, 0, 5)),
                  pl.BlockSpec((1, D_MODEL), lambda i: (0, 0)),
                  pl.BlockSpec(memory_space=pl.ANY)],
        out_specs=pl.BlockSpec((tm, D_MODEL), lambda i: (i, 0)),
        out_shape=jax.ShapeDtypeStruct((t, D_MODEL), F32),
        scratch_shapes=[pltpu.VMEM((TOP_K * tm, D_MODEL), F32), pltpu.VMEM((TOP_K * tm, D_MODEL), F32),
                        pltpu.SemaphoreType.DMA((2,))],
        compiler_params=_cparams("arbitrary"),
        name="combine",
    )(pos3, pos3, wgt, x1, mod3, fg, y_slots)


def _routing(eid_p, rank_p, eid_s, rank_s, counts):
    cnt = counts[0, :N_EXPERTS].astype(jnp.int32)
    ntile = (cnt + MOE_TILE - 1) // MOE_TILE
    tile_end = jnp.cumsum(ntile)
    poff = (tile_end - ntile) * MOE_TILE
    total = tile_end[-1]
    j = jnp.arange(MOE_TILES, dtype=jnp.int32)
    te = jnp.sum((j[:, None] >= tile_end[None, :]).astype(jnp.int32), axis=1)
    te_last = jnp.sum(((total - 1) >= tile_end).astype(jnp.int32))
    tile_expert = jnp.where(j < total, te, te_last).astype(jnp.int32)
    ex = jnp.arange(N_EXPERTS, dtype=jnp.int32)
    used = ntile > 0
    later = jnp.where(used[None, :] & (ex[None, :] > ex[:, None]), ex[None, :], N_EXPERTS)
    nxt_e = jnp.min(later, axis=1)
    nxt_e = jnp.where(nxt_e < N_EXPERTS, nxt_e, -1).astype(jnp.int32)
    par_e = ((jnp.cumsum(used.astype(jnp.int32)) - 1) % 2).astype(jnp.int32)
    pick = tile_expert[:, None] == ex[None, :]
    next_expert = jnp.sum(jnp.where(pick, nxt_e[None, :], 0), axis=1).astype(jnp.int32)
    group_parity = jnp.sum(jnp.where(pick, par_e[None, :], 0), axis=1).astype(jnp.int32)
    left = jnp.sum(jnp.where(pick, (cnt + poff)[None, :], 0), axis=1) - j * MOE_TILE
    tile_rows = jnp.where(j < total, jnp.clip(left, 0, MOE_TILE), 0).astype(jnp.int32)

    def pos_of(eid, rank):
        e = eid[:, :TOP_K]
        off = jnp.sum(jnp.where(e[:, :, None] == ex[None, None, :], poff[None, None, :], 0), axis=-1)
        return off + rank[:, :TOP_K]

    pos_p = pos_of(eid_p, rank_p)
    pos_s = pos_of(eid_s, rank_s)
    tok = jnp.concatenate([jnp.repeat(jnp.arange(N_PROMPT * SEQ, dtype=jnp.int32), TOP_K),
                           jnp.repeat(N_PROMPT * SEQ + jnp.arange(N_SAMPLE, dtype=jnp.int32), TOP_K)])
    pos_all = jnp.concatenate([pos_p.reshape(-1), pos_s.reshape(-1)])
    padc = jnp.concatenate([ntile * MOE_TILE - cnt, (N_SLOT - total * MOE_TILE).reshape(1)])
    pad_end = jnp.cumsum(padc)
    pad_base = jnp.concatenate([poff + cnt, (total * MOE_TILE).reshape(1)])
    jp = jnp.arange(N_SLOT - N_ASSIGN, dtype=jnp.int32)
    pe = jnp.sum((jp[:, None] >= pad_end[None, :]).astype(jnp.int32), axis=1)
    hit = pe[:, None] == jnp.arange(N_EXPERTS + 1, dtype=jnp.int32)[None, :]
    pad_slot = jp + jnp.sum(jnp.where(hit, (pad_base - (pad_end - padc))[None, :], 0), axis=1)
    keys = jnp.concatenate([pos_all, pad_slot.astype(jnp.int32)])
    vals = jnp.concatenate([tok, jnp.zeros((N_SLOT - N_ASSIGN,), jnp.int32)])
    tok_bits = 15
    assert N_TOKENS <= (1 << tok_bits) and N_SLOT <= (1 << (32 - tok_bits))
    packed = lax.sort((keys.astype(jnp.uint32) << tok_bits) | vals.astype(jnp.uint32))
    slot_token = (packed & ((1 << tok_bits) - 1)).astype(jnp.int32)
    return (tile_expert, total.reshape(1).astype(jnp.int32), next_expert, group_parity, tile_rows, slot_token,
            pos_p, pos_s)


def _unpack_s5_state(x):
    n = x.shape[1]
    z = x.reshape(N_SUPER, n, 2, S5_SUPER, S5_STATE).transpose(2, 1, 0, 3, 4).reshape(2, n, S5_GROUPS, S5_STATE)
    return z[0], z[1]


def _pack_s5_state(re, im):
    n = re.shape[0]
    z = jnp.stack([re, im], axis=0).reshape(2, n, N_SUPER, S5_SUPER, S5_STATE)
    return z.transpose(2, 1, 0, 3, 4).reshape(N_SUPER, n, SUPER_STATE)


def kernel(x_prompt, x_sample, c_prompt, c_sample, state_s5_re, state_s5_im, state_mlstm_C, state_mlstm_n, state_mlstm_m, norm1_g, norm2_g, final_norm_g, w_ada, b_ada, w_in, s5_lambda_re, s5_lambda_im, s5_log_dt, s5_B_re, s5_B_im, s5_C_re, s5_C_im, s5_D, s5_w_glu, mlstm_b_i, mlstm_b_f, mlstm_norm_g, mlstm_w_out, w_out, router_w, router_b, expert_w_up, expert_b_up, expert_w_down, expert_b_down):
    assert w_in.shape[0] == 1, "single layer"
    tp = N_PROMPT * SEQ
    xp = x_prompt.reshape(tp, D_MODEL).astype(F32)
    xs = x_sample.reshape(N_SAMPLE, D_MODEL).astype(F32)

    w = w_in[0]
    c0 = S5_WIDTH
    c1 = c0 + 4 * MWIDTH
    c2 = c1 + 2 * HEADS
    wu = w[:, :c0].astype(BF16)
    wq = w[:, c0:c1].astype(BF16)
    wif = jnp.pad(w[:, c1:c2].astype(F32), ((0, 0), (0, LANES - 2 * HEADS)))
    wg = w[:, c2:].astype(BF16)
    bvec = jnp.concatenate([mlstm_b_i[0], mlstm_b_f[0]]).astype(F32)
    bif = jnp.pad(bvec, (0, LANES - 2 * HEADS)).reshape(1, LANES)
    wparts = (wu, wq, wg, wif, bif)
    g1 = norm1_g[0].reshape(1, D_MODEL).astype(F32)
    n2 = norm2_g[0].reshape(1, D_MODEL).astype(F32)
    fg = final_norm_g.reshape(1, D_MODEL).astype(F32)
    mg = mlstm_norm_g[0].reshape(1, MWIDTH).astype(F32)
    wglu = s5_w_glu[0].astype(BF16)
    wmo = mlstm_w_out[0].astype(BF16)
    wo = w_out[0].astype(BF16)
    rw = jnp.pad(router_w[0].astype(F32), ((0, 0), (0, LANES - N_EXPERTS)))
    rb = jnp.pad(router_b[0].astype(F32), (0, LANES - N_EXPERTS)).reshape(1, LANES)
    wup = expert_w_up[0].astype(F32)
    wdn = expert_w_down[0].astype(F32)
    bup = expert_b_up[0].astype(F32).reshape(N_EXPERTS, 1, 2 * D_FF)
    bdn = expert_b_down[0].astype(F32).reshape(N_EXPERTS, 1, D_MODEL)
    ddiag, fcomp, ecomp, a16, a1 = _s5_tables(
        s5_lambda_re[0], s5_lambda_im[0], s5_log_dt[0], s5_B_re[0], s5_B_im[0], s5_C_re[0], s5_C_im[0], s5_D[0])

    c_all = jnp.concatenate([c_prompt, c_sample], axis=0).astype(F32)
    mod = _adaln(c_all, w_ada[0].astype(F32), b_ada[0].astype(F32))
    mod_p = mod[:N_PROMPT].reshape(N_PROMPT, 1, 6 * D_MODEL)
    mod_s = mod[N_PROMPT:].reshape(1, N_SAMPLE, 6 * D_MODEL)

    u, qkvo, gg, gcol, grow = _inproj(xp, mod_p, SEQ, 1024, g1, wparts)
    y5, xend, k0, e0, f1 = _s5_prompt(u, ddiag, fcomp, ecomp, a16, a1)
    hm, c_p, n_p, m_p = _mlstm_prompt(qkvo, gcol, grow, mg)
    cnt0 = jnp.zeros((1, LANES), F32)
    x1, h2_all, eid_p, wgt_p, rank_p, cnt_p = _post(xp, y5, hm, gg, mod_p, SEQ, 512, n2, wglu, wmo, wo, rw, rb,
                                                    cnt0, N_TOKENS)
    p_re, p_im = _unpack_s5_state(xend.reshape(N_SUPER, N_PROMPT, SUPER_STATE))
    n_p = n_p[..., 0]
    m_p = m_p[:, :HEADS, 0]

    us, qs, ggs, gcs, _ = _inproj(xs, mod_s, N_SAMPLE, N_SAMPLE, g1, wparts)
    x0 = _pack_s5_state(state_s5_re[0].astype(F32), state_s5_im[0].astype(F32))
    y5s, xns = _s5_step(us, x0, k0, e0, f1, a1)
    s_re, s_im = _unpack_s5_state(xns)
    qkt = qs[:, :2 * MWIDTH].astype(F32).reshape(N_SAMPLE // MSTEP_TOK, MSTEP_TOK, 2 * HEADS, DH).transpose(0, 2, 3, 1)
    qkt = jnp.pad(qkt, ((0, 0), (0, 0), (0, 0), (0, LANES - MSTEP_TOK)))
    pad_heads = lambda a: jnp.pad(a, ((0, 0), (0, LANES - HEADS)))
    m0 = pad_heads(state_mlstm_m[0].astype(F32))
    hms, c_s, n_s, m_s = _mlstm_step(qs, qkt, pad_heads(gcs[:, :HEADS]), pad_heads(gcs[:, HEADS:2 * HEADS]), m0,
                                     state_mlstm_C[0].astype(F32),
                                     state_mlstm_n[0].astype(F32).reshape(N_SAMPLE, MWIDTH), mg)
    x1s, h2s, eid_s, wgt_s, rank_s, cnt_all = _post(xs, y5s, hms, ggs, mod_s, N_SAMPLE, N_SAMPLE, n2, wglu, wmo, wo, rw, rb,
                                                    cnt_p, N_SAMPLE)

    h2_all = lax.dynamic_update_slice(h2_all, h2s, (tp, 0))
    tile_expert, n_tiles, next_expert, group_parity, tile_rows, slot_token, pos_p, pos_s = _routing(
        eid_p, rank_p, eid_s, rank_s, cnt_all)
    y_slots = _moe(tile_expert, n_tiles, next_expert, group_parity, tile_rows, slot_token, h2_all, wup, bup, wdn, bdn)
    y_p = _combine(pos_p, wgt_p, x1, mod_p, SEQ, fg, y_slots, CMB_TILE)
    y_s = _combine(pos_s, wgt_s, x1s, mod_s, N_SAMPLE, fg, y_slots, N_SAMPLE)

    return (y_p.reshape(N_PROMPT, SEQ, D_MODEL).astype(x_prompt.dtype),
            y_s.reshape(N_SAMPLE, 1, D_MODEL).astype(x_sample.dtype),
            p_re[None], p_im[None], c_p[None], n_p[None], m_p[None],
            s_re[None], s_im[None], c_s[None],
            n_s.reshape(1, N_SAMPLE, HEADS, DH), m_s[:, :HEADS][None])
```

```python
import functools
import math

import jax
import jax.numpy as jnp
from jax import lax
from jax.experimental import pallas as pl
from jax.experimental.pallas import tpu as pltpu

F32 = jnp.float32
BF16 = jnp.bfloat16

D_MODEL = 1024
SEQ = 2048
N_PROMPT = 8
N_SAMPLE = 128
S5_WIDTH = 512
S5_GROUP = 16
S5_GROUPS = 32
S5_STATE = 64
HEADS = 4
DH = 128
MWIDTH = HEADS * DH
N_EXPERTS = 32
TOP_K = 4
D_FF = 1024
SWIGLU_LIMIT = 7.0
SWIGLU_ALPHA = 1.702
EPS = 1e-6

LANES = 128
S5_CHUNK = 16
S5_SUPER = LANES // S5_GROUP
N_SUPER = S5_GROUPS // S5_SUPER
SUPER_STATE = 2 * S5_SUPER * S5_STATE
MCHUNK = 128
VMEM_LIMIT = 56 * 1024 * 1024


def _cparams(*sem):
    return pltpu.CompilerParams(dimension_semantics=sem, vmem_limit_bytes=VMEM_LIMIT)


def _dot(a, b):
    return jnp.dot(a, b, preferred_element_type=F32)


def _dot_dims(a, b, dims):
    return lax.dot_general(a, b, (dims, ((), ())), preferred_element_type=F32)


def _split_bf16(a):
    hi = a.astype(BF16)
    lo = (a - hi.astype(F32)).astype(BF16)
    return hi, lo


def _dot_hp(a, b, dims=((1,), (0,))):
    ah, al = _split_bf16(a)
    bh, bl = _split_bf16(b)
    return _dot_dims(ah, bh, dims) + (_dot_dims(al, bh, dims) + _dot_dims(ah, bl, dims))


def _dot_exact_rhs(a, b_exact, dims=((1,), (0,))):
    ah, al = _split_bf16(a)
    return _dot_dims(ah, b_exact, dims) + _dot_dims(al, b_exact, dims)


def _log_sigmoid(x):
    return -(jnp.maximum(-x, 0.0) + jnp.log1p(jnp.exp(-jnp.abs(x))))


def _sigmoid(x):
    return 0.5 * (1.0 + jnp.tanh(0.5 * x))


def _gelu_tanh(x):
    c = math.sqrt(2.0 / math.pi)
    return 0.5 * x * (1.0 + jnp.tanh(c * (x + 0.044715 * (x * x * x))))


def _adaln_body(c_ref, w_ref, b_ref, o_ref):
    c = c_ref[...]
    s = c * _sigmoid(c)
    o_ref[...] = _dot_hp(s, w_ref[...]) + b_ref[...]


def _adaln(c_all, w_ada, b_ada):
    n = c_all.shape[0]
    tn = 1024
    return pl.pallas_call(
        _adaln_body,
        grid=(6 * D_MODEL // tn,),
        in_specs=[pl.BlockSpec((n, D_MODEL), lambda j: (0, 0)),
                  pl.BlockSpec((D_MODEL, tn), lambda j: (0, j)),
                  pl.BlockSpec((1, tn), lambda j: (0, j))],
        out_specs=pl.BlockSpec((n, tn), lambda j: (0, j)),
        out_shape=jax.ShapeDtypeStruct((n, 6 * D_MODEL), F32),
        compiler_params=_cparams("parallel"),
        name="adaln",
    )(c_all, w_ada, b_ada.reshape(1, -1))


def _inproj_body(x_ref, g1_ref, sc_ref, sh_ref, wu_ref, wq_ref, wg_ref, wif_ref, bif_ref,
                 u_ref, q_ref, gg_ref, if_ref, ift_ref):
    x = x_ref[...]
    ms = jnp.mean(x * x, axis=-1, keepdims=True)
    h = x * lax.rsqrt(ms + EPS) * g1_ref[...] * (1.0 + sc_ref[...]) + sh_ref[...]
    hb = h.astype(BF16)
    u_ref[...] = _dot(hb, wu_ref[...])
    q_ref[...] = _dot(hb, wq_ref[...]).astype(BF16)
    gg_ref[...] = _dot(hb, wg_ref[...]).astype(BF16)
    gc = _dot_hp(h, wif_ref[...]) + bif_ref[...]
    lane = lax.broadcasted_iota(jnp.int32, gc.shape, 1)
    gates = jnp.where(lane < HEADS, gc, _log_sigmoid(gc))
    if_ref[...] = gates
    ift_ref[...] = gates.T[:2 * HEADS, :]


def _inproj(x2d, mod3, rows_per_mod, tm, g1, wparts):
    t = x2d.shape[0]
    wu, wq, wg, wif, bif = wparts
    mrows = mod3.shape[1]

    def mod_spec(j):
        return pl.BlockSpec((None, mrows, D_MODEL), lambda i: ((i * tm) // rows_per_mod, 0, j))

    const = lambda shape: pl.BlockSpec(shape, lambda i: (0,) * len(shape), pipeline_mode=pl.Buffered(1))
    return pl.pallas_call(
        _inproj_body,
        grid=(t // tm,),
        in_specs=[pl.BlockSpec((tm, D_MODEL), lambda i: (i, 0)),
                  const((1, D_MODEL)), mod_spec(1), mod_spec(0),
                  const(wu.shape), const(wq.shape), const(wg.shape), const(wif.shape), const(bif.shape)],
        out_specs=[pl.BlockSpec((tm, S5_WIDTH), lambda i: (i, 0)),
                   pl.BlockSpec((tm, 4 * MWIDTH), lambda i: (i, 0)),
                   pl.BlockSpec((tm, 2 * D_MODEL), lambda i: (i, 0)),
                   pl.BlockSpec((tm, LANES), lambda i: (i, 0)),
                   pl.BlockSpec((8, tm), lambda i: (0, i))],
        out_shape=[jax.ShapeDtypeStruct((t, S5_WIDTH), F32),
                   jax.ShapeDtypeStruct((t, 4 * MWIDTH), BF16),
                   jax.ShapeDtypeStruct((t, 2 * D_MODEL), BF16),
                   jax.ShapeDtypeStruct((t, LANES), F32),
                   jax.ShapeDtypeStruct((8, t), F32)],
        compiler_params=_cparams("parallel"),
        name="inproj",
    )(x2d, g1, mod3, mod3, wu, wq, wg, wif, bif)


def _s5_tables(lam_re, lam_im, log_dt, b_re, b_im, c_re, c_im, d_s5):
    hi = lax.Precision.HIGHEST
    dt = jnp.exp(log_dt.astype(F32))[:, None]
    lr, li = lam_re.astype(F32), lam_im.astype(F32)
    dpow = jnp.arange(S5_CHUNK + 1, dtype=F32)[:, None, None]
    mag = jnp.exp(dpow * (lr * dt))
    pw_re, pw_im = mag * jnp.cos(dpow * (li * dt)), mag * jnp.sin(dpow * (li * dt))
    ab_re, ab_im = pw_re[1], pw_im[1]
    den = lr * lr + li * li
    nr, ni = ab_re - 1.0, ab_im
    coef_re = (nr * lr + ni * li) / den
    coef_im = (ni * lr - nr * li) / den
    br, bi = b_re.astype(F32), b_im.astype(F32)
    bb_re = coef_re[..., None] * br - coef_im[..., None] * bi
    bb_im = coef_re[..., None] * bi + coef_im[..., None] * br
    cr, ci = c_re.astype(F32), c_im.astype(F32)
    cl_re = cr[None] * pw_re[:, :, None, :] - ci[None] * pw_im[:, :, None, :]
    cl_im = cr[None] * pw_im[:, :, None, :] + ci[None] * pw_re[:, :, None, :]
    ddiag = (d_s5.astype(F32).reshape(N_SUPER, LANES, 1) * jnp.eye(LANES, dtype=F32)[None])
    rp_re, rp_im = pw_re[S5_CHUNK - 1::-1], pw_im[S5_CHUNK - 1::-1]
    f_re = rp_re[..., None] * bb_re[None] - rp_im[..., None] * bb_im[None]
    f_im = rp_re[..., None] * bb_im[None] + rp_im[..., None] * bb_re[None]
    fcat = jnp.stack([f_re, f_im], axis=2).transpose(0, 1, 4, 2, 3)
    fcomp = fcat.reshape(S5_CHUNK, N_SUPER, LANES, 2 * S5_STATE).transpose(1, 0, 2, 3)
    fcomp = fcomp.reshape(N_SUPER, S5_CHUNK * LANES, 2 * S5_STATE)
    ecat = jnp.stack([cl_re[:S5_CHUNK], -cl_im[:S5_CHUNK]], axis=0)
    er = ecat.transpose(2, 0, 4, 1, 3).reshape(N_SUPER, S5_SUPER, 2, S5_STATE, S5_CHUNK * S5_GROUP)
    ecomp = er.transpose(0, 2, 1, 3, 4).reshape(N_SUPER, SUPER_STATE, S5_CHUNK * S5_GROUP)

    def lay(re, im):
        z = jnp.stack([re, im], axis=0).reshape(2, N_SUPER, S5_SUPER * S5_STATE)
        return z.transpose(1, 0, 2).reshape(N_SUPER, 1, SUPER_STATE)

    a16 = lay(pw_re[S5_CHUNK], pw_im[S5_CHUNK])
    a1 = lay(ab_re, ab_im)
    return ddiag, fcomp, ecomp, a16, a1


def _expand_groups(comp, row_div, col_blk, n_cols, col_div):
    r, c = comp.shape
    ci = lax.broadcasted_iota(jnp.int32, (c, n_cols), 0)
    co = lax.broadcasted_iota(jnp.int32, (c, n_cols), 1)
    sel = jnp.logical_and(ci // col_blk == co // (S5_SUPER * col_blk), ci % col_blk == co % col_blk)
    rep = _dot(comp.astype(BF16), sel.astype(BF16))
    ro = lax.broadcasted_iota(jnp.int32, (r, n_cols), 0)
    cc = lax.broadcasted_iota(jnp.int32, (r, n_cols), 1)
    same = (ro // row_div) % S5_SUPER == (cc // col_div) % S5_SUPER
    return jnp.where(same, rep, 0.0).astype(BF16)


S5_NB = 4


def _s5_prompt_body(u_ref, dd_ref, fc_ref, ec_ref, a_ref, a1_ref, y_ref, xe_ref, k0_ref, e0_ref, f1_ref,
                    ucat, v_sc, xp_sc, kp_ref, f_ref, e_ref):
    n = SEQ // S5_CHUNK
    half = SUPER_STATE // 2

    @pl.when(pl.program_id(1) == 0)
    def _():
        f_ref[...] = _expand_groups(fc_ref[...], S5_GROUP, S5_STATE, SUPER_STATE, S5_STATE)
        e_ref[...] = _expand_groups(ec_ref[...], S5_STATE, S5_GROUP, S5_CHUNK * LANES, S5_GROUP)
        f1 = f_ref[(S5_CHUNK - 1) * LANES:, :]
        kall = _dot(f1, e_ref[...])
        for r in range(S5_CHUNK):
            for hf in range(2):
                d = S5_CHUNK - 2 - r + hf
                blk = jnp.zeros((LANES, LANES), F32) if d < 0 else kall[:, d * LANES:(d + 1) * LANES]
                if d == 0:
                    blk = blk + dd_ref[...]
                kp_ref[r * LANES:(r + 1) * LANES, hf * LANES:(hf + 1) * LANES] = blk.astype(BF16)
        k0_ref[...] = kp_ref[(S5_CHUNK - 2) * LANES:(S5_CHUNK - 1) * LANES, :LANES]
        e0_ref[...] = e_ref[:, LANES:2 * LANES]
        f1_ref[...] = f1

    for bl in range(S5_NB):
        for j in range(S5_CHUNK):
            ucat[bl * n:(bl + 1) * n, j * LANES:(j + 1) * LANES] = (
                u_ref[pl.ds(bl * SEQ + j, n, stride=S5_CHUNK), :].astype(BF16))
    v_sc[...] = _dot(ucat[...], f_ref[...])
    a_r, a_i = a_ref[:, :half], a_ref[:, half:]
    xr = [jnp.zeros((1, half), F32) for _ in range(S5_NB)]
    xi = [jnp.zeros((1, half), F32) for _ in range(S5_NB)]
    for c in range(n):
        for bl in range(S5_NB):
            row = bl * n + c
            xp_sc[row:row + 1, :half] = xr[bl]
            xp_sc[row:row + 1, half:] = xi[bl]
            vr, vi = v_sc[row:row + 1, :half], v_sc[row:row + 1, half:]
            xr[bl], xi[bl] = a_r * xr[bl] - a_i * xi[bl] + vr, a_r * xi[bl] + a_i * xr[bl] + vi
    for bl in range(S5_NB):
        xe_ref[bl:bl + 1, :half] = xr[bl]
        xe_ref[bl:bl + 1, half:] = xi[bl]
    l_r, l_i = a1_ref[:, :half], a1_ref[:, half:]
    xpr, xpi = xp_sc[:, :half], xp_sc[:, half:]
    z = jnp.concatenate([l_r * xpr - l_i * xpi, l_r * xpi + l_i * xpr], axis=1).astype(BF16)
    ye = _dot(z, e_ref[...])
    for m in range(S5_CHUNK // 2):
        rows = (2 * m + 2) * LANES
        yp = _dot(ucat[:, :rows], kp_ref[(S5_CHUNK - 2 - 2 * m) * LANES:, :]) + ye[:, 2 * m * LANES:(2 * m + 2) * LANES]
        for bl in range(S5_NB):
            y_ref[pl.ds(bl * SEQ + 2 * m, n, stride=S5_CHUNK), :] = yp[bl * n:(bl + 1) * n, :LANES]
            y_ref[pl.ds(bl * SEQ + 2 * m + 1, n, stride=S5_CHUNK), :] = yp[bl * n:(bl + 1) * n, LANES:]


def _s5_prompt(u, ddiag, fcomp, ecomp, a16, a1):
    n = S5_NB * SEQ // S5_CHUNK
    per_super = lambda shape: pl.BlockSpec((None,) + shape, lambda s, b: (s, 0, 0))
    return pl.pallas_call(
        _s5_prompt_body,
        grid=(N_SUPER, N_PROMPT // S5_NB),
        in_specs=[pl.BlockSpec((S5_NB * SEQ, LANES), lambda s, b: (b, s)),
                  per_super(ddiag.shape[1:]), per_super(fcomp.shape[1:]), per_super(ecomp.shape[1:]),
                  per_super((1, SUPER_STATE)), per_super((1, SUPER_STATE))],
        out_specs=[pl.BlockSpec((S5_NB * SEQ, LANES), lambda s, b: (b, s)),
                   pl.BlockSpec((None, None, S5_NB, SUPER_STATE), lambda s, b: (s, b, 0, 0)),
                   per_super((LANES, LANES)), per_super((SUPER_STATE, LANES)), per_super((LANES, SUPER_STATE))],
        out_shape=[jax.ShapeDtypeStruct(u.shape, F32),
                   jax.ShapeDtypeStruct((N_SUPER, N_PROMPT // S5_NB, S5_NB, SUPER_STATE), F32),
                   jax.ShapeDtypeStruct((N_SUPER, LANES, LANES), BF16),
                   jax.ShapeDtypeStruct((N_SUPER, SUPER_STATE, LANES), BF16),
                   jax.ShapeDtypeStruct((N_SUPER, LANES, SUPER_STATE), BF16)],
        scratch_shapes=[pltpu.VMEM((n, S5_CHUNK * LANES), BF16),
                        pltpu.VMEM((n, SUPER_STATE), F32),
                        pltpu.VMEM((n, SUPER_STATE), F32),
                        pltpu.VMEM((S5_CHUNK * LANES, 2 * LANES), BF16),
                        pltpu.VMEM((S5_CHUNK * LANES, SUPER_STATE), BF16),
                        pltpu.VMEM((SUPER_STATE, S5_CHUNK * LANES), BF16)],
        compiler_params=_cparams("parallel", "arbitrary"),
        name="s5_prompt",
    )(u, ddiag, fcomp, ecomp, a16, a1)


def _s5_step_body(u_ref, x0_ref, k0_ref, e0_ref, f1_ref, a_ref, y_ref, xn_ref):
    half = SUPER_STATE // 2
    ub = u_ref[...].astype(BF16)
    x0 = x0_ref[...]
    bu = _dot(ub, f1_ref[...])
    a_r, a_i = a_ref[:, :half], a_ref[:, half:]
    x0r, x0i = x0[:, :half], x0[:, half:]
    xn_ref[:, :half] = a_r * x0r - a_i * x0i + bu[:, :half]
    xn_ref[:, half:] = a_r * x0i + a_i * x0r + bu[:, half:]
    y_ref[...] = _dot(x0.astype(BF16), e0_ref[...]) + _dot(ub, k0_ref[...])


def _s5_step(u, x0, k0, e0, f1, a1):
    t = u.shape[0]
    per = lambda a: pl.BlockSpec((None,) + a.shape[1:], lambda s: (s,) + (0,) * (a.ndim - 1))
    return pl.pallas_call(
        _s5_step_body,
        grid=(N_SUPER,),
        in_specs=[pl.BlockSpec((t, LANES), lambda s: (0, s)), per(x0), per(k0), per(e0), per(f1), per(a1)],
        out_specs=[pl.BlockSpec((t, LANES), lambda s: (0, s)), per(x0)],
        out_shape=[jax.ShapeDtypeStruct(u.shape, F32), jax.ShapeDtypeStruct(x0.shape, F32)],
        compiler_params=_cparams("parallel"),
        name="s5_step",
    )(u, x0, k0, e0, f1, a1)


MSEQ = 2


def _mlstm_prompt_body(q_all, gc_all, *rest):
    gr_refs, (mg_ref, hm_all, c_out, n_out, m_out, c_all, n_all, m_all) = rest[:MSEQ], rest[MSEQ:]
    ci = pl.program_id(1)

    @pl.when(ci == 0)
    def _():
        c_all[...] = jnp.zeros_like(c_all)
        n_all[...] = jnp.zeros_like(n_all)
        m_all[...] = jnp.zeros_like(m_all)

    for sq in range(MSEQ):
        _mlstm_chunk(q_all.at[sq], gc_all.at[sq], gr_refs[sq], mg_ref, hm_all.at[sq],
                     c_all.at[sq], n_all.at[sq], m_all.at[sq])

    @pl.when(ci == pl.num_programs(1) - 1)
    def _():
        c_out[...] = c_all[...]
        n_out[...] = n_all[...]
        m_out[...] = m_all[...]


def _mlstm_chunk(q_ref, gc_ref, gr_ref, mg_ref, hm_ref, c_sc, n_sc, m_sc):
    lc = MCHUNK
    scale = DH ** -0.5
    row = lax.broadcasted_iota(jnp.int32, (lc, lc), 0)
    col = lax.broadcasted_iota(jnp.int32, (lc, lc), 1)
    causal = col <= row
    tri = causal.astype(BF16)
    ones = jnp.ones((lc, DH), BF16)
    gc = gc_ref[...]
    gr = gr_ref[...]
    gc_hi, gc_lo = _split_bf16(gc)
    bcol_all = _dot(tri, gc_hi) + _dot(tri, gc_lo)
    brow_all = _dot_exact_rhs(gr, tri, ((1,), (1,)))
    for hd in range(HEADS):
        q = q_ref[:, hd * DH:(hd + 1) * DH]
        k = q_ref[:, MWIDTH + hd * DH:MWIDTH + (hd + 1) * DH]
        v = q_ref[:, 2 * MWIDTH + hd * DH:2 * MWIDTH + (hd + 1) * DH]
        o = q_ref[:, 3 * MWIDTH + hd * DH:3 * MWIDTH + (hd + 1) * DH]
        i_col = gc[:, hd:hd + 1]
        b_col = bcol_all[:, HEADS + hd:HEADS + hd + 1]
        i_row = gr[hd:hd + 1, :]
        b_row = brow_all[HEADS + hd:HEADS + hd + 1, :]
        m_prev = m_sc[hd:hd + 1, :][:, :1]
        dm = jnp.where(causal, b_col - b_row + i_row, -jnp.inf)
        m_inter = b_col + m_prev
        m_t = jnp.maximum(m_inter, jnp.max(dm, axis=-1, keepdims=True))
        w_inter = jnp.exp(m_inter - m_t)
        w = jnp.exp(dm - m_t)
        s = _dot_dims(q, k, ((1,), (1,))) * scale * w
        sb = s.astype(BF16)
        c_prev = c_sc[hd]
        n_prev = n_sc[hd]
        num = w_inter * _dot(q, c_prev.astype(BF16)) + _dot(sb, v)
        den = w_inter * _dot(q, n_prev.astype(BF16)) + _dot(sb, ones)
        h = num / jnp.maximum(jnp.abs(den), jnp.exp(-m_t))
        hn = h * lax.rsqrt(_dot_exact_rhs(h * h, jnp.full((DH, DH), 1.0 / DH, BF16)) + EPS)
        hm_ref[:, hd * DH:(hd + 1) * DH] = (hn * mg_ref[:, hd * DH:(hd + 1) * DH] * _sigmoid(o.astype(F32))).astype(BF16)
        m_last = m_t[lc - 1:lc, :]
        b_last = b_col[lc - 1:lc, :]
        decay = w_inter[lc - 1:lc, :]
        w_last = jnp.exp(b_last - b_col + i_col - m_last) * scale
        kw = (k.astype(F32) * w_last).astype(BF16)
        c_sc[hd] = decay * c_prev + _dot_dims(kw, v, ((0,), (0,)))
        n_sc[hd] = decay * n_prev + _dot_dims(kw, ones, ((0,), (0,)))
        m_sc[hd:hd + 1, :] = jnp.broadcast_to(m_last, (1, LANES))


def _mlstm_prompt(qkvo, gcol, grow, mnorm_g):
    nc = SEQ // MCHUNK
    q3 = qkvo.reshape(N_PROMPT, SEQ, 4 * MWIDTH)
    g3 = gcol.reshape(N_PROMPT, SEQ, LANES)
    seq_blk = lambda w: pl.BlockSpec((MSEQ, MCHUNK, w), lambda b, c: (b, c, 0))
    state = lambda shape: pl.BlockSpec((MSEQ,) + shape, lambda b, c: (b,) + (0,) * len(shape))
    gr_specs = [pl.BlockSpec((8, MCHUNK), functools.partial(lambda b, c, sq: (0, (b * MSEQ + sq) * nc + c), sq=sq))
                for sq in range(MSEQ)]
    hm, c_p, n_p, m_p = pl.pallas_call(
        _mlstm_prompt_body,
        grid=(N_PROMPT // MSEQ, nc),
        in_specs=[seq_blk(4 * MWIDTH), seq_blk(LANES)] + gr_specs + [pl.BlockSpec((1, MWIDTH), lambda b, c: (0, 0))],
        out_specs=[seq_blk(MWIDTH), state((HEADS, DH, DH)), state((HEADS, DH, DH)), state((8, LANES))],
        out_shape=[jax.ShapeDtypeStruct((N_PROMPT, SEQ, MWIDTH), BF16),
                   jax.ShapeDtypeStruct((N_PROMPT, HEADS, DH, DH), F32),
                   jax.ShapeDtypeStruct((N_PROMPT, HEADS, DH, DH), F32),
                   jax.ShapeDtypeStruct((N_PROMPT, 8, LANES), F32)],
        scratch_shapes=[pltpu.VMEM((MSEQ, HEADS, DH, DH), F32),
                        pltpu.VMEM((MSEQ, HEADS, DH, DH), F32),
                        pltpu.VMEM((MSEQ, 8, LANES), F32)],
        compiler_params=_cparams("parallel", "arbitrary"),
        name="mlstm_prompt",
    )(q3, g3, *([grow] * MSEQ), mnorm_g)
    return hm.reshape(N_PROMPT * SEQ, MWIDTH), c_p, n_p, m_p


MSTEP_TOK = 8


def _mlstm_step_body(q_ref, qkt_ref, ig_ref, lf_ref, m0_ref, c0_ref, n0_ref, mg_ref, hm_ref, c_out, n_out, m_out, qc_sc):
    scale = DH ** -0.5
    ig = ig_ref[...]
    m_inter = lf_ref[...] + m0_ref[...]
    m_t = jnp.maximum(m_inter, ig)
    w_inter_all = jnp.exp(m_inter - m_t)
    w_all = jnp.exp(ig - m_t) * scale
    floor_all = jnp.exp(-m_t)
    m_out[...] = m_t
    lane_t = lax.broadcasted_iota(jnp.int32, (DH, LANES), 1)
    ones_b = jnp.ones((LANES, LANES), BF16)
    for hd in range(HEADS):
        sl = slice(hd * DH, (hd + 1) * DH)
        q = q_ref[:, sl].astype(F32)
        k = q_ref[:, MWIDTH + hd * DH:MWIDTH + (hd + 1) * DH].astype(F32)
        v = q_ref[:, 2 * MWIDTH + hd * DH:2 * MWIDTH + (hd + 1) * DH].astype(F32)
        o = q_ref[:, 3 * MWIDTH + hd * DH:3 * MWIDTH + (hd + 1) * DH].astype(F32)
        n0 = n0_ref[:, sl]
        wi = w_inter_all[:, hd:hd + 1]
        wk = w_all[:, hd:hd + 1]
        s = jnp.sum(q * k, axis=-1, keepdims=True) * wk
        qn = jnp.sum(q * n0, axis=-1, keepdims=True)
        qt = qkt_ref[hd]
        kt = qkt_ref[HEADS + hd]
        for j in range(MSTEP_TOK):
            c0 = c0_ref[j, hd]
            q_b = _dot(jnp.where(lane_t == j, qt, 0.0).astype(BF16), ones_b)
            k_b = _dot(jnp.where(lane_t == j, kt, 0.0).astype(BF16), ones_b)
            qc_sc[j:j + 1, sl] = jnp.sum(q_b * c0, axis=0, keepdims=True)
            c_out[j, hd] = wi[j:j + 1, :] * c0 + (wk[j:j + 1, :] * k_b) * v[j:j + 1, :]
        num = wi * qc_sc[:, sl] + s * v
        den = wi * qn + s
        h = num / jnp.maximum(jnp.abs(den), floor_all[:, hd:hd + 1])
        hn = h * lax.rsqrt(jnp.mean(h * h, axis=-1, keepdims=True) + EPS)
        hm_ref[:, sl] = hn * mg_ref[:, sl] * _sigmoid(o)
        n_out[:, sl] = wi * n0 + wk * k


def _mlstm_step(qkvo, qkt, ig, lf, m0, c0, n0, mnorm_g):
    t = qkvo.shape[0]
    tk = MSTEP_TOK
    return pl.pallas_call(
        _mlstm_step_body,
        grid=(t // tk,),
        in_specs=[pl.BlockSpec((tk, 4 * MWIDTH), lambda i: (i, 0)),
                  pl.BlockSpec((None, 2 * HEADS, DH, LANES), lambda i: (i, 0, 0, 0)),
                  pl.BlockSpec((tk, LANES), lambda i: (i, 0)),
                  pl.BlockSpec((tk, LANES), lambda i: (i, 0)),
                  pl.BlockSpec((tk, LANES), lambda i: (i, 0)),
                  pl.BlockSpec((tk, HEADS, DH, DH), lambda i: (i, 0, 0, 0)),
                  pl.BlockSpec((tk, MWIDTH), lambda i: (i, 0)),
                  pl.BlockSpec((1, MWIDTH), lambda i: (0, 0))],
        out_specs=[pl.BlockSpec((tk, MWIDTH), lambda i: (i, 0)),
                   pl.BlockSpec((tk, HEADS, DH, DH), lambda i: (i, 0, 0, 0)),
                   pl.BlockSpec((tk, MWIDTH), lambda i: (i, 0)),
                   pl.BlockSpec((tk, LANES), lambda i: (i, 0))],
        out_shape=[jax.ShapeDtypeStruct((t, MWIDTH), F32),
                   jax.ShapeDtypeStruct((t, HEADS, DH, DH), F32),
                   jax.ShapeDtypeStruct((t, MWIDTH), F32),
                   jax.ShapeDtypeStruct((t, LANES), F32)],
        scratch_shapes=[pltpu.VMEM((tk, MWIDTH), F32)],
        compiler_params=_cparams("parallel"),
        name="mlstm_step",
    )(qkvo, qkt, ig, lf, m0, c0, n0, mnorm_g)


def _post_body(x_ref, y5_ref, hm_ref, gg_ref, g1_ref, sc_ref, sh_ref, n2_ref, wglu_ref, wmo_ref, wo_ref, rw_ref, rb_ref,
               cnt0_ref, x1_ref, h2_ref, eid_ref, wgt_ref, rank_ref, cnt_ref, run_sc, *, n_real, n_steps):
    i = pl.program_id(0)

    @pl.when(i == 0)
    def _():
        run_sc[...] = cnt0_ref[...]

    if n_steps > n_real:
        @pl.when(i >= n_real)
        def _():
            h2_ref[...] = jnp.zeros_like(h2_ref)

        pl.when(i < n_real)(functools.partial(
            _post_tile, x_ref, y5_ref, hm_ref, gg_ref, g1_ref, sc_ref, sh_ref, n2_ref, wglu_ref, wmo_ref, wo_ref,
            rw_ref, rb_ref, x1_ref, h2_ref, eid_ref, wgt_ref, rank_ref, run_sc))
    else:
        _post_tile(x_ref, y5_ref, hm_ref, gg_ref, g1_ref, sc_ref, sh_ref, n2_ref, wglu_ref, wmo_ref, wo_ref,
                   rw_ref, rb_ref, x1_ref, h2_ref, eid_ref, wgt_ref, rank_ref, run_sc)
    cnt_ref[...] = run_sc[...]


def _post_tile(x_ref, y5_ref, hm_ref, gg_ref, g1_ref, sc_ref, sh_ref, n2_ref, wglu_ref, wmo_ref, wo_ref, rw_ref, rb_ref,
               x1_ref, h2_ref, eid_ref, wgt_ref, rank_ref, run_sc):
    g5 = _gelu_tanh(y5_ref[...]).astype(BF16)
    glu = _dot(g5, wglu_ref[...])
    br_s5 = glu[:, :D_MODEL] * _sigmoid(glu[:, D_MODEL:])
    br_m = _dot(hm_ref[...].astype(BF16), wmo_ref[...])
    gg = gg_ref[...].astype(F32)
    merged = _sigmoid(gg[:, :D_MODEL]) * br_s5 + _sigmoid(gg[:, D_MODEL:]) * br_m
    x1 = x_ref[...] + g1_ref[...] * _dot(merged.astype(BF16), wo_ref[...])
    x1_ref[...] = x1
    ms = jnp.mean(x1 * x1, axis=-1, keepdims=True)
    h2 = x1 * lax.rsqrt(ms + EPS) * n2_ref[...] * (1.0 + sc_ref[...]) + sh_ref[...]
    h2_ref[...] = h2
    logits = _dot_hp(h2, rw_ref[...]) + rb_ref[...]
    tm = logits.shape[0]
    lane = lax.broadcasted_iota(jnp.int32, logits.shape, 1)
    l = jnp.where(lane < N_EXPERTS, logits, -jnp.inf)
    vals, hots, idxs = [], [], []
    for _ in range(TOP_K):
        mx = jnp.max(l, axis=-1, keepdims=True)
        idx = jnp.min(jnp.where(l == mx, lane, LANES), axis=-1, keepdims=True)
        hot = lane == idx
        vals.append(mx)
        hots.append(hot)
        idxs.append(idx)
        l = jnp.where(hot, -jnp.inf, l)
    ex = [jnp.exp(vk - vals[0]) for vk in vals]
    tot = ex[0] + ex[1] + ex[2] + ex[3]
    chosen = jnp.where(hots[0] | hots[1] | hots[2] | hots[3], 1.0, 0.0)
    r_io = lax.broadcasted_iota(jnp.int32, (tm, tm), 0)
    c_io = lax.broadcasted_iota(jnp.int32, (tm, tm), 1)
    earlier = (c_io < r_io).astype(BF16)
    before = run_sc[...] + _dot(earlier, chosen.astype(BF16))
    eid = jnp.zeros(logits.shape, jnp.int32)
    wgt = jnp.zeros(logits.shape, F32)
    rank = jnp.zeros(logits.shape, F32)
    for k in range(TOP_K):
        eid = jnp.where(lane == k, idxs[k], eid)
        wgt = jnp.where(lane == k, ex[k] / tot, wgt)
        rank = jnp.where(lane == k, jnp.sum(jnp.where(hots[k], before, 0.0), axis=-1, keepdims=True), rank)
    eid_ref[...] = eid
    wgt_ref[...] = wgt
    rank_ref[...] = rank.astype(jnp.int32)
    run_sc[...] += jnp.sum(chosen, axis=0, keepdims=True)


def _post(x2d, y5, hm, gg, mod3, rows_per_mod, tm, n2, wglu, wmo, wo, rw, rb, cnt0, h2_rows):
    t = x2d.shape[0]
    mrows = mod3.shape[1]
    n_real = t // tm
    n_steps = -(-h2_rows // tm)
    last = n_real - 1

    def mod_spec(j):
        return pl.BlockSpec((None, mrows, D_MODEL), lambda i: ((jnp.minimum(i, last) * tm) // rows_per_mod, 0, j))

    const = lambda shape: pl.BlockSpec(shape, lambda i: (0,) * len(shape))
    tile = lambda w: pl.BlockSpec((tm, w), lambda i: (jnp.minimum(i, last), 0))
    return pl.pallas_call(
        functools.partial(_post_body, n_real=n_real, n_steps=n_steps),
        grid=(n_steps,),
        in_specs=[tile(D_MODEL), tile(S5_WIDTH), tile(MWIDTH), tile(2 * D_MODEL),
                  mod_spec(2), mod_spec(4), mod_spec(3), const((1, D_MODEL)),
                  const(wglu.shape), const(wmo.shape), const(wo.shape), const(rw.shape), const(rb.shape),
                  const((1, LANES))],
        out_specs=[tile(D_MODEL), pl.BlockSpec((tm, D_MODEL), lambda i: (i, 0)), tile(LANES), tile(LANES), tile(LANES),
                   const((1, LANES))],
        out_shape=[jax.ShapeDtypeStruct((t, D_MODEL), F32),
                   jax.ShapeDtypeStruct((h2_rows, D_MODEL), F32),
                   jax.ShapeDtypeStruct((t, LANES), jnp.int32),
                   jax.ShapeDtypeStruct((t, LANES), F32),
                   jax.ShapeDtypeStruct((t, LANES), jnp.int32),
                   jax.ShapeDtypeStruct((1, LANES), F32)],
        scratch_shapes=[pltpu.VMEM((1, LANES), F32)],
        compiler_params=_cparams("arbitrary"),
        name="post",
    )(x2d, y5, hm, gg, mod3, mod3, mod3, n2, wglu, wmo, wo, rw, rb, cnt0)


N_TOKENS = N_PROMPT * SEQ + N_SAMPLE
N_ASSIGN = N_TOKENS * TOP_K
MOE_TILE = 256
MOE_TILES = N_ASSIGN // MOE_TILE + N_EXPERTS
N_SLOT = MOE_TILES * MOE_TILE
CMB_TILE = 512


MOE_RING = 3
MOE_SHORT = 64


def _moe_body(te_ref, nt_ref, nxe_ref, gpar_ref, rows_ref, idx0_ref, idx1_ref, idx2_ref, h2_hbm, wup_hbm, bup_ref,
              wdn_hbm, bdn_ref, y_ref, xb0, xb1, xb2, sem, wbuf_up, wbuf_dn, wsem, wup_bf, wdn_bf):
    i = pl.program_id(0)
    nt = nt_ref[0]
    bufs = (xb0, xb1, xb2)
    last = MOE_TILES - 1

    def gather(idx_ref, slot):
        for r in range(MOE_TILE):
            pltpu.make_async_copy(h2_hbm.at[pl.ds(idx_ref[0, r], 1)], bufs[slot].at[pl.ds(r, 1)],
                                  sem.at[slot]).start()

    def wait(slot):
        pltpu.make_async_copy(h2_hbm.at[pl.ds(0, MOE_TILE)], bufs[slot], sem.at[slot]).wait()

    def weights(e, slot):
        return (pltpu.make_async_copy(wup_hbm.at[e], wbuf_up.at[slot], wsem.at[0, slot]),
                pltpu.make_async_copy(wdn_hbm.at[e], wbuf_dn.at[slot], wsem.at[1, slot]))

    wslot = gpar_ref[i]

    @pl.when(i == 0)
    def _():
        for c in weights(te_ref[0], wslot):
            c.start(priority=1)
        gather(idx0_ref, 0)
        gather(idx1_ref, 1)

    new_expert = jnp.logical_or(i == 0, te_ref[i] != te_ref[jnp.maximum(i - 1, 0)])
    live = i < nt

    @pl.when(jnp.logical_and(new_expert, live))
    def _():
        for c in weights(0, wslot):
            c.wait()
        wup_bf[...] = wbuf_up[wslot].astype(BF16)
        wdn_bf[...] = wbuf_dn[wslot].astype(BF16)
        nxe = nxe_ref[i]

        @pl.when(nxe >= 0)
        def _():
            for c in weights(nxe, 1 - wslot):
                c.start(priority=1)

    def step(slot, rows):
        wait(slot)
        gather(idx2_ref, (slot + 2) % MOE_RING)
        gu = _dot(bufs[slot][:rows, :].astype(BF16), wup_bf[...]) + bup_ref[...]
        g = jnp.minimum(gu[:, :D_FF], SWIGLU_LIMIT)
        up = jnp.clip(gu[:, D_FF:], -SWIGLU_LIMIT, SWIGLU_LIMIT)
        act = (up + 1.0) * g * _sigmoid(SWIGLU_ALPHA * g)
        y_ref[:rows, :] = _dot(act.astype(BF16), wdn_bf[...]) + bdn_ref[...]
        if rows < MOE_TILE:
            y_ref[rows:, :] = jnp.zeros((MOE_TILE - rows, D_MODEL), F32)

    short = rows_ref[i] <= MOE_SHORT
    for slot in range(MOE_RING):
        mine = jnp.logical_and(live, i % MOE_RING == slot)
        pl.when(jnp.logical_and(mine, jnp.logical_not(short)))(functools.partial(step, slot, MOE_TILE))
        pl.when(jnp.logical_and(mine, short))(functools.partial(step, slot, MOE_SHORT))

    @pl.when(jnp.logical_not(live))
    def _():
        y_ref[...] = jnp.zeros_like(y_ref)

    drain = jnp.logical_and(jnp.logical_not(live), i < nt + 2)
    for slot in range(MOE_RING):
        pl.when(jnp.logical_and(drain, i % MOE_RING == slot))(functools.partial(wait, slot))
    pl.when(jnp.logical_and(i == last, last - 1 < nt))(functools.partial(wait, (last + 1) % MOE_RING))
    pl.when(jnp.logical_and(i == last, last < nt))(functools.partial(wait, (last + 2) % MOE_RING))


def _moe(tile_expert, n_tiles, next_expert, group_parity, tile_rows, slot_token, h2_all, wup, bup, wdn, bdn):
    idx3 = slot_token.reshape(MOE_TILES, 1, MOE_TILE)
    ahead = lambda k: pl.BlockSpec((None, 1, MOE_TILE),
                                   lambda i, *_: (jnp.minimum(i + k, MOE_TILES - 1), 0, 0), memory_space=pltpu.SMEM)
    grid_spec = pltpu.PrefetchScalarGridSpec(
        num_scalar_prefetch=5,
        grid=(MOE_TILES,),
        in_specs=[ahead(0), ahead(1), ahead(2),
                  pl.BlockSpec(memory_space=pl.ANY),
                  pl.BlockSpec(memory_space=pl.ANY),
                  pl.BlockSpec((None, 1, 2 * D_FF), lambda i, te, *_: (te[i], 0, 0)),
                  pl.BlockSpec(memory_space=pl.ANY),
                  pl.BlockSpec((None, 1, D_MODEL), lambda i, te, *_: (te[i], 0, 0))],
        out_specs=pl.BlockSpec((MOE_TILE, D_MODEL), lambda i, *_: (i, 0)),
        scratch_shapes=[pltpu.VMEM((MOE_TILE, D_MODEL), F32), pltpu.VMEM((MOE_TILE, D_MODEL), F32),
                        pltpu.VMEM((MOE_TILE, D_MODEL), F32),
                        pltpu.SemaphoreType.DMA((MOE_RING,)),
                        pltpu.VMEM((2, D_MODEL, 2 * D_FF), F32), pltpu.VMEM((2, D_FF, D_MODEL), F32),
                        pltpu.SemaphoreType.DMA((2, 2)),
                        pltpu.VMEM((D_MODEL, 2 * D_FF), BF16), pltpu.VMEM((D_FF, D_MODEL), BF16)])
    return pl.pallas_call(
        _moe_body,
        grid_spec=grid_spec,
        out_shape=jax.ShapeDtypeStruct((N_SLOT, D_MODEL), F32),
        compiler_params=_cparams("arbitrary"),
        name="moe",
    )(tile_expert, n_tiles, next_expert, group_parity, tile_rows, idx3, idx3, idx3, h2_all, wup, bup, wdn, bdn)


def _combine_body(cur_ref, nxt_ref, wgt_ref, x1_ref, g2_ref, fg_ref, y_hbm, o_ref, yb0, yb1, sem, *, n_steps, tm):
    i = pl.program_id(0)

    def gather(idx_ref, buf, s):
        for r in range(tm):
            for k in range(TOP_K):
                pltpu.make_async_copy(y_hbm.at[pl.ds(idx_ref[0, r * TOP_K + k], 1)],
                                      buf.at[pl.ds(k * tm + r, 1)], sem.at[s]).start(priority=k % 2)

    def wait(buf, s):
        pltpu.make_async_copy(y_hbm.at[pl.ds(0, TOP_K * tm)], buf, sem.at[s]).wait()

    @pl.when(i == 0)
    def _():
        gather(cur_ref, yb0, 0)

    def step(cur, nxt, s_cur, s_nxt):
        wait(cur, s_cur)
        gather(nxt_ref, nxt, s_nxt)
        wgt = wgt_ref[...]
        acc = wgt[:, 0:1] * cur[0:tm, :]
        for k in range(1, TOP_K):
            acc = acc + wgt[:, k:k + 1] * cur[k * tm:(k + 1) * tm, :]
        xo = x1_ref[...] + g2_ref[...] * acc
        ms = jnp.mean(xo * xo, axis=-1, keepdims=True)
        o_ref[...] = xo * lax.rsqrt(ms + EPS) * fg_ref[...]

    pl.when(i % 2 == 0)(functools.partial(step, yb0, yb1, 0, 1))
    pl.when(i % 2 == 1)(functools.partial(step, yb1, yb0, 1, 0))
    last_slot = n_steps % 2
    pl.when(i == n_steps - 1)(functools.partial(wait, yb1 if last_slot else yb0, last_slot))


def _combine(pos, wgt, x1, mod3, rows_per_mod, fg, y_slots, tm):
    t = x1.shape[0]
    n = t // tm
    mrows = mod3.shape[1]
    assert mrows in (1, tm) and rows_per_mod % tm == 0
    pos3 = pos.reshape(n, 1, tm * TOP_K)
    return pl.pallas_call(
        functools.partial(_combine_body, n_steps=n, tm=tm),
        grid=(n,),
        in_specs=[pl.BlockSpec((None, 1, tm * TOP_K), lambda i: (i, 0, 0), memory_space=pltpu.SMEM),
                  pl.BlockSpec((None, 1, tm * TOP_K), lambda i: (jnp.minimum(i + 1, n - 1), 0, 0),
                               memory_space=pltpu.SMEM),
                  pl.BlockSpec((tm, LANES), lambda i: (i, 0)),
                  pl.BlockSpec((tm, D_MODEL), lambda i: (i, 0)),
                  pl.BlockSpec((None, mrows, D_MODEL), lambda i: ((i * tm) // rows_per_mod, 0, 5)),
                  pl.BlockSpec((1, D_MODEL), lambda i: (0, 0)),
                  pl.BlockSpec(memory_space=pl.ANY)],
        out_specs=pl.BlockSpec((tm, D_MODEL), lambda i: (i, 0)),
        out_shape=jax.ShapeDtypeStruct((t, D_MODEL), F32),
        scratch_shapes=[pltpu.VMEM((TOP_K * tm, D_MODEL), F32), pltpu.VMEM((TOP_K * tm, D_MODEL), F32),
                        pltpu.SemaphoreType.DMA((2,))],
        compiler_params=_cparams("arbitrary"),
        name="combine",
    )(pos3, pos3, wgt, x1, mod3, fg, y_slots)


def _routing(eid_p, rank_p, eid_s, rank_s, counts):
    cnt = counts[0, :N_EXPERTS].astype(jnp.int32)
    ntile = (cnt + MOE_TILE - 1) // MOE_TILE
    tile_end = jnp.cumsum(ntile)
    poff = (tile_end - ntile) * MOE_TILE
    total = tile_end[-1]
    j = jnp.arange(MOE_TILES, dtype=jnp.int32)
    te = jnp.sum((j[:, None] >= tile_end[None, :]).astype(jnp.int32), axis=1)
    te_last = jnp.sum(((total - 1) >= tile_end).astype(jnp.int32))
    tile_expert = jnp.where(j < total, te, te_last).astype(jnp.int32)
    ex = jnp.arange(N_EXPERTS, dtype=jnp.int32)
    used = ntile > 0
    later = jnp.where(used[None, :] & (ex[None, :] > ex[:, None]), ex[None, :], N_EXPERTS)
    nxt_e = jnp.min(later, axis=1)
    nxt_e = jnp.where(nxt_e < N_EXPERTS, nxt_e, -1).astype(jnp.int32)
    par_e = ((jnp.cumsum(used.astype(jnp.int32)) - 1) % 2).astype(jnp.int32)
    pick = tile_expert[:, None] == ex[None, :]
    next_expert = jnp.sum(jnp.where(pick, nxt_e[None, :], 0), axis=1).astype(jnp.int32)
    group_parity = jnp.sum(jnp.where(pick, par_e[None, :], 0), axis=1).astype(jnp.int32)
    left = jnp.sum(jnp.where(pick, (cnt + poff)[None, :], 0), axis=1) - j * MOE_TILE
    tile_rows = jnp.where(j < total, jnp.clip(left, 0, MOE_TILE), 0).astype(jnp.int32)

    def pos_of(eid, rank):
        e = eid[:, :TOP_K]
        off = jnp.sum(jnp.where(e[:, :, None] == ex[None, None, :], poff[None, None, :], 0), axis=-1)
        return off + rank[:, :TOP_K]

    pos_p = pos_of(eid_p, rank_p)
    pos_s = pos_of(eid_s, rank_s)
    tok = jnp.concatenate([jnp.repeat(jnp.arange(N_PROMPT * SEQ, dtype=jnp.int32), TOP_K),
                           jnp.repeat(N_PROMPT * SEQ + jnp.arange(N_SAMPLE, dtype=jnp.int32), TOP_K)])
    pos_all = jnp.concatenate([pos_p.reshape(-1), pos_s.reshape(-1)])
    padc = jnp.concatenate([ntile * MOE_TILE - cnt, (N_SLOT - total * MOE_TILE).reshape(1)])
    pad_end = jnp.cumsum(padc)
    pad_base = jnp.concatenate([poff + cnt, (total * MOE_TILE).reshape(1)])
    jp = jnp.arange(N_SLOT - N_ASSIGN, dtype=jnp.int32)
    pe = jnp.sum((jp[:, None] >= pad_end[None, :]).astype(jnp.int32), axis=1)
    hit = pe[:, None] == jnp.arange(N_EXPERTS + 1, dtype=jnp.int32)[None, :]
    pad_slot = jp + jnp.sum(jnp.where(hit, (pad_base - (pad_end - padc))[None, :], 0), axis=1)
    keys = jnp.concatenate([pos_all, pad_slot.astype(jnp.int32)])
    vals = jnp.concatenate([tok, jnp.zeros((N_SLOT - N_ASSIGN,), jnp.int32)])
    tok_bits = 15
    assert N_TOKENS <= (1 << tok_bits) and N_SLOT <= (1 << (32 - tok_bits))
    packed = lax.sort((keys.astype(jnp.uint32) << tok_bits) | vals.astype(jnp.uint32))
    slot_token = (packed & ((1 << tok_bits) - 1)).astype(jnp.int32)
    return (tile_expert, total.reshape(1).astype(jnp.int32), next_expert, group_parity, tile_rows, slot_token,
            pos_p, pos_s)


def _unpack_s5_state(x):
    n = x.shape[1]
    z = x.reshape(N_SUPER, n, 2, S5_SUPER, S5_STATE).transpose(2, 1, 0, 3, 4).reshape(2, n, S5_GROUPS, S5_STATE)
    return z[0], z[1]


def _pack_s5_state(re, im):
    n = re.shape[0]
    z = jnp.stack([re, im], axis=0).reshape(2, n, N_SUPER, S5_SUPER, S5_STATE)
    return z.transpose(2, 1, 0, 3, 4).reshape(N_SUPER, n, SUPER_STATE)


def kernel(x_prompt, x_sample, c_prompt, c_sample, state_s5_re, state_s5_im, state_mlstm_C, state_mlstm_n, state_mlstm_m, norm1_g, norm2_g, final_norm_g, w_ada, b_ada, w_in, s5_lambda_re, s5_lambda_im, s5_log_dt, s5_B_re, s5_B_im, s5_C_re, s5_C_im, s5_D, s5_w_glu, mlstm_b_i, mlstm_b_f, mlstm_norm_g, mlstm_w_out, w_out, router_w, router_b, expert_w_up, expert_b_up, expert_w_down, expert_b_down):
    assert w_in.shape[0] == 1, "single layer"
    tp = N_PROMPT * SEQ
    xp = x_prompt.reshape(tp, D_MODEL).astype(F32)
    xs = x_sample.reshape(N_SAMPLE, D_MODEL).astype(F32)

    w = w_in[0]
    c0 = S5_WIDTH
    c1 = c0 + 4 * MWIDTH
    c2 = c1 + 2 * HEADS
    wu = w[:, :c0].astype(BF16)
    wq = w[:, c0:c1].astype(BF16)
    wif = jnp.pad(w[:, c1:c2].astype(F32), ((0, 0), (0, LANES - 2 * HEADS)))
    wg = w[:, c2:].astype(BF16)
    bvec = jnp.concatenate([mlstm_b_i[0], mlstm_b_f[0]]).astype(F32)
    bif = jnp.pad(bvec, (0, LANES - 2 * HEADS)).reshape(1, LANES)
    wparts = (wu, wq, wg, wif, bif)
    g1 = norm1_g[0].reshape(1, D_MODEL).astype(F32)
    n2 = norm2_g[0].reshape(1, D_MODEL).astype(F32)
    fg = final_norm_g.reshape(1, D_MODEL).astype(F32)
    mg = mlstm_norm_g[0].reshape(1, MWIDTH).astype(F32)
    wglu = s5_w_glu[0].astype(BF16)
    wmo = mlstm_w_out[0].astype(BF16)
    wo = w_out[0].astype(BF16)
    rw = jnp.pad(router_w[0].astype(F32), ((0, 0), (0, LANES - N_EXPERTS)))
    rb = jnp.pad(router_b[0].astype(F32), (0, LANES - N_EXPERTS)).reshape(1, LANES)
    wup = expert_w_up[0].astype(F32)
    wdn = expert_w_down[0].astype(F32)
    bup = expert_b_up[0].astype(F32).reshape(N_EXPERTS, 1, 2 * D_FF)
    bdn = expert_b_down[0].astype(F32).reshape(N_EXPERTS, 1, D_MODEL)
    ddiag, fcomp, ecomp, a16, a1 = _s5_tables(
        s5_lambda_re[0], s5_lambda_im[0], s5_log_dt[0], s5_B_re[0], s5_B_im[0], s5_C_re[0], s5_C_im[0], s5_D[0])

    c_all = jnp.concatenate([c_prompt, c_sample], axis=0).astype(F32)
    mod = _adaln(c_all, w_ada[0].astype(F32), b_ada[0].astype(F32))
    mod_p = mod[:N_PROMPT].reshape(N_PROMPT, 1, 6 * D_MODEL)
    mod_s = mod[N_PROMPT:].reshape(1, N_SAMPLE, 6 * D_MODEL)

    u, qkvo, gg, gcol, grow = _inproj(xp, mod_p, SEQ, 1024, g1, wparts)
    y5, xend, k0, e0, f1 = _s5_prompt(u, ddiag, fcomp, ecomp, a16, a1)
    hm, c_p, n_p, m_p = _mlstm_prompt(qkvo, gcol, grow, mg)
    cnt0 = jnp.zeros((1, LANES), F32)
    x1, h2_all, eid_p, wgt_p, rank_p, cnt_p = _post(xp, y5, hm, gg, mod_p, SEQ, 512, n2, wglu, wmo, wo, rw, rb,
                                                    cnt0, N_TOKENS)
    p_re, p_im = _unpack_s5_state(xend.reshape(N_SUPER, N_PROMPT, SUPER_STATE))
    n_p = n_p[..., 0]
    m_p = m_p[:, :HEADS, 0]

    us, qs, ggs, gcs, _ = _inproj(xs, mod_s, N_SAMPLE, N_SAMPLE, g1, wparts)
    x0 = _pack_s5_state(state_s5_re[0].astype(F32), state_s5_im[0].astype(F32))
    y5s, xns = _s5_step(us, x0, k0, e0, f1, a1)
    s_re, s_im = _unpack_s5_state(xns)
    qkt = qs[:, :2 * MWIDTH].astype(F32).reshape(N_SAMPLE // MSTEP_TOK, MSTEP_TOK, 2 * HEADS, DH).transpose(0, 2, 3, 1)
    qkt = jnp.pad(qkt, ((0, 0), (0, 0), (0, 0), (0, LANES - MSTEP_TOK)))
    pad_heads = lambda a: jnp.pad(a, ((0, 0), (0, LANES - HEADS)))
    m0 = pad_heads(state_mlstm_m[0].astype(F32))
    hms, c_s, n_s, m_s = _mlstm_step(qs, qkt, pad_heads(gcs[:, :HEADS]), pad_heads(gcs[:, HEADS:2 * HEADS]), m0,
                                     state_mlstm_C[0].astype(F32),
                                     state_mlstm_n[0].astype(F32).reshape(N_SAMPLE, MWIDTH), mg)
    x1s, h2s, eid_s, wgt_s, rank_s, cnt_all = _post(xs, y5s, hms, ggs, mod_s, N_SAMPLE, N_SAMPLE, n2, wglu, wmo, wo, rw, rb,
                                                    cnt_p, N_SAMPLE)

    h2_all = lax.dynamic_update_slice(h2_all, h2s, (tp, 0))
    tile_expert, n_tiles, next_expert, group_parity, tile_rows, slot_token, pos_p, pos_s = _routing(
        eid_p, rank_p, eid_s, rank_s, cnt_all)
    y_slots = _moe(tile_expert, n_tiles, next_expert, group_parity, tile_rows, slot_token, h2_all, wup, bup, wdn, bdn)
    y_p = _combine(pos_p, wgt_p, x1, mod_p, SEQ, fg, y_slots, CMB_TILE)
    y_s = _combine(pos_s, wgt_s, x1s, mod_s, N_SAMPLE, fg, y_slots, N_SAMPLE)

    return (y_p.reshape(N_PROMPT, SEQ, D_MODEL).astype(x_prompt.dtype),
            y_s.reshape(N_SAMPLE, 1, D_MODEL).astype(x_sample.dtype),
            p_re[None], p_im[None], c_p[None], n_p[None], m_p[None],
            s_re[None], s_im[None], c_s[None],
            n_s.reshape(1, N_SAMPLE, HEADS, DH), m_s[:, :HEADS][None])
```
